```python
import jax, jax.numpy as jnp
from jax import lax
import numpy as np

D_MODEL = 2048
BATCH = 4
SEQ = 2048
DEPTH = 1
DEC_BATCH = 128
DEC_SEQ = 8
PAST_LEN = 16384
PAGE_SIZE = 128

SSD_EXPAND = 2
SSD_D_INNER = SSD_EXPAND * D_MODEL
SSD_HEAD_DIM = 64
SSD_HEADS = SSD_D_INNER // SSD_HEAD_DIM
SSD_GROUPS = 8
SSD_STATE = 128
SSD_CONV_W = 4
SSD_CONV_DIM = SSD_D_INNER + 2 * SSD_GROUPS * SSD_STATE
SSD_CHUNK = 128
SSD_NORM_EPS = 1e-5
SC_DIM = D_MODEL
SC_W = 3
MOE_GROUPS = 4
MOE_EXPERTS_PER_GROUP = 8
MOE_EXPERTS = MOE_GROUPS * MOE_EXPERTS_PER_GROUP
MOE_TOP_K = 2
MOE_D_FF = D_MODEL // 2
MOE_BLOCK = 128
NORM_EPS = 1e-6

OFF_GATE_A = 0
OFF_GATE_B = D_MODEL
OFF_Z = 2 * D_MODEL
OFF_XBC = OFF_Z + SSD_D_INNER
OFF_DT = OFF_XBC + SSD_CONV_DIM
OFF_SC = OFF_DT + SSD_HEADS
IN_PROJ_DIM = OFF_SC + 3 * SC_DIM

kernel_name = "hybrid_ssd_shortconv_hmoe_step"


def rmsnorm(x, w, eps=NORM_EPS):
    xf = x.astype(jnp.float32)
    y = xf * lax.rsqrt(jnp.mean(xf * xf, axis=-1, keepdims=True) + eps)
    return (y * w.astype(jnp.float32)).astype(x.dtype)


def causal_dwconv(u, buf, w):
    L = u.shape[1]
    K = w.shape[0]
    up = jnp.concatenate([buf.astype(u.dtype), u], axis=1)
    y = sum(up[:, k:k + L] * w[k] for k in range(K))
    return y, up[:, L:]


def gated_group_rmsnorm(y, z, w):
    g = y.astype(jnp.float32) * jax.nn.silu(z.astype(jnp.float32))
    shp = g.shape
    g = g.reshape(shp[:-1] + (SSD_GROUPS, shp[-1] // SSD_GROUPS))
    g = g * lax.rsqrt(jnp.mean(g * g, axis=-1, keepdims=True) + SSD_NORM_EPS)
    return g.reshape(shp) * w.astype(jnp.float32)


def ssd_chunked(xh, dt, A, Bm, Cm, h0):
    b, L, H, P = xh.shape
    G, N = Bm.shape[2], Bm.shape[3]
    Hg = H // G
    Q = SSD_CHUNK if L % SSD_CHUNK == 0 else L
    nc = L // Q
    f32 = jnp.float32
    x = xh.astype(f32).reshape(b, nc, Q, G, Hg, P)
    dtc = dt.reshape(b, nc, Q, G, Hg)
    a = dtc * A.reshape(G, Hg)
    Bc = Bm.astype(f32).reshape(b, nc, Q, G, N)
    Cc = Cm.astype(f32).reshape(b, nc, Q, G, N)
    cs = jnp.cumsum(a, axis=2)
    causal = jnp.tril(jnp.ones((Q, Q), dtype=bool))[:, :, None, None]
    seg = cs[:, :, :, None] - cs[:, :, None, :]
    decay = jnp.exp(jnp.where(causal, seg, -jnp.inf))
    CB = jnp.einsum('bcqgn,bcsgn->bcqsg', Cc, Bc)
    scores = CB[..., None] * decay
    y_diag = jnp.einsum('bcqsgh,bcsgh,bcsghp->bcqghp', scores, dtc, x)
    decay_end = jnp.exp(cs[:, :, -1:] - cs)
    chunk_states = jnp.einsum('bcsgn,bcsgh,bcsghp->bcghpn', Bc, decay_end * dtc, x)
    chunk_decay = jnp.exp(cs[:, :, -1])

    def step(h, inp):
        st, dec = inp
        return dec[..., None, None] * h + st, h

    h_init = h0.astype(f32).reshape(b, G, Hg, P, N)
    h_final, h_prev = lax.scan(step, h_init, (jnp.moveaxis(chunk_states, 1, 0), jnp.moveaxis(chunk_decay, 1, 0)))
    h_prev = jnp.moveaxis(h_prev, 0, 1)
    y_off = jnp.einsum('bcqgn,bcghpn,bcqgh->bcqghp', Cc, h_prev, jnp.exp(cs))
    y = (y_diag + y_off).reshape(b, L, H, P)
    return y, h_final.reshape(b, H, P, N)


def token_mixers(h, ssm0, conv0, sc0, w_in, conv_w, conv_b, dt_bias, a_log, d_skip, ssd_norm_w, sc_conv_w, w_branch_out, w_out):
    b, L, _ = h.shape
    proj = h @ w_in
    gate_a = jax.nn.sigmoid(proj[..., OFF_GATE_A:OFF_GATE_B])
    gate_b = jax.nn.sigmoid(proj[..., OFF_GATE_B:OFF_Z])
    z = proj[..., OFF_Z:OFF_XBC]
    xbc, conv_new = causal_dwconv(proj[..., OFF_XBC:OFF_DT], conv0, conv_w)
    xbc = jax.nn.silu(xbc + conv_b)
    dt_raw = proj[..., OFF_DT:OFF_SC]
    sc_b = proj[..., OFF_SC:OFF_SC + SC_DIM]
    sc_c = proj[..., OFF_SC + SC_DIM:OFF_SC + 2 * SC_DIM]
    sc_h = proj[..., OFF_SC + 2 * SC_DIM:]
    GN = SSD_GROUPS * SSD_STATE
    xs = xbc[..., :SSD_D_INNER].reshape(b, L, SSD_HEADS, SSD_HEAD_DIM)
    Bm = xbc[..., SSD_D_INNER:SSD_D_INNER + GN].reshape(b, L, SSD_GROUPS, SSD_STATE)
    Cm = xbc[..., SSD_D_INNER + GN:].reshape(b, L, SSD_GROUPS, SSD_STATE)
    dt = jax.nn.softplus(dt_raw.astype(jnp.float32) + dt_bias.astype(jnp.float32))
    A = -jnp.exp(a_log.astype(jnp.float32))
    y, ssm_new = ssd_chunked(xs, dt, A, Bm, Cm, ssm0)
    y = y + xs.astype(jnp.float32) * d_skip.astype(jnp.float32)[:, None]
    y_a = gated_group_rmsnorm(y.reshape(b, L, SSD_D_INNER), z, ssd_norm_w).astype(h.dtype)
    v, sc_new = causal_dwconv(sc_c * sc_h, sc0, sc_conv_w)
    y_b = sc_b * v
    merged = gate_a * (y_a @ w_branch_out[:SSD_D_INNER]) + gate_b * (y_b @ w_branch_out[SSD_D_INNER:])
    return merged @ w_out, ssm_new.astype(ssm0.dtype), conv_new, sc_new


def hier_moe(h, w_rc, w_rf, w_gate, w_up, w_down):
    b, L, D = h.shape
    T = b * L
    xt = h.reshape(T, D)
    lc = (xt @ w_rc).astype(jnp.float32)
    pc = jax.nn.softmax(lc, axis=-1)
    grp = jnp.argmax(lc, axis=-1)
    p_grp = jnp.take_along_axis(pc, grp[:, None], axis=1)[:, 0]
    lf = (xt @ w_rf).astype(jnp.float32).reshape(T, MOE_GROUPS, MOE_EXPERTS_PER_GROUP)
    lf_sel = jnp.take_along_axis(lf, grp[:, None, None], axis=1)[:, 0]
    topv, topi = lax.top_k(lf_sel, MOE_TOP_K)
    wts = jax.nn.softmax(topv, axis=-1) * p_grp[:, None]
    eid = (grp[:, None] * MOE_EXPERTS_PER_GROUP + topi).astype(jnp.int32)
    A_n = T * MOE_TOP_K
    e_flat = eid.reshape(A_n)
    w_flat = wts.reshape(A_n)
    tok_flat = jnp.broadcast_to(jnp.arange(T, dtype=jnp.int32)[:, None], (T, MOE_TOP_K)).reshape(A_n)
    order = jnp.argsort(e_flat)
    e_s, tok_s, w_s = e_flat[order], tok_flat[order], w_flat[order]
    counts = jnp.bincount(e_flat, length=MOE_EXPERTS).astype(jnp.int32)
    start = jnp.cumsum(counts) - counts
    padded = (counts + MOE_BLOCK - 1) // MOE_BLOCK * MOE_BLOCK
    pstart = jnp.cumsum(padded) - padded
    pend = pstart + padded
    dest = pstart[e_s] + (jnp.arange(A_n, dtype=jnp.int32) - start[e_s])
    NB = -(-A_n // MOE_BLOCK) + MOE_EXPERTS
    R = NB * MOE_BLOCK
    row_tok = jnp.full((R,), T, dtype=jnp.int32).at[dest].set(tok_s)
    row_w = jnp.zeros((R,), jnp.float32).at[dest].set(w_s)
    blk_e = jnp.clip(jnp.searchsorted(pend, jnp.arange(NB, dtype=jnp.int32) * MOE_BLOCK, side='right'), 0, MOE_EXPERTS - 1)
    x_pad = jnp.concatenate([xt, jnp.zeros((1, D), xt.dtype)], axis=0)
    xb = x_pad[row_tok].reshape(NB, MOE_BLOCK, D)

    def expert_block(args):
        xblk, e = args
        return (jax.nn.silu(xblk @ w_gate[e]) * (xblk @ w_up[e])) @ w_down[e]

    yb = lax.map(expert_block, (xb, blk_e)).reshape(R, D)
    y = jax.ops.segment_sum(yb.astype(jnp.float32) * row_w[:, None], row_tok, num_segments=T + 1)[:T]
    return y.astype(h.dtype).reshape(b, L, D)


def trunk(x, st_ssm, st_conv, st_sc, weights):
    (norm_mixer, w_in, ssd_conv_w, ssd_conv_b, ssd_dt_bias, ssd_a_log, ssd_d, ssd_norm, sc_conv_w,
     w_branch_out, w_out, norm_ffn, w_router_coarse, w_router_fine, w_expert_gate, w_expert_up,
     w_expert_down, norm_final) = weights
    new_ssm, new_conv, new_sc = [], [], []
    for l in range(DEPTH):
        h = rmsnorm(x, norm_mixer[l])
        mix, s_ssm, s_conv, s_sc = token_mixers(h, st_ssm[l], st_conv[l], st_sc[l], w_in[l], ssd_conv_w[l], ssd_conv_b[l],
                                                 ssd_dt_bias[l], ssd_a_log[l], ssd_d[l], ssd_norm[l], sc_conv_w[l],
                                                 w_branch_out[l], w_out[l])
        x = x + mix
        x = x + hier_moe(rmsnorm(x, norm_ffn[l]), w_router_coarse[l], w_router_fine[l],
                         w_expert_gate[l], w_expert_up[l], w_expert_down[l])
        new_ssm.append(s_ssm)
        new_conv.append(s_conv)
        new_sc.append(s_sc)
    return rmsnorm(x, norm_final), jnp.stack(new_ssm), jnp.stack(new_conv), jnp.stack(new_sc)


def setup_inputs(seed: int = 0) -> dict:
    key = jax.random.key(seed)
    ks = jax.random.split(key, 24)
    f32 = jnp.float32

    def nrm(k, shape, scale):
        return jax.random.normal(k, shape, f32) * scale

    dt0 = jnp.exp(jax.random.uniform(ks[8], (DEPTH, SSD_HEADS), f32, np.log(1e-3), np.log(1e-1)))
    return {
        "x_prompt": nrm(ks[0], (BATCH, SEQ, D_MODEL), 1.0),
        "x_sample": nrm(ks[1], (DEC_BATCH, DEC_SEQ, D_MODEL), 1.0),
        "state_ssm": nrm(ks[2], (DEPTH, DEC_BATCH, SSD_HEADS, SSD_HEAD_DIM, SSD_STATE), 0.1),
        "state_ssd_conv": nrm(ks[3], (DEPTH, DEC_BATCH, SSD_CONV_W - 1, SSD_CONV_DIM), 1.0),
        "state_short_conv": nrm(ks[4], (DEPTH, DEC_BATCH, SC_W - 1, SC_DIM), 0.5),
        "norm_mixer": 1.0 + nrm(ks[5], (DEPTH, D_MODEL), 0.02),
        "w_in": nrm(ks[6], (DEPTH, D_MODEL, IN_PROJ_DIM), D_MODEL ** -0.5),
        "ssd_conv_w": nrm(ks[7], (DEPTH, SSD_CONV_W, SSD_CONV_DIM), SSD_CONV_W ** -0.5),
        "ssd_conv_b": nrm(ks[9], (DEPTH, SSD_CONV_DIM), 0.02),
        "ssd_dt_bias": dt0 + jnp.log(-jnp.expm1(-dt0)),
        "ssd_a_log": jnp.log(jax.random.uniform(ks[10], (DEPTH, SSD_HEADS), f32, 1.0, 16.0)),
        "ssd_d": 1.0 + nrm(ks[11], (DEPTH, SSD_HEADS), 0.1),
        "ssd_norm": 1.0 + nrm(ks[12], (DEPTH, SSD_D_INNER), 0.02),
        "sc_conv_w": nrm(ks[13], (DEPTH, SC_W, SC_DIM), SC_W ** -0.5),
        "w_branch_out": nrm(ks[14], (DEPTH, SSD_D_INNER + SC_DIM, D_MODEL), (SSD_D_INNER + SC_DIM) ** -0.5),
        "w_out": nrm(ks[15], (DEPTH, D_MODEL, D_MODEL), D_MODEL ** -0.5),
        "norm_ffn": 1.0 + nrm(ks[16], (DEPTH, D_MODEL), 0.02),
        "w_router_coarse": nrm(ks[17], (DEPTH, D_MODEL, MOE_GROUPS), D_MODEL ** -0.5),
        "w_router_fine": nrm(ks[18], (DEPTH, D_MODEL, MOE_EXPERTS), D_MODEL ** -0.5),
        "w_expert_gate": nrm(ks[19], (DEPTH, MOE_EXPERTS, D_MODEL, MOE_D_FF), D_MODEL ** -0.5),
        "w_expert_up": nrm(ks[20], (DEPTH, MOE_EXPERTS, D_MODEL, MOE_D_FF), D_MODEL ** -0.5),
        "w_expert_down": nrm(ks[21], (DEPTH, MOE_EXPERTS, MOE_D_FF, D_MODEL), MOE_D_FF ** -0.5),
        "norm_final": 1.0 + nrm(ks[22], (D_MODEL,), 0.02),
    }


def reference(x_prompt, x_sample, state_ssm, state_ssd_conv, state_short_conv, norm_mixer, w_in, ssd_conv_w,
              ssd_conv_b, ssd_dt_bias, ssd_a_log, ssd_d, ssd_norm, sc_conv_w, w_branch_out, w_out, norm_ffn,
              w_router_coarse, w_router_fine, w_expert_gate, w_expert_up, w_expert_down, norm_final):
    weights = (norm_mixer, w_in, ssd_conv_w, ssd_conv_b, ssd_dt_bias, ssd_a_log, ssd_d, ssd_norm, sc_conv_w,
               w_branch_out, w_out, norm_ffn, w_router_coarse, w_router_fine, w_expert_gate, w_expert_up,
               w_expert_down, norm_final)
    bp = x_prompt.shape[0]
    z_ssm = jnp.zeros((DEPTH, bp, SSD_HEADS, SSD_HEAD_DIM, SSD_STATE), state_ssm.dtype)
    z_conv = jnp.zeros((DEPTH, bp, SSD_CONV_W - 1, SSD_CONV_DIM), x_prompt.dtype)
    z_sc = jnp.zeros((DEPTH, bp, SC_W - 1, SC_DIM), x_prompt.dtype)
    y_prompt, p_ssm, p_conv, p_sc = trunk(x_prompt, z_ssm, z_conv, z_sc, weights)
    y_sample, s_ssm, s_conv, s_sc = trunk(x_sample, state_ssm, state_ssd_conv, state_short_conv, weights)
    return (y_prompt, y_sample, p_ssm, p_conv, p_sc, s_ssm, s_conv, s_sc)
```

```python
import functools
import math

import jax
import jax.numpy as jnp
from jax import lax
from jax.experimental import pallas as pl
from jax.experimental.pallas import tpu as pltpu

F32 = jnp.float32
BF16 = jnp.bfloat16

NORM_EPS = 1e-6
SSD_NORM_EPS = 1e-5
HEAD_DIM = 64
STATE_DIM = 128
HEADS_PER_GROUP = 8
GROUP_W = HEADS_PER_GROUP * HEAD_DIM
SSD_CHUNK = 128
MOE_TOP_K = 2
MOE_ROWS = 256

LANE = 128
SUBLANE = 8
VMEM_LIMIT = 56 * 1024 * 1024

NT_DIMS = (((1,), (1,)), ((), ()))
TN_DIMS = (((0,), (0,)), ((), ()))


def _tile(n, target, align):
    best = None
    for t in range(align, min(n, target) + 1, align):
        if n % t == 0:
            best = t
    assert best is not None, (n, target, align)
    return best


def _split2(v):
    hi = v.astype(BF16)
    lo = (v - hi.astype(F32)).astype(BF16)
    return hi, lo


def _split3(v):
    hi = v.astype(BF16)
    r = v - hi.astype(F32)
    mid = r.astype(BF16)
    lo = (r - mid.astype(F32)).astype(BF16)
    return hi, mid, lo


def _softplus(x):
    return jnp.maximum(x, 0.0) + jnp.log1p(jnp.exp(-jnp.abs(x)))


def _silu(x):
    return x * jax.nn.sigmoid(x)


def _inproj_body(x_ref, nw_ref, w_ref, wdt_ref, p_ref, dt_ref, h_scr):
    @pl.when(pl.program_id(1) == 0)
    def _():
        x = x_ref[...]
        h = x * lax.rsqrt(jnp.mean(x * x, axis=-1, keepdims=True) + NORM_EPS) * nw_ref[...]
        hb = h.astype(BF16)
        h_scr[...] = hb
        dt_ref[...] = jnp.dot(hb, wdt_ref[...], preferred_element_type=F32)

    p_ref[...] = jnp.dot(h_scr[...], w_ref[...], preferred_element_type=F32)


def _inproj(x, norm_w, w_main, w_dt):
    T, D = x.shape
    n_main = w_main.shape[1]
    tm = _tile(T, 1024, 16)
    tn = _tile(n_main, 1024, LANE)
    return pl.pallas_call(
        _inproj_body,
        grid=(T // tm, n_main // tn),
        in_specs=[
            pl.BlockSpec((tm, D), lambda i, j: (i, 0)),
            pl.BlockSpec((1, D), lambda i, j: (0, 0)),
            pl.BlockSpec((D, tn), lambda i, j: (0, j)),
            pl.BlockSpec((D, LANE), lambda i, j: (0, 0)),
        ],
        out_specs=[
            pl.BlockSpec((tm, tn), lambda i, j: (i, j)),
            pl.BlockSpec((tm, LANE), lambda i, j: (i, 0)),
        ],
        out_shape=[
            jax.ShapeDtypeStruct((T, n_main), F32),
            jax.ShapeDtypeStruct((T, LANE), F32),
        ],
        scratch_shapes=[pltpu.VMEM((tm, D), BF16)],
        compiler_params=pltpu.CompilerParams(
            dimension_semantics=("arbitrary", "arbitrary"), vmem_limit_bytes=VMEM_LIMIT),
        name="inproj",
    )(x, norm_w, w_main, w_dt)


def _conv(ext, lo, hi, q, width, w_ref, first):
    acc = None
    for k in range(width):
        term = w_ref[k:k + 1, lo:hi] * ext[first + k:first + k + q, lo:hi]
        acc = term if acc is None else acc + term
    return acc


def _gated_norm(y, z, nw):
    g = y * _silu(z)
    return g * lax.rsqrt(jnp.mean(g * g, axis=-1, keepdims=True) + SSD_NORM_EPS) * nw


def _decay_col(cs_last_row, g):
    d = jnp.exp(cs_last_row)
    parts = [
        jnp.broadcast_to(d[0:1, g * HEADS_PER_GROUP + j:g * HEADS_PER_GROUP + j + 1], (HEAD_DIM, STATE_DIM))
        for j in range(HEADS_PER_GROUP)
    ]
    return jnp.concatenate(parts, axis=0)


def _ssd_prompt_body(x4, x5, x6, z2, z3, scb, scc, sch, dtr,
                     cw, cb, dtb, alog, dskip, nw, scw, expand,
                     ya_ref, yb_ref, st_ref, sct_ref, ext, ext2):
    q, d = x4.shape
    n_groups = st_ref.shape[1]
    gn = n_groups * STATE_DIM
    kw = cw.shape[0]
    kw2 = scw.shape[0]

    @pl.when(pl.program_id(1) == 0)
    def _():
        st_ref[...] = jnp.zeros(st_ref.shape, F32)
        ext[0:SUBLANE, :] = jnp.zeros((SUBLANE, 3 * d), F32)
        ext2[0:SUBLANE, :] = jnp.zeros((SUBLANE, d), F32)

    ext[SUBLANE:SUBLANE + q, 0:d] = x4[...]
    ext[SUBLANE:SUBLANE + q, d:2 * d] = x5[...]
    ext[SUBLANE:SUBLANE + q, 2 * d:3 * d] = x6[...]
    first = SUBLANE - (kw - 1)

    ext2[SUBLANE:SUBLANE + q, :] = scc[...] * sch[...]
    v = _conv(ext2, 0, d, q, kw2, scw, SUBLANE - (kw2 - 1))
    yb_ref[...] = (scb[...] * v).astype(yb_ref.dtype)
    tail2 = ext2[q:q + SUBLANE, :]
    ext2[0:SUBLANE, :] = tail2
    sct_ref[0] = tail2

    dt = _softplus(dtr[...] + dtb[...])
    a = dt * (-jnp.exp(alog[...]))
    row = lax.broadcasted_iota(jnp.int32, (q, q), 0)
    col = lax.broadcasted_iota(jnp.int32, (q, q), 1)
    causal = row >= col
    tri = causal.astype(BF16)
    cs = sum(jnp.dot(tri, part, preferred_element_type=F32) for part in _split3(a))
    cs_t = cs.T
    cs_last = cs[q - 1:q, :]
    dend = jnp.exp(cs_last - cs)
    ecs = jnp.exp(cs)
    stacked = jnp.concatenate([dt, dend, ecs], axis=0)
    st_hi, st_lo = _split2(stacked)

    bmat = _silu(_conv(ext, 2 * d, 2 * d + gn, q, kw, cw, first) + cb[:, 2 * d:2 * d + gn])
    cmat = _silu(_conv(ext, 2 * d + gn, 3 * d, q, kw, cw, first) + cb[:, 2 * d + gn:3 * d])
    lane = lax.broadcasted_iota(jnp.int32, (q, LANE), 1)
    groups_per_block = d // GROUP_W

    for g in range(n_groups):
        c0 = g * GROUP_W
        xs = _silu(_conv(ext, c0, c0 + GROUP_W, q, kw, cw, first) + cb[:, c0:c0 + GROUP_W])
        e_g = expand[:, c0:c0 + GROUP_W]
        ex = (jnp.dot(st_hi, e_g, preferred_element_type=F32)
              + jnp.dot(st_lo, e_g, preferred_element_type=F32))
        xdt = xs * ex[0:q]
        xdt_b = xdt.astype(BF16)
        xdd_b = (xdt * ex[q:2 * q]).astype(BF16)
        bg = bmat[:, g * STATE_DIM:(g + 1) * STATE_DIM].astype(BF16)
        cg = cmat[:, g * STATE_DIM:(g + 1) * STATE_DIM].astype(BF16)
        cbm = lax.dot_general(cg, bg, NT_DIMS, preferred_element_type=F32)
        state = st_ref[0, g]
        y_off = lax.dot_general(cg, state.astype(BF16), NT_DIMS, preferred_element_type=F32)
        y_parts = []
        for j in range(HEADS_PER_GROUP // 2):
            scores = []
            for h in (g * HEADS_PER_GROUP + 2 * j, g * HEADS_PER_GROUP + 2 * j + 1):
                seg = cs[:, h:h + 1] - cs_t[h:h + 1, :]
                dec = jnp.exp(jnp.where(causal, seg, -jnp.inf))
                scores.append((cbm * dec).astype(BF16))
            xp = xdt_b[:, j * LANE:(j + 1) * LANE]
            zero = jnp.zeros_like(xp)
            rhs = jnp.concatenate(
                [jnp.where(lane < HEAD_DIM, xp, zero), jnp.where(lane >= HEAD_DIM, xp, zero)], axis=0)
            y_parts.append(jnp.dot(jnp.concatenate(scores, axis=1), rhs, preferred_element_type=F32))
        y = jnp.concatenate(y_parts, axis=1) + y_off * ex[2 * q:3 * q] + xs * dskip[:, c0:c0 + GROUP_W]
        zref = z2 if g < groups_per_block else z3
        zc = (g % groups_per_block) * GROUP_W
        ya_ref[:, c0:c0 + GROUP_W] = _gated_norm(
            y, zref[:, zc:zc + GROUP_W], nw[:, c0:c0 + GROUP_W]).astype(ya_ref.dtype)
        st_ref[0, g] = state * _decay_col(cs_last, g) + lax.dot_general(
            xdd_b, bg, TN_DIMS, preferred_element_type=F32)

    ext[0:SUBLANE, :] = ext[q:q + SUBLANE, :]


def _ssd_prompt(p, dt_raw, consts, n_seq, seq_len, d):
    (cw, cb, dtb, alog, dskip, nw, scw, expand) = consts
    q = SSD_CHUNK if seq_len % SSD_CHUNK == 0 else seq_len
    nc = seq_len // q
    n_groups = 2 * d // GROUP_W
    t_p = n_seq * seq_len

    def blk(cidx):
        return pl.BlockSpec((q, d), lambda b, c, cidx=cidx: (b * nc + c, cidx))

    def const(arr):
        return pl.BlockSpec(arr.shape, lambda b, c: (0,) * arr.ndim)

    return pl.pallas_call(
        _ssd_prompt_body,
        grid=(n_seq, nc),
        in_specs=[blk(4), blk(5), blk(6), blk(2), blk(3), blk(7), blk(8), blk(9),
                  pl.BlockSpec((q, LANE), lambda b, c: (b * nc + c, 0)),
                  const(cw), const(cb), const(dtb), const(alog), const(dskip), const(nw), const(scw),
                  const(expand)],
        out_specs=[
            pl.BlockSpec((q, 2 * d), lambda b, c: (b * nc + c, 0)),
            pl.BlockSpec((q, d), lambda b, c: (b * nc + c, 0)),
            pl.BlockSpec((1, n_groups, GROUP_W, STATE_DIM), lambda b, c: (b, 0, 0, 0)),
            pl.BlockSpec((1, SUBLANE, d), lambda b, c: (b, 0, 0)),
        ],
        out_shape=[
            jax.ShapeDtypeStruct((t_p, 2 * d), BF16),
            jax.ShapeDtypeStruct((t_p, d), BF16),
            jax.ShapeDtypeStruct((n_seq, n_groups, GROUP_W, STATE_DIM), F32),
            jax.ShapeDtypeStruct((n_seq, SUBLANE, d), F32),
        ],
        scratch_shapes=[pltpu.VMEM((q + SUBLANE, 3 * d), F32), pltpu.VMEM((q + SUBLANE, d), F32)],
        compiler_params=pltpu.CompilerParams(
            dimension_semantics=("arbitrary", "arbitrary"), vmem_limit_bytes=VMEM_LIMIT),
        name="ssd_prompt",
    )(p, p, p, p, p, p, p, p, dt_raw, cw, cb, dtb, alog, dskip, nw, scw, expand)


SAMPLE_SEQS_PER_STEP = 2


def _ssd_sample_body(x4, x5, x6, z2, z3, scb, scc, sch, dtr, ssm_in, conv_in, sc_in,
                     cw, cb, dtb, alog, dskip, nw, scw, expand, gsel,
                     ya_ref, yb_ref, ssm_out, conv_out, sc_out, ext, ext2):
    d = x4.shape[1]
    n_groups = ssm_in.shape[1]
    gn = n_groups * STATE_DIM
    kw = cw.shape[0]
    kw2 = scw.shape[0]
    q = x4.shape[0] // SAMPLE_SEQS_PER_STEP
    groups_per_block = d // GROUP_W
    first = SUBLANE - (kw - 1)
    first2 = SUBLANE - (kw2 - 1)
    nrep = q * q

    rep_t = lax.broadcasted_iota(jnp.int32, (nrep, LANE), 0) % q
    rep_s = lax.broadcasted_iota(jnp.int32, (nrep, LANE), 0) // q
    rep_causal = rep_t >= rep_s
    row_q = lax.broadcasted_iota(jnp.int32, (q, LANE), 0)

    def rep_rows(m):
        return jnp.concatenate([jnp.broadcast_to(m[s:s + 1], (q, m.shape[1])) for s in range(q)], axis=0)

    def tile_rows(m):
        return jnp.concatenate([m] * q, axis=0)

    yb_rows = []
    ya_rows = [[] for _ in range(n_groups)]
    for sidx in range(SAMPLE_SEQS_PER_STEP):
        r0 = sidx * q
        ext[first:SUBLANE, :] = conv_in[sidx]
        ext[SUBLANE:SUBLANE + q, 0:d] = x4[r0:r0 + q, :]
        ext[SUBLANE:SUBLANE + q, d:2 * d] = x5[r0:r0 + q, :]
        ext[SUBLANE:SUBLANE + q, 2 * d:3 * d] = x6[r0:r0 + q, :]
        conv_out[sidx] = ext[SUBLANE + q - (kw - 1):SUBLANE + q, :]

        ext2[first2:SUBLANE, :] = sc_in[sidx]
        ext2[SUBLANE:SUBLANE + q, :] = scc[r0:r0 + q, :] * sch[r0:r0 + q, :]
        yb_rows.append(scb[r0:r0 + q, :] * _conv(ext2, 0, d, q, kw2, scw, first2))
        sc_out[sidx] = ext2[SUBLANE + q - (kw2 - 1):SUBLANE + q, :]

        dt = _softplus(dtr[r0:r0 + q, :] + dtb[...])
        a = dt * (-jnp.exp(alog[...]))
        cs = jnp.zeros((q, LANE), F32)
        for r in range(q):
            cs = cs + jnp.where(row_q >= r, jnp.broadcast_to(a[r:r + 1], (q, LANE)), 0.0)
        cs_last = cs[q - 1:q, :]
        dend = jnp.exp(cs_last - cs)
        ecs = jnp.exp(cs)

        bmat = _silu(_conv(ext, 2 * d, 2 * d + gn, q, kw, cw, first) + cb[:, 2 * d:2 * d + gn])
        cmat = _silu(_conv(ext, 2 * d + gn, 3 * d, q, kw, cw, first) + cb[:, 2 * d + gn:3 * d])

        cb_hi, cb_lo = _split2(tile_rows(cmat) * rep_rows(bmat))
        cbh = (jnp.dot(cb_hi, gsel[...], preferred_element_type=F32)
               + jnp.dot(cb_lo, gsel[...], preferred_element_type=F32))
        dec = jnp.exp(jnp.where(rep_causal, tile_rows(cs) - rep_rows(cs), -jnp.inf))
        stacked = jnp.concatenate([dt, dend, ecs], axis=0)
        st_hi = stacked.astype(BF16).astype(F32)
        pad = jnp.zeros((LANE - nrep - 6 * q, LANE), F32)
        lhs = jnp.concatenate([cbh * dec, st_hi, stacked - st_hi, pad], axis=0).astype(BF16)

        for g in range(n_groups):
            c0 = g * GROUP_W
            xs = _silu(_conv(ext, c0, c0 + GROUP_W, q, kw, cw, first) + cb[:, c0:c0 + GROUP_W])
            ex = jnp.dot(lhs, expand[:, c0:c0 + GROUP_W], preferred_element_type=F32)
            o = nrep
            dtx = ex[o:o + q] + ex[o + 3 * q:o + 4 * q]
            dendx = ex[o + q:o + 2 * q] + ex[o + 4 * q:o + 5 * q]
            ecsx = ex[o + 2 * q:o + 3 * q] + ex[o + 5 * q:o + 6 * q]
            xdt = xs * dtx
            xdd = xdt * dendx
            y = xs * dskip[:, c0:c0 + GROUP_W]
            for s in range(q):
                y = y + ex[s * q:(s + 1) * q] * jnp.broadcast_to(xdt[s:s + 1], (q, GROUP_W))
            state = ssm_in[sidx, g]
            cg = cmat[:, g * STATE_DIM:(g + 1) * STATE_DIM]
            bg = bmat[:, g * STATE_DIM:(g + 1) * STATE_DIM]
            y = y + lax.dot_general(cg, state, NT_DIMS, preferred_element_type=F32) * ecsx
            zref = z2 if g < groups_per_block else z3
            zc = (g % groups_per_block) * GROUP_W
            ya_rows[g].append(_gated_norm(y, zref[r0:r0 + q, zc:zc + GROUP_W], nw[:, c0:c0 + GROUP_W]))
            ssm_out[sidx, g] = state * _decay_col(cs_last, g) + lax.dot_general(
                xdd, bg, TN_DIMS, preferred_element_type=F32)

    yb_ref[...] = jnp.concatenate(yb_rows, axis=0).astype(yb_ref.dtype)
    for g in range(n_groups):
        ya_ref[:, g * GROUP_W:(g + 1) * GROUP_W] = jnp.concatenate(ya_rows[g], axis=0).astype(ya_ref.dtype)


def _ssd_sample(p, dt_raw, ssm, conv_state, sc_state, consts, gsel, row0, seq_len, d):
    (cw, cb, dtb, alog, dskip, nw, scw, expand) = consts
    n_seq = ssm.shape[0]
    sp = SAMPLE_SEQS_PER_STEP
    rows = sp * seq_len
    assert seq_len == SUBLANE and n_seq % sp == 0 and row0 % rows == 0
    b0 = row0 // rows

    def blk(cidx):
        return pl.BlockSpec((rows, d), lambda i, cidx=cidx: (b0 + i, cidx))

    def const(arr):
        return pl.BlockSpec(arr.shape, lambda i: (0,) * arr.ndim)

    def per_seq(arr):
        return pl.BlockSpec((sp,) + arr.shape[1:], lambda i: (i,) + (0,) * (arr.ndim - 1))

    return pl.pallas_call(
        _ssd_sample_body,
        grid=(n_seq // sp,),
        in_specs=[blk(4), blk(5), blk(6), blk(2), blk(3), blk(7), blk(8), blk(9),
                  pl.BlockSpec((rows, LANE), lambda i: (b0 + i, 0)),
                  per_seq(ssm), per_seq(conv_state), per_seq(sc_state),
                  const(cw), const(cb), const(dtb), const(alog), const(dskip), const(nw), const(scw),
                  const(expand), const(gsel)],
        out_specs=[
            pl.BlockSpec((rows, 2 * d), lambda i: (i, 0)),
            pl.BlockSpec((rows, d), lambda i: (i, 0)),
            per_seq(ssm), per_seq(conv_state), per_seq(sc_state),
        ],
        out_shape=[
            jax.ShapeDtypeStruct((n_seq * seq_len, 2 * d), BF16),
            jax.ShapeDtypeStruct((n_seq * seq_len, d), BF16),
            jax.ShapeDtypeStruct(ssm.shape, F32),
            jax.ShapeDtypeStruct(conv_state.shape, F32),
            jax.ShapeDtypeStruct(sc_state.shape, F32),
        ],
        scratch_shapes=[pltpu.VMEM((2 * SUBLANE, 3 * d), F32), pltpu.VMEM((2 * SUBLANE, d), F32)],
        compiler_params=pltpu.CompilerParams(
            dimension_semantics=("arbitrary",), vmem_limit_bytes=VMEM_LIMIT),
        name="ssd_sample",
    )(p, p, p, p, p, p, p, p, dt_raw, ssm, conv_state, sc_state,
      cw, cb, dtb, alog, dskip, nw, scw, expand, gsel)


def _branch_out_body(yap_ref, ybp_ref, yas_ref, ybs_ref, ga_ref, gb_ref, wa_ref, wb_ref, o_ref, *, prompt_tiles):
    def run(ya_ref, yb_ref):
        pa = jnp.dot(ya_ref[...], wa_ref[...], preferred_element_type=F32)
        pb = jnp.dot(yb_ref[...], wb_ref[...], preferred_element_type=F32)
        merged = jax.nn.sigmoid(ga_ref[...]) * pa + jax.nn.sigmoid(gb_ref[...]) * pb
        o_ref[...] = merged.astype(o_ref.dtype)

    @pl.when(pl.program_id(0) < prompt_tiles)
    def _():
        run(yap_ref, ybp_ref)

    @pl.when(pl.program_id(0) >= prompt_tiles)
    def _():
        run(yas_ref, ybs_ref)


def _branch_out(ya_p, yb_p, ya_s, yb_s, p, wa, wb):
    t_p, d = yb_p.shape
    t_s = yb_s.shape[0]
    tm = _tile(math.gcd(t_p, t_s), 512, 16)
    tn = _tile(d, 512, LANE)
    nj = d // tn
    npt = t_p // tm
    prompt_row = lambda i, j: (jnp.minimum(i, npt - 1), 0)
    sample_row = lambda i, j: (jnp.maximum(i - npt, 0), 0)
    return pl.pallas_call(
        functools.partial(_branch_out_body, prompt_tiles=npt),
        grid=((t_p + t_s) // tm, nj),
        in_specs=[
            pl.BlockSpec((tm, 2 * d), prompt_row),
            pl.BlockSpec((tm, d), prompt_row),
            pl.BlockSpec((tm, 2 * d), sample_row),
            pl.BlockSpec((tm, d), sample_row),
            pl.BlockSpec((tm, tn), lambda i, j: (i, j)),
            pl.BlockSpec((tm, tn), lambda i, j: (i, nj + j)),
            pl.BlockSpec((2 * d, tn), lambda i, j: (0, j)),
            pl.BlockSpec((d, tn), lambda i, j: (0, j)),
        ],
        out_specs=pl.BlockSpec((tm, tn), lambda i, j: (i, j)),
        out_shape=jax.ShapeDtypeStruct((t_p + t_s, d), BF16),
        compiler_params=pltpu.CompilerParams(
            dimension_semantics=("arbitrary", "arbitrary"), vmem_limit_bytes=VMEM_LIMIT),
        name="branch_out",
    )(ya_p, yb_p, ya_s, yb_s, p, p, wa, wb)


def _mix_route_body(m_ref, x_ref, wo_ref, nw_ref, rhi_ref, rlo_ref, x1_ref, h2_ref, eid_ref, wts_ref,
                    *, n_coarse, per_group):
    x1 = x_ref[...] + jnp.dot(m_ref[...], wo_ref[...], preferred_element_type=F32)
    x1_ref[...] = x1
    h2 = x1 * lax.rsqrt(jnp.mean(x1 * x1, axis=-1, keepdims=True) + NORM_EPS) * nw_ref[...]
    h2_ref[...] = h2
    h_hi, h_lo = _split2(h2)
    logits = (jnp.dot(h_hi, rhi_ref[...], preferred_element_type=F32)
              + jnp.dot(h_hi, rlo_ref[...], preferred_element_type=F32)
              + jnp.dot(h_lo, rhi_ref[...], preferred_element_type=F32))

    n_fine = n_coarse * per_group
    lane = lax.broadcasted_iota(jnp.int32, logits.shape, 1)
    big = jnp.int32(LANE)
    neg = -jnp.inf
    is_c = lane < n_coarse
    lc = jnp.where(is_c, logits, neg)
    mc = jnp.max(lc, axis=-1, keepdims=True)
    grp = jnp.min(jnp.where(is_c & (lc == mc), lane, big), axis=-1, keepdims=True)
    p_grp = 1.0 / jnp.sum(jnp.where(is_c, jnp.exp(lc - mc), 0.0), axis=-1, keepdims=True)
    eidx = lane - n_coarse
    sel = (eidx >= 0) & (eidx < n_fine) & ((eidx // per_group) == grp)
    lf = jnp.where(sel, logits, neg)
    v1 = jnp.max(lf, axis=-1, keepdims=True)
    i1 = jnp.min(jnp.where(sel & (lf == v1), eidx, big), axis=-1, keepdims=True)
    sel2 = sel & (eidx != i1)
    lf2 = jnp.where(sel2, logits, neg)
    v2 = jnp.max(lf2, axis=-1, keepdims=True)
    i2 = jnp.min(jnp.where(sel2 & (lf2 == v2), eidx, big), axis=-1, keepdims=True)
    e2 = jnp.exp(v2 - v1)
    w1 = p_grp / (1.0 + e2)
    w2 = p_grp * e2 / (1.0 + e2)
    eid_ref[...] = jnp.where(lane == 0, i1, jnp.where(lane == 1, i2, 0))
    wts_ref[...] = jnp.where(lane == 0, w1, jnp.where(lane == 1, w2, 0.0))


def _mix_route(merged, x, wo, norm_w, r_hi, r_lo, n_coarse, per_group):
    T, d = x.shape
    tm = _tile(T, 256, 16)
    row = lambda i: (i, 0)
    fixed = lambda i: (0, 0)
    return pl.pallas_call(
        functools.partial(_mix_route_body, n_coarse=n_coarse, per_group=per_group),
        grid=(T // tm,),
        in_specs=[
            pl.BlockSpec((tm, d), row), pl.BlockSpec((tm, d), row), pl.BlockSpec((d, d), fixed),
            pl.BlockSpec((1, d), fixed), pl.BlockSpec((d, LANE), fixed), pl.BlockSpec((d, LANE), fixed),
        ],
        out_specs=[pl.BlockSpec((tm, d), row), pl.BlockSpec((tm, d), row),
                   pl.BlockSpec((tm, LANE), row), pl.BlockSpec((tm, LANE), row)],
        out_shape=[
            jax.ShapeDtypeStruct((T, d), F32), jax.ShapeDtypeStruct((T, d), F32),
            jax.ShapeDtypeStruct((T, LANE), jnp.int32), jax.ShapeDtypeStruct((T, LANE), F32),
        ],
        compiler_params=pltpu.CompilerParams(
            dimension_semantics=("arbitrary",), vmem_limit_bytes=VMEM_LIMIT),
        name="mix_route",
    )(merged, x, wo, norm_w, r_hi, r_lo)


MOE_SCATTER_TOKENS = 128


def _moe_scatter_body(dest_ref, h_hbm, xs_in, xs_hbm, sem):
    del xs_in
    t0 = pl.program_id(0) * MOE_SCATTER_TOKENS

    def copy(i, k):
        return pltpu.make_async_copy(
            h_hbm.at[pl.ds(t0 + i, 1)], xs_hbm.at[pl.ds(dest_ref[(t0 + i) * MOE_TOP_K + k], 1)], sem)

    def start(i, carry):
        for k in range(MOE_TOP_K):
            copy(i, k).start()
        return carry

    def wait(i, carry):
        for k in range(MOE_TOP_K):
            copy(i, k).wait()
        return carry

    lax.fori_loop(0, MOE_SCATTER_TOKENS, start, 0)
    lax.fori_loop(0, MOE_SCATTER_TOKENS, wait, 0)


def _moe_scatter(dest, h2, xs_zero):
    T, d = h2.shape
    assert T % MOE_SCATTER_TOKENS == 0
    any_spec = pl.BlockSpec(memory_space=pl.ANY)
    return pl.pallas_call(
        _moe_scatter_body,
        grid_spec=pltpu.PrefetchScalarGridSpec(
            num_scalar_prefetch=1,
            grid=(T // MOE_SCATTER_TOKENS,),
            in_specs=[any_spec, any_spec],
            out_specs=any_spec,
            scratch_shapes=[pltpu.SemaphoreType.DMA(())],
        ),
        out_shape=jax.ShapeDtypeStruct(xs_zero.shape, xs_zero.dtype),
        input_output_aliases={2: 0},
        compiler_params=pltpu.CompilerParams(dimension_semantics=("arbitrary",)),
        name="moe_scatter",
    )(dest, h2, xs_zero)


def _moe_ffn_body(blk_e_ref, n_used_ref, x_ref, wg_ref, wu_ref, wd_ref, o_ref):
    del blk_e_ref

    @pl.when(pl.program_id(0) < n_used_ref[0])
    def _():
        xb = x_ref[...].astype(BF16)
        gate = jnp.dot(xb, wg_ref[0], preferred_element_type=F32)
        up = jnp.dot(xb, wu_ref[0], preferred_element_type=F32)
        act = (_silu(gate) * up).astype(BF16)
        o_ref[...] = jnp.dot(act, wd_ref[0], preferred_element_type=F32)

    @pl.when(pl.program_id(0) >= n_used_ref[0])
    def _():
        o_ref[...] = jnp.zeros(o_ref.shape, o_ref.dtype)


def _moe_ffn(blk_e, n_used, xs, wg, wu, wd):
    R, d = xs.shape
    f = wg.shape[2]
    nb = R // MOE_ROWS
    return pl.pallas_call(
        _moe_ffn_body,
        grid_spec=pltpu.PrefetchScalarGridSpec(
            num_scalar_prefetch=2,
            grid=(nb,),
            in_specs=[
                pl.BlockSpec((MOE_ROWS, d), lambda b, be, nu: (b, 0)),
                pl.BlockSpec((1, d, f), lambda b, be, nu: (be[b], 0, 0)),
                pl.BlockSpec((1, d, f), lambda b, be, nu: (be[b], 0, 0)),
                pl.BlockSpec((1, f, d), lambda b, be, nu: (be[b], 0, 0)),
            ],
            out_specs=pl.BlockSpec((MOE_ROWS, d), lambda b, be, nu: (b, 0)),
        ),
        out_shape=jax.ShapeDtypeStruct((R, d), F32),
        compiler_params=pltpu.CompilerParams(
            dimension_semantics=("arbitrary",), vmem_limit_bytes=VMEM_LIMIT),
        name="moe_ffn",
    )(blk_e, n_used, xs, wg, wu, wd)


def _moe_combine_body(dest_ref, x1_ref, wts_ref, nw_ref, yb_hbm, o_ref, buf, sem):
    t0 = pl.program_id(0) * MOE_SCATTER_TOKENS

    def copy(i, k):
        return pltpu.make_async_copy(
            yb_hbm.at[pl.ds(dest_ref[(t0 + i) * MOE_TOP_K + k], 1)], buf.at[k, pl.ds(i, 1)], sem)

    def start(i, carry):
        for k in range(MOE_TOP_K):
            copy(i, k).start()
        return carry

    def wait(i, carry):
        for k in range(MOE_TOP_K):
            copy(i, k).wait()
        return carry

    lax.fori_loop(0, MOE_SCATTER_TOKENS, start, 0)
    lax.fori_loop(0, MOE_SCATTER_TOKENS, wait, 0)
    w = wts_ref[...]
    y = buf[0] * w[:, 0:1] + buf[1] * w[:, 1:2]
    x2 = x1_ref[...] + y
    o_ref[...] = x2 * lax.rsqrt(jnp.mean(x2 * x2, axis=-1, keepdims=True) + NORM_EPS) * nw_ref[...]


def _moe_combine(dest, x1, wts, norm_w, yb):
    T, d = x1.shape
    tt = MOE_SCATTER_TOKENS
    return pl.pallas_call(
        _moe_combine_body,
        grid_spec=pltpu.PrefetchScalarGridSpec(
            num_scalar_prefetch=1,
            grid=(T // tt,),
            in_specs=[
                pl.BlockSpec((tt, d), lambda i, dr: (i, 0)),
                pl.BlockSpec((tt, LANE), lambda i, dr: (i, 0)),
                pl.BlockSpec((1, d), lambda i, dr: (0, 0)),
                pl.BlockSpec(memory_space=pl.ANY),
            ],
            out_specs=pl.BlockSpec((tt, d), lambda i, dr: (i, 0)),
            scratch_shapes=[pltpu.VMEM((MOE_TOP_K, tt, d), F32), pltpu.SemaphoreType.DMA(())],
        ),
        out_shape=jax.ShapeDtypeStruct((T, d), F32),
        compiler_params=pltpu.CompilerParams(dimension_semantics=("arbitrary",)),
        name="moe_combine",
    )(dest, x1, wts, norm_w, yb)


def _route_rows(eid, n_experts):
    T = eid.shape[0]
    n_assign = T * MOE_TOP_K
    e_flat = eid.reshape(n_assign)
    onehot = (e_flat[:, None] == jnp.arange(n_experts, dtype=jnp.int32)[None, :]).astype(jnp.int32)
    running = jnp.cumsum(onehot, axis=0)
    counts = running[-1]
    rank = jnp.sum(running * onehot, axis=1) - 1
    padded = (counts + MOE_ROWS - 1) // MOE_ROWS * MOE_ROWS
    pend = jnp.cumsum(padded)
    pstart = pend - padded
    dest = (jnp.sum(onehot * pstart[None, :], axis=1) + rank).astype(jnp.int32)
    n_blocks = -(-n_assign // MOE_ROWS) + n_experts
    blk_start = jnp.arange(n_blocks, dtype=jnp.int32) * MOE_ROWS
    blk_e = jnp.sum((blk_start[:, None] >= pend[None, :]).astype(jnp.int32), axis=1)
    n_used = (pend[-1] // MOE_ROWS).astype(jnp.int32).reshape(1)
    last_e = jnp.sum((jnp.maximum(pend[-1] - 1, 0) >= pend).astype(jnp.int32))
    blk_e = jnp.where(blk_start < pend[-1], blk_e, last_e).astype(jnp.int32)
    return dest, blk_e, n_used, n_blocks


def _pad_lanes(v, fill=0.0):
    return jnp.pad(v.astype(F32), (0, LANE - v.shape[0]), constant_values=fill).reshape(1, LANE)


def kernel(x_prompt, x_sample, state_ssm, state_ssd_conv, state_short_conv, norm_mixer, w_in, ssd_conv_w,
           ssd_conv_b, ssd_dt_bias, ssd_a_log, ssd_d, ssd_norm, sc_conv_w, w_branch_out, w_out, norm_ffn,
           w_router_coarse, w_router_fine, w_expert_gate, w_expert_up, w_expert_down, norm_final):
    depth = w_in.shape[0]
    assert depth == 1
    n_p, seq_p, d = x_prompt.shape
    n_s, seq_s, _ = x_sample.shape
    d_inner = 2 * d
    n_heads = d_inner // HEAD_DIM
    n_groups = d_inner // GROUP_W
    gn = n_groups * STATE_DIM
    conv_dim = d_inner + 2 * gn
    assert conv_dim == 3 * d and n_heads <= LANE and ssd_conv_w.shape[2] == conv_dim
    t_p, t_s = n_p * seq_p, n_s * seq_s
    n_coarse = w_router_coarse.shape[2]
    n_experts = w_router_fine.shape[2]
    assert n_coarse + n_experts <= LANE

    off_dt = 2 * d + d_inner + conv_dim
    off_sc = off_dt + n_heads
    w_in0 = w_in[0]
    w_main = jnp.concatenate([w_in0[:, :off_dt], w_in0[:, off_sc:]], axis=1).astype(BF16)
    w_dt = jnp.pad(w_in0[:, off_dt:off_sc], ((0, 0), (0, LANE - n_heads))).astype(BF16)
    head_of_col = jnp.arange(d_inner, dtype=jnp.int32) // HEAD_DIM
    expand = (jnp.arange(LANE, dtype=jnp.int32)[:, None] == head_of_col[None, :]).astype(BF16)
    group_of_n = jnp.arange(gn, dtype=jnp.int32) // STATE_DIM
    group_of_head = jnp.arange(LANE, dtype=jnp.int32) // HEADS_PER_GROUP
    gsel = ((group_of_n[:, None] == group_of_head[None, :])
            & (jnp.arange(LANE)[None, :] < n_heads)).astype(BF16)
    consts = (
        ssd_conv_w[0], ssd_conv_b[0].reshape(1, conv_dim), _pad_lanes(ssd_dt_bias[0]), _pad_lanes(ssd_a_log[0]),
        jnp.repeat(ssd_d[0].astype(F32), HEAD_DIM).reshape(1, d_inner), ssd_norm[0].reshape(1, d_inner),
        sc_conv_w[0], expand,
    )
    wbo = w_branch_out[0].astype(BF16)
    wa, wb = wbo[:d_inner], wbo[d_inner:]
    wo = w_out[0].astype(BF16)
    w_router = jnp.pad(jnp.concatenate([w_router_coarse[0], w_router_fine[0]], axis=1),
                       ((0, 0), (0, LANE - n_coarse - n_experts)))
    r_hi = w_router.astype(BF16)
    r_lo = (w_router - r_hi.astype(F32)).astype(BF16)
    wg = w_expert_gate[0].astype(BF16)
    wu = w_expert_up[0].astype(BF16)
    wd = w_expert_down[0].astype(BF16)

    x = jnp.concatenate([x_prompt.reshape(t_p, d), x_sample.reshape(t_s, d)], axis=0)
    p, dt_raw = _inproj(x, norm_mixer[0].reshape(1, d), w_main, w_dt)
    ya_p, yb_p, p_ssm, p_sc_tail = _ssd_prompt(p, dt_raw, consts, n_p, seq_p, d)
    ya_s, yb_s, s_ssm, s_conv, s_sc = _ssd_sample(
        p, dt_raw, state_ssm[0].reshape(n_s, n_groups, GROUP_W, STATE_DIM), state_ssd_conv[0],
        state_short_conv[0], consts, gsel, t_p, seq_s, d)
    merged = _branch_out(ya_p, yb_p, ya_s, yb_s, p, wa, wb)
    x1, h2, eid, wts = _mix_route(merged, x, wo, norm_ffn[0].reshape(1, d), r_hi, r_lo,
                                  n_coarse, n_experts // n_coarse)

    dest, blk_e, n_used, n_blocks = _route_rows(eid[:, :MOE_TOP_K], n_experts)
    xs = _moe_scatter(dest, h2, jnp.zeros((n_blocks * MOE_ROWS, d), F32))
    yrows = _moe_ffn(blk_e, n_used, xs, wg, wu, wd)
    out = _moe_combine(dest, x1, wts, norm_final.reshape(1, d), yrows)

    kw = ssd_conv_w.shape[1]
    kw2 = sc_conv_w.shape[1]
    p_conv = p[:t_p, 4 * d:7 * d].reshape(n_p, seq_p, conv_dim)[:, seq_p - (kw - 1):, :]
    return (
        out[:t_p].reshape(n_p, seq_p, d),
        out[t_p:].reshape(n_s, seq_s, d),
        p_ssm.reshape(1, n_p, n_heads, HEAD_DIM, STATE_DIM),
        p_conv[None],
        p_sc_tail[:, SUBLANE - (kw2 - 1):, :][None],
        s_ssm.reshape(1, n_s, n_heads, HEAD_DIM, STATE_DIM),
        s_conv[None],
        s_sc[None],
    )
```

```python
import functools
import math

import jax
import jax.numpy as jnp
from jax import lax
from jax.experimental import pallas as pl
from jax.experimental.pallas import tpu as pltpu

F32 = jnp.float32
BF16 = jnp.bfloat16

NORM_EPS = 1e-6
SSD_NORM_EPS = 1e-5
HEAD_DIM = 64
STATE_DIM = 128
HEADS_PER_GROUP = 8
GROUP_W = HEADS_PER_GROUP * HEAD_DIM
SSD_CHUNK = 128
MOE_TOP_K = 2
MOE_ROWS = 256

LANE = 128
SUBLANE = 8
VMEM_LIMIT = 56 * 1024 * 1024

NT_DIMS = (((1,), (1,)), ((), ()))
TN_DIMS = (((0,), (0,)), ((), ()))


def _tile(n, target, align):
    best = None
    for t in range(align, min(n, target) + 1, align):
        if n % t == 0:
            best = t
    assert best is not None, (n, target, align)
    return best


def _split2(v):
    hi = v.astype(BF16)
    lo = (v - hi.astype(F32)).astype(BF16)
    return hi, lo


def _split3(v):
    hi = v.astype(BF16)
    r = v - hi.astype(F32)
    mid = r.astype(BF16)
    lo = (r - mid.astype(F32)).astype(BF16)
    return hi, mid, lo


def _softplus(x):
    return jnp.maximum(x, 0.0) + jnp.log1p(jnp.exp(-jnp.abs(x)))


def _silu(x):
    return x * jax.nn.sigmoid(x)


def _by_stream(tile, prompt_tiles, fn, prompt_ref, sample_ref):
    @pl.when(tile < prompt_tiles)
    def _():
        fn(prompt_ref)

    @pl.when(tile >= prompt_tiles)
    def _():
        fn(sample_ref)


def _stream_specs(block, prompt_tiles, **kwargs):
    pad = (0,) * (len(block) - 1)
    prompt = lambda i, *_: (jnp.minimum(i, prompt_tiles - 1),) + pad
    sample = lambda i, *_: (jnp.maximum(i - prompt_tiles, 0),) + pad
    return pl.BlockSpec(block, prompt, **kwargs), pl.BlockSpec(block, sample, **kwargs)


def _inproj_body(xp_ref, xs_ref, nw_ref, w_ref, wdt_ref, p_ref, dt_ref, h_scr, *, prompt_tiles):
    @pl.when(pl.program_id(1) == 0)
    def _():
        def norm(x_ref):
            x = x_ref[...]
            h = x * lax.rsqrt(jnp.mean(x * x, axis=-1, keepdims=True) + NORM_EPS) * nw_ref[...]
            hb = h.astype(BF16)
            h_scr[...] = hb
            dt_ref[...] = jnp.dot(hb, wdt_ref[...], preferred_element_type=F32)

        _by_stream(pl.program_id(0), prompt_tiles, norm, xp_ref, xs_ref)

    p_ref[...] = jnp.dot(h_scr[...], w_ref[...], preferred_element_type=F32)


def _inproj(x_p, x_s, norm_w, w_main, w_dt):
    t_p, D = x_p.shape
    T = t_p + x_s.shape[0]
    n_main = w_main.shape[1]
    tm = _tile(math.gcd(t_p, x_s.shape[0]), 1024, 16)
    tn = _tile(n_main, 1024, LANE)
    npt = t_p // tm
    xp_spec, xs_spec = _stream_specs((tm, D), npt, pipeline_mode=pl.Buffered(1))
    return pl.pallas_call(
        functools.partial(_inproj_body, prompt_tiles=npt),
        grid=(T // tm, n_main // tn),
        in_specs=[
            xp_spec, xs_spec,
            pl.BlockSpec((1, D), lambda i, j: (0, 0)),
            pl.BlockSpec((D, tn), lambda i, j: (0, j)),
            pl.BlockSpec((D, LANE), lambda i, j: (0, 0)),
        ],
        out_specs=[
            pl.BlockSpec((tm, tn), lambda i, j: (i, j)),
            pl.BlockSpec((tm, LANE), lambda i, j: (i, 0)),
        ],
        out_shape=[
            jax.ShapeDtypeStruct((T, n_main), F32),
            jax.ShapeDtypeStruct((T, LANE), F32),
        ],
        scratch_shapes=[pltpu.VMEM((tm, D), BF16)],
        compiler_params=pltpu.CompilerParams(
            dimension_semantics=("arbitrary", "arbitrary"), vmem_limit_bytes=VMEM_LIMIT),
        name="inproj",
    )(x_p, x_s, norm_w, w_main, w_dt)


def _conv(ext, lo, hi, q, width, w_ref, first):
    acc = None
    for k in range(width):
        term = w_ref[k:k + 1, lo:hi] * ext[first + k:first + k + q, lo:hi]
        acc = term if acc is None else acc + term
    return acc


def _gated_norm(y, z, nw):
    g = y * _silu(z)
    return g * lax.rsqrt(jnp.mean(g * g, axis=-1, keepdims=True) + SSD_NORM_EPS) * nw


def _decay_col(cs_last_row, g):
    d = jnp.exp(cs_last_row)
    parts = [
        jnp.broadcast_to(d[0:1, g * HEADS_PER_GROUP + j:g * HEADS_PER_GROUP + j + 1], (HEAD_DIM, STATE_DIM))
        for j in range(HEADS_PER_GROUP)
    ]
    return jnp.concatenate(parts, axis=0)


def _ssd_prompt_body(x4, x5, x6, z2, z3, scb, scc, sch, dtr,
                     cw, cb, dtb, alog, dskip, nw, scw, expand,
                     ya_ref, yb_ref, st_ref, cvt_ref, sct_ref, ext, ext2):
    q, d = x4.shape
    n_groups = st_ref.shape[1]
    gn = n_groups * STATE_DIM
    kw = cw.shape[0]
    kw2 = scw.shape[0]

    @pl.when(pl.program_id(1) == 0)
    def _():
        st_ref[...] = jnp.zeros(st_ref.shape, F32)
        ext[0:SUBLANE, :] = jnp.zeros((SUBLANE, 3 * d), F32)
        ext2[0:SUBLANE, :] = jnp.zeros((SUBLANE, d), F32)

    ext[SUBLANE:SUBLANE + q, 0:d] = x4[...]
    ext[SUBLANE:SUBLANE + q, d:2 * d] = x5[...]
    ext[SUBLANE:SUBLANE + q, 2 * d:3 * d] = x6[...]
    first = SUBLANE - (kw - 1)

    ext2[SUBLANE:SUBLANE + q, :] = scc[...] * sch[...]
    v = _conv(ext2, 0, d, q, kw2, scw, SUBLANE - (kw2 - 1))
    yb_ref[...] = (scb[...] * v).astype(yb_ref.dtype)
    tail2 = ext2[q:q + SUBLANE, :]
    ext2[0:SUBLANE, :] = tail2
    sct_ref[0] = tail2

    dt = _softplus(dtr[...] + dtb[...])
    a = dt * (-jnp.exp(alog[...]))
    row = lax.broadcasted_iota(jnp.int32, (q, q), 0)
    col = lax.broadcasted_iota(jnp.int32, (q, q), 1)
    causal = row >= col
    tri = causal.astype(BF16)
    cs = sum(jnp.dot(tri, part, preferred_element_type=F32) for part in _split3(a))
    cs_t = cs.T
    cs_last = cs[q - 1:q, :]
    dend = jnp.exp(cs_last - cs)
    ecs = jnp.exp(cs)
    stacked = jnp.concatenate([dt, dend, ecs], axis=0)
    st_hi, st_lo = _split2(stacked)

    bmat = _silu(_conv(ext, 2 * d, 2 * d + gn, q, kw, cw, first) + cb[:, 2 * d:2 * d + gn])
    cmat = _silu(_conv(ext, 2 * d + gn, 3 * d, q, kw, cw, first) + cb[:, 2 * d + gn:3 * d])
    lane = lax.broadcasted_iota(jnp.int32, (q, LANE), 1)
    groups_per_block = d // GROUP_W

    for g in range(n_groups):
        c0 = g * GROUP_W
        xs = _silu(_conv(ext, c0, c0 + GROUP_W, q, kw, cw, first) + cb[:, c0:c0 + GROUP_W])
        e_g = expand[:, c0:c0 + GROUP_W]
        ex = (jnp.dot(st_hi, e_g, preferred_element_type=F32)
              + jnp.dot(st_lo, e_g, preferred_element_type=F32))
        xdt = xs * ex[0:q]
        xdt_b = xdt.astype(BF16)
        xdd_b = (xdt * ex[q:2 * q]).astype(BF16)
        bg = bmat[:, g * STATE_DIM:(g + 1) * STATE_DIM].astype(BF16)
        cg = cmat[:, g * STATE_DIM:(g + 1) * STATE_DIM].astype(BF16)
        cbm = lax.dot_general(cg, bg, NT_DIMS, preferred_element_type=F32)
        state = st_ref[0, g]
        y_off = lax.dot_general(cg, state.astype(BF16), NT_DIMS, preferred_element_type=F32)
        y_parts = []
        for j in range(HEADS_PER_GROUP // 2):
            scores = []
            for h in (g * HEADS_PER_GROUP + 2 * j, g * HEADS_PER_GROUP + 2 * j + 1):
                seg = cs[:, h:h + 1] - cs_t[h:h + 1, :]
                dec = jnp.exp(jnp.where(causal, seg, -jnp.inf))
                scores.append((cbm * dec).astype(BF16))
            xp = xdt_b[:, j * LANE:(j + 1) * LANE]
            zero = jnp.zeros_like(xp)
            rhs = jnp.concatenate(
                [jnp.where(lane < HEAD_DIM, xp, zero), jnp.where(lane >= HEAD_DIM, xp, zero)], axis=0)
            y_parts.append(jnp.dot(jnp.concatenate(scores, axis=1), rhs, preferred_element_type=F32))
        y = jnp.concatenate(y_parts, axis=1) + y_off * ex[2 * q:3 * q] + xs * dskip[:, c0:c0 + GROUP_W]
        zref = z2 if g < groups_per_block else z3
        zc = (g % groups_per_block) * GROUP_W
        ya_ref[:, c0:c0 + GROUP_W] = _gated_norm(
            y, zref[:, zc:zc + GROUP_W], nw[:, c0:c0 + GROUP_W]).astype(ya_ref.dtype)
        st_ref[0, g] = state * _decay_col(cs_last, g) + lax.dot_general(
            xdd_b, bg, TN_DIMS, preferred_element_type=F32)

    tail = ext[q:q + SUBLANE, :]
    ext[0:SUBLANE, :] = tail
    cvt_ref[0] = tail


def _ssd_prompt(p, dt_raw, consts, n_seq, seq_len, d):
    (cw, cb, dtb, alog, dskip, nw, scw, expand) = consts
    q = SSD_CHUNK if seq_len % SSD_CHUNK == 0 else seq_len
    nc = seq_len // q
    n_groups = 2 * d // GROUP_W
    t_p = n_seq * seq_len

    def blk(cidx):
        return pl.BlockSpec((q, d), lambda b, c, cidx=cidx: (b * nc + c, cidx))

    def const(arr):
        return pl.BlockSpec(arr.shape, lambda b, c: (0,) * arr.ndim)

    return pl.pallas_call(
        _ssd_prompt_body,
        grid=(n_seq, nc),
        in_specs=[blk(4), blk(5), blk(6), blk(2), blk(3), blk(7), blk(8), blk(9),
                  pl.BlockSpec((q, LANE), lambda b, c: (b * nc + c, 0)),
                  const(cw), const(cb), const(dtb), const(alog), const(dskip), const(nw), const(scw),
                  const(expand)],
        out_specs=[
            pl.BlockSpec((q, 2 * d), lambda b, c: (b * nc + c, 0)),
            pl.BlockSpec((q, d), lambda b, c: (b * nc + c, 0)),
            pl.BlockSpec((1, n_groups, GROUP_W, STATE_DIM), lambda b, c: (b, 0, 0, 0)),
            pl.BlockSpec((1, SUBLANE, 3 * d), lambda b, c: (b, 0, 0)),
            pl.BlockSpec((1, SUBLANE, d), lambda b, c: (b, 0, 0)),
        ],
        out_shape=[
            jax.ShapeDtypeStruct((t_p, 2 * d), BF16),
            jax.ShapeDtypeStruct((t_p, d), BF16),
            jax.ShapeDtypeStruct((n_seq, n_groups, GROUP_W, STATE_DIM), F32),
            jax.ShapeDtypeStruct((n_seq, SUBLANE, 3 * d), F32),
            jax.ShapeDtypeStruct((n_seq, SUBLANE, d), F32),
        ],
        scratch_shapes=[pltpu.VMEM((q + SUBLANE, 3 * d), F32), pltpu.VMEM((q + SUBLANE, d), F32)],
        compiler_params=pltpu.CompilerParams(
            dimension_semantics=("arbitrary", "arbitrary"), vmem_limit_bytes=VMEM_LIMIT),
        name="ssd_prompt",
    )(p, p, p, p, p, p, p, p, dt_raw, cw, cb, dtb, alog, dskip, nw, scw, expand)


SAMPLE_SEQS_PER_STEP = 2


def _ssd_sample_body(x4, x5, x6, z2, z3, scb, scc, sch, dtr, ssm_in, conv_in, sc_in,
                     cw, cb, dtb, alog, dskip, nw, scw, expand, gsel,
                     ya_ref, yb_ref, ssm_out, conv_out, sc_out, ext, ext2):
    d = x4.shape[1]
    n_groups = ssm_in.shape[1]
    gn = n_groups * STATE_DIM
    kw = cw.shape[0]
    kw2 = scw.shape[0]
    q = x4.shape[0] // SAMPLE_SEQS_PER_STEP
    groups_per_block = d // GROUP_W
    first = SUBLANE - (kw - 1)
    first2 = SUBLANE - (kw2 - 1)
    nrep = q * q

    rep_t = lax.broadcasted_iota(jnp.int32, (nrep, LANE), 0) % q
    rep_s = lax.broadcasted_iota(jnp.int32, (nrep, LANE), 0) // q
    rep_causal = rep_t >= rep_s
    row_q = lax.broadcasted_iota(jnp.int32, (q, LANE), 0)

    def rep_rows(m):
        return jnp.concatenate([jnp.broadcast_to(m[s:s + 1], (q, m.shape[1])) for s in range(q)], axis=0)

    def tile_rows(m):
        return jnp.concatenate([m] * q, axis=0)

    yb_rows = []
    ya_rows = [[] for _ in range(n_groups)]
    for sidx in range(SAMPLE_SEQS_PER_STEP):
        r0 = sidx * q
        ext[first:SUBLANE, :] = conv_in[sidx]
        ext[SUBLANE:SUBLANE + q, 0:d] = x4[r0:r0 + q, :]
        ext[SUBLANE:SUBLANE + q, d:2 * d] = x5[r0:r0 + q, :]
        ext[SUBLANE:SUBLANE + q, 2 * d:3 * d] = x6[r0:r0 + q, :]
        conv_out[sidx] = ext[SUBLANE + q - (kw - 1):SUBLANE + q, :]

        ext2[first2:SUBLANE, :] = sc_in[sidx]
        ext2[SUBLANE:SUBLANE + q, :] = scc[r0:r0 + q, :] * sch[r0:r0 + q, :]
        yb_rows.append(scb[r0:r0 + q, :] * _conv(ext2, 0, d, q, kw2, scw, first2))
        sc_out[sidx] = ext2[SUBLANE + q - (kw2 - 1):SUBLANE + q, :]

        dt = _softplus(dtr[r0:r0 + q, :] + dtb[...])
        a = dt * (-jnp.exp(alog[...]))
        cs = jnp.zeros((q, LANE), F32)
        for r in range(q):
            cs = cs + jnp.where(row_q >= r, jnp.broadcast_to(a[r:r + 1], (q, LANE)), 0.0)
        cs_last = cs[q - 1:q, :]
        dend = jnp.exp(cs_last - cs)
        ecs = jnp.exp(cs)

        bmat = _silu(_conv(ext, 2 * d, 2 * d + gn, q, kw, cw, first) + cb[:, 2 * d:2 * d + gn])
        cmat = _silu(_conv(ext, 2 * d + gn, 3 * d, q, kw, cw, first) + cb[:, 2 * d + gn:3 * d])

        cb_hi, cb_lo = _split2(tile_rows(cmat) * rep_rows(bmat))
        cbh = (jnp.dot(cb_hi, gsel[...], preferred_element_type=F32)
               + jnp.dot(cb_lo, gsel[...], preferred_element_type=F32))
        dec = jnp.exp(jnp.where(rep_causal, tile_rows(cs) - rep_rows(cs), -jnp.inf))
        stacked = jnp.concatenate([dt, dend, ecs], axis=0)
        st_hi = stacked.astype(BF16).astype(F32)
        pad = jnp.zeros((LANE - nrep - 6 * q, LANE), F32)
        lhs = jnp.concatenate([cbh * dec, st_hi, stacked - st_hi, pad], axis=0).astype(BF16)

        for g in range(n_groups):
            c0 = g * GROUP_W
            xs = _silu(_conv(ext, c0, c0 + GROUP_W, q, kw, cw, first) + cb[:, c0:c0 + GROUP_W])
            ex = jnp.dot(lhs, expand[:, c0:c0 + GROUP_W], preferred_element_type=F32)
            o = nrep
            dtx = ex[o:o + q] + ex[o + 3 * q:o + 4 * q]
            dendx = ex[o + q:o + 2 * q] + ex[o + 4 * q:o + 5 * q]
            ecsx = ex[o + 2 * q:o + 3 * q] + ex[o + 5 * q:o + 6 * q]
            xdt = xs * dtx
            xdd = xdt * dendx
            y = xs * dskip[:, c0:c0 + GROUP_W]
            for s in range(q):
                y = y + ex[s * q:(s + 1) * q] * jnp.broadcast_to(xdt[s:s + 1], (q, GROUP_W))
            state = ssm_in[sidx, g]
            cg = cmat[:, g * STATE_DIM:(g + 1) * STATE_DIM]
            bg = bmat[:, g * STATE_DIM:(g + 1) * STATE_DIM]
            y = y + lax.dot_general(cg, state, NT_DIMS, preferred_element_type=F32) * ecsx
            zref = z2 if g < groups_per_block else z3
            zc = (g % groups_per_block) * GROUP_W
            ya_rows[g].append(_gated_norm(y, zref[r0:r0 + q, zc:zc + GROUP_W], nw[:, c0:c0 + GROUP_W]))
            ssm_out[sidx, g] = state * _decay_col(cs_last, g) + lax.dot_general(
                xdd, bg, TN_DIMS, preferred_element_type=F32)

    yb_ref[...] = jnp.concatenate(yb_rows, axis=0).astype(yb_ref.dtype)
    for g in range(n_groups):
        ya_ref[:, g * GROUP_W:(g + 1) * GROUP_W] = jnp.concatenate(ya_rows[g], axis=0).astype(ya_ref.dtype)


def _ssd_sample(p, dt_raw, ssm, conv_state, sc_state, consts, gsel, row0, seq_len, d):
    (cw, cb, dtb, alog, dskip, nw, scw, expand) = consts
    n_seq = ssm.shape[0]
    sp = SAMPLE_SEQS_PER_STEP
    rows = sp * seq_len
    assert seq_len == SUBLANE and n_seq % sp == 0 and row0 % rows == 0
    b0 = row0 // rows

    def blk(cidx):
        return pl.BlockSpec((rows, d), lambda i, cidx=cidx: (b0 + i, cidx))

    def const(arr):
        return pl.BlockSpec(arr.shape, lambda i: (0,) * arr.ndim)

    def per_seq(arr):
        return pl.BlockSpec((sp,) + arr.shape[1:], lambda i: (i,) + (0,) * (arr.ndim - 1))

    return pl.pallas_call(
        _ssd_sample_body,
        grid=(n_seq // sp,),
        in_specs=[blk(4), blk(5), blk(6), blk(2), blk(3), blk(7), blk(8), blk(9),
                  pl.BlockSpec((rows, LANE), lambda i: (b0 + i, 0)),
                  per_seq(ssm), per_seq(conv_state), per_seq(sc_state),
                  const(cw), const(cb), const(dtb), const(alog), const(dskip), const(nw), const(scw),
                  const(expand), const(gsel)],
        out_specs=[
            pl.BlockSpec((rows, 2 * d), lambda i: (i, 0)),
            pl.BlockSpec((rows, d), lambda i: (i, 0)),
            per_seq(ssm), per_seq(conv_state), per_seq(sc_state),
        ],
        out_shape=[
            jax.ShapeDtypeStruct((n_seq * seq_len, 2 * d), BF16),
            jax.ShapeDtypeStruct((n_seq * seq_len, d), BF16),
            jax.ShapeDtypeStruct(ssm.shape, F32),
            jax.ShapeDtypeStruct(conv_state.shape, F32),
            jax.ShapeDtypeStruct(sc_state.shape, F32),
        ],
        scratch_shapes=[pltpu.VMEM((2 * SUBLANE, 3 * d), F32), pltpu.VMEM((2 * SUBLANE, d), F32)],
        compiler_params=pltpu.CompilerParams(
            dimension_semantics=("arbitrary",), vmem_limit_bytes=VMEM_LIMIT),
        name="ssd_sample",
    )(p, p, p, p, p, p, p, p, dt_raw, ssm, conv_state, sc_state,
      cw, cb, dtb, alog, dskip, nw, scw, expand, gsel)


def _branch_out_body(yap_ref, ybp_ref, yas_ref, ybs_ref, ga_ref, gb_ref, wa_ref, wb_ref, o_ref, *, prompt_tiles):
    def run(refs):
        ya_ref, yb_ref = refs
        pa = jnp.dot(ya_ref[...], wa_ref[...], preferred_element_type=F32)
        pb = jnp.dot(yb_ref[...], wb_ref[...], preferred_element_type=F32)
        merged = jax.nn.sigmoid(ga_ref[...]) * pa + jax.nn.sigmoid(gb_ref[...]) * pb
        o_ref[...] = merged.astype(o_ref.dtype)

    _by_stream(pl.program_id(0), prompt_tiles, run, (yap_ref, ybp_ref), (yas_ref, ybs_ref))


def _branch_out(ya_p, yb_p, ya_s, yb_s, p, wa, wb):
    t_p, d = yb_p.shape
    t_s = yb_s.shape[0]
    tm = _tile(math.gcd(t_p, t_s), 512, 16)
    tn = _tile(d, 512, LANE)
    nj = d // tn
    npt = t_p // tm
    yap_spec, yas_spec = _stream_specs((tm, 2 * d), npt)
    ybp_spec, ybs_spec = _stream_specs((tm, d), npt)
    return pl.pallas_call(
        functools.partial(_branch_out_body, prompt_tiles=npt),
        grid=((t_p + t_s) // tm, nj),
        in_specs=[
            yap_spec, ybp_spec, yas_spec, ybs_spec,
            pl.BlockSpec((tm, tn), lambda i, j: (i, j)),
            pl.BlockSpec((tm, tn), lambda i, j: (i, nj + j)),
            pl.BlockSpec((2 * d, tn), lambda i, j: (0, j)),
            pl.BlockSpec((d, tn), lambda i, j: (0, j)),
        ],
        out_specs=pl.BlockSpec((tm, tn), lambda i, j: (i, j)),
        out_shape=jax.ShapeDtypeStruct((t_p + t_s, d), BF16),
        compiler_params=pltpu.CompilerParams(
            dimension_semantics=("arbitrary", "arbitrary"), vmem_limit_bytes=VMEM_LIMIT),
        name="branch_out",
    )(ya_p, yb_p, ya_s, yb_s, p, p, wa, wb)


def _mix_route_body(m_ref, xp_ref, xs_ref, wo_ref, nw_ref, rhi_ref, rlo_ref,
                    x1_ref, h2_ref, route_ref, wts_ref, cnt_ref, *, n_coarse, per_group, prompt_tiles):
    @pl.when(pl.program_id(0) == 0)
    def _():
        cnt_ref[...] = jnp.zeros(cnt_ref.shape, F32)

    def run(x_ref):
        x1 = x_ref[...] + jnp.dot(m_ref[...], wo_ref[...], preferred_element_type=F32)
        x1_ref[...] = x1
        h2 = x1 * lax.rsqrt(jnp.mean(x1 * x1, axis=-1, keepdims=True) + NORM_EPS) * nw_ref[...]
        h2_ref[...] = h2
        h_hi, h_lo = _split2(h2)
        logits = (jnp.dot(h_hi, rhi_ref[...], preferred_element_type=F32)
                  + jnp.dot(h_hi, rlo_ref[...], preferred_element_type=F32)
                  + jnp.dot(h_lo, rhi_ref[...], preferred_element_type=F32))

        tm = logits.shape[0]
        n_fine = n_coarse * per_group
        lane = lax.broadcasted_iota(jnp.int32, logits.shape, 1)
        big = jnp.int32(LANE)
        neg = -jnp.inf
        is_c = lane < n_coarse
        lc = jnp.where(is_c, logits, neg)
        mc = jnp.max(lc, axis=-1, keepdims=True)
        grp = jnp.min(jnp.where(is_c & (lc == mc), lane, big), axis=-1, keepdims=True)
        p_grp = 1.0 / jnp.sum(jnp.where(is_c, jnp.exp(lc - mc), 0.0), axis=-1, keepdims=True)
        eidx = lane - n_coarse
        sel = (eidx >= 0) & (eidx < n_fine) & ((eidx // per_group) == grp)
        lf = jnp.where(sel, logits, neg)
        v1 = jnp.max(lf, axis=-1, keepdims=True)
        i1 = jnp.min(jnp.where(sel & (lf == v1), eidx, big), axis=-1, keepdims=True)
        sel2 = sel & (eidx != i1)
        lf2 = jnp.where(sel2, logits, neg)
        v2 = jnp.max(lf2, axis=-1, keepdims=True)
        i2 = jnp.min(jnp.where(sel2 & (lf2 == v2), eidx, big), axis=-1, keepdims=True)
        e2 = jnp.exp(v2 - v1)
        w1 = p_grp / (1.0 + e2)
        w2 = p_grp * e2 / (1.0 + e2)
        wts_ref[...] = jnp.where(lane == 0, w1, jnp.where(lane == 1, w2, 0.0))

        hit1 = eidx == i1
        hit2 = eidx == i2
        hits = hit1.astype(F32) + hit2.astype(F32)
        earlier = (lax.broadcasted_iota(jnp.int32, (tm, tm), 0)
                   > lax.broadcasted_iota(jnp.int32, (tm, tm), 1)).astype(BF16)
        before = jnp.dot(earlier, hits.astype(BF16), preferred_element_type=F32) + cnt_ref[...]
        r1 = jnp.sum(jnp.where(hit1, before, 0.0), axis=-1, keepdims=True).astype(jnp.int32)
        r2 = jnp.sum(jnp.where(hit2, before, 0.0), axis=-1, keepdims=True).astype(jnp.int32)
        cnt_ref[...] += jnp.sum(hits, axis=0, keepdims=True)
        route_ref[...] = jnp.where(lane == 0, i1, jnp.where(lane == 1, i2, jnp.where(
            lane == 2, r1, jnp.where(lane == 3, r2, 0))))

    _by_stream(pl.program_id(0), prompt_tiles, run, xp_ref, xs_ref)


def _mix_route(merged, x_p, x_s, wo, norm_w, r_hi, r_lo, n_coarse, per_group):
    t_p, d = x_p.shape
    T = t_p + x_s.shape[0]
    tm = _tile(math.gcd(t_p, x_s.shape[0]), 256, 16)
    npt = t_p // tm
    row = lambda i: (i, 0)
    fixed = lambda i: (0, 0)
    xp_spec, xs_spec = _stream_specs((tm, d), npt)
    return pl.pallas_call(
        functools.partial(_mix_route_body, n_coarse=n_coarse, per_group=per_group, prompt_tiles=npt),
        grid=(T // tm,),
        in_specs=[
            pl.BlockSpec((tm, d), row), xp_spec, xs_spec, pl.BlockSpec((d, d), fixed),
            pl.BlockSpec((1, d), fixed), pl.BlockSpec((d, LANE), fixed), pl.BlockSpec((d, LANE), fixed),
        ],
        out_specs=[pl.BlockSpec((tm, d), row), pl.BlockSpec((tm, d), row),
                   pl.BlockSpec((tm, LANE), row), pl.BlockSpec((tm, LANE), row),
                   pl.BlockSpec((1, LANE), fixed)],
        out_shape=[
            jax.ShapeDtypeStruct((T, d), F32), jax.ShapeDtypeStruct((T, d), F32),
            jax.ShapeDtypeStruct((T, LANE), jnp.int32), jax.ShapeDtypeStruct((T, LANE), F32),
            jax.ShapeDtypeStruct((1, LANE), F32),
        ],
        compiler_params=pltpu.CompilerParams(
            dimension_semantics=("arbitrary",), vmem_limit_bytes=VMEM_LIMIT),
        name="mix_route",
    )(merged, x_p, x_s, wo, norm_w, r_hi, r_lo)


MOE_MOVE_TOKENS = 128
MOE_MOVE_UNROLL = 8


def _moe_scatter_body(dest_ref, h_ref, xs_in, xs_hbm, sem):
    del xs_in
    tt = h_ref.shape[0]
    t0 = pl.program_id(0) * tt

    def start(i, carry):
        for k in range(MOE_TOP_K):
            pltpu.make_async_copy(
                h_ref.at[pl.ds(i, 1)], xs_hbm.at[pl.ds(dest_ref[(t0 + i) * MOE_TOP_K + k], 1)], sem).start()
        return carry

    lax.fori_loop(0, tt, start, 0, unroll=MOE_MOVE_UNROLL)
    for _ in range(MOE_TOP_K):
        pltpu.make_async_copy(h_ref, xs_hbm.at[pl.ds(0, tt)], sem).wait()


def _moe_scatter(dest, h2, xs_zero):
    T, d = h2.shape
    tt = MOE_MOVE_TOKENS
    assert T % tt == 0
    any_spec = pl.BlockSpec(memory_space=pl.ANY)
    return pl.pallas_call(
        _moe_scatter_body,
        grid_spec=pltpu.PrefetchScalarGridSpec(
            num_scalar_prefetch=1,
            grid=(T // tt,),
            in_specs=[pl.BlockSpec((tt, d), lambda i, dr: (i, 0)), any_spec],
            out_specs=any_spec,
            scratch_shapes=[pltpu.SemaphoreType.DMA(())],
        ),
        out_shape=jax.ShapeDtypeStruct(xs_zero.shape, xs_zero.dtype),
        input_output_aliases={2: 0},
        compiler_params=pltpu.CompilerParams(dimension_semantics=("arbitrary",)),
        name="moe_scatter",
    )(dest, h2, xs_zero)


def _moe_ffn_body(blk_e_ref, n_used_ref, x_ref, wg_ref, wu_ref, wd_ref, o_ref):
    del blk_e_ref

    @pl.when(pl.program_id(0) < n_used_ref[0])
    def _():
        xb = x_ref[...].astype(BF16)
        gate = jnp.dot(xb, wg_ref[0], preferred_element_type=F32)
        up = jnp.dot(xb, wu_ref[0], preferred_element_type=F32)
        act = (_silu(gate) * up).astype(BF16)
        o_ref[...] = jnp.dot(act, wd_ref[0], preferred_element_type=F32)

    @pl.when(pl.program_id(0) >= n_used_ref[0])
    def _():
        o_ref[...] = jnp.zeros(o_ref.shape, o_ref.dtype)


def _moe_ffn(blk_e, n_used, xs, wg, wu, wd):
    R, d = xs.shape
    f = wg.shape[2]
    nb = R // MOE_ROWS
    return pl.pallas_call(
        _moe_ffn_body,
        grid_spec=pltpu.PrefetchScalarGridSpec(
            num_scalar_prefetch=2,
            grid=(nb,),
            in_specs=[
                pl.BlockSpec((MOE_ROWS, d), lambda b, be, nu: (b, 0)),
                pl.BlockSpec((1, d, f), lambda b, be, nu: (be[b], 0, 0)),
                pl.BlockSpec((1, d, f), lambda b, be, nu: (be[b], 0, 0)),
                pl.BlockSpec((1, f, d), lambda b, be, nu: (be[b], 0, 0)),
            ],
            out_specs=pl.BlockSpec((MOE_ROWS, d), lambda b, be, nu: (b, 0)),
        ),
        out_shape=jax.ShapeDtypeStruct((R, d), F32),
        compiler_params=pltpu.CompilerParams(
            dimension_semantics=("arbitrary",), vmem_limit_bytes=VMEM_LIMIT),
        name="moe_ffn",
    )(blk_e, n_used, xs, wg, wu, wd)


def _moe_combine_body(dest_ref, x1_ref, wts_ref, nw_ref, yb_hbm, op_ref, os_ref, buf, sem, *, prompt_tiles):
    tt = x1_ref.shape[0]
    t0 = pl.program_id(0) * tt

    def start(i, carry):
        for k in range(MOE_TOP_K):
            pltpu.make_async_copy(
                yb_hbm.at[pl.ds(dest_ref[(t0 + i) * MOE_TOP_K + k], 1)], buf.at[k, pl.ds(i, 1)], sem).start()
        return carry

    lax.fori_loop(0, tt, start, 0, unroll=MOE_MOVE_UNROLL)
    for k in range(MOE_TOP_K):
        pltpu.make_async_copy(yb_hbm.at[pl.ds(0, tt)], buf.at[k], sem).wait()

    def finish(o_ref):
        w = wts_ref[...]
        x2 = x1_ref[...] + (buf[0] * w[:, 0:1] + buf[1] * w[:, 1:2])
        o_ref[...] = x2 * lax.rsqrt(jnp.mean(x2 * x2, axis=-1, keepdims=True) + NORM_EPS) * nw_ref[...]

    _by_stream(pl.program_id(0), prompt_tiles, finish, op_ref, os_ref)


def _moe_combine(dest, x1, wts, norm_w, yb, t_p):
    T, d = x1.shape
    tt = MOE_MOVE_TOKENS
    assert t_p % tt == 0 and T % tt == 0
    npt = t_p // tt
    op_spec, os_spec = _stream_specs((tt, d), npt)
    return pl.pallas_call(
        functools.partial(_moe_combine_body, prompt_tiles=npt),
        grid_spec=pltpu.PrefetchScalarGridSpec(
            num_scalar_prefetch=1,
            grid=(T // tt,),
            in_specs=[
                pl.BlockSpec((tt, d), lambda i, dr: (i, 0)),
                pl.BlockSpec((tt, LANE), lambda i, dr: (i, 0)),
                pl.BlockSpec((1, d), lambda i, dr: (0, 0)),
                pl.BlockSpec(memory_space=pl.ANY),
            ],
            out_specs=[op_spec, os_spec],
            scratch_shapes=[pltpu.VMEM((MOE_TOP_K, tt, d), F32), pltpu.SemaphoreType.DMA(())],
        ),
        out_shape=[jax.ShapeDtypeStruct((t_p, d), F32), jax.ShapeDtypeStruct((T - t_p, d), F32)],
        compiler_params=pltpu.CompilerParams(dimension_semantics=("arbitrary",)),
        name="moe_combine",
    )(dest, x1, wts, norm_w, yb)


def _route_rows(eid, rank, counts, n_assign):
    n_experts = counts.shape[0]
    padded = (counts + MOE_ROWS - 1) // MOE_ROWS * MOE_ROWS
    pend = jnp.cumsum(padded)
    pstart = pend - padded
    onehot = eid[:, :, None] == jnp.arange(n_experts, dtype=jnp.int32)[None, None, :]
    dest = (jnp.sum(jnp.where(onehot, pstart[None, None, :], 0), axis=-1) + rank).astype(jnp.int32)
    n_blocks = -(-n_assign // MOE_ROWS) + n_experts
    blk_start = jnp.arange(n_blocks, dtype=jnp.int32) * MOE_ROWS
    blk_e = jnp.sum((blk_start[:, None] >= pend[None, :]).astype(jnp.int32), axis=1)
    n_used = (pend[-1] // MOE_ROWS).astype(jnp.int32).reshape(1)
    last_e = jnp.sum((jnp.maximum(pend[-1] - 1, 0) >= pend).astype(jnp.int32))
    blk_e = jnp.where(blk_start < pend[-1], blk_e, last_e).astype(jnp.int32)
    return dest.reshape(-1), blk_e, n_used, n_blocks


def _pad_lanes(v, fill=0.0):
    return jnp.pad(v.astype(F32), (0, LANE - v.shape[0]), constant_values=fill).reshape(1, LANE)


def kernel(x_prompt, x_sample, state_ssm, state_ssd_conv, state_short_conv, norm_mixer, w_in, ssd_conv_w,
           ssd_conv_b, ssd_dt_bias, ssd_a_log, ssd_d, ssd_norm, sc_conv_w, w_branch_out, w_out, norm_ffn,
           w_router_coarse, w_router_fine, w_expert_gate, w_expert_up, w_expert_down, norm_final):
    depth = w_in.shape[0]
    assert depth == 1
    n_p, seq_p, d = x_prompt.shape
    n_s, seq_s, _ = x_sample.shape
    d_inner = 2 * d
    n_heads = d_inner // HEAD_DIM
    n_groups = d_inner // GROUP_W
    gn = n_groups * STATE_DIM
    conv_dim = d_inner + 2 * gn
    assert conv_dim == 3 * d and n_heads <= LANE and ssd_conv_w.shape[2] == conv_dim
    t_p, t_s = n_p * seq_p, n_s * seq_s
    n_coarse = w_router_coarse.shape[2]
    n_experts = w_router_fine.shape[2]
    assert n_coarse + n_experts <= LANE

    off_dt = 2 * d + d_inner + conv_dim
    off_sc = off_dt + n_heads
    w_in0 = w_in[0]
    w_main = jnp.concatenate([w_in0[:, :off_dt], w_in0[:, off_sc:]], axis=1).astype(BF16)
    w_dt = jnp.pad(w_in0[:, off_dt:off_sc], ((0, 0), (0, LANE - n_heads))).astype(BF16)
    head_of_col = jnp.arange(d_inner, dtype=jnp.int32) // HEAD_DIM
    expand = (jnp.arange(LANE, dtype=jnp.int32)[:, None] == head_of_col[None, :]).astype(BF16)
    group_of_n = jnp.arange(gn, dtype=jnp.int32) // STATE_DIM
    group_of_head = jnp.arange(LANE, dtype=jnp.int32) // HEADS_PER_GROUP
    gsel = ((group_of_n[:, None] == group_of_head[None, :])
            & (jnp.arange(LANE)[None, :] < n_heads)).astype(BF16)
    consts = (
        ssd_conv_w[0], ssd_conv_b[0].reshape(1, conv_dim), _pad_lanes(ssd_dt_bias[0]), _pad_lanes(ssd_a_log[0]),
        jnp.repeat(ssd_d[0].astype(F32), HEAD_DIM).reshape(1, d_inner), ssd_norm[0].reshape(1, d_inner),
        sc_conv_w[0], expand,
    )
    wbo = w_branch_out[0].astype(BF16)
    wa, wb = wbo[:d_inner], wbo[d_inner:]
    wo = w_out[0].astype(BF16)
    w_router = jnp.pad(jnp.concatenate([w_router_coarse[0], w_router_fine[0]], axis=1),
                       ((0, 0), (0, LANE - n_coarse - n_experts)))
    r_hi = w_router.astype(BF16)
    r_lo = (w_router - r_hi.astype(F32)).astype(BF16)
    wg = w_expert_gate[0].astype(BF16)
    wu = w_expert_up[0].astype(BF16)
    wd = w_expert_down[0].astype(BF16)

    x_p = x_prompt.reshape(t_p, d)
    x_s = x_sample.reshape(t_s, d)
    p, dt_raw = _inproj(x_p, x_s, norm_mixer[0].reshape(1, d), w_main, w_dt)
    ya_p, yb_p, p_ssm, p_conv_tail, p_sc_tail = _ssd_prompt(p, dt_raw, consts, n_p, seq_p, d)
    ya_s, yb_s, s_ssm, s_conv, s_sc = _ssd_sample(
        p, dt_raw, state_ssm[0].reshape(n_s, n_groups, GROUP_W, STATE_DIM), state_ssd_conv[0],
        state_short_conv[0], consts, gsel, t_p, seq_s, d)
    merged = _branch_out(ya_p, yb_p, ya_s, yb_s, p, wa, wb)
    x1, h2, route, wts, counts = _mix_route(merged, x_p, x_s, wo, norm_ffn[0].reshape(1, d), r_hi, r_lo,
                                            n_coarse, n_experts // n_coarse)

    n_assign = (t_p + t_s) * MOE_TOP_K
    dest, blk_e, n_used, n_blocks = _route_rows(
        route[:, 0:MOE_TOP_K], route[:, MOE_TOP_K:2 * MOE_TOP_K],
        counts[0, n_coarse:n_coarse + n_experts].astype(jnp.int32), n_assign)
    xs = _moe_scatter(dest, h2, jnp.zeros((n_blocks * MOE_ROWS, d), F32))
    yrows = _moe_ffn(blk_e, n_used, xs, wg, wu, wd)
    out_p, out_s = _moe_combine(dest, x1, wts, norm_final.reshape(1, d), yrows, t_p)

    kw = ssd_conv_w.shape[1]
    kw2 = sc_conv_w.shape[1]
    return (
        out_p.reshape(n_p, seq_p, d),
        out_s.reshape(n_s, seq_s, d),
        p_ssm.reshape(1, n_p, n_heads, HEAD_DIM, STATE_DIM),
        p_conv_tail[:, SUBLANE - (kw - 1):, :][None],
        p_sc_tail[:, SUBLANE - (kw2 - 1):, :][None],
        s_ssm.reshape(1, n_s, n_heads, HEAD_DIM, STATE_DIM),
        s_conv[None],
        s_sc[None],
    )
```

```python
import functools
import math

import jax
import jax.numpy as jnp
from jax import lax
from jax.experimental import pallas as pl
from jax.experimental.pallas import tpu as pltpu

F32 = jnp.float32
BF16 = jnp.bfloat16

NORM_EPS = 1e-6
SSD_NORM_EPS = 1e-5
HEAD_DIM = 64
STATE_DIM = 128
HEADS_PER_GROUP = 8
GROUP_W = HEADS_PER_GROUP * HEAD_DIM
SSD_CHUNK = 128
MOE_TOP_K = 2
MOE_ROWS = 256

LANE = 128
SUBLANE = 8
VMEM_LIMIT = 56 * 1024 * 1024

NT_DIMS = (((1,), (1,)), ((), ()))
TN_DIMS = (((0,), (0,)), ((), ()))


def _tile(n, target, align):
    best = None
    for t in range(align, min(n, target) + 1, align):
        if n % t == 0:
            best = t
    assert best is not None, (n, target, align)
    return best


def _split2(v):
    hi = v.astype(BF16)
    lo = (v - hi.astype(F32)).astype(BF16)
    return hi, lo


def _split3(v):
    hi = v.astype(BF16)
    r = v - hi.astype(F32)
    mid = r.astype(BF16)
    lo = (r - mid.astype(F32)).astype(BF16)
    return hi, mid, lo


def _softplus(x):
    return jnp.maximum(x, 0.0) + jnp.log1p(jnp.exp(-jnp.abs(x)))


def _silu(x):
    return x * jax.nn.sigmoid(x)


def _by_stream(tile, prompt_tiles, fn, prompt_ref, sample_ref):
    @pl.when(tile < prompt_tiles)
    def _():
        fn(prompt_ref)

    @pl.when(tile >= prompt_tiles)
    def _():
        fn(sample_ref)


def _stream_specs(block, prompt_tiles, **kwargs):
    pad = (0,) * (len(block) - 1)
    prompt = lambda i, *_: (jnp.minimum(i, prompt_tiles - 1),) + pad
    sample = lambda i, *_: (jnp.maximum(i - prompt_tiles, 0),) + pad
    return pl.BlockSpec(block, prompt, **kwargs), pl.BlockSpec(block, sample, **kwargs)


def _prenorm_body(xp_ref, xs_ref, nw_ref, wdt_ref, h_ref, dt_ref, *, prompt_tiles):
    def run(x_ref):
        x = x_ref[...]
        h = x * lax.rsqrt(jnp.mean(x * x, axis=-1, keepdims=True) + NORM_EPS) * nw_ref[...]
        hb = h.astype(BF16)
        h_ref[...] = hb
        dt_ref[...] = jnp.dot(hb, wdt_ref[...], preferred_element_type=F32)

    _by_stream(pl.program_id(0), prompt_tiles, run, xp_ref, xs_ref)


def _prenorm(x_p, x_s, norm_w, w_dt):
    t_p, D = x_p.shape
    T = t_p + x_s.shape[0]
    tm = _tile(math.gcd(t_p, x_s.shape[0]), 512, 16)
    npt = t_p // tm
    xp_spec, xs_spec = _stream_specs((tm, D), npt)
    return pl.pallas_call(
        functools.partial(_prenorm_body, prompt_tiles=npt),
        grid=(T // tm,),
        in_specs=[xp_spec, xs_spec, pl.BlockSpec((1, D), lambda i: (0, 0)),
                  pl.BlockSpec((D, LANE), lambda i: (0, 0))],
        out_specs=[pl.BlockSpec((tm, D), lambda i: (i, 0)), pl.BlockSpec((tm, LANE), lambda i: (i, 0))],
        out_shape=[jax.ShapeDtypeStruct((T, D), BF16), jax.ShapeDtypeStruct((T, LANE), F32)],
        compiler_params=pltpu.CompilerParams(
            dimension_semantics=("arbitrary",), vmem_limit_bytes=VMEM_LIMIT),
        name="prenorm",
    )(x_p, x_s, norm_w, w_dt)


def _inproj_body(h_ref, wa_ref, wb_ref, cw_ref, cb_ref, p_ref, tail_ref, w_scr, cext,
                 *, main_tiles, blocks_per_d, prompt_tiles, tiles_per_seq):
    j = pl.program_id(0)
    i = pl.program_id(1)
    tm = p_ref.shape[0]
    kw = cw_ref.shape[0]

    @pl.when(i == 0)
    def _():
        def cast(w_ref):
            w_scr[...] = w_ref[...].astype(BF16)

        _by_stream(j, main_tiles, cast, wa_ref, wb_ref)

    raw = jnp.dot(h_ref[...], w_scr[...], preferred_element_type=F32)
    tail_ref[...] = raw[tm - SUBLANE:tm, :]
    is_gate = j < 2 * blocks_per_d
    is_z = (j >= 2 * blocks_per_d) & (j < 4 * blocks_per_d)
    is_conv = (j >= 4 * blocks_per_d) & (j < 7 * blocks_per_d) & (i < prompt_tiles)

    @pl.when(is_gate)
    def _():
        p_ref[...] = jax.nn.sigmoid(raw)

    @pl.when(is_z)
    def _():
        p_ref[...] = _silu(raw)

    @pl.when(is_conv)
    def _():
        @pl.when(i % tiles_per_seq == 0)
        def _():
            cext[0:SUBLANE, :] = jnp.zeros((SUBLANE, cext.shape[1]), F32)

        cext[SUBLANE:SUBLANE + tm, :] = raw
        acc = cb_ref[...] + _conv(cext, 0, cext.shape[1], tm, kw, cw_ref, SUBLANE - (kw - 1))
        p_ref[...] = _silu(acc)
        cext[0:SUBLANE, :] = raw[tm - SUBLANE:tm, :]

    @pl.when(jnp.logical_not(is_gate | is_z | is_conv))
    def _():
        p_ref[...] = raw


def _inproj(h, w_in0, w_tail, n_head_cols, cw, cb, t_p, seq_p):
    T, D = h.shape
    tm = _tile(math.gcd(seq_p, T - t_p), 1024, 16)
    tn = _tile(math.gcd(D, w_tail.shape[1]), 1024, LANE)
    main_tiles = n_head_cols // tn
    n_out = n_head_cols + w_tail.shape[1]
    bpd = D // tn
    conv_block = lambda j, i: (0, jnp.clip(j - 4 * bpd, 0, 3 * bpd - 1))
    return pl.pallas_call(
        functools.partial(_inproj_body, main_tiles=main_tiles, blocks_per_d=bpd,
                          prompt_tiles=t_p // tm, tiles_per_seq=seq_p // tm),
        grid=(n_out // tn, T // tm),
        in_specs=[
            pl.BlockSpec((tm, D), lambda j, i: (i, 0)),
            pl.BlockSpec((D, tn), lambda j, i: (0, jnp.minimum(j, main_tiles - 1))),
            pl.BlockSpec((D, tn), lambda j, i: (0, jnp.maximum(j - main_tiles, 0)),
                         pipeline_mode=pl.Buffered(1)),
            pl.BlockSpec((cw.shape[0], tn), conv_block),
            pl.BlockSpec((1, tn), conv_block),
        ],
        out_specs=[pl.BlockSpec((tm, tn), lambda j, i: (i, j)),
                   pl.BlockSpec((SUBLANE, tn), lambda j, i: (i, j))],
        out_shape=[jax.ShapeDtypeStruct((T, n_out), F32),
                   jax.ShapeDtypeStruct((T // tm * SUBLANE, n_out), F32)],
        scratch_shapes=[pltpu.VMEM((D, tn), BF16), pltpu.VMEM((tm + SUBLANE, tn), F32)],
        compiler_params=pltpu.CompilerParams(
            dimension_semantics=("arbitrary", "arbitrary"), vmem_limit_bytes=VMEM_LIMIT),
        name="inproj",
    )(h, w_in0, w_tail, cw, cb)


def _conv(ext, lo, hi, q, width, w_ref, first):
    acc = None
    for k in range(width):
        term = w_ref[k:k + 1, lo:hi] * ext[first + k:first + k + q, lo:hi]
        acc = term if acc is None else acc + term
    return acc


def _gated_norm(y, z_act, nw):
    g = y * z_act
    return g * lax.rsqrt(jnp.mean(g * g, axis=-1, keepdims=True) + SSD_NORM_EPS) * nw


def _decay_col(cs_last_row, g):
    d = jnp.exp(cs_last_row)
    parts = [
        jnp.broadcast_to(d[0:1, g * HEADS_PER_GROUP + j:g * HEADS_PER_GROUP + j + 1], (HEAD_DIM, STATE_DIM))
        for j in range(HEADS_PER_GROUP)
    ]
    return jnp.concatenate(parts, axis=0)


def _ssd_prompt_body(x4, x5, x6, z2, z3, scb, scc, sch, dtr,
                     dtb, alog, dskip, nw, scw, expand,
                     ya_ref, yb_ref, st_ref, sct_ref, ext2):
    q, d = x4.shape
    n_groups = st_ref.shape[1]
    gn = n_groups * STATE_DIM
    kw2 = scw.shape[0]

    @pl.when(pl.program_id(1) == 0)
    def _():
        st_ref[...] = jnp.zeros(st_ref.shape, F32)
        ext2[0:SUBLANE, :] = jnp.zeros((SUBLANE, d), F32)

    ext2[SUBLANE:SUBLANE + q, :] = scc[...] * sch[...]
    v = _conv(ext2, 0, d, q, kw2, scw, SUBLANE - (kw2 - 1))
    yb_ref[...] = (scb[...] * v).astype(yb_ref.dtype)
    tail2 = ext2[q:q + SUBLANE, :]
    ext2[0:SUBLANE, :] = tail2
    sct_ref[0] = tail2

    dt = _softplus(dtr[...] + dtb[...])
    a = dt * (-jnp.exp(alog[...]))
    row = lax.broadcasted_iota(jnp.int32, (q, q), 0)
    col = lax.broadcasted_iota(jnp.int32, (q, q), 1)
    causal = row >= col
    tri = causal.astype(BF16)
    cs = sum(jnp.dot(tri, part, preferred_element_type=F32) for part in _split3(a))
    cs_t = cs.T
    cs_last = cs[q - 1:q, :]
    dend = jnp.exp(cs_last - cs)
    ecs = jnp.exp(cs)
    stacked = jnp.concatenate([dt, dend, ecs], axis=0)
    st_hi, st_lo = _split2(stacked)

    lane = lax.broadcasted_iota(jnp.int32, (q, LANE), 1)
    groups_per_block = d // GROUP_W

    for g in range(n_groups):
        c0 = g * GROUP_W
        xref = x4 if g < groups_per_block else x5
        bc = (g % groups_per_block) * GROUP_W
        xs = xref[:, bc:bc + GROUP_W]
        e_g = expand[:, c0:c0 + GROUP_W]
        ex = (jnp.dot(st_hi, e_g, preferred_element_type=F32)
              + jnp.dot(st_lo, e_g, preferred_element_type=F32))
        xdt = xs * ex[0:q]
        xdt_b = xdt.astype(BF16)
        xdd_b = (xdt * ex[q:2 * q]).astype(BF16)
        bg = x6[:, g * STATE_DIM:(g + 1) * STATE_DIM].astype(BF16)
        cg = x6[:, gn + g * STATE_DIM:gn + (g + 1) * STATE_DIM].astype(BF16)
        cbm = lax.dot_general(cg, bg, NT_DIMS, preferred_element_type=F32)
        state = st_ref[0, g]
        y_off = lax.dot_general(cg, state.astype(BF16), NT_DIMS, preferred_element_type=F32)
        y_parts = []
        for j in range(HEADS_PER_GROUP // 2):
            scores = []
            for h in (g * HEADS_PER_GROUP + 2 * j, g * HEADS_PER_GROUP + 2 * j + 1):
                seg = cs[:, h:h + 1] - cs_t[h:h + 1, :]
                dec = jnp.exp(jnp.where(causal, seg, -jnp.inf))
                scores.append((cbm * dec).astype(BF16))
            xp = xdt_b[:, j * LANE:(j + 1) * LANE]
            zero = jnp.zeros_like(xp)
            rhs = jnp.concatenate(
                [jnp.where(lane < HEAD_DIM, xp, zero), jnp.where(lane >= HEAD_DIM, xp, zero)], axis=0)
            y_parts.append(jnp.dot(jnp.concatenate(scores, axis=1), rhs, preferred_element_type=F32))
        y = jnp.concatenate(y_parts, axis=1) + y_off * ex[2 * q:3 * q] + xs * dskip[:, c0:c0 + GROUP_W]
        zref = z2 if g < groups_per_block else z3
        ya_ref[:, c0:c0 + GROUP_W] = _gated_norm(
            y, zref[:, bc:bc + GROUP_W], nw[:, c0:c0 + GROUP_W]).astype(ya_ref.dtype)
        st_ref[0, g] = state * _decay_col(cs_last, g) + lax.dot_general(
            xdd_b, bg, TN_DIMS, preferred_element_type=F32)


def _ssd_prompt(p, dt_raw, consts, n_seq, seq_len, d):
    (_, _, dtb, alog, dskip, nw, scw, expand) = consts
    q = SSD_CHUNK if seq_len % SSD_CHUNK == 0 else seq_len
    nc = seq_len // q
    n_groups = 2 * d // GROUP_W
    t_p = n_seq * seq_len
    assert 2 * n_groups * STATE_DIM == d

    def blk(cidx):
        return pl.BlockSpec((q, d), lambda b, c, cidx=cidx: (b * nc + c, cidx))

    def const(arr):
        return pl.BlockSpec(arr.shape, lambda b, c: (0,) * arr.ndim)

    return pl.pallas_call(
        _ssd_prompt_body,
        grid=(n_seq, nc),
        in_specs=[blk(4), blk(5), blk(6), blk(2), blk(3), blk(7), blk(8), blk(9),
                  pl.BlockSpec((q, LANE), lambda b, c: (b * nc + c, 0)),
                  const(dtb), const(alog), const(dskip), const(nw), const(scw), const(expand)],
        out_specs=[
            pl.BlockSpec((q, 2 * d), lambda b, c: (b * nc + c, 0)),
            pl.BlockSpec((q, d), lambda b, c: (b * nc + c, 0)),
            pl.BlockSpec((1, n_groups, GROUP_W, STATE_DIM), lambda b, c: (b, 0, 0, 0)),
            pl.BlockSpec((1, SUBLANE, d), lambda b, c: (b, 0, 0)),
        ],
        out_shape=[
            jax.ShapeDtypeStruct((t_p, 2 * d), BF16),
            jax.ShapeDtypeStruct((t_p, d), BF16),
            jax.ShapeDtypeStruct((n_seq, n_groups, GROUP_W, STATE_DIM), F32),
            jax.ShapeDtypeStruct((n_seq, SUBLANE, d), F32),
        ],
        scratch_shapes=[pltpu.VMEM((q + SUBLANE, d), F32)],
        compiler_params=pltpu.CompilerParams(
            dimension_semantics=("arbitrary", "arbitrary"), vmem_limit_bytes=VMEM_LIMIT),
        name="ssd_prompt",
    )(p, p, p, p, p, p, p, p, dt_raw, dtb, alog, dskip, nw, scw, expand)


SAMPLE_SEQS_PER_STEP = 2


def _ssd_sample_body(x4, x5, x6, z2, z3, scb, scc, sch, dtr, ssm_in, conv_in, sc_in,
                     cw, cb, dtb, alog, dskip, nw, scw, expand, gsel,
                     ya_ref, yb_ref, ssm_out, conv_out, sc_out, ext, ext2):
    d = x4.shape[1]
    n_groups = ssm_in.shape[1]
    gn = n_groups * STATE_DIM
    kw = cw.shape[0]
    kw2 = scw.shape[0]
    q = x4.shape[0] // SAMPLE_SEQS_PER_STEP
    groups_per_block = d // GROUP_W
    first = SUBLANE - (kw - 1)
    first2 = SUBLANE - (kw2 - 1)
    nrep = q * q

    rep_t = lax.broadcasted_iota(jnp.int32, (nrep, LANE), 0) % q
    rep_s = lax.broadcasted_iota(jnp.int32, (nrep, LANE), 0) // q
    rep_causal = rep_t >= rep_s
    row_q = lax.broadcasted_iota(jnp.int32, (q, LANE), 0)

    def rep_rows(m):
        return jnp.concatenate([jnp.broadcast_to(m[s:s + 1], (q, m.shape[1])) for s in range(q)], axis=0)

    def tile_rows(m):
        return jnp.concatenate([m] * q, axis=0)

    yb_rows = []
    ya_rows = [[] for _ in range(n_groups)]
    for sidx in range(SAMPLE_SEQS_PER_STEP):
        r0 = sidx * q
        ext[first:SUBLANE, :] = conv_in[sidx]
        ext[SUBLANE:SUBLANE + q, 0:d] = x4[r0:r0 + q, :]
        ext[SUBLANE:SUBLANE + q, d:2 * d] = x5[r0:r0 + q, :]
        ext[SUBLANE:SUBLANE + q, 2 * d:3 * d] = x6[r0:r0 + q, :]
        conv_out[sidx] = ext[SUBLANE + q - (kw - 1):SUBLANE + q, :]

        ext2[first2:SUBLANE, :] = sc_in[sidx]
        ext2[SUBLANE:SUBLANE + q, :] = scc[r0:r0 + q, :] * sch[r0:r0 + q, :]
        yb_rows.append(scb[r0:r0 + q, :] * _conv(ext2, 0, d, q, kw2, scw, first2))
        sc_out[sidx] = ext2[SUBLANE + q - (kw2 - 1):SUBLANE + q, :]

        dt = _softplus(dtr[r0:r0 + q, :] + dtb[...])
        a = dt * (-jnp.exp(alog[...]))
        cs = jnp.zeros((q, LANE), F32)
        for r in range(q):
            cs = cs + jnp.where(row_q >= r, jnp.broadcast_to(a[r:r + 1], (q, LANE)), 0.0)
        cs_last = cs[q - 1:q, :]
        dend = jnp.exp(cs_last - cs)
        ecs = jnp.exp(cs)

        bmat = _silu(_conv(ext, 2 * d, 2 * d + gn, q, kw, cw, first) + cb[:, 2 * d:2 * d + gn])
        cmat = _silu(_conv(ext, 2 * d + gn, 3 * d, q, kw, cw, first) + cb[:, 2 * d + gn:3 * d])

        cb_hi, cb_lo = _split2(tile_rows(cmat) * rep_rows(bmat))
        cbh = (jnp.dot(cb_hi, gsel[...], preferred_element_type=F32)
               + jnp.dot(cb_lo, gsel[...], preferred_element_type=F32))
        dec = jnp.exp(jnp.where(rep_causal, tile_rows(cs) - rep_rows(cs), -jnp.inf))
        stacked = jnp.concatenate([dt, dend, ecs], axis=0)
        st_hi = stacked.astype(BF16).astype(F32)
        pad = jnp.zeros((LANE - nrep - 6 * q, LANE), F32)
        lhs = jnp.concatenate([cbh * dec, st_hi, stacked - st_hi, pad], axis=0).astype(BF16)

        for g in range(n_groups):
            c0 = g * GROUP_W
            xs = _silu(_conv(ext, c0, c0 + GROUP_W, q, kw, cw, first) + cb[:, c0:c0 + GROUP_W])
            ex = jnp.dot(lhs, expand[:, c0:c0 + GROUP_W], preferred_element_type=F32)
            o = nrep
            dtx = ex[o:o + q] + ex[o + 3 * q:o + 4 * q]
            dendx = ex[o + q:o + 2 * q] + ex[o + 4 * q:o + 5 * q]
            ecsx = ex[o + 2 * q:o + 3 * q] + ex[o + 5 * q:o + 6 * q]
            xdt = xs * dtx
            xdd = xdt * dendx
            y = xs * dskip[:, c0:c0 + GROUP_W]
            for s in range(q):
                y = y + ex[s * q:(s + 1) * q] * jnp.broadcast_to(xdt[s:s + 1], (q, GROUP_W))
            state = ssm_in[sidx, g]
            cg = cmat[:, g * STATE_DIM:(g + 1) * STATE_DIM]
            bg = bmat[:, g * STATE_DIM:(g + 1) * STATE_DIM]
            y = y + lax.dot_general(cg, state, NT_DIMS, preferred_element_type=F32) * ecsx
            zref = z2 if g < groups_per_block else z3
            zc = (g % groups_per_block) * GROUP_W
            ya_rows[g].append(_gated_norm(y, zref[r0:r0 + q, zc:zc + GROUP_W], nw[:, c0:c0 + GROUP_W]))
            ssm_out[sidx, g] = state * _decay_col(cs_last, g) + lax.dot_general(
                xdd, bg, TN_DIMS, preferred_element_type=F32)

    yb_ref[...] = jnp.concatenate(yb_rows, axis=0).astype(yb_ref.dtype)
    for g in range(n_groups):
        ya_ref[:, g * GROUP_W:(g + 1) * GROUP_W] = jnp.concatenate(ya_rows[g], axis=0).astype(ya_ref.dtype)


def _ssd_sample(p, dt_raw, ssm, conv_state, sc_state, consts, gsel, row0, seq_len, d):
    (cw, cb, dtb, alog, dskip, nw, scw, expand) = consts
    n_seq = ssm.shape[0]
    sp = SAMPLE_SEQS_PER_STEP
    rows = sp * seq_len
    assert seq_len == SUBLANE and n_seq % sp == 0 and row0 % rows == 0
    b0 = row0 // rows

    def blk(cidx):
        return pl.BlockSpec((rows, d), lambda i, cidx=cidx: (b0 + i, cidx))

    def const(arr):
        return pl.BlockSpec(arr.shape, lambda i: (0,) * arr.ndim)

    def per_seq(arr):
        return pl.BlockSpec((sp,) + arr.shape[1:], lambda i: (i,) + (0,) * (arr.ndim - 1))

    return pl.pallas_call(
        _ssd_sample_body,
        grid=(n_seq // sp,),
        in_specs=[blk(4), blk(5), blk(6), blk(2), blk(3), blk(7), blk(8), blk(9),
                  pl.BlockSpec((rows, LANE), lambda i: (b0 + i, 0)),
                  per_seq(ssm), per_seq(conv_state), per_seq(sc_state),
                  const(cw), const(cb), const(dtb), const(alog), const(dskip), const(nw), const(scw),
                  const(expand), const(gsel)],
        out_specs=[
            pl.BlockSpec((rows, 2 * d), lambda i: (i, 0)),
            pl.BlockSpec((rows, d), lambda i: (i, 0)),
            per_seq(ssm), per_seq(conv_state), per_seq(sc_state),
        ],
        out_shape=[
            jax.ShapeDtypeStruct((n_seq * seq_len, 2 * d), BF16),
            jax.ShapeDtypeStruct((n_seq * seq_len, d), BF16),
            jax.ShapeDtypeStruct(ssm.shape, F32),
            jax.ShapeDtypeStruct(conv_state.shape, F32),
            jax.ShapeDtypeStruct(sc_state.shape, F32),
        ],
        scratch_shapes=[pltpu.VMEM((2 * SUBLANE, 3 * d), F32), pltpu.VMEM((2 * SUBLANE, d), F32)],
        compiler_params=pltpu.CompilerParams(
            dimension_semantics=("arbitrary",), vmem_limit_bytes=VMEM_LIMIT),
        name="ssd_sample",
    )(p, p, p, p, p, p, p, p, dt_raw, ssm, conv_state, sc_state,
      cw, cb, dtb, alog, dskip, nw, scw, expand, gsel)


def _branch_out_body(yap_ref, ybp_ref, yas_ref, ybs_ref, ga_ref, gb_ref, wa_ref, wb_ref, o_ref, *, prompt_tiles):
    def run(refs):
        ya_ref, yb_ref = refs
        pa = jnp.dot(ya_ref[...], wa_ref[...], preferred_element_type=F32)
        pb = jnp.dot(yb_ref[...], wb_ref[...], preferred_element_type=F32)
        merged = ga_ref[...] * pa + gb_ref[...] * pb
        o_ref[...] = merged.astype(o_ref.dtype)

    _by_stream(pl.program_id(0), prompt_tiles, run, (yap_ref, ybp_ref), (yas_ref, ybs_ref))


def _branch_out(ya_p, yb_p, ya_s, yb_s, p, wa, wb):
    t_p, d = yb_p.shape
    t_s = yb_s.shape[0]
    tm = _tile(math.gcd(t_p, t_s), 512, 16)
    tn = _tile(d, 512, LANE)
    nj = d // tn
    npt = t_p // tm
    yap_spec, yas_spec = _stream_specs((tm, 2 * d), npt)
    ybp_spec, ybs_spec = _stream_specs((tm, d), npt)
    return pl.pallas_call(
        functools.partial(_branch_out_body, prompt_tiles=npt),
        grid=((t_p + t_s) // tm, nj),
        in_specs=[
            yap_spec, ybp_spec, yas_spec, ybs_spec,
            pl.BlockSpec((tm, tn), lambda i, j: (i, j)),
            pl.BlockSpec((tm, tn), lambda i, j: (i, nj + j)),
            pl.BlockSpec((2 * d, tn), lambda i, j: (0, j)),
            pl.BlockSpec((d, tn), lambda i, j: (0, j)),
        ],
        out_specs=pl.BlockSpec((tm, tn), lambda i, j: (i, j)),
        out_shape=jax.ShapeDtypeStruct((t_p + t_s, d), BF16),
        compiler_params=pltpu.CompilerParams(
            dimension_semantics=("arbitrary", "arbitrary"), vmem_limit_bytes=VMEM_LIMIT),
        name="branch_out",
    )(ya_p, yb_p, ya_s, yb_s, p, p, wa, wb)


def _mix_route_body(m_ref, xp_ref, xs_ref, wo_ref, nw_ref, rhi_ref, rlo_ref,
                    x1_ref, h2_ref, route_ref, wts_ref, cnt_ref, *, n_coarse, per_group, prompt_tiles):
    @pl.when(pl.program_id(0) == 0)
    def _():
        cnt_ref[...] = jnp.zeros(cnt_ref.shape, F32)

    def run(x_ref):
        x1 = x_ref[...] + jnp.dot(m_ref[...], wo_ref[...], preferred_element_type=F32)
        x1_ref[...] = x1
        h2 = x1 * lax.rsqrt(jnp.mean(x1 * x1, axis=-1, keepdims=True) + NORM_EPS) * nw_ref[...]
        h2_ref[...] = h2
        h_hi, h_lo = _split2(h2)
        logits = (jnp.dot(h_hi, rhi_ref[...], preferred_element_type=F32)
                  + jnp.dot(h_hi, rlo_ref[...], preferred_element_type=F32)
                  + jnp.dot(h_lo, rhi_ref[...], preferred_element_type=F32))

        tm = logits.shape[0]
        n_fine = n_coarse * per_group
        lane = lax.broadcasted_iota(jnp.int32, logits.shape, 1)
        big = jnp.int32(LANE)
        neg = -jnp.inf
        is_c = lane < n_coarse
        lc = jnp.where(is_c, logits, neg)
        mc = jnp.max(lc, axis=-1, keepdims=True)
        grp = jnp.min(jnp.where(is_c & (lc == mc), lane, big), axis=-1, keepdims=True)
        p_grp = 1.0 / jnp.sum(jnp.where(is_c, jnp.exp(lc - mc), 0.0), axis=-1, keepdims=True)
        eidx = lane - n_coarse
        sel = (eidx >= 0) & (eidx < n_fine) & ((eidx // per_group) == grp)
        lf = jnp.where(sel, logits, neg)
        v1 = jnp.max(lf, axis=-1, keepdims=True)
        i1 = jnp.min(jnp.where(sel & (lf == v1), eidx, big), axis=-1, keepdims=True)
        sel2 = sel & (eidx != i1)
        lf2 = jnp.where(sel2, logits, neg)
        v2 = jnp.max(lf2, axis=-1, keepdims=True)
        i2 = jnp.min(jnp.where(sel2 & (lf2 == v2), eidx, big), axis=-1, keepdims=True)
        e2 = jnp.exp(v2 - v1)
        w1 = p_grp / (1.0 + e2)
        w2 = p_grp * e2 / (1.0 + e2)
        wts_ref[...] = jnp.where(lane == 0, w1, jnp.where(lane == 1, w2, 0.0))

        hit1 = eidx == i1
        hit2 = eidx == i2
        hits = hit1.astype(F32) + hit2.astype(F32)
        earlier = (lax.broadcasted_iota(jnp.int32, (tm, tm), 0)
                   > lax.broadcasted_iota(jnp.int32, (tm, tm), 1)).astype(BF16)
        before = jnp.dot(earlier, hits.astype(BF16), preferred_element_type=F32) + cnt_ref[...]
        r1 = jnp.sum(jnp.where(hit1, before, 0.0), axis=-1, keepdims=True).astype(jnp.int32)
        r2 = jnp.sum(jnp.where(hit2, before, 0.0), axis=-1, keepdims=True).astype(jnp.int32)
        cnt_ref[...] += jnp.sum(hits, axis=0, keepdims=True)
        route_ref[...] = jnp.where(lane == 0, i1, jnp.where(lane == 1, i2, jnp.where(
            lane == 2, r1, jnp.where(lane == 3, r2, 0))))

    _by_stream(pl.program_id(0), prompt_tiles, run, xp_ref, xs_ref)


def _mix_route(merged, x_p, x_s, wo, norm_w, r_hi, r_lo, n_coarse, per_group):
    t_p, d = x_p.shape
    T = t_p + x_s.shape[0]
    tm = _tile(math.gcd(t_p, x_s.shape[0]), 256, 16)
    npt = t_p // tm
    row = lambda i: (i, 0)
    fixed = lambda i: (0, 0)
    xp_spec, xs_spec = _stream_specs((tm, d), npt)
    return pl.pallas_call(
        functools.partial(_mix_route_body, n_coarse=n_coarse, per_group=per_group, prompt_tiles=npt),
        grid=(T // tm,),
        in_specs=[
            pl.BlockSpec((tm, d), row), xp_spec, xs_spec, pl.BlockSpec((d, d), fixed),
            pl.BlockSpec((1, d), fixed), pl.BlockSpec((d, LANE), fixed), pl.BlockSpec((d, LANE), fixed),
        ],
        out_specs=[pl.BlockSpec((tm, d), row), pl.BlockSpec((tm, d), row),
                   pl.BlockSpec((tm, LANE), row), pl.BlockSpec((tm, LANE), row),
                   pl.BlockSpec((1, LANE), fixed)],
        out_shape=[
            jax.ShapeDtypeStruct((T, d), F32), jax.ShapeDtypeStruct((T, d), F32),
            jax.ShapeDtypeStruct((T, LANE), jnp.int32), jax.ShapeDtypeStruct((T, LANE), F32),
            jax.ShapeDtypeStruct((1, LANE), F32),
        ],
        compiler_params=pltpu.CompilerParams(
            dimension_semantics=("arbitrary",), vmem_limit_bytes=VMEM_LIMIT),
        name="mix_route",
    )(merged, x_p, x_s, wo, norm_w, r_hi, r_lo)


MOE_MOVE_TOKENS = 128
MOE_MOVE_UNROLL = 8


def _moe_scatter_body(dest_ref, h_ref, xs_in, xs_hbm, sem):
    del xs_in
    tt = h_ref.shape[0]
    t0 = pl.program_id(0) * tt

    def start(i, carry):
        for k in range(MOE_TOP_K):
            pltpu.make_async_copy(
                h_ref.at[pl.ds(i, 1)], xs_hbm.at[pl.ds(dest_ref[(t0 + i) * MOE_TOP_K + k], 1)], sem).start()
        return carry

    lax.fori_loop(0, tt, start, 0, unroll=MOE_MOVE_UNROLL)
    for _ in range(MOE_TOP_K):
        pltpu.make_async_copy(h_ref, xs_hbm.at[pl.ds(0, tt)], sem).wait()


def _moe_scatter(dest, h2, xs_zero):
    T, d = h2.shape
    tt = MOE_MOVE_TOKENS
    assert T % tt == 0
    any_spec = pl.BlockSpec(memory_space=pl.ANY)
    return pl.pallas_call(
        _moe_scatter_body,
        grid_spec=pltpu.PrefetchScalarGridSpec(
            num_scalar_prefetch=1,
            grid=(T // tt,),
            in_specs=[pl.BlockSpec((tt, d), lambda i, dr: (i, 0)), any_spec],
            out_specs=any_spec,
            scratch_shapes=[pltpu.SemaphoreType.DMA(())],
        ),
        out_shape=jax.ShapeDtypeStruct(xs_zero.shape, xs_zero.dtype),
        input_output_aliases={2: 0},
        compiler_params=pltpu.CompilerParams(dimension_semantics=("arbitrary",)),
        name="moe_scatter",
    )(dest, h2, xs_zero)


def _moe_ffn_body(blk_e_ref, n_used_ref, x_ref, wg_ref, wu_ref, wd_ref, o_ref):
    del blk_e_ref

    @pl.when(pl.program_id(0) < n_used_ref[0])
    def _():
        xb = x_ref[...].astype(BF16)
        gate = jnp.dot(xb, wg_ref[0], preferred_element_type=F32)
        up = jnp.dot(xb, wu_ref[0], preferred_element_type=F32)
        act = (_silu(gate) * up).astype(BF16)
        o_ref[...] = jnp.dot(act, wd_ref[0], preferred_element_type=F32)

    @pl.when(pl.program_id(0) >= n_used_ref[0])
    def _():
        o_ref[...] = jnp.zeros(o_ref.shape, o_ref.dtype)


def _moe_ffn(blk_e, n_used, xs, wg, wu, wd):
    R, d = xs.shape
    f = wg.shape[2]
    nb = R // MOE_ROWS
    return pl.pallas_call(
        _moe_ffn_body,
        grid_spec=pltpu.PrefetchScalarGridSpec(
            num_scalar_prefetch=2,
            grid=(nb,),
            in_specs=[
                pl.BlockSpec((MOE_ROWS, d), lambda b, be, nu: (b, 0)),
                pl.BlockSpec((1, d, f), lambda b, be, nu: (be[b], 0, 0)),
                pl.BlockSpec((1, d, f), lambda b, be, nu: (be[b], 0, 0)),
                pl.BlockSpec((1, f, d), lambda b, be, nu: (be[b], 0, 0)),
            ],
            out_specs=pl.BlockSpec((MOE_ROWS, d), lambda b, be, nu: (b, 0)),
        ),
        out_shape=jax.ShapeDtypeStruct((R, d), F32),
        compiler_params=pltpu.CompilerParams(
            dimension_semantics=("arbitrary",), vmem_limit_bytes=VMEM_LIMIT),
        name="moe_ffn",
    )(blk_e, n_used, xs, wg, wu, wd)


def _moe_combine_body(dest_ref, x1_ref, wts_ref, nw_ref, yb_hbm, op_ref, os_ref, buf, sem, *, prompt_tiles):
    tt = x1_ref.shape[0]
    t0 = pl.program_id(0) * tt

    def start(i, carry):
        for k in range(MOE_TOP_K):
            pltpu.make_async_copy(
                yb_hbm.at[pl.ds(dest_ref[(t0 + i) * MOE_TOP_K + k], 1)], buf.at[k, pl.ds(i, 1)], sem).start()
        return carry

    lax.fori_loop(0, tt, start, 0, unroll=MOE_MOVE_UNROLL)
    for k in range(MOE_TOP_K):
        pltpu.make_async_copy(yb_hbm.at[pl.ds(0, tt)], buf.at[k], sem).wait()

    def finish(o_ref):
        w = wts_ref[...]
        x2 = x1_ref[...] + (buf[0] * w[:, 0:1] + buf[1] * w[:, 1:2])
        o_ref[...] = x2 * lax.rsqrt(jnp.mean(x2 * x2, axis=-1, keepdims=True) + NORM_EPS) * nw_ref[...]

    _by_stream(pl.program_id(0), prompt_tiles, finish, op_ref, os_ref)


def _moe_combine(dest, x1, wts, norm_w, yb, t_p):
    T, d = x1.shape
    tt = MOE_MOVE_TOKENS
    assert t_p % tt == 0 and T % tt == 0
    npt = t_p // tt
    op_spec, os_spec = _stream_specs((tt, d), npt)
    return pl.pallas_call(
        functools.partial(_moe_combine_body, prompt_tiles=npt),
        grid_spec=pltpu.PrefetchScalarGridSpec(
            num_scalar_prefetch=1,
            grid=(T // tt,),
            in_specs=[
                pl.BlockSpec((tt, d), lambda i, dr: (i, 0)),
                pl.BlockSpec((tt, LANE), lambda i, dr: (i, 0)),
                pl.BlockSpec((1, d), lambda i, dr: (0, 0)),
                pl.BlockSpec(memory_space=pl.ANY),
            ],
            out_specs=[op_spec, os_spec],
            scratch_shapes=[pltpu.VMEM((MOE_TOP_K, tt, d), F32), pltpu.SemaphoreType.DMA(())],
        ),
        out_shape=[jax.ShapeDtypeStruct((t_p, d), F32), jax.ShapeDtypeStruct((T - t_p, d), F32)],
        compiler_params=pltpu.CompilerParams(dimension_semantics=("arbitrary",)),
        name="moe_combine",
    )(dest, x1, wts, norm_w, yb)


def _route_rows(eid, rank, counts, n_assign):
    n_experts = counts.shape[0]
    padded = (counts + MOE_ROWS - 1) // MOE_ROWS * MOE_ROWS
    pend = jnp.cumsum(padded)
    pstart = pend - padded
    onehot = eid[:, :, None] == jnp.arange(n_experts, dtype=jnp.int32)[None, None, :]
    dest = (jnp.sum(jnp.where(onehot, pstart[None, None, :], 0), axis=-1) + rank).astype(jnp.int32)
    n_blocks = -(-n_assign // MOE_ROWS) + n_experts
    blk_start = jnp.arange(n_blocks, dtype=jnp.int32) * MOE_ROWS
    blk_e = jnp.sum((blk_start[:, None] >= pend[None, :]).astype(jnp.int32), axis=1)
    n_used = (pend[-1] // MOE_ROWS).astype(jnp.int32).reshape(1)
    last_e = jnp.sum((jnp.maximum(pend[-1] - 1, 0) >= pend).astype(jnp.int32))
    blk_e = jnp.where(blk_start < pend[-1], blk_e, last_e).astype(jnp.int32)
    return dest.reshape(-1), blk_e, n_used, n_blocks


def _pad_lanes(v, fill=0.0):
    return jnp.pad(v.astype(F32), (0, LANE - v.shape[0]), constant_values=fill).reshape(1, LANE)


def kernel(x_prompt, x_sample, state_ssm, state_ssd_conv, state_short_conv, norm_mixer, w_in, ssd_conv_w,
           ssd_conv_b, ssd_dt_bias, ssd_a_log, ssd_d, ssd_norm, sc_conv_w, w_branch_out, w_out, norm_ffn,
           w_router_coarse, w_router_fine, w_expert_gate, w_expert_up, w_expert_down, norm_final):
    depth = w_in.shape[0]
    assert depth == 1
    n_p, seq_p, d = x_prompt.shape
    n_s, seq_s, _ = x_sample.shape
    d_inner = 2 * d
    n_heads = d_inner // HEAD_DIM
    n_groups = d_inner // GROUP_W
    gn = n_groups * STATE_DIM
    conv_dim = d_inner + 2 * gn
    assert conv_dim == 3 * d and n_heads <= LANE and ssd_conv_w.shape[2] == conv_dim
    t_p, t_s = n_p * seq_p, n_s * seq_s
    n_coarse = w_router_coarse.shape[2]
    n_experts = w_router_fine.shape[2]
    assert n_coarse + n_experts <= LANE

    off_dt = 2 * d + d_inner + conv_dim
    off_sc = off_dt + n_heads
    w_in0 = w_in[0]
    w_tail = w_in0[:, off_sc:]
    w_dt =jnp.pad(w_in0[:, off_dt:off_sc], ((0, 0), (0, LANE - n_heads))).astype(BF16)
    head_of_col = jnp.arange(d_inner, dtype=jnp.int32) // HEAD_DIM
    expand = (jnp.arange(LANE, dtype=jnp.int32)[:, None] == head_of_col[None, :]).astype(BF16)
    group_of_n = jnp.arange(gn, dtype=jnp.int32) // STATE_DIM
    group_of_head = jnp.arange(LANE, dtype=jnp.int32) // HEADS_PER_GROUP
    gsel = ((group_of_n[:, None] == group_of_head[None, :])
            & (jnp.arange(LANE)[None, :] < n_heads)).astype(BF16)
    consts = (
        ssd_conv_w[0], ssd_conv_b[0].reshape(1, conv_dim), _pad_lanes(ssd_dt_bias[0]), _pad_lanes(ssd_a_log[0]),
        jnp.repeat(ssd_d[0].astype(F32), HEAD_DIM).reshape(1, d_inner), ssd_norm[0].reshape(1, d_inner),
        sc_conv_w[0], expand,
    )
    wbo = w_branch_out[0].astype(BF16)
    wa, wb = wbo[:d_inner], wbo[d_inner:]
    wo = w_out[0].astype(BF16)
    w_router = jnp.pad(jnp.concatenate([w_router_coarse[0], w_router_fine[0]], axis=1),
                       ((0, 0), (0, LANE - n_coarse - n_experts)))
    r_hi = w_router.astype(BF16)
    r_lo = (w_router - r_hi.astype(F32)).astype(BF16)
    wg = w_expert_gate[0].astype(BF16)
    wu = w_expert_up[0].astype(BF16)
    wd = w_expert_down[0].astype(BF16)

    x_p = x_prompt.reshape(t_p, d)
    x_s = x_sample.reshape(t_s, d)
    h, dt_raw = _prenorm(x_p, x_s, norm_mixer[0].reshape(1, d), w_dt)
    p, tails = _inproj(h, w_in0, w_tail, off_dt, consts[0], consts[1], t_p, seq_p)
    ya_p, yb_p, p_ssm, p_sc_tail = _ssd_prompt(p, dt_raw, consts, n_p, seq_p, d)
    ya_s, yb_s, s_ssm, s_conv, s_sc = _ssd_sample(
        p, dt_raw, state_ssm[0].reshape(n_s, n_groups, GROUP_W, STATE_DIM), state_ssd_conv[0],
        state_short_conv[0], consts, gsel, t_p, seq_s, d)
    merged = _branch_out(ya_p, yb_p, ya_s, yb_s, p, wa, wb)
    x1, h2, route, wts, counts = _mix_route(merged, x_p, x_s, wo, norm_ffn[0].reshape(1, d), r_hi, r_lo,
                                            n_coarse, n_experts // n_coarse)

    n_assign = (t_p + t_s) * MOE_TOP_K
    dest, blk_e, n_used, n_blocks = _route_rows(
        route[:, 0:MOE_TOP_K], route[:, MOE_TOP_K:2 * MOE_TOP_K],
        counts[0, n_coarse:n_coarse + n_experts].astype(jnp.int32), n_assign)
    xs = _moe_scatter(dest, h2, jnp.zeros((n_blocks * MOE_ROWS, d), F32))
    yrows = _moe_ffn(blk_e, n_used, xs, wg, wu, wd)
    out_p, out_s = _moe_combine(dest, x1, wts, norm_final.reshape(1, d), yrows, t_p)

    kw = ssd_conv_w.shape[1]
    kw2 = sc_conv_w.shape[1]
    tiles_per_seq = (tails.shape[0] // SUBLANE) * seq_p // (t_p + t_s)
    seq_tails = tails.reshape(-1, SUBLANE, tails.shape[1])[tiles_per_seq - 1:n_p * tiles_per_seq:tiles_per_seq]
    p_conv = seq_tails[:, SUBLANE - (kw - 1):, 4 * d:7 * d]
    return (
        out_p.reshape(n_p, seq_p, d),
        out_s.reshape(n_s, seq_s, d),
        p_ssm.reshape(1, n_p, n_heads, HEAD_DIM, STATE_DIM),
        p_conv[None],
        p_sc_tail[:, SUBLANE - (kw2 - 1):, :][None],
        s_ssm.reshape(1, n_s, n_heads, HEAD_DIM, STATE_DIM),
        s_conv[None],
        s_sc[None],
    )
```

```python
import functools
import math

import jax
import jax.numpy as jnp
from jax import lax
from jax.experimental import pallas as pl
from jax.experimental.pallas import tpu as pltpu

F32 = jnp.float32
BF16 = jnp.bfloat16

NORM_EPS = 1e-6
SSD_NORM_EPS = 1e-5
HEAD_DIM = 64
STATE_DIM = 128
HEADS_PER_GROUP = 8
GROUP_W = HEADS_PER_GROUP * HEAD_DIM
SSD_CHUNK = 128
MOE_TOP_K = 2
MOE_ROWS = 256

LANE = 128
SUBLANE = 8
VMEM_LIMIT = 56 * 1024 * 1024

NT_DIMS = (((1,), (1,)), ((), ()))
TN_DIMS = (((0,), (0,)), ((), ()))


def _tile(n, target, align):
    best = None
    for t in range(align, min(n, target) + 1, align):
        if n % t == 0:
            best = t
    assert best is not None, (n, target, align)
    return best


def _split2(v):
    hi = v.astype(BF16)
    lo = (v - hi.astype(F32)).astype(BF16)
    return hi, lo


def _split3(v):
    hi = v.astype(BF16)
    r = v - hi.astype(F32)
    mid = r.astype(BF16)
    lo = (r - mid.astype(F32)).astype(BF16)
    return hi, mid, lo


def _softplus(x):
    return jnp.maximum(x, 0.0) + jnp.log1p(jnp.exp(-jnp.abs(x)))


def _silu(x):
    return x * jax.nn.sigmoid(x)


def _by_stream(tile, prompt_tiles, fn, prompt_ref, sample_ref):
    @pl.when(tile < prompt_tiles)
    def _():
        fn(prompt_ref)

    @pl.when(tile >= prompt_tiles)
    def _():
        fn(sample_ref)


def _stream_specs(block, prompt_tiles, **kwargs):
    pad = (0,) * (len(block) - 1)
    prompt = lambda i, *_: (jnp.minimum(i, prompt_tiles - 1),) + pad
    sample = lambda i, *_: (jnp.maximum(i - prompt_tiles, 0),) + pad
    return pl.BlockSpec(block, prompt, **kwargs), pl.BlockSpec(block, sample, **kwargs)


def _prenorm_body(xp_ref, xs_ref, nw_ref, wdt_ref, h_ref, dt_ref, *, prompt_tiles):
    def run(x_ref):
        x = x_ref[...]
        h = x * lax.rsqrt(jnp.mean(x * x, axis=-1, keepdims=True) + NORM_EPS) * nw_ref[...]
        hb = h.astype(BF16)
        h_ref[...] = hb
        dt_ref[...] = jnp.dot(hb, wdt_ref[...], preferred_element_type=F32)

    _by_stream(pl.program_id(0), prompt_tiles, run, xp_ref, xs_ref)


def _prenorm(x_p, x_s, norm_w, w_dt):
    t_p, D = x_p.shape
    T = t_p + x_s.shape[0]
    tm = _tile(math.gcd(t_p, x_s.shape[0]), 512, 16)
    npt = t_p // tm
    xp_spec, xs_spec = _stream_specs((tm, D), npt)
    return pl.pallas_call(
        functools.partial(_prenorm_body, prompt_tiles=npt),
        grid=(T // tm,),
        in_specs=[xp_spec, xs_spec, pl.BlockSpec((1, D), lambda i: (0, 0)),
                  pl.BlockSpec((D, LANE), lambda i: (0, 0))],
        out_specs=[pl.BlockSpec((tm, D), lambda i: (i, 0)), pl.BlockSpec((tm, LANE), lambda i: (i, 0))],
        out_shape=[jax.ShapeDtypeStruct((T, D), BF16), jax.ShapeDtypeStruct((T, LANE), F32)],
        compiler_params=pltpu.CompilerParams(
            dimension_semantics=("arbitrary",), vmem_limit_bytes=VMEM_LIMIT),
        name="prenorm",
    )(x_p, x_s, norm_w, w_dt)


def _inproj_body(h_ref, wa_ref, wb_ref, p_ref, w_scr, *, main_tiles):
    @pl.when(pl.program_id(1) == 0)
    def _():
        def cast(w_ref):
            w_scr[...] = w_ref[...].astype(BF16)

        _by_stream(pl.program_id(0), main_tiles, cast, wa_ref, wb_ref)

    p_ref[...] = lax.dot_general(h_ref[...], w_scr[...], NT_DIMS, preferred_element_type=F32)


def _inproj(h, w_in_t, w_tail_t, n_head_cols):
    T, D = h.shape
    tm = _tile(T, 1024, 16)
    tn = _tile(math.gcd(n_head_cols, w_tail_t.shape[0]), 1024, LANE)
    main_tiles = n_head_cols // tn
    n_out = n_head_cols + w_tail_t.shape[0]
    return pl.pallas_call(
        functools.partial(_inproj_body, main_tiles=main_tiles),
        grid=(n_out // tn, T // tm),
        in_specs=[
            pl.BlockSpec((tm, D), lambda j, i: (i, 0)),
            pl.BlockSpec((None, tn, D), lambda j, i: (0, jnp.minimum(j, main_tiles - 1), 0)),
            pl.BlockSpec((tn, D), lambda j, i: (jnp.maximum(j - main_tiles, 0), 0),
                         pipeline_mode=pl.Buffered(1)),
        ],
        out_specs=pl.BlockSpec((tm, tn), lambda j, i: (i, j)),
        out_shape=jax.ShapeDtypeStruct((T, n_out), F32),
        scratch_shapes=[pltpu.VMEM((tn, D), BF16)],
        compiler_params=pltpu.CompilerParams(
            dimension_semantics=("arbitrary", "arbitrary"), vmem_limit_bytes=VMEM_LIMIT),
        name="inproj",
    )(h, w_in_t, w_tail_t)


def _conv(ext, lo, hi, q, width, w_ref, first):
    acc = None
    for k in range(width):
        term = w_ref[k:k + 1, lo:hi] * ext[first + k:first + k + q, lo:hi]
        acc = term if acc is None else acc + term
    return acc


def _gated_norm(y, z, nw):
    g = y * _silu(z)
    return g * lax.rsqrt(jnp.mean(g * g, axis=-1, keepdims=True) + SSD_NORM_EPS) * nw


def _decay_col(cs_last_row, g):
    d = jnp.exp(cs_last_row)
    parts = [
        jnp.broadcast_to(d[0:1, g * HEADS_PER_GROUP + j:g * HEADS_PER_GROUP + j + 1], (HEAD_DIM, STATE_DIM))
        for j in range(HEADS_PER_GROUP)
    ]
    return jnp.concatenate(parts, axis=0)


def _ssd_prompt_body(x4, x5, x6, z2, z3, scb, scc, sch, dtr,
                     cw, cb, dtb, alog, dskip, nw, scw, expand,
                     ya_ref, yb_ref, st_ref, cvt_ref, sct_ref, ext, ext2):
    q, d = x4.shape
    n_groups = st_ref.shape[1]
    gn = n_groups * STATE_DIM
    kw = cw.shape[0]
    kw2 = scw.shape[0]

    @pl.when(pl.program_id(1) == 0)
    def _():
        st_ref[...] = jnp.zeros(st_ref.shape, F32)
        ext[0:SUBLANE, :] = jnp.zeros((SUBLANE, 3 * d), F32)
        ext2[0:SUBLANE, :] = jnp.zeros((SUBLANE, d), F32)

    ext[SUBLANE:SUBLANE + q, 0:d] = x4[...]
    ext[SUBLANE:SUBLANE + q, d:2 * d] = x5[...]
    ext[SUBLANE:SUBLANE + q, 2 * d:3 * d] = x6[...]
    first = SUBLANE - (kw - 1)

    ext2[SUBLANE:SUBLANE + q, :] = scc[...] * sch[...]
    v = _conv(ext2, 0, d, q, kw2, scw, SUBLANE - (kw2 - 1))
    yb_ref[...] = (scb[...] * v).astype(yb_ref.dtype)
    tail2 = ext2[q:q + SUBLANE, :]
    ext2[0:SUBLANE, :] = tail2
    sct_ref[0] = tail2

    dt = _softplus(dtr[...] + dtb[...])
    a = dt * (-jnp.exp(alog[...]))
    row = lax.broadcasted_iota(jnp.int32, (q, q), 0)
    col = lax.broadcasted_iota(jnp.int32, (q, q), 1)
    causal = row >= col
    tri = causal.astype(BF16)
    cs = sum(jnp.dot(tri, part, preferred_element_type=F32) for part in _split3(a))
    cs_t = cs.T
    cs_last = cs[q - 1:q, :]
    dend = jnp.exp(cs_last - cs)
    ecs = jnp.exp(cs)
    stacked = jnp.concatenate([dt, dend, ecs], axis=0)
    st_hi, st_lo = _split2(stacked)

    bmat = _silu(_conv(ext, 2 * d, 2 * d + gn, q, kw, cw, first) + cb[:, 2 * d:2 * d + gn])
    cmat = _silu(_conv(ext, 2 * d + gn, 3 * d, q, kw, cw, first) + cb[:, 2 * d + gn:3 * d])
    lane = lax.broadcasted_iota(jnp.int32, (q, LANE), 1)
    groups_per_block = d // GROUP_W

    for g in range(n_groups):
        c0 = g * GROUP_W
        xs = _silu(_conv(ext, c0, c0 + GROUP_W, q, kw, cw, first) + cb[:, c0:c0 + GROUP_W])
        e_g = expand[:, c0:c0 + GROUP_W]
        ex = (jnp.dot(st_hi, e_g, preferred_element_type=F32)
              + jnp.dot(st_lo, e_g, preferred_element_type=F32))
        xdt = xs * ex[0:q]
        xdt_b = xdt.astype(BF16)
        xdd_b = (xdt * ex[q:2 * q]).astype(BF16)
        bg = bmat[:, g * STATE_DIM:(g + 1) * STATE_DIM].astype(BF16)
        cg = cmat[:, g * STATE_DIM:(g + 1) * STATE_DIM].astype(BF16)
        cbm = lax.dot_general(cg, bg, NT_DIMS, preferred_element_type=F32)
        state = st_ref[0, g]
        y_off = lax.dot_general(cg, state.astype(BF16), NT_DIMS, preferred_element_type=F32)
        y_parts = []
        for j in range(HEADS_PER_GROUP // 2):
            scores = []
            for h in (g * HEADS_PER_GROUP + 2 * j, g * HEADS_PER_GROUP + 2 * j + 1):
                seg = cs[:, h:h + 1] - cs_t[h:h + 1, :]
                dec = jnp.exp(jnp.where(causal, seg, -jnp.inf))
                scores.append((cbm * dec).astype(BF16))
            xp = xdt_b[:, j * LANE:(j + 1) * LANE]
            zero = jnp.zeros_like(xp)
            rhs = jnp.concatenate(
                [jnp.where(lane < HEAD_DIM, xp, zero), jnp.where(lane >= HEAD_DIM, xp, zero)], axis=0)
            y_parts.append(jnp.dot(jnp.concatenate(scores, axis=1), rhs, preferred_element_type=F32))
        y = jnp.concatenate(y_parts, axis=1) + y_off * ex[2 * q:3 * q] + xs * dskip[:, c0:c0 + GROUP_W]
        zref = z2 if g < groups_per_block else z3
        zc = (g % groups_per_block) * GROUP_W
        ya_ref[:, c0:c0 + GROUP_W] = _gated_norm(
            y, zref[:, zc:zc + GROUP_W], nw[:, c0:c0 + GROUP_W]).astype(ya_ref.dtype)
        st_ref[0, g] = state * _decay_col(cs_last, g) + lax.dot_general(
            xdd_b, bg, TN_DIMS, preferred_element_type=F32)

    tail = ext[q:q + SUBLANE, :]
    ext[0:SUBLANE, :] = tail
    cvt_ref[0] = tail


def _ssd_prompt(p, dt_raw, consts, n_seq, seq_len, d):
    (cw, cb, dtb, alog, dskip, nw, scw, expand) = consts
    q = SSD_CHUNK if seq_len % SSD_CHUNK == 0 else seq_len
    nc = seq_len // q
    n_groups = 2 * d // GROUP_W
    t_p = n_seq * seq_len

    def blk(cidx):
        return pl.BlockSpec((q, d), lambda b, c, cidx=cidx: (b * nc + c, cidx))

    def const(arr):
        return pl.BlockSpec(arr.shape, lambda b, c: (0,) * arr.ndim)

    return pl.pallas_call(
        _ssd_prompt_body,
        grid=(n_seq, nc),
        in_specs=[blk(4), blk(5), blk(6), blk(2), blk(3), blk(7), blk(8), blk(9),
                  pl.BlockSpec((q, LANE), lambda b, c: (b * nc + c, 0)),
                  const(cw), const(cb), const(dtb), const(alog), const(dskip), const(nw), const(scw),
                  const(expand)],
        out_specs=[
            pl.BlockSpec((q, 2 * d), lambda b, c: (b * nc + c, 0)),
            pl.BlockSpec((q, d), lambda b, c: (b * nc + c, 0)),
            pl.BlockSpec((1, n_groups, GROUP_W, STATE_DIM), lambda b, c: (b, 0, 0, 0)),
            pl.BlockSpec((1, SUBLANE, 3 * d), lambda b, c: (b, 0, 0)),
            pl.BlockSpec((1, SUBLANE, d), lambda b, c: (b, 0, 0)),
        ],
        out_shape=[
            jax.ShapeDtypeStruct((t_p, 2 * d), BF16),
            jax.ShapeDtypeStruct((t_p, d), BF16),
            jax.ShapeDtypeStruct((n_seq, n_groups, GROUP_W, STATE_DIM), F32),
            jax.ShapeDtypeStruct((n_seq, SUBLANE, 3 * d), F32),
            jax.ShapeDtypeStruct((n_seq, SUBLANE, d), F32),
        ],
        scratch_shapes=[pltpu.VMEM((q + SUBLANE, 3 * d), F32), pltpu.VMEM((q + SUBLANE, d), F32)],
        compiler_params=pltpu.CompilerParams(
            dimension_semantics=("arbitrary", "arbitrary"), vmem_limit_bytes=VMEM_LIMIT),
        name="ssd_prompt",
    )(p, p, p, p, p, p, p, p, dt_raw, cw, cb, dtb, alog, dskip, nw, scw, expand)


SAMPLE_SEQS_PER_STEP = 2


def _ssd_sample_body(x4, x5, x6, z2, z3, scb, scc, sch, dtr, ssm_in, conv_in, sc_in,
                     cw, cb, dtb, alog, dskip, nw, scw, expand, gsel,
                     ya_ref, yb_ref, ssm_out, conv_out, sc_out, ext, ext2):
    d = x4.shape[1]
    n_groups = ssm_in.shape[1]
    gn = n_groups * STATE_DIM
    kw = cw.shape[0]
    kw2 = scw.shape[0]
    q = x4.shape[0] // SAMPLE_SEQS_PER_STEP
    groups_per_block = d // GROUP_W
    first = SUBLANE - (kw - 1)
    first2 = SUBLANE - (kw2 - 1)
    nrep = q * q

    rep_t = lax.broadcasted_iota(jnp.int32, (nrep, LANE), 0) % q
    rep_s = lax.broadcasted_iota(jnp.int32, (nrep, LANE), 0) // q
    rep_causal = rep_t >= rep_s
    row_q = lax.broadcasted_iota(jnp.int32, (q, LANE), 0)

    def rep_rows(m):
        return jnp.concatenate([jnp.broadcast_to(m[s:s + 1], (q, m.shape[1])) for s in range(q)], axis=0)

    def tile_rows(m):
        return jnp.concatenate([m] * q, axis=0)

    yb_rows = []
    ya_rows = [[] for _ in range(n_groups)]
    for sidx in range(SAMPLE_SEQS_PER_STEP):
        r0 = sidx * q
        ext[first:SUBLANE, :] = conv_in[sidx]
        ext[SUBLANE:SUBLANE + q, 0:d] = x4[r0:r0 + q, :]
        ext[SUBLANE:SUBLANE + q, d:2 * d] = x5[r0:r0 + q, :]
        ext[SUBLANE:SUBLANE + q, 2 * d:3 * d] = x6[r0:r0 + q, :]
        conv_out[sidx] = ext[SUBLANE + q - (kw - 1):SUBLANE + q, :]

        ext2[first2:SUBLANE, :] = sc_in[sidx]
        ext2[SUBLANE:SUBLANE + q, :] = scc[r0:r0 + q, :] * sch[r0:r0 + q, :]
        yb_rows.append(scb[r0:r0 + q, :] * _conv(ext2, 0, d, q, kw2, scw, first2))
        sc_out[sidx] = ext2[SUBLANE + q - (kw2 - 1):SUBLANE + q, :]

        dt = _softplus(dtr[r0:r0 + q, :] + dtb[...])
        a = dt * (-jnp.exp(alog[...]))
        cs = jnp.zeros((q, LANE), F32)
        for r in range(q):
            cs = cs + jnp.where(row_q >= r, jnp.broadcast_to(a[r:r + 1], (q, LANE)), 0.0)
        cs_last = cs[q - 1:q, :]
        dend = jnp.exp(cs_last - cs)
        ecs = jnp.exp(cs)

        bmat = _silu(_conv(ext, 2 * d, 2 * d + gn, q, kw, cw, first) + cb[:, 2 * d:2 * d + gn])
        cmat = _silu(_conv(ext, 2 * d + gn, 3 * d, q, kw, cw, first) + cb[:, 2 * d + gn:3 * d])

        cb_hi, cb_lo = _split2(tile_rows(cmat) * rep_rows(bmat))
        cbh = (jnp.dot(cb_hi, gsel[...], preferred_element_type=F32)
               + jnp.dot(cb_lo, gsel[...], preferred_element_type=F32))
        dec = jnp.exp(jnp.where(rep_causal, tile_rows(cs) - rep_rows(cs), -jnp.inf))
        stacked = jnp.concatenate([dt, dend, ecs], axis=0)
        st_hi = stacked.astype(BF16).astype(F32)
        pad = jnp.zeros((LANE - nrep - 6 * q, LANE), F32)
        lhs = jnp.concatenate([cbh * dec, st_hi, stacked - st_hi, pad], axis=0).astype(BF16)

        for g in range(n_groups):
            c0 = g * GROUP_W
            xs = _silu(_conv(ext, c0, c0 + GROUP_W, q, kw, cw, first) + cb[:, c0:c0 + GROUP_W])
            ex = jnp.dot(lhs, expand[:, c0:c0 + GROUP_W], preferred_element_type=F32)
            o = nrep
            dtx = ex[o:o + q] + ex[o + 3 * q:o + 4 * q]
            dendx = ex[o + q:o + 2 * q] + ex[o + 4 * q:o + 5 * q]
            ecsx = ex[o + 2 * q:o + 3 * q] + ex[o + 5 * q:o + 6 * q]
            xdt = xs * dtx
            xdd = xdt * dendx
            y = xs * dskip[:, c0:c0 + GROUP_W]
            for s in range(q):
                y = y + ex[s * q:(s + 1) * q] * jnp.broadcast_to(xdt[s:s + 1], (q, GROUP_W))
            state = ssm_in[sidx, g]
            cg = cmat[:, g * STATE_DIM:(g + 1) * STATE_DIM]
            bg = bmat[:, g * STATE_DIM:(g + 1) * STATE_DIM]
            y = y + lax.dot_general(cg, state, NT_DIMS, preferred_element_type=F32) * ecsx
            zref = z2 if g < groups_per_block else z3
            zc = (g % groups_per_block) * GROUP_W
            ya_rows[g].append(_gated_norm(y, zref[r0:r0 + q, zc:zc + GROUP_W], nw[:, c0:c0 + GROUP_W]))
            ssm_out[sidx, g] = state * _decay_col(cs_last, g) + lax.dot_general(
                xdd, bg, TN_DIMS, preferred_element_type=F32)

    yb_ref[...] = jnp.concatenate(yb_rows, axis=0).astype(yb_ref.dtype)
    for g in range(n_groups):
        ya_ref[:, g * GROUP_W:(g + 1) * GROUP_W] = jnp.concatenate(ya_rows[g], axis=0).astype(ya_ref.dtype)


def _ssd_sample(p, dt_raw, ssm, conv_state, sc_state, consts, gsel, row0, seq_len, d):
    (cw, cb, dtb, alog, dskip, nw, scw, expand) = consts
    n_seq = ssm.shape[0]
    sp = SAMPLE_SEQS_PER_STEP
    rows = sp * seq_len
    assert seq_len == SUBLANE and n_seq % sp == 0 and row0 % rows == 0
    b0 = row0 // rows

    def blk(cidx):
        return pl.BlockSpec((rows, d), lambda i, cidx=cidx: (b0 + i, cidx))

    def const(arr):
        return pl.BlockSpec(arr.shape, lambda i: (0,) * arr.ndim)

    def per_seq(arr):
        return pl.BlockSpec((sp,) + arr.shape[1:], lambda i: (i,) + (0,) * (arr.ndim - 1))

    return pl.pallas_call(
        _ssd_sample_body,
        grid=(n_seq // sp,),
        in_specs=[blk(4), blk(5), blk(6), blk(2), blk(3), blk(7), blk(8), blk(9),
                  pl.BlockSpec((rows, LANE), lambda i: (b0 + i, 0)),
                  per_seq(ssm), per_seq(conv_state), per_seq(sc_state),
                  const(cw), const(cb), const(dtb), const(alog), const(dskip), const(nw), const(scw),
                  const(expand), const(gsel)],
        out_specs=[
            pl.BlockSpec((rows, 2 * d), lambda i: (i, 0)),
            pl.BlockSpec((rows, d), lambda i: (i, 0)),
            per_seq(ssm), per_seq(conv_state), per_seq(sc_state),
        ],
        out_shape=[
            jax.ShapeDtypeStruct((n_seq * seq_len, 2 * d), BF16),
            jax.ShapeDtypeStruct((n_seq * seq_len, d), BF16),
            jax.ShapeDtypeStruct(ssm.shape, F32),
            jax.ShapeDtypeStruct(conv_state.shape, F32),
            jax.ShapeDtypeStruct(sc_state.shape, F32),
        ],
        scratch_shapes=[pltpu.VMEM((2 * SUBLANE, 3 * d), F32), pltpu.VMEM((2 * SUBLANE, d), F32)],
        compiler_params=pltpu.CompilerParams(
            dimension_semantics=("arbitrary",), vmem_limit_bytes=VMEM_LIMIT),
        name="ssd_sample",
    )(p, p, p, p, p, p, p, p, dt_raw, ssm, conv_state, sc_state,
      cw, cb, dtb, alog, dskip, nw, scw, expand, gsel)


def _branch_out_body(yap_ref, ybp_ref, yas_ref, ybs_ref, ga_ref, gb_ref, wa_ref, wb_ref, o_ref, *, prompt_tiles):
    def run(refs):
        ya_ref, yb_ref = refs
        pa = jnp.dot(ya_ref[...], wa_ref[...], preferred_element_type=F32)
        pb = jnp.dot(yb_ref[...], wb_ref[...], preferred_element_type=F32)
        merged = jax.nn.sigmoid(ga_ref[...]) * pa + jax.nn.sigmoid(gb_ref[...]) * pb
        o_ref[...] = merged.astype(o_ref.dtype)

    _by_stream(pl.program_id(0), prompt_tiles, run, (yap_ref, ybp_ref), (yas_ref, ybs_ref))


def _branch_out(ya_p, yb_p, ya_s, yb_s, p, wa, wb):
    t_p, d = yb_p.shape
    t_s = yb_s.shape[0]
    tm = _tile(math.gcd(t_p, t_s), 512, 16)
    tn = _tile(d, 512, LANE)
    nj = d // tn
    npt = t_p // tm
    yap_spec, yas_spec = _stream_specs((tm, 2 * d), npt)
    ybp_spec, ybs_spec = _stream_specs((tm, d), npt)
    return pl.pallas_call(
        functools.partial(_branch_out_body, prompt_tiles=npt),
        grid=((t_p + t_s) // tm, nj),
        in_specs=[
            yap_spec, ybp_spec, yas_spec, ybs_spec,
            pl.BlockSpec((tm, tn), lambda i, j: (i, j)),
            pl.BlockSpec((tm, tn), lambda i, j: (i, nj + j)),
            pl.BlockSpec((2 * d, tn), lambda i, j: (0, j)),
            pl.BlockSpec((d, tn), lambda i, j: (0, j)),
        ],
        out_specs=pl.BlockSpec((tm, tn), lambda i, j: (i, j)),
        out_shape=jax.ShapeDtypeStruct((t_p + t_s, d), BF16),
        compiler_params=pltpu.CompilerParams(
            dimension_semantics=("arbitrary", "arbitrary"), vmem_limit_bytes=VMEM_LIMIT),
        name="branch_out",
    )(ya_p, yb_p, ya_s, yb_s, p, p, wa, wb)


def _mix_route_body(m_ref, xp_ref, xs_ref, wo_ref, nw_ref, rhi_ref, rlo_ref,
                    x1_ref, h2_ref, route_ref, wts_ref, cnt_ref, *, n_coarse, per_group, prompt_tiles):
    @pl.when(pl.program_id(0) == 0)
    def _():
        cnt_ref[...] = jnp.zeros(cnt_ref.shape, F32)

    def run(x_ref):
        x1 = x_ref[...] + jnp.dot(m_ref[...], wo_ref[...], preferred_element_type=F32)
        x1_ref[...] = x1
        h2 = x1 * lax.rsqrt(jnp.mean(x1 * x1, axis=-1, keepdims=True) + NORM_EPS) * nw_ref[...]
        h2_ref[...] = h2
        h_hi, h_lo = _split2(h2)
        logits = (jnp.dot(h_hi, rhi_ref[...], preferred_element_type=F32)
                  + jnp.dot(h_hi, rlo_ref[...], preferred_element_type=F32)
                  + jnp.dot(h_lo, rhi_ref[...], preferred_element_type=F32))

        tm = logits.shape[0]
        n_fine = n_coarse * per_group
        lane = lax.broadcasted_iota(jnp.int32, logits.shape, 1)
        big = jnp.int32(LANE)
        neg = -jnp.inf
        is_c = lane < n_coarse
        lc = jnp.where(is_c, logits, neg)
        mc = jnp.max(lc, axis=-1, keepdims=True)
        grp = jnp.min(jnp.where(is_c & (lc == mc), lane, big), axis=-1, keepdims=True)
        p_grp = 1.0 / jnp.sum(jnp.where(is_c, jnp.exp(lc - mc), 0.0), axis=-1, keepdims=True)
        eidx = lane - n_coarse
        sel = (eidx >= 0) & (eidx < n_fine) & ((eidx // per_group) == grp)
        lf = jnp.where(sel, logits, neg)
        v1 = jnp.max(lf, axis=-1, keepdims=True)
        i1 = jnp.min(jnp.where(sel & (lf == v1), eidx, big), axis=-1, keepdims=True)
        sel2 = sel & (eidx != i1)
        lf2 = jnp.where(sel2, logits, neg)
        v2 = jnp.max(lf2, axis=-1, keepdims=True)
        i2 = jnp.min(jnp.where(sel2 & (lf2 == v2), eidx, big), axis=-1, keepdims=True)
        e2 = jnp.exp(v2 - v1)
        w1 = p_grp / (1.0 + e2)
        w2 = p_grp * e2 / (1.0 + e2)
        wts_ref[...] = jnp.where(lane == 0, w1, jnp.where(lane == 1, w2, 0.0))

        hit1 = eidx == i1
        hit2 = eidx == i2
        hits = hit1.astype(F32) + hit2.astype(F32)
        earlier = (lax.broadcasted_iota(jnp.int32, (tm, tm), 0)
                   > lax.broadcasted_iota(jnp.int32, (tm, tm), 1)).astype(BF16)
        before = jnp.dot(earlier, hits.astype(BF16), preferred_element_type=F32) + cnt_ref[...]
        r1 = jnp.sum(jnp.where(hit1, before, 0.0), axis=-1, keepdims=True).astype(jnp.int32)
        r2 = jnp.sum(jnp.where(hit2, before, 0.0), axis=-1, keepdims=True).astype(jnp.int32)
        cnt_ref[...] += jnp.sum(hits, axis=0, keepdims=True)
        route_ref[...] = jnp.where(lane == 0, i1, jnp.where(lane == 1, i2, jnp.where(
            lane == 2, r1, jnp.where(lane == 3, r2, 0))))

    _by_stream(pl.program_id(0), prompt_tiles, run, xp_ref, xs_ref)


def _mix_route(merged, x_p, x_s, wo, norm_w, r_hi, r_lo, n_coarse, per_group):
    t_p, d = x_p.shape
    T = t_p + x_s.shape[0]
    tm = _tile(math.gcd(t_p, x_s.shape[0]), 256, 16)
    npt = t_p // tm
    row = lambda i: (i, 0)
    fixed = lambda i: (0, 0)
    xp_spec, xs_spec = _stream_specs((tm, d), npt)
    return pl.pallas_call(
        functools.partial(_mix_route_body, n_coarse=n_coarse, per_group=per_group, prompt_tiles=npt),
        grid=(T // tm,),
        in_specs=[
            pl.BlockSpec((tm, d), row), xp_spec, xs_spec, pl.BlockSpec((d, d), fixed),
            pl.BlockSpec((1, d), fixed), pl.BlockSpec((d, LANE), fixed), pl.BlockSpec((d, LANE), fixed),
        ],
        out_specs=[pl.BlockSpec((tm, d), row), pl.BlockSpec((tm, d), row),
                   pl.BlockSpec((tm, LANE), row), pl.BlockSpec((tm, LANE), row),
                   pl.BlockSpec((1, LANE), fixed)],
        out_shape=[
            jax.ShapeDtypeStruct((T, d), F32), jax.ShapeDtypeStruct((T, d), F32),
            jax.ShapeDtypeStruct((T, LANE), jnp.int32), jax.ShapeDtypeStruct((T, LANE), F32),
            jax.ShapeDtypeStruct((1, LANE), F32),
        ],
        compiler_params=pltpu.CompilerParams(
            dimension_semantics=("arbitrary",), vmem_limit_bytes=VMEM_LIMIT),
        name="mix_route",
    )(merged, x_p, x_s, wo, norm_w, r_hi, r_lo)


MOE_MOVE_TOKENS = 128
MOE_MOVE_UNROLL = 8


def _moe_scatter_body(dest_ref, h_ref, xs_in, xs_hbm, sem):
    del xs_in
    tt = h_ref.shape[0]
    t0 = pl.program_id(0) * tt

    def start(i, carry):
        for k in range(MOE_TOP_K):
            pltpu.make_async_copy(
                h_ref.at[pl.ds(i, 1)], xs_hbm.at[pl.ds(dest_ref[(t0 + i) * MOE_TOP_K + k], 1)], sem).start()
        return carry

    lax.fori_loop(0, tt, start, 0, unroll=MOE_MOVE_UNROLL)
    for _ in range(MOE_TOP_K):
        pltpu.make_async_copy(h_ref, xs_hbm.at[pl.ds(0, tt)], sem).wait()


def _moe_scatter(dest, h2, xs_zero):
    T, d = h2.shape
    tt = MOE_MOVE_TOKENS
    assert T % tt == 0
    any_spec = pl.BlockSpec(memory_space=pl.ANY)
    return pl.pallas_call(
        _moe_scatter_body,
        grid_spec=pltpu.PrefetchScalarGridSpec(
            num_scalar_prefetch=1,
            grid=(T // tt,),
            in_specs=[pl.BlockSpec((tt, d), lambda i, dr: (i, 0)), any_spec],
            out_specs=any_spec,
            scratch_shapes=[pltpu.SemaphoreType.DMA(())],
        ),
        out_shape=jax.ShapeDtypeStruct(xs_zero.shape, xs_zero.dtype),
        input_output_aliases={2: 0},
        compiler_params=pltpu.CompilerParams(dimension_semantics=("arbitrary",)),
        name="moe_scatter",
    )(dest, h2, xs_zero)


MOE_WEIGHT_PIECES = 8
MOE_WEIGHT_RING = 3


def _moe_ffn_body(sched_ref, n_used_ref, x_ref, wg_hbm, wu_hbm, wd_hbm, o_ref,
                  wg_b, wu_b, wd_b, sg, su, sd, sems):
    b = pl.program_id(0)
    hbm = (wg_hbm, wu_hbm, wd_hbm)
    stage = (sg, su, sd)
    resident = (wg_b, wu_b, wd_b)

    def copies(e, piece, ring):
        out = []
        for k in range(3):
            rows = stage[k].shape[1]
            out.append(pltpu.make_async_copy(
                hbm[k].at[0, e, pl.ds(pl.multiple_of(piece * rows, rows), rows), :],
                stage[k].at[ring], sems.at[ring, k]))
        return out

    def start(e, piece, ring):
        for c in copies(e, piece, ring):
            c.start()

    def land(e, piece, ring, slot):
        for k, c in enumerate(copies(e, piece, ring)):
            c.wait()
            rows = stage[k].shape[1]
            resident[k][slot, pl.ds(pl.multiple_of(piece * rows, rows), rows), :] = stage[k][ring].astype(BF16)

    def stream(e, first, count, slot):
        def body(c, carry):
            ring = c % MOE_WEIGHT_RING
            land(e, first + c, ring, slot)

            @pl.when(c + MOE_WEIGHT_RING < count)
            def _():
                start(e, first + c + MOE_WEIGHT_RING, ring)
            return carry

        lax.fori_loop(0, count, body, 0)

    def prime(e, first, count):
        for r in range(MOE_WEIGHT_RING):
            @pl.when(r < count)
            def _():
                start(e, first + r, r)

    @pl.when(b < n_used_ref[0])
    def _():
        slot = sched_ref[0, b]

        @pl.when(b == 0)
        def _():
            first_expert = sched_ref[1, pl.num_programs(0)]
            prime(first_expert, 0, MOE_WEIGHT_PIECES)
            stream(first_expert, 0, MOE_WEIGHT_PIECES, slot)

        nxt, first, count = sched_ref[1, b], sched_ref[2, b], sched_ref[3, b]
        prime(nxt, first, count)
        xb = x_ref[...].astype(BF16)
        gate = jnp.dot(xb, wg_b[slot], preferred_element_type=F32)
        up = jnp.dot(xb, wu_b[slot], preferred_element_type=F32)
        act = (_silu(gate) * up).astype(BF16)
        o_ref[...] = jnp.dot(act, wd_b[slot], preferred_element_type=F32)
        stream(nxt, first, count, 1 - slot)

    @pl.when(b >= n_used_ref[0])
    def _():
        o_ref[...] = jnp.zeros(o_ref.shape, o_ref.dtype)


def _moe_ffn(sched, n_used, xs, wg, wu, wd):
    R, d = xs.shape
    f = wg.shape[3]
    nb = R // MOE_ROWS
    np_ = MOE_WEIGHT_PIECES
    assert d % np_ == 0 and f % np_ == 0
    any_spec = pl.BlockSpec(memory_space=pl.ANY)
    return pl.pallas_call(
        _moe_ffn_body,
        grid_spec=pltpu.PrefetchScalarGridSpec(
            num_scalar_prefetch=2,
            grid=(nb,),
            in_specs=[pl.BlockSpec((MOE_ROWS, d), lambda b, sc, nu: (b, 0)), any_spec, any_spec, any_spec],
            out_specs=pl.BlockSpec((MOE_ROWS, d), lambda b, sc, nu: (b, 0)),
            scratch_shapes=[
                pltpu.VMEM((2, d, f), BF16), pltpu.VMEM((2, d, f), BF16), pltpu.VMEM((2, f, d), BF16),
                pltpu.VMEM((MOE_WEIGHT_RING, d // np_, f), F32), pltpu.VMEM((MOE_WEIGHT_RING, d // np_, f), F32),
                pltpu.VMEM((MOE_WEIGHT_RING, f // np_, d), F32),
                pltpu.SemaphoreType.DMA((MOE_WEIGHT_RING, 3)),
            ],
        ),
        out_shape=jax.ShapeDtypeStruct((R, d), F32),
        compiler_params=pltpu.CompilerParams(
            dimension_semantics=("arbitrary",), vmem_limit_bytes=VMEM_LIMIT),
        name="moe_ffn",
    )(sched, n_used, xs, wg, wu, wd)


def _moe_combine_body(dest_ref, x1_ref, wts_ref, nw_ref, yb_hbm, op_ref, os_ref, buf, sem, *, prompt_tiles):
    tt = x1_ref.shape[0]
    t0 = pl.program_id(0) * tt

    def start(i, carry):
        for k in range(MOE_TOP_K):
            pltpu.make_async_copy(
                yb_hbm.at[pl.ds(dest_ref[(t0 + i) * MOE_TOP_K + k], 1)], buf.at[k, pl.ds(i, 1)], sem).start()
        return carry

    lax.fori_loop(0, tt, start, 0, unroll=MOE_MOVE_UNROLL)
    for k in range(MOE_TOP_K):
        pltpu.make_async_copy(yb_hbm.at[pl.ds(0, tt)], buf.at[k], sem).wait()

    def finish(o_ref):
        w = wts_ref[...]
        x2 = x1_ref[...] + (buf[0] * w[:, 0:1] + buf[1] * w[:, 1:2])
        o_ref[...] = x2 * lax.rsqrt(jnp.mean(x2 * x2, axis=-1, keepdims=True) + NORM_EPS) * nw_ref[...]

    _by_stream(pl.program_id(0), prompt_tiles, finish, op_ref, os_ref)


def _moe_combine(dest, x1, wts, norm_w, yb, t_p):
    T, d = x1.shape
    tt = MOE_MOVE_TOKENS
    assert t_p % tt == 0 and T % tt == 0
    npt = t_p // tt
    op_spec, os_spec = _stream_specs((tt, d), npt)
    return pl.pallas_call(
        functools.partial(_moe_combine_body, prompt_tiles=npt),
        grid_spec=pltpu.PrefetchScalarGridSpec(
            num_scalar_prefetch=1,
            grid=(T // tt,),
            in_specs=[
                pl.BlockSpec((tt, d), lambda i, dr: (i, 0)),
                pl.BlockSpec((tt, LANE), lambda i, dr: (i, 0)),
                pl.BlockSpec((1, d), lambda i, dr: (0, 0)),
                pl.BlockSpec(memory_space=pl.ANY),
            ],
            out_specs=[op_spec, os_spec],
            scratch_shapes=[pltpu.VMEM((MOE_TOP_K, tt, d), F32), pltpu.SemaphoreType.DMA(())],
        ),
        out_shape=[jax.ShapeDtypeStruct((t_p, d), F32), jax.ShapeDtypeStruct((T - t_p, d), F32)],
        compiler_params=pltpu.CompilerParams(dimension_semantics=("arbitrary",)),
        name="moe_combine",
    )(dest, x1, wts, norm_w, yb)


def _route_rows(eid, rank, counts, n_assign):
    n_experts = counts.shape[0]
    padded = (counts + MOE_ROWS - 1) // MOE_ROWS * MOE_ROWS
    pend = jnp.cumsum(padded)
    pstart = pend - padded
    experts = jnp.arange(n_experts, dtype=jnp.int32)
    onehot = eid[:, :, None] == experts[None, None, :]
    dest = (jnp.sum(jnp.where(onehot, pstart[None, None, :], 0), axis=-1) + rank).astype(jnp.int32)
    n_blocks = -(-n_assign // MOE_ROWS) + n_experts
    blk_start = jnp.arange(n_blocks, dtype=jnp.int32) * MOE_ROWS
    used = blk_start < pend[-1]
    blk_e = jnp.minimum(jnp.sum((blk_start[:, None] >= pend[None, :]).astype(jnp.int32), axis=1), n_experts - 1)
    n_used = (pend[-1] // MOE_ROWS).astype(jnp.int32).reshape(1)

    nonempty = counts > 0
    ordinal = jnp.cumsum(nonempty.astype(jnp.int32)) - 1
    later = (experts[None, :] > experts[:, None]) & nonempty[None, :]
    nxt = jnp.min(jnp.where(later, experts[None, :], n_experts), axis=1)
    pos = blk_start // MOE_ROWS - (pstart // MOE_ROWS)[blk_e]
    nblk = jnp.maximum((padded // MOE_ROWS)[blk_e], 1)
    has_next = used & (nxt[blk_e] < n_experts)
    first = jnp.where(has_next, MOE_WEIGHT_PIECES * pos // nblk, 0)
    last = jnp.where(has_next, MOE_WEIGHT_PIECES * (pos + 1) // nblk, 0)
    slot = jnp.where(used, ordinal[blk_e] % 2, 0)
    per_block = jnp.stack([slot, jnp.where(has_next, nxt[blk_e], 0), first, last - first]).astype(jnp.int32)
    first_expert = jnp.stack([slot[0], blk_e[0], 0, MOE_WEIGHT_PIECES]).astype(jnp.int32)[:, None]
    return dest.reshape(-1), jnp.concatenate([per_block, first_expert], axis=1), n_used, n_blocks


def _pad_lanes(v, fill=0.0):
    return jnp.pad(v.astype(F32), (0, LANE - v.shape[0]), constant_values=fill).reshape(1, LANE)


def kernel(x_prompt, x_sample, state_ssm, state_ssd_conv, state_short_conv, norm_mixer, w_in, ssd_conv_w,
           ssd_conv_b, ssd_dt_bias, ssd_a_log, ssd_d, ssd_norm, sc_conv_w, w_branch_out, w_out, norm_ffn,
           w_router_coarse, w_router_fine, w_expert_gate, w_expert_up, w_expert_down, norm_final):
    depth = w_in.shape[0]
    assert depth == 1
    n_p, seq_p, d = x_prompt.shape
    n_s, seq_s, _ = x_sample.shape
    d_inner = 2 * d
    n_heads = d_inner // HEAD_DIM
    n_groups = d_inner // GROUP_W
    gn = n_groups * STATE_DIM
    conv_dim = d_inner + 2 * gn
    assert conv_dim == 3 * d and n_heads <= LANE and ssd_conv_w.shape[2] == conv_dim
    t_p, t_s = n_p * seq_p, n_s * seq_s
    n_coarse = w_router_coarse.shape[2]
    n_experts = w_router_fine.shape[2]
    assert n_coarse + n_experts <= LANE

    off_dt = 2 * d + d_inner + conv_dim
    off_sc = off_dt + n_heads
    w_in_t = jnp.swapaxes(w_in, 1, 2)
    w_tail_t = w_in_t[0, off_sc:, :]
    w_dt = jnp.pad(w_in_t[0, off_dt:off_sc, :].T, ((0, 0), (0, LANE - n_heads))).astype(BF16)
    head_of_col = jnp.arange(d_inner, dtype=jnp.int32) // HEAD_DIM
    expand = (jnp.arange(LANE, dtype=jnp.int32)[:, None] == head_of_col[None, :]).astype(BF16)
    group_of_n = jnp.arange(gn, dtype=jnp.int32) // STATE_DIM
    group_of_head = jnp.arange(LANE, dtype=jnp.int32) // HEADS_PER_GROUP
    gsel = ((group_of_n[:, None] == group_of_head[None, :])
            & (jnp.arange(LANE)[None, :] < n_heads)).astype(BF16)
    consts = (
        ssd_conv_w[0], ssd_conv_b[0].reshape(1, conv_dim), _pad_lanes(ssd_dt_bias[0]), _pad_lanes(ssd_a_log[0]),
        jnp.repeat(ssd_d[0].astype(F32), HEAD_DIM).reshape(1, d_inner), ssd_norm[0].reshape(1, d_inner),
        sc_conv_w[0], expand,
    )
    wbo = w_branch_out[0].astype(BF16)
    wa, wb = wbo[:d_inner], wbo[d_inner:]
    wo = w_out[0].astype(BF16)
    w_router = jnp.pad(jnp.concatenate([w_router_coarse[0], w_router_fine[0]], axis=1),
                       ((0, 0), (0, LANE - n_coarse - n_experts)))
    r_hi = w_router.astype(BF16)
    r_lo = (w_router - r_hi.astype(F32)).astype(BF16)

    x_p = x_prompt.reshape(t_p, d)
    x_s = x_sample.reshape(t_s, d)
    h, dt_raw = _prenorm(x_p, x_s, norm_mixer[0].reshape(1, d), w_dt)
    p = _inproj(h, w_in_t, w_tail_t, off_dt)
    ya_p, yb_p, p_ssm, p_conv_tail, p_sc_tail = _ssd_prompt(p, dt_raw, consts, n_p, seq_p, d)
    ya_s, yb_s, s_ssm, s_conv, s_sc = _ssd_sample(
        p, dt_raw, state_ssm[0].reshape(n_s, n_groups, GROUP_W, STATE_DIM), state_ssd_conv[0],
        state_short_conv[0], consts, gsel, t_p, seq_s, d)
    merged = _branch_out(ya_p, yb_p, ya_s, yb_s, p, wa, wb)
    x1, h2, route, wts, counts = _mix_route(merged, x_p, x_s, wo, norm_ffn[0].reshape(1, d), r_hi, r_lo,
                                            n_coarse, n_experts // n_coarse)

    n_assign = (t_p + t_s) * MOE_TOP_K
    dest, sched, n_used, n_blocks = _route_rows(
        route[:, 0:MOE_TOP_K], route[:, MOE_TOP_K:2 * MOE_TOP_K],
        counts[0, n_coarse:n_coarse + n_experts].astype(jnp.int32), n_assign)
    xs = _moe_scatter(dest, h2, jnp.zeros((n_blocks * MOE_ROWS, d), F32))
    yrows = _moe_ffn(sched, n_used, xs, w_expert_gate, w_expert_up, w_expert_down)
    out_p, out_s = _moe_combine(dest, x1, wts, norm_final.reshape(1, d), yrows, t_p)

    kw = ssd_conv_w.shape[1]
    kw2 = sc_conv_w.shape[1]
    return (
        out_p.reshape(n_p, seq_p, d),
        out_s.reshape(n_s, seq_s, d),
        p_ssm.reshape(1, n_p, n_heads, HEAD_DIM, STATE_DIM),
        p_conv_tail[:, SUBLANE - (kw - 1):, :][None],
        p_sc_tail[:, SUBLANE - (kw2 - 1):, :][None],
        s_ssm.reshape(1, n_s, n_heads, HEAD_DIM, STATE_DIM),
        s_conv[None],
        s_sc[None],
    )
```

```python
import functools
import math

import jax
import jax.numpy as jnp
from jax import lax
from jax.experimental import pallas as pl
from jax.experimental.pallas import tpu as pltpu

F32 = jnp.float32
BF16 = jnp.bfloat16

NORM_EPS = 1e-6
SSD_NORM_EPS = 1e-5
HEAD_DIM = 64
STATE_DIM = 128
HEADS_PER_GROUP = 8
GROUP_W = HEADS_PER_GROUP * HEAD_DIM
SSD_CHUNK = 128
MOE_TOP_K = 2
MOE_ROWS = 256

LANE = 128
SUBLANE = 8
VMEM_LIMIT = 56 * 1024 * 1024

NT_DIMS = (((1,), (1,)), ((), ()))
TN_DIMS = (((0,), (0,)), ((), ()))


def _tile(n, target, align):
    best = None
    for t in range(align, min(n, target) + 1, align):
        if n % t == 0:
            best = t
    assert best is not None, (n, target, align)
    return best


def _split2(v):
    hi = v.astype(BF16)
    lo = (v - hi.astype(F32)).astype(BF16)
    return hi, lo


def _split3(v):
    hi = v.astype(BF16)
    r = v - hi.astype(F32)
    mid = r.astype(BF16)
    lo = (r - mid.astype(F32)).astype(BF16)
    return hi, mid, lo


def _softplus(x):
    return jnp.maximum(x, 0.0) + jnp.log1p(jnp.exp(-jnp.abs(x)))


def _silu(x):
    return x * jax.nn.sigmoid(x)


def _by_stream(tile, prompt_tiles, fn, prompt_ref, sample_ref):
    @pl.when(tile < prompt_tiles)
    def _():
        fn(prompt_ref)

    @pl.when(tile >= prompt_tiles)
    def _():
        fn(sample_ref)


def _stream_specs(block, prompt_tiles, **kwargs):
    pad = (0,) * (len(block) - 1)
    prompt = lambda i, *_: (jnp.minimum(i, prompt_tiles - 1),) + pad
    sample = lambda i, *_: (jnp.maximum(i - prompt_tiles, 0),) + pad
    return pl.BlockSpec(block, prompt, **kwargs), pl.BlockSpec(block, sample, **kwargs)


def _prenorm_body(xp_ref, xs_ref, nw_ref, wdt_ref, h_ref, dt_ref, *, prompt_tiles):
    def run(x_ref):
        x = x_ref[...]
        h = x * lax.rsqrt(jnp.mean(x * x, axis=-1, keepdims=True) + NORM_EPS) * nw_ref[...]
        hb = h.astype(BF16)
        h_ref[...] = hb
        dt_ref[...] = jnp.dot(hb, wdt_ref[...], preferred_element_type=F32)

    _by_stream(pl.program_id(0), prompt_tiles, run, xp_ref, xs_ref)


def _prenorm(x_p, x_s, norm_w, w_dt):
    t_p, D = x_p.shape
    T = t_p + x_s.shape[0]
    tm = _tile(math.gcd(t_p, x_s.shape[0]), 512, 16)
    npt = t_p // tm
    xp_spec, xs_spec = _stream_specs((tm, D), npt)
    return pl.pallas_call(
        functools.partial(_prenorm_body, prompt_tiles=npt),
        grid=(T // tm,),
        in_specs=[xp_spec, xs_spec, pl.BlockSpec((1, D), lambda i: (0, 0)),
                  pl.BlockSpec((D, LANE), lambda i: (0, 0))],
        out_specs=[pl.BlockSpec((tm, D), lambda i: (i, 0)), pl.BlockSpec((tm, LANE), lambda i: (i, 0))],
        out_shape=[jax.ShapeDtypeStruct((T, D), BF16), jax.ShapeDtypeStruct((T, LANE), F32)],
        compiler_params=pltpu.CompilerParams(
            dimension_semantics=("arbitrary",), vmem_limit_bytes=VMEM_LIMIT),
        name="prenorm",
    )(x_p, x_s, norm_w, w_dt)


INPROJ_CONV_SLABS = 4


def _inproj_body(h_ref, wa_ref, wb_ref, cw_ref, cb_ref, p_ref, tail_ref, w_scr, cext,
                 *, main_tiles, blocks_per_d, prompt_tiles, tiles_per_seq):
    j = pl.program_id(0)
    i = pl.program_id(1)
    tm, tn = p_ref.shape
    kw = cw_ref.shape[0]

    @pl.when(i == 0)
    def _():
        def cast(w_ref):
            w_scr[...] = w_ref[...].astype(BF16)

        _by_stream(j, main_tiles, cast, wa_ref, wb_ref)

    is_conv = (j >= 4 * blocks_per_d) & (j < 7 * blocks_per_d) & (i < prompt_tiles)

    @pl.when(jnp.logical_not(is_conv))
    def _():
        raw = lax.dot_general(h_ref[...], w_scr[...], NT_DIMS, preferred_element_type=F32)
        p_ref[...] = raw
        tail_ref[...] = raw[tm - SUBLANE:tm, :]

    @pl.when(is_conv)
    def _():
        @pl.when(i % tiles_per_seq == 0)
        def _():
            cext[:, 0:SUBLANE, :] = jnp.zeros((cext.shape[0], SUBLANE, LANE), F32)

        ws = tn // INPROJ_CONV_SLABS
        first = SUBLANE - (kw - 1)
        for k in range(INPROJ_CONV_SLABS):
            raw = lax.dot_general(h_ref[...], w_scr[k * ws:(k + 1) * ws, :], NT_DIMS, preferred_element_type=F32)
            for s in range(k * ws // LANE, (k + 1) * ws // LANE):
                lo, hi = s * LANE, (s + 1) * LANE
                cext[s, SUBLANE:SUBLANE + tm, :] = raw[:, lo - k * ws:hi - k * ws]
                acc = cb_ref[:, lo:hi]
                for t in range(kw):
                    acc = acc + cw_ref[t:t + 1, lo:hi] * cext[s, first + t:first + t + tm, :]
                p_ref[:, lo:hi] = _silu(acc)
                tail = cext[s, tm:tm + SUBLANE, :]
                tail_ref[:, lo:hi] = tail
                cext[s, 0:SUBLANE, :] = tail


def _inproj(h, w_in_t, w_tail_t, n_head_cols, cw, cb, t_p, seq_p):
    T, D = h.shape
    tm = _tile(math.gcd(seq_p, T - t_p), 1024, 16)
    tn = _tile(math.gcd(D, w_tail_t.shape[0]), 1024, LANE)
    main_tiles = n_head_cols // tn
    n_out = n_head_cols + w_tail_t.shape[0]
    bpd = D // tn
    conv_block = lambda j, i: (0, jnp.clip(j - 4 * bpd, 0, 3 * bpd - 1))
    return pl.pallas_call(
        functools.partial(_inproj_body, main_tiles=main_tiles, blocks_per_d=bpd,
                          prompt_tiles=t_p // tm, tiles_per_seq=seq_p // tm),
        grid=(n_out // tn, T // tm),
        in_specs=[
            pl.BlockSpec((tm, D), lambda j, i: (i, 0)),
            pl.BlockSpec((None, tn, D), lambda j, i: (0, jnp.minimum(j, main_tiles - 1), 0)),
            pl.BlockSpec((tn, D), lambda j, i: (jnp.maximum(j - main_tiles, 0), 0),
                         pipeline_mode=pl.Buffered(1)),
            pl.BlockSpec((cw.shape[0], tn), conv_block),
            pl.BlockSpec((1, tn), conv_block),
        ],
        out_specs=[pl.BlockSpec((tm, tn), lambda j, i: (i, j)),
                   pl.BlockSpec((SUBLANE, tn), lambda j, i: (i, j))],
        out_shape=[jax.ShapeDtypeStruct((T, n_out), F32),
                   jax.ShapeDtypeStruct((T // tm * SUBLANE, n_out), F32)],
        scratch_shapes=[pltpu.VMEM((tn, D), BF16), pltpu.VMEM((tn // LANE, tm + SUBLANE, LANE), F32)],
        compiler_params=pltpu.CompilerParams(
            dimension_semantics=("arbitrary", "arbitrary"), vmem_limit_bytes=VMEM_LIMIT),
        name="inproj",
    )(h, w_in_t, w_tail_t, cw, cb)


def _slab_store(ext, row0, value, col0=0):
    for s in range(value.shape[1] // LANE):
        ext[col0 // LANE + s, row0:row0 + value.shape[0], :] = value[:, s * LANE:(s + 1) * LANE]


def _slab_load(ext, row0, rows, lo, hi):
    return jnp.concatenate([ext[s, row0:row0 + rows, :] for s in range(lo // LANE, hi // LANE)], axis=1)


def _conv(ext, lo, hi, q, width, w_ref, first):
    parts = []
    for s in range(lo // LANE, hi // LANE):
        acc = None
        for k in range(width):
            term = w_ref[k:k + 1, s * LANE:(s + 1) * LANE] * ext[s, first + k:first + k + q, :]
            acc = term if acc is None else acc + term
        parts.append(acc)
    return jnp.concatenate(parts, axis=1)


def _gated_norm(y, z, nw):
    g = y * _silu(z)
    return g * lax.rsqrt(jnp.mean(g * g, axis=-1, keepdims=True) + SSD_NORM_EPS) * nw


def _decay_col(cs_last_row, g):
    d = jnp.exp(cs_last_row)
    parts = [
        jnp.broadcast_to(d[0:1, g * HEADS_PER_GROUP + j:g * HEADS_PER_GROUP + j + 1], (HEAD_DIM, STATE_DIM))
        for j in range(HEADS_PER_GROUP)
    ]
    return jnp.concatenate(parts, axis=0)


def _ssd_prompt_body(x4, x5, x6, z2, z3, scb, scc, sch, dtr,
                     dtb, alog, dskip, nw, scw, expand,
                     ya_ref, yb_ref, st_ref, sct_ref, ext2):
    q, d = x4.shape
    n_groups = st_ref.shape[1]
    gn = n_groups * STATE_DIM
    kw2 = scw.shape[0]

    @pl.when(pl.program_id(1) == 0)
    def _():
        st_ref[...] = jnp.zeros(st_ref.shape, F32)
        ext2[:, 0:SUBLANE, :] = jnp.zeros((ext2.shape[0], SUBLANE, LANE), F32)

    _slab_store(ext2, SUBLANE, scc[...] * sch[...])
    v = _conv(ext2, 0, d, q, kw2, scw, SUBLANE - (kw2 - 1))
    yb_ref[...] = (scb[...] * v).astype(yb_ref.dtype)
    tail2 = _slab_load(ext2, q, SUBLANE, 0, d)
    _slab_store(ext2, 0, tail2)
    sct_ref[0] = tail2

    dt = _softplus(dtr[...] + dtb[...])
    a = dt * (-jnp.exp(alog[...]))
    row = lax.broadcasted_iota(jnp.int32, (q, q), 0)
    col = lax.broadcasted_iota(jnp.int32, (q, q), 1)
    causal = row >= col
    tri = causal.astype(BF16)
    cs = sum(jnp.dot(tri, part, preferred_element_type=F32) for part in _split3(a))
    cs_t = cs.T
    cs_last = cs[q - 1:q, :]
    dend = jnp.exp(cs_last - cs)
    ecs = jnp.exp(cs)
    stacked = jnp.concatenate([dt, dend, ecs], axis=0)
    st_hi, st_lo = _split2(stacked)

    lane = lax.broadcasted_iota(jnp.int32, (q, LANE), 1)
    groups_per_block = d // GROUP_W

    for g in range(n_groups):
        c0 = g * GROUP_W
        xref = x4 if g < groups_per_block else x5
        bc = (g % groups_per_block) * GROUP_W
        xs = xref[:, bc:bc + GROUP_W]
        e_g = expand[:, c0:c0 + GROUP_W]
        ex = (jnp.dot(st_hi, e_g, preferred_element_type=F32)
              + jnp.dot(st_lo, e_g, preferred_element_type=F32))
        xdt = xs * ex[0:q]
        xdt_b = xdt.astype(BF16)
        xdd_b = (xdt * ex[q:2 * q]).astype(BF16)
        bg = x6[:, g * STATE_DIM:(g + 1) * STATE_DIM].astype(BF16)
        cg = x6[:, gn + g * STATE_DIM:gn + (g + 1) * STATE_DIM].astype(BF16)
        cbm = lax.dot_general(cg, bg, NT_DIMS, preferred_element_type=F32)
        state = st_ref[0, g]
        y_off = lax.dot_general(cg, state.astype(BF16), NT_DIMS, preferred_element_type=F32)
        y_parts = []
        for j in range(HEADS_PER_GROUP // 2):
            scores = []
            for h in (g * HEADS_PER_GROUP + 2 * j, g * HEADS_PER_GROUP + 2 * j + 1):
                seg = cs[:, h:h + 1] - cs_t[h:h + 1, :]
                dec = jnp.exp(jnp.where(causal, seg, -jnp.inf))
                scores.append((cbm * dec).astype(BF16))
            xp = xdt_b[:, j * LANE:(j + 1) * LANE]
            zero = jnp.zeros_like(xp)
            rhs = jnp.concatenate(
                [jnp.where(lane < HEAD_DIM, xp, zero), jnp.where(lane >= HEAD_DIM, xp, zero)], axis=0)
            y_parts.append(jnp.dot(jnp.concatenate(scores, axis=1), rhs, preferred_element_type=F32))
        y = jnp.concatenate(y_parts, axis=1) + y_off * ex[2 * q:3 * q] + xs * dskip[:, c0:c0 + GROUP_W]
        zref = z2 if g < groups_per_block else z3
        ya_ref[:, c0:c0 + GROUP_W] = _gated_norm(
            y, zref[:, bc:bc + GROUP_W], nw[:, c0:c0 + GROUP_W]).astype(ya_ref.dtype)
        st_ref[0, g] = state * _decay_col(cs_last, g) + lax.dot_general(
            xdd_b, bg, TN_DIMS, preferred_element_type=F32)


def _ssd_prompt(p, dt_raw, consts, n_seq, seq_len, d):
    (_, _, dtb, alog, dskip, nw, scw, expand) = consts
    q = SSD_CHUNK if seq_len % SSD_CHUNK == 0 else seq_len
    nc = seq_len // q
    n_groups = 2 * d // GROUP_W
    t_p = n_seq * seq_len
    assert 2 * n_groups * STATE_DIM == d

    def blk(cidx):
        return pl.BlockSpec((q, d), lambda b, c, cidx=cidx: (b * nc + c, cidx))

    def const(arr):
        return pl.BlockSpec(arr.shape, lambda b, c: (0,) * arr.ndim)

    return pl.pallas_call(
        _ssd_prompt_body,
        grid=(n_seq, nc),
        in_specs=[blk(4), blk(5), blk(6), blk(2), blk(3), blk(7), blk(8), blk(9),
                  pl.BlockSpec((q, LANE), lambda b, c: (b * nc + c, 0)),
                  const(dtb), const(alog), const(dskip), const(nw), const(scw), const(expand)],
        out_specs=[
            pl.BlockSpec((q, 2 * d), lambda b, c: (b * nc + c, 0)),
            pl.BlockSpec((q, d), lambda b, c: (b * nc + c, 0)),
            pl.BlockSpec((1, n_groups, GROUP_W, STATE_DIM), lambda b, c: (b, 0, 0, 0)),
            pl.BlockSpec((1, SUBLANE, d), lambda b, c: (b, 0, 0)),
        ],
        out_shape=[
            jax.ShapeDtypeStruct((t_p, 2 * d), BF16),
            jax.ShapeDtypeStruct((t_p, d), BF16),
            jax.ShapeDtypeStruct((n_seq, n_groups, GROUP_W, STATE_DIM), F32),
            jax.ShapeDtypeStruct((n_seq, SUBLANE, d), F32),
        ],
        scratch_shapes=[pltpu.VMEM((d // LANE, q + SUBLANE, LANE), F32)],
        compiler_params=pltpu.CompilerParams(
            dimension_semantics=("arbitrary", "arbitrary"), vmem_limit_bytes=VMEM_LIMIT),
        name="ssd_prompt",
    )(p, p, p, p, p, p, p, p, dt_raw, dtb, alog, dskip, nw, scw, expand)


SAMPLE_SEQS_PER_STEP = 2


def _ssd_sample_body(x4, x5, x6, z2, z3, scb, scc, sch, dtr, ssm_in, conv_in, sc_in,
                     cw, cb, dtb, alog, dskip, nw, scw, expand, gsel,
                     ya_ref, yb_ref, ssm_out, conv_out, sc_out, exts, ext2s):
    d = x4.shape[1]
    n_groups = ssm_in.shape[1]
    gn = n_groups * STATE_DIM
    kw = cw.shape[0]
    kw2 = scw.shape[0]
    q = x4.shape[0] // SAMPLE_SEQS_PER_STEP
    groups_per_block = d // GROUP_W
    first = SUBLANE - (kw - 1)
    first2 = SUBLANE - (kw2 - 1)
    nrep = q * q

    rep_t = lax.broadcasted_iota(jnp.int32, (nrep, LANE), 0) % q
    rep_s = lax.broadcasted_iota(jnp.int32, (nrep, LANE), 0) // q
    rep_causal = rep_t >= rep_s
    row_q = lax.broadcasted_iota(jnp.int32, (q, LANE), 0)

    def rep_rows(m):
        return jnp.concatenate([jnp.broadcast_to(m[s:s + 1], (q, m.shape[1])) for s in range(q)], axis=0)

    def tile_rows(m):
        return jnp.concatenate([m] * q, axis=0)

    yb_rows = []
    ya_rows = [[] for _ in range(n_groups)]
    for sidx in range(SAMPLE_SEQS_PER_STEP):
        r0 = sidx * q
        ext, ext2 = exts.at[sidx], ext2s.at[sidx]
        _slab_store(ext, first, conv_in[sidx])
        _slab_store(ext, SUBLANE, x4[r0:r0 + q, :])
        _slab_store(ext, SUBLANE, x5[r0:r0 + q, :], d)
        _slab_store(ext, SUBLANE, x6[r0:r0 + q, :], 2 * d)
        conv_out[sidx] = _slab_load(ext, SUBLANE + q - (kw - 1), kw - 1, 0, 3 * d)

        _slab_store(ext2, first2, sc_in[sidx])
        _slab_store(ext2, SUBLANE, scc[r0:r0 + q, :] * sch[r0:r0 + q, :])
        yb_rows.append(scb[r0:r0 + q, :] * _conv(ext2, 0, d, q, kw2, scw, first2))
        sc_out[sidx] = _slab_load(ext2, SUBLANE + q - (kw2 - 1), kw2 - 1, 0, d)

        dt = _softplus(dtr[r0:r0 + q, :] + dtb[...])
        a = dt * (-jnp.exp(alog[...]))
        cs = jnp.zeros((q, LANE), F32)
        for r in range(q):
            cs = cs + jnp.where(row_q >= r, jnp.broadcast_to(a[r:r + 1], (q, LANE)), 0.0)
        cs_last = cs[q - 1:q, :]
        dend = jnp.exp(cs_last - cs)
        ecs = jnp.exp(cs)

        bmat = _silu(_conv(ext, 2 * d, 2 * d + gn, q, kw, cw, first) + cb[:, 2 * d:2 * d + gn])
        cmat = _silu(_conv(ext, 2 * d + gn, 3 * d, q, kw, cw, first) + cb[:, 2 * d + gn:3 * d])

        cb_hi, cb_lo = _split2(tile_rows(cmat) * rep_rows(bmat))
        cbh = (jnp.dot(cb_hi, gsel[...], preferred_element_type=F32)
               + jnp.dot(cb_lo, gsel[...], preferred_element_type=F32))
        dec = jnp.exp(jnp.where(rep_causal, tile_rows(cs) - rep_rows(cs), -jnp.inf))
        stacked = jnp.concatenate([dt, dend, ecs], axis=0)
        st_hi = stacked.astype(BF16).astype(F32)
        pad = jnp.zeros((LANE - nrep - 6 * q, LANE), F32)
        lhs = jnp.concatenate([cbh * dec, st_hi, stacked - st_hi, pad], axis=0).astype(BF16)

        for g in range(n_groups):
            c0 = g * GROUP_W
            xs = _silu(_conv(ext, c0, c0 + GROUP_W, q, kw, cw, first) + cb[:, c0:c0 + GROUP_W])
            ex = jnp.dot(lhs, expand[:, c0:c0 + GROUP_W], preferred_element_type=F32)
            o = nrep
            dtx = ex[o:o + q] + ex[o + 3 * q:o + 4 * q]
            dendx = ex[o + q:o + 2 * q] + ex[o + 4 * q:o + 5 * q]
            ecsx = ex[o + 2 * q:o + 3 * q] + ex[o + 5 * q:o + 6 * q]
            xdt = xs * dtx
            xdd = xdt * dendx
            y = xs * dskip[:, c0:c0 + GROUP_W]
            for s in range(q):
                y = y + ex[s * q:(s + 1) * q] * jnp.broadcast_to(xdt[s:s + 1], (q, GROUP_W))
            state = ssm_in[sidx, g]
            cg = cmat[:, g * STATE_DIM:(g + 1) * STATE_DIM]
            bg = bmat[:, g * STATE_DIM:(g + 1) * STATE_DIM]
            y = y + lax.dot_general(cg, state, NT_DIMS, preferred_element_type=F32) * ecsx
            zref = z2 if g < groups_per_block else z3
            zc = (g % groups_per_block) * GROUP_W
            ya_rows[g].append(_gated_norm(y, zref[r0:r0 + q, zc:zc + GROUP_W], nw[:, c0:c0 + GROUP_W]))
            ssm_out[sidx, g] = state * _decay_col(cs_last, g) + lax.dot_general(
                xdd, bg, TN_DIMS, preferred_element_type=F32)

    yb_ref[...] = jnp.concatenate(yb_rows, axis=0).astype(yb_ref.dtype)
    for g in range(n_groups):
        ya_ref[:, g * GROUP_W:(g + 1) * GROUP_W] = jnp.concatenate(ya_rows[g], axis=0).astype(ya_ref.dtype)


def _ssd_sample(p, dt_raw, ssm, conv_state, sc_state, consts, gsel, row0, seq_len, d):
    (cw, cb, dtb, alog, dskip, nw, scw, expand) = consts
    n_seq = ssm.shape[0]
    sp = SAMPLE_SEQS_PER_STEP
    rows = sp * seq_len
    assert seq_len == SUBLANE and n_seq % sp == 0 and row0 % rows == 0
    b0 = row0 // rows

    def blk(cidx):
        return pl.BlockSpec((rows, d), lambda i, cidx=cidx: (b0 + i, cidx))

    def const(arr):
        return pl.BlockSpec(arr.shape, lambda i: (0,) * arr.ndim)

    def per_seq(arr):
        return pl.BlockSpec((sp,) + arr.shape[1:], lambda i: (i,) + (0,) * (arr.ndim - 1))

    return pl.pallas_call(
        _ssd_sample_body,
        grid=(n_seq // sp,),
        in_specs=[blk(4), blk(5), blk(6), blk(2), blk(3), blk(7), blk(8), blk(9),
                  pl.BlockSpec((rows, LANE), lambda i: (b0 + i, 0)),
                  per_seq(ssm), per_seq(conv_state), per_seq(sc_state),
                  const(cw), const(cb), const(dtb), const(alog), const(dskip), const(nw), const(scw),
                  const(expand), const(gsel)],
        out_specs=[
            pl.BlockSpec((rows, 2 * d), lambda i: (i, 0)),
            pl.BlockSpec((rows, d), lambda i: (i, 0)),
            per_seq(ssm), per_seq(conv_state), per_seq(sc_state),
        ],
        out_shape=[
            jax.ShapeDtypeStruct((n_seq * seq_len, 2 * d), BF16),
            jax.ShapeDtypeStruct((n_seq * seq_len, d), BF16),
            jax.ShapeDtypeStruct(ssm.shape, F32),
            jax.ShapeDtypeStruct(conv_state.shape, F32),
            jax.ShapeDtypeStruct(sc_state.shape, F32),
        ],
        scratch_shapes=[pltpu.VMEM((sp, 3 * d // LANE, 2 * SUBLANE, LANE), F32),
                        pltpu.VMEM((sp, d // LANE, 2 * SUBLANE, LANE), F32)],
        compiler_params=pltpu.CompilerParams(
            dimension_semantics=("arbitrary",), vmem_limit_bytes=VMEM_LIMIT),
        name="ssd_sample",
    )(p, p, p, p, p, p, p, p, dt_raw, ssm, conv_state, sc_state,
      cw, cb, dtb, alog, dskip, nw, scw, expand, gsel)


def _branch_out_body(yap_ref, ybp_ref, yas_ref, ybs_ref, ga_ref, gb_ref, wa_ref, wb_ref, o_ref, *, prompt_tiles):
    def run(refs):
        ya_ref, yb_ref = refs
        pa = jnp.dot(ya_ref[...], wa_ref[...], preferred_element_type=F32)
        pb = jnp.dot(yb_ref[...], wb_ref[...], preferred_element_type=F32)
        merged = jax.nn.sigmoid(ga_ref[...]) * pa + jax.nn.sigmoid(gb_ref[...]) * pb
        o_ref[...] = merged.astype(o_ref.dtype)

    _by_stream(pl.program_id(0), prompt_tiles, run, (yap_ref, ybp_ref), (yas_ref, ybs_ref))


def _branch_out(ya_p, yb_p, ya_s, yb_s, p, wa, wb):
    t_p, d = yb_p.shape
    t_s = yb_s.shape[0]
    tm = _tile(math.gcd(t_p, t_s), 512, 16)
    tn = _tile(d, 512, LANE)
    nj = d // tn
    npt = t_p // tm
    yap_spec, yas_spec = _stream_specs((tm, 2 * d), npt)
    ybp_spec, ybs_spec = _stream_specs((tm, d), npt)
    return pl.pallas_call(
        functools.partial(_branch_out_body, prompt_tiles=npt),
        grid=((t_p + t_s) // tm, nj),
        in_specs=[
            yap_spec, ybp_spec, yas_spec, ybs_spec,
            pl.BlockSpec((tm, tn), lambda i, j: (i, j)),
            pl.BlockSpec((tm, tn), lambda i, j: (i, nj + j)),
            pl.BlockSpec((2 * d, tn), lambda i, j: (0, j)),
            pl.BlockSpec((d, tn), lambda i, j: (0, j)),
        ],
        out_specs=pl.BlockSpec((tm, tn), lambda i, j: (i, j)),
        out_shape=jax.ShapeDtypeStruct((t_p + t_s, d), BF16),
        compiler_params=pltpu.CompilerParams(
            dimension_semantics=("arbitrary", "arbitrary"), vmem_limit_bytes=VMEM_LIMIT),
        name="branch_out",
    )(ya_p, yb_p, ya_s, yb_s, p, p, wa, wb)


def _mix_route_body(m_ref, xp_ref, xs_ref, wo_ref, nw_ref, rhi_ref, rlo_ref,
                    x1_ref, h2_ref, route_ref, wts_ref, cnt_ref, *, n_coarse, per_group, prompt_tiles):
    @pl.when(pl.program_id(0) == 0)
    def _():
        cnt_ref[...] = jnp.zeros(cnt_ref.shape, F32)

    def run(x_ref):
        x1 = x_ref[...] + jnp.dot(m_ref[...], wo_ref[...], preferred_element_type=F32)
        x1_ref[...] = x1
        h2 = x1 * lax.rsqrt(jnp.mean(x1 * x1, axis=-1, keepdims=True) + NORM_EPS) * nw_ref[...]
        h2_ref[...] = h2
        h_hi, h_lo = _split2(h2)
        logits = (jnp.dot(h_hi, rhi_ref[...], preferred_element_type=F32)
                  + jnp.dot(h_hi, rlo_ref[...], preferred_element_type=F32)
                  + jnp.dot(h_lo, rhi_ref[...], preferred_element_type=F32))

        tm = logits.shape[0]
        n_fine = n_coarse * per_group
        lane = lax.broadcasted_iota(jnp.int32, logits.shape, 1)
        big = jnp.int32(LANE)
        neg = -jnp.inf
        is_c = lane < n_coarse
        lc = jnp.where(is_c, logits, neg)
        mc = jnp.max(lc, axis=-1, keepdims=True)
        grp = jnp.min(jnp.where(is_c & (lc == mc), lane, big), axis=-1, keepdims=True)
        p_grp = 1.0 / jnp.sum(jnp.where(is_c, jnp.exp(lc - mc), 0.0), axis=-1, keepdims=True)
        eidx = lane - n_coarse
        sel = (eidx >= 0) & (eidx < n_fine) & ((eidx // per_group) == grp)
        lf = jnp.where(sel, logits, neg)
        v1 = jnp.max(lf, axis=-1, keepdims=True)
        i1 = jnp.min(jnp.where(sel & (lf == v1), eidx, big), axis=-1, keepdims=True)
        sel2 = sel & (eidx != i1)
        lf2 = jnp.where(sel2, logits, neg)
        v2 = jnp.max(lf2, axis=-1, keepdims=True)
        i2 = jnp.min(jnp.where(sel2 & (lf2 == v2), eidx, big), axis=-1, keepdims=True)
        e2 = jnp.exp(v2 - v1)
        w1 = p_grp / (1.0 + e2)
        w2 = p_grp * e2 / (1.0 + e2)
        wts_ref[...] = jnp.where(lane == 0, w1, jnp.where(lane == 1, w2, 0.0))

        hit1 = eidx == i1
        hit2 = eidx == i2
        hits = hit1.astype(F32) + hit2.astype(F32)
        earlier = (lax.broadcasted_iota(jnp.int32, (tm, tm), 0)
                   > lax.broadcasted_iota(jnp.int32, (tm, tm), 1)).astype(BF16)
        before = jnp.dot(earlier, hits.astype(BF16), preferred_element_type=F32) + cnt_ref[...]
        r1 = jnp.sum(jnp.where(hit1, before, 0.0), axis=-1, keepdims=True).astype(jnp.int32)
        r2 = jnp.sum(jnp.where(hit2, before, 0.0), axis=-1, keepdims=True).astype(jnp.int32)
        cnt_ref[...] += jnp.sum(hits, axis=0, keepdims=True)
        route_ref[...] = jnp.where(lane == 0, i1, jnp.where(lane == 1, i2, jnp.where(
            lane == 2, r1, jnp.where(lane == 3, r2, 0))))

    _by_stream(pl.program_id(0), prompt_tiles, run, xp_ref, xs_ref)


def _mix_route(merged, x_p, x_s, wo, norm_w, r_hi, r_lo, n_coarse, per_group):
    t_p, d = x_p.shape
    T = t_p + x_s.shape[0]
    tm = _tile(math.gcd(t_p, x_s.shape[0]), 256, 16)
    npt = t_p // tm
    row = lambda i: (i, 0)
    fixed = lambda i: (0, 0)
    xp_spec, xs_spec = _stream_specs((tm, d), npt)
    return pl.pallas_call(
        functools.partial(_mix_route_body, n_coarse=n_coarse, per_group=per_group, prompt_tiles=npt),
        grid=(T // tm,),
        in_specs=[
            pl.BlockSpec((tm, d), row), xp_spec, xs_spec, pl.BlockSpec((d, d), fixed),
            pl.BlockSpec((1, d), fixed), pl.BlockSpec((d, LANE), fixed), pl.BlockSpec((d, LANE), fixed),
        ],
        out_specs=[pl.BlockSpec((tm, d), row), pl.BlockSpec((tm, d), row),
                   pl.BlockSpec((tm, LANE), row), pl.BlockSpec((tm, LANE), row),
                   pl.BlockSpec((1, LANE), fixed)],
        out_shape=[
            jax.ShapeDtypeStruct((T, d), F32), jax.ShapeDtypeStruct((T, d), F32),
            jax.ShapeDtypeStruct((T, LANE), jnp.int32), jax.ShapeDtypeStruct((T, LANE), F32),
            jax.ShapeDtypeStruct((1, LANE), F32),
        ],
        compiler_params=pltpu.CompilerParams(
            dimension_semantics=("arbitrary",), vmem_limit_bytes=VMEM_LIMIT),
        name="mix_route",
    )(merged, x_p, x_s, wo, norm_w, r_hi, r_lo)


MOE_MOVE_TOKENS = 128
MOE_MOVE_UNROLL = 8


def _moe_scatter_body(dest_ref, h_ref, xs_in, xs_hbm, sem):
    del xs_in
    tt = h_ref.shape[0]
    t0 = pl.program_id(0) * tt

    def start(i, carry):
        for k in range(MOE_TOP_K):
            pltpu.make_async_copy(
                h_ref.at[pl.ds(i, 1)], xs_hbm.at[pl.ds(dest_ref[(t0 + i) * MOE_TOP_K + k], 1)], sem).start()
        return carry

    lax.fori_loop(0, tt, start, 0, unroll=MOE_MOVE_UNROLL)
    for _ in range(MOE_TOP_K):
        pltpu.make_async_copy(h_ref, xs_hbm.at[pl.ds(0, tt)], sem).wait()


def _moe_scatter(dest, h2, xs_zero):
    T, d = h2.shape
    tt = MOE_MOVE_TOKENS
    assert T % tt == 0
    any_spec = pl.BlockSpec(memory_space=pl.ANY)
    return pl.pallas_call(
        _moe_scatter_body,
        grid_spec=pltpu.PrefetchScalarGridSpec(
            num_scalar_prefetch=1,
            grid=(T // tt,),
            in_specs=[pl.BlockSpec((tt, d), lambda i, dr: (i, 0)), any_spec],
            out_specs=any_spec,
            scratch_shapes=[pltpu.SemaphoreType.DMA(())],
        ),
        out_shape=jax.ShapeDtypeStruct(xs_zero.shape, xs_zero.dtype),
        input_output_aliases={2: 0},
        compiler_params=pltpu.CompilerParams(dimension_semantics=("arbitrary",)),
        name="moe_scatter",
    )(dest, h2, xs_zero)


MOE_WEIGHT_PIECES = 8
MOE_WEIGHT_RING = 3


def _moe_ffn_body(sched_ref, n_used_ref, x_ref, wg_hbm, wu_hbm, wd_hbm, o_ref,
                  wg_b, wu_b, wd_b, sg, su, sd, sems):
    b = pl.program_id(0)
    hbm = (wg_hbm, wu_hbm, wd_hbm)
    stage = (sg, su, sd)
    resident = (wg_b, wu_b, wd_b)

    def copies(e, piece, ring):
        out = []
        for k in range(3):
            rows = stage[k].shape[1]
            out.append(pltpu.make_async_copy(
                hbm[k].at[0, e, pl.ds(pl.multiple_of(piece * rows, rows), rows), :],
                stage[k].at[ring], sems.at[ring, k]))
        return out

    def start(e, piece, ring):
        for c in copies(e, piece, ring):
            c.start()

    def land(e, piece, ring, slot):
        for k, c in enumerate(copies(e, piece, ring)):
            c.wait()
            rows = stage[k].shape[1]
            resident[k][slot, pl.ds(pl.multiple_of(piece * rows, rows), rows), :] = stage[k][ring].astype(BF16)

    def stream(e, first, count, slot):
        def body(c, carry):
            ring = c % MOE_WEIGHT_RING
            land(e, first + c, ring, slot)

            @pl.when(c + MOE_WEIGHT_RING < count)
            def _():
                start(e, first + c + MOE_WEIGHT_RING, ring)
            return carry

        lax.fori_loop(0, count, body, 0)

    def prime(e, first, count):
        for r in range(MOE_WEIGHT_RING):
            @pl.when(r < count)
            def _():
                start(e, first + r, r)

    @pl.when(b < n_used_ref[0])
    def _():
        slot = sched_ref[0, b]

        @pl.when(b == 0)
        def _():
            first_expert = sched_ref[1, pl.num_programs(0)]
            prime(first_expert, 0, MOE_WEIGHT_PIECES)
            stream(first_expert, 0, MOE_WEIGHT_PIECES, slot)

        nxt, first, count = sched_ref[1, b], sched_ref[2, b], sched_ref[3, b]
        prime(nxt, first, count)
        xb = x_ref[...].astype(BF16)
        gate = jnp.dot(xb, wg_b[slot], preferred_element_type=F32)
        up = jnp.dot(xb, wu_b[slot], preferred_element_type=F32)
        act = (_silu(gate) * up).astype(BF16)
        o_ref[...] = jnp.dot(act, wd_b[slot], preferred_element_type=F32)
        stream(nxt, first, count, 1 - slot)

    @pl.when(b >= n_used_ref[0])
    def _():
        o_ref[...] = jnp.zeros(o_ref.shape, o_ref.dtype)


def _moe_ffn(sched, n_used, xs, wg, wu, wd):
    R, d = xs.shape
    f = wg.shape[3]
    nb = R // MOE_ROWS
    np_ = MOE_WEIGHT_PIECES
    assert d % np_ == 0 and f % np_ == 0
    any_spec = pl.BlockSpec(memory_space=pl.ANY)
    return pl.pallas_call(
        _moe_ffn_body,
        grid_spec=pltpu.PrefetchScalarGridSpec(
            num_scalar_prefetch=2,
            grid=(nb,),
            in_specs=[pl.BlockSpec((MOE_ROWS, d), lambda b, sc, nu: (b, 0)), any_spec, any_spec, any_spec],
            out_specs=pl.BlockSpec((MOE_ROWS, d), lambda b, sc, nu: (b, 0)),
            scratch_shapes=[
                pltpu.VMEM((2, d, f), BF16), pltpu.VMEM((2, d, f), BF16), pltpu.VMEM((2, f, d), BF16),
                pltpu.VMEM((MOE_WEIGHT_RING, d // np_, f), F32), pltpu.VMEM((MOE_WEIGHT_RING, d // np_, f), F32),
                pltpu.VMEM((MOE_WEIGHT_RING, f // np_, d), F32),
                pltpu.SemaphoreType.DMA((MOE_WEIGHT_RING, 3)),
            ],
        ),
        out_shape=jax.ShapeDtypeStruct((R, d), F32),
        compiler_params=pltpu.CompilerParams(
            dimension_semantics=("arbitrary",), vmem_limit_bytes=VMEM_LIMIT),
        name="moe_ffn",
    )(sched, n_used, xs, wg, wu, wd)


def _moe_combine_body(dest_ref, x1_ref, wts_ref, nw_ref, yb_hbm, op_ref, os_ref, buf, sem, *, prompt_tiles):
    tt = x1_ref.shape[0]
    t0 = pl.program_id(0) * tt

    def start(i, carry):
        for k in range(MOE_TOP_K):
            pltpu.make_async_copy(
                yb_hbm.at[pl.ds(dest_ref[(t0 + i) * MOE_TOP_K + k], 1)], buf.at[k, pl.ds(i, 1)], sem).start()
        return carry

    lax.fori_loop(0, tt, start, 0, unroll=MOE_MOVE_UNROLL)
    for k in range(MOE_TOP_K):
        pltpu.make_async_copy(yb_hbm.at[pl.ds(0, tt)], buf.at[k], sem).wait()

    def finish(o_ref):
        w = wts_ref[...]
        x2 = x1_ref[...] + (buf[0] * w[:, 0:1] + buf[1] * w[:, 1:2])
        o_ref[...] = x2 * lax.rsqrt(jnp.mean(x2 * x2, axis=-1, keepdims=True) + NORM_EPS) * nw_ref[...]

    _by_stream(pl.program_id(0), prompt_tiles, finish, op_ref, os_ref)


def _moe_combine(dest, x1, wts, norm_w, yb, t_p):
    T, d = x1.shape
    tt = MOE_MOVE_TOKENS
    assert t_p % tt == 0 and T % tt == 0
    npt = t_p // tt
    op_spec, os_spec = _stream_specs((tt, d), npt)
    return pl.pallas_call(
        functools.partial(_moe_combine_body, prompt_tiles=npt),
        grid_spec=pltpu.PrefetchScalarGridSpec(
            num_scalar_prefetch=1,
            grid=(T // tt,),
            in_specs=[
                pl.BlockSpec((tt, d), lambda i, dr: (i, 0)),
                pl.BlockSpec((tt, LANE), lambda i, dr: (i, 0)),
                pl.BlockSpec((1, d), lambda i, dr: (0, 0)),
                pl.BlockSpec(memory_space=pl.ANY),
            ],
            out_specs=[op_spec, os_spec],
            scratch_shapes=[pltpu.VMEM((MOE_TOP_K, tt, d), F32), pltpu.SemaphoreType.DMA(())],
        ),
        out_shape=[jax.ShapeDtypeStruct((t_p, d), F32), jax.ShapeDtypeStruct((T - t_p, d), F32)],
        compiler_params=pltpu.CompilerParams(dimension_semantics=("arbitrary",)),
        name="moe_combine",
    )(dest, x1, wts, norm_w, yb)


def _route_rows(eid, rank, counts, n_assign):
    n_experts = counts.shape[0]
    padded = (counts + MOE_ROWS - 1) // MOE_ROWS * MOE_ROWS
    pend = jnp.cumsum(padded)
    pstart = pend - padded
    experts = jnp.arange(n_experts, dtype=jnp.int32)
    onehot = eid[:, :, None] == experts[None, None, :]
    dest = (jnp.sum(jnp.where(onehot, pstart[None, None, :], 0), axis=-1) + rank).astype(jnp.int32)
    n_blocks = -(-n_assign // MOE_ROWS) + n_experts
    blk_start = jnp.arange(n_blocks, dtype=jnp.int32) * MOE_ROWS
    used = blk_start < pend[-1]
    blk_e = jnp.minimum(jnp.sum((blk_start[:, None] >= pend[None, :]).astype(jnp.int32), axis=1), n_experts - 1)
    n_used = (pend[-1] // MOE_ROWS).astype(jnp.int32).reshape(1)

    nonempty = counts > 0
    ordinal = jnp.cumsum(nonempty.astype(jnp.int32)) - 1
    later = (experts[None, :] > experts[:, None]) & nonempty[None, :]
    nxt = jnp.min(jnp.where(later, experts[None, :], n_experts), axis=1)
    pos = blk_start // MOE_ROWS - (pstart // MOE_ROWS)[blk_e]
    nblk = jnp.maximum((padded // MOE_ROWS)[blk_e], 1)
    has_next = used & (nxt[blk_e] < n_experts)
    first = jnp.where(has_next, MOE_WEIGHT_PIECES * pos // nblk, 0)
    last = jnp.where(has_next, MOE_WEIGHT_PIECES * (pos + 1) // nblk, 0)
    slot = jnp.where(used, ordinal[blk_e] % 2, 0)
    per_block = jnp.stack([slot, jnp.where(has_next, nxt[blk_e], 0), first, last - first]).astype(jnp.int32)
    first_expert = jnp.stack([slot[0], blk_e[0], 0, MOE_WEIGHT_PIECES]).astype(jnp.int32)[:, None]
    return dest.reshape(-1), jnp.concatenate([per_block, first_expert], axis=1), n_used, n_blocks


def _pad_lanes(v, fill=0.0):
    return jnp.pad(v.astype(F32), (0, LANE - v.shape[0]), constant_values=fill).reshape(1, LANE)


def kernel(x_prompt, x_sample, state_ssm, state_ssd_conv, state_short_conv, norm_mixer, w_in, ssd_conv_w,
           ssd_conv_b, ssd_dt_bias, ssd_a_log, ssd_d, ssd_norm, sc_conv_w, w_branch_out, w_out, norm_ffn,
           w_router_coarse, w_router_fine, w_expert_gate, w_expert_up, w_expert_down, norm_final):
    depth = w_in.shape[0]
    assert depth == 1
    n_p, seq_p, d = x_prompt.shape
    n_s, seq_s, _ = x_sample.shape
    d_inner = 2 * d
    n_heads = d_inner // HEAD_DIM
    n_groups = d_inner // GROUP_W
    gn = n_groups * STATE_DIM
    conv_dim = d_inner + 2 * gn
    assert conv_dim == 3 * d and n_heads <= LANE and ssd_conv_w.shape[2] == conv_dim
    t_p, t_s = n_p * seq_p, n_s * seq_s
    n_coarse = w_router_coarse.shape[2]
    n_experts = w_router_fine.shape[2]
    assert n_coarse + n_experts <= LANE

    off_dt = 2 * d + d_inner + conv_dim
    off_sc = off_dt + n_heads
    w_in_t = jnp.swapaxes(w_in, 1, 2)
    w_tail_t = w_in_t[0, off_sc:, :]
    w_dt = jnp.pad(w_in_t[0, off_dt:off_sc, :].T, ((0, 0), (0, LANE - n_heads))).astype(BF16)
    head_of_col = jnp.arange(d_inner, dtype=jnp.int32) // HEAD_DIM
    expand = (jnp.arange(LANE, dtype=jnp.int32)[:, None] == head_of_col[None, :]).astype(BF16)
    group_of_n = jnp.arange(gn, dtype=jnp.int32) // STATE_DIM
    group_of_head = jnp.arange(LANE, dtype=jnp.int32) // HEADS_PER_GROUP
    gsel = ((group_of_n[:, None] == group_of_head[None, :])
            & (jnp.arange(LANE)[None, :] < n_heads)).astype(BF16)
    consts = (
        ssd_conv_w[0], ssd_conv_b[0].reshape(1, conv_dim), _pad_lanes(ssd_dt_bias[0]), _pad_lanes(ssd_a_log[0]),
        jnp.repeat(ssd_d[0].astype(F32), HEAD_DIM).reshape(1, d_inner), ssd_norm[0].reshape(1, d_inner),
        sc_conv_w[0], expand,
    )
    wbo = w_branch_out[0].astype(BF16)
    wa, wb = wbo[:d_inner], wbo[d_inner:]
    wo = w_out[0].astype(BF16)
    w_router = jnp.pad(jnp.concatenate([w_router_coarse[0], w_router_fine[0]], axis=1),
                       ((0, 0), (0, LANE - n_coarse - n_experts)))
    r_hi = w_router.astype(BF16)
    r_lo = (w_router - r_hi.astype(F32)).astype(BF16)

    x_p = x_prompt.reshape(t_p, d)
    x_s = x_sample.reshape(t_s, d)
    h, dt_raw = _prenorm(x_p, x_s, norm_mixer[0].reshape(1, d), w_dt)
    p, tails = _inproj(h, w_in_t, w_tail_t, off_dt, consts[0], consts[1], t_p, seq_p)
    ya_p, yb_p, p_ssm, p_sc_tail = _ssd_prompt(p, dt_raw, consts, n_p, seq_p, d)
    ya_s, yb_s, s_ssm, s_conv, s_sc = _ssd_sample(
        p, dt_raw, state_ssm[0].reshape(n_s, n_groups, GROUP_W, STATE_DIM), state_ssd_conv[0],
        state_short_conv[0], consts, gsel, t_p, seq_s, d)
    merged = _branch_out(ya_p, yb_p, ya_s, yb_s, p, wa, wb)
    x1, h2, route, wts, counts = _mix_route(merged, x_p, x_s, wo, norm_ffn[0].reshape(1, d), r_hi, r_lo,
                                            n_coarse, n_experts // n_coarse)

    n_assign = (t_p + t_s) * MOE_TOP_K
    dest, sched, n_used, n_blocks = _route_rows(
        route[:, 0:MOE_TOP_K], route[:, MOE_TOP_K:2 * MOE_TOP_K],
        counts[0, n_coarse:n_coarse + n_experts].astype(jnp.int32), n_assign)
    xs = _moe_scatter(dest, h2, jnp.zeros((n_blocks * MOE_ROWS, d), F32))
    yrows = _moe_ffn(sched, n_used, xs, w_expert_gate, w_expert_up, w_expert_down)
    out_p, out_s = _moe_combine(dest, x1, wts, norm_final.reshape(1, d), yrows, t_p)

    kw = ssd_conv_w.shape[1]
    kw2 = sc_conv_w.shape[1]
    tiles_per_seq = (tails.shape[0] // SUBLANE) * seq_p // (t_p + t_s)
    seq_tails = tails.reshape(-1, SUBLANE, tails.shape[1])[tiles_per_seq - 1:n_p * tiles_per_seq:tiles_per_seq]
    p_conv = seq_tails[:, SUBLANE - (kw - 1):, 4 * d:7 * d]
    return (
        out_p.reshape(n_p, seq_p, d),
        out_s.reshape(n_s, seq_s, d),
        p_ssm.reshape(1, n_p, n_heads, HEAD_DIM, STATE_DIM),
        p_conv[None],
        p_sc_tail[:, SUBLANE - (kw2 - 1):, :][None],
        s_ssm.reshape(1, n_s, n_heads, HEAD_DIM, STATE_DIM),
        s_conv[None],
        s_sc[None],
    )
```

```python
import functools
import math

import jax
import jax.numpy as jnp
from jax import lax
from jax.experimental import pallas as pl
from jax.experimental.pallas import tpu as pltpu

F32 = jnp.float32
BF16 = jnp.bfloat16

NORM_EPS = 1e-6
SSD_NORM_EPS = 1e-5
HEAD_DIM = 64
STATE_DIM = 128
HEADS_PER_GROUP = 8
GROUP_W = HEADS_PER_GROUP * HEAD_DIM
SSD_CHUNK = 128
MOE_TOP_K = 2
MOE_ROWS = 256

LANE = 128
SUBLANE = 8
VMEM_LIMIT = 56 * 1024 * 1024

NT_DIMS = (((1,), (1,)), ((), ()))
TN_DIMS = (((0,), (0,)), ((), ()))


def _tile(n, target, align):
    best = None
    for t in range(align, min(n, target) + 1, align):
        if n % t == 0:
            best = t
    assert best is not None, (n, target, align)
    return best


def _split2(v):
    hi = v.astype(BF16)
    lo = (v - hi.astype(F32)).astype(BF16)
    return hi, lo


def _split3(v):
    hi = v.astype(BF16)
    r = v - hi.astype(F32)
    mid = r.astype(BF16)
    lo = (r - mid.astype(F32)).astype(BF16)
    return hi, mid, lo


def _softplus(x):
    return jnp.maximum(x, 0.0) + jnp.log1p(jnp.exp(-jnp.abs(x)))


def _silu(x):
    return x * jax.nn.sigmoid(x)


def _by_stream(tile, prompt_tiles, fn, prompt_ref, sample_ref):
    @pl.when(tile < prompt_tiles)
    def _():
        fn(prompt_ref)

    @pl.when(tile >= prompt_tiles)
    def _():
        fn(sample_ref)


def _stream_specs(block, prompt_tiles, **kwargs):
    pad = (0,) * (len(block) - 1)
    prompt = lambda i, *_: (jnp.minimum(i, prompt_tiles - 1),) + pad
    sample = lambda i, *_: (jnp.maximum(i - prompt_tiles, 0),) + pad
    return pl.BlockSpec(block, prompt, **kwargs), pl.BlockSpec(block, sample, **kwargs)


def _prenorm_body(xp_ref, xs_ref, nw_ref, wdt_ref, h_ref, dt_ref, *, prompt_tiles):
    def run(x_ref):
        x = x_ref[...]
        h = x * lax.rsqrt(jnp.mean(x * x, axis=-1, keepdims=True) + NORM_EPS) * nw_ref[...]
        hb = h.astype(BF16)
        h_ref[...] = hb
        dt_ref[...] = jnp.dot(hb, wdt_ref[...], preferred_element_type=F32)

    _by_stream(pl.program_id(0), prompt_tiles, run, xp_ref, xs_ref)


def _prenorm(x_p, x_s, norm_w, w_dt):
    t_p, D = x_p.shape
    T = t_p + x_s.shape[0]
    tm = _tile(math.gcd(t_p, x_s.shape[0]), 512, 16)
    npt = t_p // tm
    xp_spec, xs_spec = _stream_specs((tm, D), npt)
    return pl.pallas_call(
        functools.partial(_prenorm_body, prompt_tiles=npt),
        grid=(T // tm,),
        in_specs=[xp_spec, xs_spec, pl.BlockSpec((1, D), lambda i: (0, 0)),
                  pl.BlockSpec((D, LANE), lambda i: (0, 0))],
        out_specs=[pl.BlockSpec((tm, D), lambda i: (i, 0)), pl.BlockSpec((tm, LANE), lambda i: (i, 0))],
        out_shape=[jax.ShapeDtypeStruct((T, D), BF16), jax.ShapeDtypeStruct((T, LANE), F32)],
        compiler_params=pltpu.CompilerParams(
            dimension_semantics=("arbitrary",), vmem_limit_bytes=VMEM_LIMIT),
        name="prenorm",
    )(x_p, x_s, norm_w, w_dt)


INPROJ_CONV_SLABS = 4


def _inproj_body(h_ref, wa_ref, wb_ref, cw_ref, cb_ref, p_ref, tail_ref, w_scr, cext,
                 *, main_tiles, blocks_per_d, prompt_tiles, tiles_per_seq):
    j = pl.program_id(0)
    i = pl.program_id(1)
    tm, tn = p_ref.shape
    kw = cw_ref.shape[0]

    @pl.when(i == 0)
    def _():
        def cast(w_ref):
            w_scr[...] = w_ref[...].astype(BF16)

        _by_stream(j, main_tiles, cast, wa_ref, wb_ref)

    is_conv = (j >= 4 * blocks_per_d) & (j < 7 * blocks_per_d) & (i < prompt_tiles)

    @pl.when(jnp.logical_not(is_conv))
    def _():
        raw = lax.dot_general(h_ref[...], w_scr[...], NT_DIMS, preferred_element_type=F32)
        p_ref[...] = raw
        tail_ref[...] = raw[tm - SUBLANE:tm, :]

    @pl.when(is_conv)
    def _():
        @pl.when(i % tiles_per_seq == 0)
        def _():
            cext[:, 0:SUBLANE, :] = jnp.zeros((cext.shape[0], SUBLANE, LANE), F32)

        ws = tn // INPROJ_CONV_SLABS
        first = SUBLANE - (kw - 1)
        for k in range(INPROJ_CONV_SLABS):
            raw = lax.dot_general(h_ref[...], w_scr[k * ws:(k + 1) * ws, :], NT_DIMS, preferred_element_type=F32)
            for s in range(k * ws // LANE, (k + 1) * ws // LANE):
                lo, hi = s * LANE, (s + 1) * LANE
                cext[s, SUBLANE:SUBLANE + tm, :] = raw[:, lo - k * ws:hi - k * ws]
                acc = cb_ref[:, lo:hi]
                for t in range(kw):
                    acc = acc + cw_ref[t:t + 1, lo:hi] * cext[s, first + t:first + t + tm, :]
                p_ref[:, lo:hi] = _silu(acc)
                tail = cext[s, tm:tm + SUBLANE, :]
                tail_ref[:, lo:hi] = tail
                cext[s, 0:SUBLANE, :] = tail


def _inproj(h, w_in_t, w_tail_t, n_head_cols, cw, cb, t_p, seq_p):
    T, D = h.shape
    tm = _tile(math.gcd(seq_p, T - t_p), 1024, 16)
    tn = _tile(math.gcd(D, w_tail_t.shape[0]), 1024, LANE)
    main_tiles = n_head_cols // tn
    n_out = n_head_cols + w_tail_t.shape[0]
    bpd = D // tn
    conv_block = lambda j, i: (0, jnp.clip(j - 4 * bpd, 0, 3 * bpd - 1))
    return pl.pallas_call(
        functools.partial(_inproj_body, main_tiles=main_tiles, blocks_per_d=bpd,
                          prompt_tiles=t_p // tm, tiles_per_seq=seq_p // tm),
        grid=(n_out // tn, T // tm),
        in_specs=[
            pl.BlockSpec((tm, D), lambda j, i: (i, 0)),
            pl.BlockSpec((None, tn, D), lambda j, i: (0, jnp.minimum(j, main_tiles - 1), 0)),
            pl.BlockSpec((tn, D), lambda j, i: (jnp.maximum(j - main_tiles, 0), 0),
                         pipeline_mode=pl.Buffered(1)),
            pl.BlockSpec((cw.shape[0], tn), conv_block),
            pl.BlockSpec((1, tn), conv_block),
        ],
        out_specs=[pl.BlockSpec((tm, tn), lambda j, i: (i, j)),
                   pl.BlockSpec((SUBLANE, tn), lambda j, i: (i, j))],
        out_shape=[jax.ShapeDtypeStruct((T, n_out), F32),
                   jax.ShapeDtypeStruct((T // tm * SUBLANE, n_out), F32)],
        scratch_shapes=[pltpu.VMEM((tn, D), BF16), pltpu.VMEM((tn // LANE, tm + SUBLANE, LANE), F32)],
        compiler_params=pltpu.CompilerParams(
            dimension_semantics=("arbitrary", "arbitrary"), vmem_limit_bytes=VMEM_LIMIT),
        name="inproj",
    )(h, w_in_t, w_tail_t, cw, cb)


def _slab_store(ext, row0, value, col0=0):
    for s in range(value.shape[1] // LANE):
        ext[col0 // LANE + s, row0:row0 + value.shape[0], :] = value[:, s * LANE:(s + 1) * LANE]


def _slab_load(ext, row0, rows, lo, hi):
    return jnp.concatenate([ext[s, row0:row0 + rows, :] for s in range(lo // LANE, hi // LANE)], axis=1)


def _conv(ext, lo, hi, q, width, w_ref, first):
    parts = []
    for s in range(lo // LANE, hi // LANE):
        acc = None
        for k in range(width):
            term = w_ref[k:k + 1, s * LANE:(s + 1) * LANE] * ext[s, first + k:first + k + q, :]
            acc = term if acc is None else acc + term
        parts.append(acc)
    return jnp.concatenate(parts, axis=1)


def _gated_norm(y, z, nw):
    g = y * _silu(z)
    return g * lax.rsqrt(jnp.mean(g * g, axis=-1, keepdims=True) + SSD_NORM_EPS) * nw


def _decay_col(cs_last_row, g):
    d = jnp.exp(cs_last_row)
    parts = [
        jnp.broadcast_to(d[0:1, g * HEADS_PER_GROUP + j:g * HEADS_PER_GROUP + j + 1], (HEAD_DIM, STATE_DIM))
        for j in range(HEADS_PER_GROUP)
    ]
    return jnp.concatenate(parts, axis=0)


def _ssd_prompt_body(x4, x5, x6, z2, z3, scb, scc, sch, dtr,
                     dtb, alog, dskip, nw, scw, expand,
                     ya_ref, yb_ref, st_ref, sct_ref, ext2):
    q, d = x4.shape
    n_groups = st_ref.shape[1]
    gn = n_groups * STATE_DIM
    kw2 = scw.shape[0]

    @pl.when(pl.program_id(1) == 0)
    def _():
        st_ref[...] = jnp.zeros(st_ref.shape, F32)
        ext2[:, 0:SUBLANE, :] = jnp.zeros((ext2.shape[0], SUBLANE, LANE), F32)

    _slab_store(ext2, SUBLANE, scc[...] * sch[...])
    v = _conv(ext2, 0, d, q, kw2, scw, SUBLANE - (kw2 - 1))
    yb_ref[...] = (scb[...] * v).astype(yb_ref.dtype)
    tail2 = _slab_load(ext2, q, SUBLANE, 0, d)
    _slab_store(ext2, 0, tail2)
    sct_ref[0] = tail2

    dt = _softplus(dtr[...] + dtb[...])
    a = dt * (-jnp.exp(alog[...]))
    row = lax.broadcasted_iota(jnp.int32, (q, q), 0)
    col = lax.broadcasted_iota(jnp.int32, (q, q), 1)
    causal = row >= col
    tri = causal.astype(BF16)
    cs = sum(jnp.dot(tri, part, preferred_element_type=F32) for part in _split3(a))
    cs_t = cs.T
    cs_last = cs[q - 1:q, :]
    dend = jnp.exp(cs_last - cs)
    ecs = jnp.exp(cs)
    stacked = jnp.concatenate([dt, dend, ecs], axis=0)
    st_b = stacked.astype(BF16)

    lane = lax.broadcasted_iota(jnp.int32, (q, LANE), 1)
    groups_per_block = d // GROUP_W

    for g in range(n_groups):
        c0 = g * GROUP_W
        xref = x4 if g < groups_per_block else x5
        bc = (g % groups_per_block) * GROUP_W
        xs = xref[:, bc:bc + GROUP_W]
        e_g = expand[:, c0:c0 + GROUP_W]
        ex = jnp.dot(st_b, e_g, preferred_element_type=F32)
        xdt = xs * ex[0:q]
        xdt_b = xdt.astype(BF16)
        xdd_b = (xdt * ex[q:2 * q]).astype(BF16)
        bg = x6[:, g * STATE_DIM:(g + 1) * STATE_DIM].astype(BF16)
        cg = x6[:, gn + g * STATE_DIM:gn + (g + 1) * STATE_DIM].astype(BF16)
        cbm = lax.dot_general(cg, bg, NT_DIMS, preferred_element_type=F32)
        state = st_ref[0, g]
        y_off = lax.dot_general(cg, state.astype(BF16), NT_DIMS, preferred_element_type=F32)
        y_parts = []
        for j in range(HEADS_PER_GROUP // 2):
            scores = []
            for h in (g * HEADS_PER_GROUP + 2 * j, g * HEADS_PER_GROUP + 2 * j + 1):
                seg = cs[:, h:h + 1] - cs_t[h:h + 1, :]
                dec = jnp.exp(jnp.where(causal, seg, -jnp.inf))
                scores.append((cbm * dec).astype(BF16))
            xp = xdt_b[:, j * LANE:(j + 1) * LANE]
            zero = jnp.zeros_like(xp)
            rhs = jnp.concatenate(
                [jnp.where(lane < HEAD_DIM, xp, zero), jnp.where(lane >= HEAD_DIM, xp, zero)], axis=0)
            y_parts.append(jnp.dot(jnp.concatenate(scores, axis=1), rhs, preferred_element_type=F32))
        y = jnp.concatenate(y_parts, axis=1) + y_off * ex[2 * q:3 * q] + xs * dskip[:, c0:c0 + GROUP_W]
        zref = z2 if g < groups_per_block else z3
        ya_ref[:, c0:c0 + GROUP_W] = _gated_norm(
            y, zref[:, bc:bc + GROUP_W], nw[:, c0:c0 + GROUP_W]).astype(ya_ref.dtype)
        st_ref[0, g] = state * _decay_col(cs_last, g) + lax.dot_general(
            xdd_b, bg, TN_DIMS, preferred_element_type=F32)


def _ssd_prompt(p, dt_raw, consts, n_seq, seq_len, d):
    (_, _, dtb, alog, dskip, nw, scw, expand) = consts
    q = SSD_CHUNK if seq_len % SSD_CHUNK == 0 else seq_len
    nc = seq_len // q
    n_groups = 2 * d // GROUP_W
    t_p = n_seq * seq_len
    assert 2 * n_groups * STATE_DIM == d

    def blk(cidx):
        return pl.BlockSpec((q, d), lambda b, c, cidx=cidx: (b * nc + c, cidx))

    def const(arr):
        return pl.BlockSpec(arr.shape, lambda b, c: (0,) * arr.ndim)

    return pl.pallas_call(
        _ssd_prompt_body,
        grid=(n_seq, nc),
        in_specs=[blk(4), blk(5), blk(6), blk(2), blk(3), blk(7), blk(8), blk(9),
                  pl.BlockSpec((q, LANE), lambda b, c: (b * nc + c, 0)),
                  const(dtb), const(alog), const(dskip), const(nw), const(scw), const(expand)],
        out_specs=[
            pl.BlockSpec((q, 2 * d), lambda b, c: (b * nc + c, 0)),
            pl.BlockSpec((q, d), lambda b, c: (b * nc + c, 0)),
            pl.BlockSpec((1, n_groups, GROUP_W, STATE_DIM), lambda b, c: (b, 0, 0, 0)),
            pl.BlockSpec((1, SUBLANE, d), lambda b, c: (b, 0, 0)),
        ],
        out_shape=[
            jax.ShapeDtypeStruct((t_p, 2 * d), BF16),
            jax.ShapeDtypeStruct((t_p, d), BF16),
            jax.ShapeDtypeStruct((n_seq, n_groups, GROUP_W, STATE_DIM), F32),
            jax.ShapeDtypeStruct((n_seq, SUBLANE, d), F32),
        ],
        scratch_shapes=[pltpu.VMEM((d // LANE, q + SUBLANE, LANE), F32)],
        compiler_params=pltpu.CompilerParams(
            dimension_semantics=("arbitrary", "arbitrary"), vmem_limit_bytes=VMEM_LIMIT),
        name="ssd_prompt",
    )(p, p, p, p, p, p, p, p, dt_raw, dtb, alog, dskip, nw, scw, expand)


SAMPLE_SEQS_PER_STEP = 2


def _ssd_sample_body(x4, x5, x6, z2, z3, scb, scc, sch, dtr, ssm_in, conv_in, sc_in,
                     cw, cb, dtb, alog, dskip, nw, scw, expand, gsel,
                     ya_ref, yb_ref, ssm_out, conv_out, sc_out, exts, ext2s):
    d = x4.shape[1]
    n_groups = ssm_in.shape[1]
    gn = n_groups * STATE_DIM
    d_inner = n_groups * GROUP_W
    kw = cw.shape[0]
    kw2 = scw.shape[0]
    q = x4.shape[0] // SAMPLE_SEQS_PER_STEP
    groups_per_block = d // GROUP_W
    first = SUBLANE - (kw - 1)
    first2 = SUBLANE - (kw2 - 1)
    nrep = q * q

    rep_t = lax.broadcasted_iota(jnp.int32, (nrep, LANE), 0) % q
    rep_s = lax.broadcasted_iota(jnp.int32, (nrep, LANE), 0) // q
    rep_causal = rep_t >= rep_s
    row_q = lax.broadcasted_iota(jnp.int32, (q, LANE), 0)

    def rep_rows(m):
        return jnp.concatenate([jnp.broadcast_to(m[s:s + 1], (q, m.shape[1])) for s in range(q)], axis=0)

    def tile_rows(m):
        return jnp.concatenate([m] * q, axis=0)

    yb_rows = []
    ya_rows = [[] for _ in range(n_groups)]
    for sidx in range(SAMPLE_SEQS_PER_STEP):
        r0 = sidx * q
        ext, ext2 = exts.at[sidx], ext2s.at[sidx]
        _slab_store(ext, first, conv_in[sidx])
        _slab_store(ext, SUBLANE, x4[r0:r0 + q, :])
        _slab_store(ext, SUBLANE, x5[r0:r0 + q, :], d)
        _slab_store(ext, SUBLANE, x6[r0:r0 + q, :], 2 * d)
        conv_out[sidx] = _slab_load(ext, SUBLANE + q - (kw - 1), kw - 1, 0, 3 * d)

        _slab_store(ext2, first2, sc_in[sidx])
        _slab_store(ext2, SUBLANE, scc[r0:r0 + q, :] * sch[r0:r0 + q, :])
        yb_rows.append(scb[r0:r0 + q, :] * _conv(ext2, 0, d, q, kw2, scw, first2))
        sc_out[sidx] = _slab_load(ext2, SUBLANE + q - (kw2 - 1), kw2 - 1, 0, d)

        dt = _softplus(dtr[r0:r0 + q, :] + dtb[...])
        a = dt * (-jnp.exp(alog[...]))
        cs = jnp.zeros((q, LANE), F32)
        for r in range(q):
            cs = cs + jnp.where(row_q >= r, jnp.broadcast_to(a[r:r + 1], (q, LANE)), 0.0)
        cs_last = cs[q - 1:q, :]
        dend = jnp.exp(cs_last - cs)
        ecs = jnp.exp(cs)

        bmat = _silu(_conv(ext, 2 * d, 2 * d + gn, q, kw, cw, first) + cb[:, 2 * d:2 * d + gn])
        cmat = _silu(_conv(ext, 2 * d + gn, 3 * d, q, kw, cw, first) + cb[:, 2 * d + gn:3 * d])

        cb_hi, cb_lo = _split2(tile_rows(cmat) * rep_rows(bmat))
        cbh = (jnp.dot(cb_hi, gsel[...], preferred_element_type=F32)
               + jnp.dot(cb_lo, gsel[...], preferred_element_type=F32))
        dec = jnp.exp(jnp.where(rep_causal, tile_rows(cs) - rep_rows(cs), -jnp.inf))
        stacked = jnp.concatenate([dt, dend, ecs], axis=0)
        st_hi = stacked.astype(BF16).astype(F32)
        pad = jnp.zeros((LANE - nrep - 6 * q, LANE), F32)
        lhs = jnp.concatenate([cbh * dec, st_hi, stacked - st_hi, pad], axis=0).astype(BF16)

        xdd_parts = []
        for g in range(n_groups):
            c0 = g * GROUP_W
            xs = _silu(_conv(ext, c0, c0 + GROUP_W, q, kw, cw, first) + cb[:, c0:c0 + GROUP_W])
            ex = jnp.dot(lhs, expand[:, c0:c0 + GROUP_W], preferred_element_type=F32)
            o = nrep
            dtx = ex[o:o + q] + ex[o + 3 * q:o + 4 * q]
            dendx = ex[o + q:o + 2 * q] + ex[o + 4 * q:o + 5 * q]
            ecsx = ex[o + 2 * q:o + 3 * q] + ex[o + 5 * q:o + 6 * q]
            xdt = xs * dtx
            xdd = xdt * dendx
            y = xs * dskip[:, c0:c0 + GROUP_W]
            for s in range(q):
                y = y + ex[s * q:(s + 1) * q] * jnp.broadcast_to(xdt[s:s + 1], (q, GROUP_W))
            state = ssm_in[sidx, g]
            cg = cmat[:, g * STATE_DIM:(g + 1) * STATE_DIM]
            y = y + lax.dot_general(cg, state, NT_DIMS, preferred_element_type=F32) * ecsx
            zref = z2 if g < groups_per_block else z3
            zc = (g % groups_per_block) * GROUP_W
            ya_rows[g].append(_gated_norm(y, zref[r0:r0 + q, zc:zc + GROUP_W], nw[:, c0:c0 + GROUP_W]))
            xdd_parts.append(xdd)

        zrows = jnp.zeros((LANE - q, d_inner), F32)
        xdd_t = jnp.concatenate([jnp.concatenate(xdd_parts, axis=1), zrows], axis=0).T
        for g in range(n_groups):
            b_pad = jnp.concatenate(
                [bmat[:, g * STATE_DIM:(g + 1) * STATE_DIM], jnp.zeros((LANE - q, STATE_DIM), F32)], axis=0)
            ssm_out[sidx, g] = ssm_in[sidx, g] * _decay_col(cs_last, g) + jnp.dot(
                xdd_t[g * GROUP_W:(g + 1) * GROUP_W, :], b_pad, preferred_element_type=F32)

    yb_ref[...] = jnp.concatenate(yb_rows, axis=0).astype(yb_ref.dtype)
    for g in range(n_groups):
        ya_ref[:, g * GROUP_W:(g + 1) * GROUP_W] = jnp.concatenate(ya_rows[g], axis=0).astype(ya_ref.dtype)


def _ssd_sample(p, dt_raw, ssm, conv_state, sc_state, consts, gsel, row0, seq_len, d):
    (cw, cb, dtb, alog, dskip, nw, scw, expand) = consts
    n_seq = ssm.shape[0]
    sp = SAMPLE_SEQS_PER_STEP
    rows = sp * seq_len
    assert seq_len == SUBLANE and n_seq % sp == 0 and row0 % rows == 0
    b0 = row0 // rows

    def blk(cidx):
        return pl.BlockSpec((rows, d), lambda i, cidx=cidx: (b0 + i, cidx))

    def const(arr):
        return pl.BlockSpec(arr.shape, lambda i: (0,) * arr.ndim)

    def per_seq(arr):
        return pl.BlockSpec((sp,) + arr.shape[1:], lambda i: (i,) + (0,) * (arr.ndim - 1))

    return pl.pallas_call(
        _ssd_sample_body,
        grid=(n_seq // sp,),
        in_specs=[blk(4), blk(5), blk(6), blk(2), blk(3), blk(7), blk(8), blk(9),
                  pl.BlockSpec((rows, LANE), lambda i: (b0 + i, 0)),
                  per_seq(ssm), per_seq(conv_state), per_seq(sc_state),
                  const(cw), const(cb), const(dtb), const(alog), const(dskip), const(nw), const(scw),
                  const(expand), const(gsel)],
        out_specs=[
            pl.BlockSpec((rows, 2 * d), lambda i: (i, 0)),
            pl.BlockSpec((rows, d), lambda i: (i, 0)),
            per_seq(ssm), per_seq(conv_state), per_seq(sc_state),
        ],
        out_shape=[
            jax.ShapeDtypeStruct((n_seq * seq_len, 2 * d), BF16),
            jax.ShapeDtypeStruct((n_seq * seq_len, d), BF16),
            jax.ShapeDtypeStruct(ssm.shape, F32),
            jax.ShapeDtypeStruct(conv_state.shape, F32),
            jax.ShapeDtypeStruct(sc_state.shape, F32),
        ],
        scratch_shapes=[pltpu.VMEM((sp, 3 * d // LANE, 2 * SUBLANE, LANE), F32),
                        pltpu.VMEM((sp, d // LANE, 2 * SUBLANE, LANE), F32)],
        compiler_params=pltpu.CompilerParams(
            dimension_semantics=("arbitrary",), vmem_limit_bytes=VMEM_LIMIT),
        name="ssd_sample",
    )(p, p, p, p, p, p, p, p, dt_raw, ssm, conv_state, sc_state,
      cw, cb, dtb, alog, dskip, nw, scw, expand, gsel)


def _branch_out_body(yap_ref, ybp_ref, yas_ref, ybs_ref, ga_ref, gb_ref, wa_ref, wb_ref, o_ref, *, prompt_tiles):
    def run(refs):
        ya_ref, yb_ref = refs
        pa = jnp.dot(ya_ref[...], wa_ref[...], preferred_element_type=F32)
        pb = jnp.dot(yb_ref[...], wb_ref[...], preferred_element_type=F32)
        merged = jax.nn.sigmoid(ga_ref[...]) * pa + jax.nn.sigmoid(gb_ref[...]) * pb
        o_ref[...] = merged.astype(o_ref.dtype)

    _by_stream(pl.program_id(0), prompt_tiles, run, (yap_ref, ybp_ref), (yas_ref, ybs_ref))


def _branch_out(ya_p, yb_p, ya_s, yb_s, p, wa, wb):
    t_p, d = yb_p.shape
    t_s = yb_s.shape[0]
    tm = _tile(math.gcd(t_p, t_s), 512, 16)
    tn = _tile(d, 512, LANE)
    nj = d // tn
    npt = t_p // tm
    yap_spec, yas_spec = _stream_specs((tm, 2 * d), npt)
    ybp_spec, ybs_spec = _stream_specs((tm, d), npt)
    return pl.pallas_call(
        functools.partial(_branch_out_body, prompt_tiles=npt),
        grid=((t_p + t_s) // tm, nj),
        in_specs=[
            yap_spec, ybp_spec, yas_spec, ybs_spec,
            pl.BlockSpec((tm, tn), lambda i, j: (i, j)),
            pl.BlockSpec((tm, tn), lambda i, j: (i, nj + j)),
            pl.BlockSpec((2 * d, tn), lambda i, j: (0, j)),
            pl.BlockSpec((d, tn), lambda i, j: (0, j)),
        ],
        out_specs=pl.BlockSpec((tm, tn), lambda i, j: (i, j)),
        out_shape=jax.ShapeDtypeStruct((t_p + t_s, d), BF16),
        compiler_params=pltpu.CompilerParams(
            dimension_semantics=("arbitrary", "arbitrary"), vmem_limit_bytes=VMEM_LIMIT),
        name="branch_out",
    )(ya_p, yb_p, ya_s, yb_s, p, p, wa, wb)


def _mix_route_body(m_ref, xp_ref, xs_ref, wo_ref, nw_ref, rhi_ref, rlo_ref,
                    x1_ref, h2_ref, route_ref, wts_ref, cnt_ref, *, n_coarse, per_group, prompt_tiles):
    @pl.when(pl.program_id(0) == 0)
    def _():
        cnt_ref[...] = jnp.zeros(cnt_ref.shape, F32)

    def run(x_ref):
        x1 = x_ref[...] + jnp.dot(m_ref[...], wo_ref[...], preferred_element_type=F32)
        x1_ref[...] = x1
        h2 = x1 * lax.rsqrt(jnp.mean(x1 * x1, axis=-1, keepdims=True) + NORM_EPS) * nw_ref[...]
        h2_ref[...] = h2
        h_hi, h_lo = _split2(h2)
        logits = (jnp.dot(h_hi, rhi_ref[...], preferred_element_type=F32)
                  + jnp.dot(h_hi, rlo_ref[...], preferred_element_type=F32)
                  + jnp.dot(h_lo, rhi_ref[...], preferred_element_type=F32))

        tm = logits.shape[0]
        n_fine = n_coarse * per_group
        lane = lax.broadcasted_iota(jnp.int32, logits.shape, 1)
        big = jnp.int32(LANE)
        neg = -jnp.inf
        is_c = lane < n_coarse
        lc = jnp.where(is_c, logits, neg)
        mc = jnp.max(lc, axis=-1, keepdims=True)
        grp = jnp.min(jnp.where(is_c & (lc == mc), lane, big), axis=-1, keepdims=True)
        p_grp = 1.0 / jnp.sum(jnp.where(is_c, jnp.exp(lc - mc), 0.0), axis=-1, keepdims=True)
        eidx = lane - n_coarse
        sel = (eidx >= 0) & (eidx < n_fine) & ((eidx // per_group) == grp)
        lf = jnp.where(sel, logits, neg)
        v1 = jnp.max(lf, axis=-1, keepdims=True)
        i1 = jnp.min(jnp.where(sel & (lf == v1), eidx, big), axis=-1, keepdims=True)
        sel2 = sel & (eidx != i1)
        lf2 = jnp.where(sel2, logits, neg)
        v2 = jnp.max(lf2, axis=-1, keepdims=True)
        i2 = jnp.min(jnp.where(sel2 & (lf2 == v2), eidx, big), axis=-1, keepdims=True)
        e2 = jnp.exp(v2 - v1)
        w1 = p_grp / (1.0 + e2)
        w2 = p_grp * e2 / (1.0 + e2)
        wts_ref[...] = jnp.where(lane == 0, w1, jnp.where(lane == 1, w2, 0.0))

        hit1 = eidx == i1
        hit2 = eidx == i2
        hits = hit1.astype(F32) + hit2.astype(F32)
        earlier = (lax.broadcasted_iota(jnp.int32, (tm, tm), 0)
                   > lax.broadcasted_iota(jnp.int32, (tm, tm), 1)).astype(BF16)
        before = jnp.dot(earlier, hits.astype(BF16), preferred_element_type=F32) + cnt_ref[...]
        r1 = jnp.sum(jnp.where(hit1, before, 0.0), axis=-1, keepdims=True).astype(jnp.int32)
        r2 = jnp.sum(jnp.where(hit2, before, 0.0), axis=-1, keepdims=True).astype(jnp.int32)
        cnt_ref[...] += jnp.sum(hits, axis=0, keepdims=True)
        route_ref[...] = jnp.where(lane == 0, i1, jnp.where(lane == 1, i2, jnp.where(
            lane == 2, r1, jnp.where(lane == 3, r2, 0))))

    _by_stream(pl.program_id(0), prompt_tiles, run, xp_ref, xs_ref)


def _mix_route(merged, x_p, x_s, wo, norm_w, r_hi, r_lo, n_coarse, per_group):
    t_p, d = x_p.shape
    T = t_p + x_s.shape[0]
    tm = _tile(math.gcd(t_p, x_s.shape[0]), 256, 16)
    npt = t_p // tm
    row = lambda i: (i, 0)
    fixed = lambda i: (0, 0)
    xp_spec, xs_spec = _stream_specs((tm, d), npt)
    return pl.pallas_call(
        functools.partial(_mix_route_body, n_coarse=n_coarse, per_group=per_group, prompt_tiles=npt),
        grid=(T // tm,),
        in_specs=[
            pl.BlockSpec((tm, d), row), xp_spec, xs_spec, pl.BlockSpec((d, d), fixed),
            pl.BlockSpec((1, d), fixed), pl.BlockSpec((d, LANE), fixed), pl.BlockSpec((d, LANE), fixed),
        ],
        out_specs=[pl.BlockSpec((tm, d), row), pl.BlockSpec((tm, d), row),
                   pl.BlockSpec((tm, LANE), row), pl.BlockSpec((tm, LANE), row),
                   pl.BlockSpec((1, LANE), fixed)],
        out_shape=[
            jax.ShapeDtypeStruct((T, d), F32), jax.ShapeDtypeStruct((T, d), F32),
            jax.ShapeDtypeStruct((T, LANE), jnp.int32), jax.ShapeDtypeStruct((T, LANE), F32),
            jax.ShapeDtypeStruct((1, LANE), F32),
        ],
        compiler_params=pltpu.CompilerParams(
            dimension_semantics=("arbitrary",), vmem_limit_bytes=VMEM_LIMIT),
        name="mix_route",
    )(merged, x_p, x_s, wo, norm_w, r_hi, r_lo)


MOE_MOVE_TOKENS = 128
MOE_MOVE_UNROLL = 8


def _moe_scatter_body(dest_ref, h_ref, xs_in, xs_hbm, sem):
    del xs_in
    tt = h_ref.shape[0]
    t0 = pl.program_id(0) * tt

    def start(i, carry):
        for k in range(MOE_TOP_K):
            pltpu.make_async_copy(
                h_ref.at[pl.ds(i, 1)], xs_hbm.at[pl.ds(dest_ref[(t0 + i) * MOE_TOP_K + k], 1)], sem
            ).start(priority=k % 2)
        return carry

    lax.fori_loop(0, tt, start, 0, unroll=MOE_MOVE_UNROLL)
    for _ in range(MOE_TOP_K):
        pltpu.make_async_copy(h_ref, xs_hbm.at[pl.ds(0, tt)], sem).wait()


def _moe_scatter(dest, h2, xs_zero):
    T, d = h2.shape
    tt = MOE_MOVE_TOKENS
    assert T % tt == 0
    any_spec = pl.BlockSpec(memory_space=pl.ANY)
    return pl.pallas_call(
        _moe_scatter_body,
        grid_spec=pltpu.PrefetchScalarGridSpec(
            num_scalar_prefetch=1,
            grid=(T // tt,),
            in_specs=[pl.BlockSpec((tt, d), lambda i, dr: (i, 0)), any_spec],
            out_specs=any_spec,
            scratch_shapes=[pltpu.SemaphoreType.DMA(())],
        ),
        out_shape=jax.ShapeDtypeStruct(xs_zero.shape, xs_zero.dtype),
        input_output_aliases={2: 0},
        compiler_params=pltpu.CompilerParams(dimension_semantics=("arbitrary",)),
        name="moe_scatter",
    )(dest, h2, xs_zero)


MOE_WEIGHT_PIECES = 8
MOE_WEIGHT_RING = 3


def _moe_ffn_body(sched_ref, n_used_ref, x_ref, wg_hbm, wu_hbm, wd_hbm, o_ref,
                  wg_b, wu_b, wd_b, sg, su, sd, sems):
    b = pl.program_id(0)
    hbm = (wg_hbm, wu_hbm, wd_hbm)
    stage = (sg, su, sd)
    resident = (wg_b, wu_b, wd_b)

    def copies(e, piece, ring):
        out = []
        for k in range(3):
            rows = stage[k].shape[1]
            out.append(pltpu.make_async_copy(
                hbm[k].at[0, e, pl.ds(pl.multiple_of(piece * rows, rows), rows), :],
                stage[k].at[ring], sems.at[ring, k]))
        return out

    def start(e, piece, ring):
        for c in copies(e, piece, ring):
            c.start()

    def land(e, piece, ring, slot):
        for k, c in enumerate(copies(e, piece, ring)):
            c.wait()
            rows = stage[k].shape[1]
            resident[k][slot, pl.ds(pl.multiple_of(piece * rows, rows), rows), :] = stage[k][ring].astype(BF16)

    def stream(e, first, count, slot):
        def body(c, carry):
            ring = c % MOE_WEIGHT_RING
            land(e, first + c, ring, slot)

            @pl.when(c + MOE_WEIGHT_RING < count)
            def _():
                start(e, first + c + MOE_WEIGHT_RING, ring)
            return carry

        lax.fori_loop(0, count, body, 0)

    def prime(e, first, count):
        for r in range(MOE_WEIGHT_RING):
            @pl.when(r < count)
            def _():
                start(e, first + r, r)

    @pl.when(b < n_used_ref[0])
    def _():
        slot = sched_ref[0, b]

        @pl.when(b == 0)
        def _():
            first_expert = sched_ref[1, pl.num_programs(0)]
            prime(first_expert, 0, MOE_WEIGHT_PIECES)
            stream(first_expert, 0, MOE_WEIGHT_PIECES, slot)

        nxt, first, count = sched_ref[1, b], sched_ref[2, b], sched_ref[3, b]
        prime(nxt, first, count)
        xb = x_ref[...].astype(BF16)
        gate = jnp.dot(xb, wg_b[slot], preferred_element_type=F32)
        up = jnp.dot(xb, wu_b[slot], preferred_element_type=F32)
        act = (_silu(gate) * up).astype(BF16)
        o_ref[...] = jnp.dot(act, wd_b[slot], preferred_element_type=F32)
        stream(nxt, first, count, 1 - slot)

    @pl.when(b >= n_used_ref[0])
    def _():
        o_ref[...] = jnp.zeros(o_ref.shape, o_ref.dtype)


def _moe_ffn(sched, n_used, xs, wg, wu, wd):
    R, d = xs.shape
    f = wg.shape[3]
    nb = R // MOE_ROWS
    np_ = MOE_WEIGHT_PIECES
    assert d % np_ == 0 and f % np_ == 0
    any_spec = pl.BlockSpec(memory_space=pl.ANY)
    return pl.pallas_call(
        _moe_ffn_body,
        grid_spec=pltpu.PrefetchScalarGridSpec(
            num_scalar_prefetch=2,
            grid=(nb,),
            in_specs=[pl.BlockSpec((MOE_ROWS, d), lambda b, sc, nu: (b, 0)), any_spec, any_spec, any_spec],
            out_specs=pl.BlockSpec((MOE_ROWS, d), lambda b, sc, nu: (b, 0)),
            scratch_shapes=[
                pltpu.VMEM((2, d, f), BF16), pltpu.VMEM((2, d, f), BF16), pltpu.VMEM((2, f, d), BF16),
                pltpu.VMEM((MOE_WEIGHT_RING, d // np_, f), F32), pltpu.VMEM((MOE_WEIGHT_RING, d // np_, f), F32),
                pltpu.VMEM((MOE_WEIGHT_RING, f // np_, d), F32),
                pltpu.SemaphoreType.DMA((MOE_WEIGHT_RING, 3)),
            ],
        ),
        out_shape=jax.ShapeDtypeStruct((R, d), F32),
        compiler_params=pltpu.CompilerParams(
            dimension_semantics=("arbitrary",), vmem_limit_bytes=VMEM_LIMIT),
        name="moe_ffn",
    )(sched, n_used, xs, wg, wu, wd)


def _moe_combine_body(dest_ref, x1_ref, wts_ref, nw_ref, yb_hbm, op_ref, os_ref, buf, sems, *, prompt_tiles):
    tt = x1_ref.shape[0]
    i = pl.program_id(0)

    def gather(tile, slot):
        def start(r, carry):
            for k in range(MOE_TOP_K):
                pltpu.make_async_copy(
                    yb_hbm.at[pl.ds(dest_ref[(tile * tt + r) * MOE_TOP_K + k], 1)],
                    buf.at[slot, k, pl.ds(r, 1)], sems.at[slot]).start(priority=k % 2)
            return carry

        lax.fori_loop(0, tt, start, 0, unroll=MOE_MOVE_UNROLL)

    @pl.when(i == 0)
    def _():
        gather(0, 0)

    @pl.when(i + 1 < pl.num_programs(0))
    def _():
        gather(i + 1, (i + 1) % 2)

    slot = i % 2
    for k in range(MOE_TOP_K):
        pltpu.make_async_copy(yb_hbm.at[pl.ds(0, tt)], buf.at[slot, k], sems.at[slot]).wait()

    def finish(o_ref):
        w = wts_ref[...]
        x2 = x1_ref[...] + (buf[slot, 0] * w[:, 0:1] + buf[slot, 1] * w[:, 1:2])
        o_ref[...] = x2 * lax.rsqrt(jnp.mean(x2 * x2, axis=-1, keepdims=True) + NORM_EPS) * nw_ref[...]

    _by_stream(i, prompt_tiles, finish, op_ref, os_ref)


def _moe_combine(dest, x1, wts, norm_w, yb, t_p):
    T, d = x1.shape
    tt = MOE_MOVE_TOKENS
    assert t_p % tt == 0 and T % tt == 0
    npt = t_p // tt
    op_spec, os_spec = _stream_specs((tt, d), npt)
    return pl.pallas_call(
        functools.partial(_moe_combine_body, prompt_tiles=npt),
        grid_spec=pltpu.PrefetchScalarGridSpec(
            num_scalar_prefetch=1,
            grid=(T // tt,),
            in_specs=[
                pl.BlockSpec((tt, d), lambda i, dr: (i, 0)),
                pl.BlockSpec((tt, LANE), lambda i, dr: (i, 0)),
                pl.BlockSpec((1, d), lambda i, dr: (0, 0)),
                pl.BlockSpec(memory_space=pl.ANY),
            ],
            out_specs=[op_spec, os_spec],
            scratch_shapes=[pltpu.VMEM((2, MOE_TOP_K, tt, d), F32), pltpu.SemaphoreType.DMA((2,))],
        ),
        out_shape=[jax.ShapeDtypeStruct((t_p, d), F32), jax.ShapeDtypeStruct((T - t_p, d), F32)],
        compiler_params=pltpu.CompilerParams(dimension_semantics=("arbitrary",)),
        name="moe_combine",
    )(dest, x1, wts, norm_w, yb)


def _route_rows(eid, rank, counts, n_assign):
    n_experts = counts.shape[0]
    padded = (counts + MOE_ROWS - 1) // MOE_ROWS * MOE_ROWS
    pend = jnp.cumsum(padded)
    pstart = pend - padded
    experts = jnp.arange(n_experts, dtype=jnp.int32)
    onehot = eid[:, :, None] == experts[None, None, :]
    dest = (jnp.sum(jnp.where(onehot, pstart[None, None, :], 0), axis=-1) + rank).astype(jnp.int32)
    n_blocks = -(-n_assign // MOE_ROWS) + n_experts
    blk_start = jnp.arange(n_blocks, dtype=jnp.int32) * MOE_ROWS
    used = blk_start < pend[-1]
    blk_e = jnp.minimum(jnp.sum((blk_start[:, None] >= pend[None, :]).astype(jnp.int32), axis=1), n_experts - 1)
    n_used = (pend[-1] // MOE_ROWS).astype(jnp.int32).reshape(1)

    nonempty = counts > 0
    ordinal = jnp.cumsum(nonempty.astype(jnp.int32)) - 1
    later = (experts[None, :] > experts[:, None]) & nonempty[None, :]
    nxt = jnp.min(jnp.where(later, experts[None, :], n_experts), axis=1)
    pos = blk_start // MOE_ROWS - (pstart // MOE_ROWS)[blk_e]
    nblk = jnp.maximum((padded // MOE_ROWS)[blk_e], 1)
    has_next = used & (nxt[blk_e] < n_experts)
    first = jnp.where(has_next, MOE_WEIGHT_PIECES * pos // nblk, 0)
    last = jnp.where(has_next, MOE_WEIGHT_PIECES * (pos + 1) // nblk, 0)
    slot = jnp.where(used, ordinal[blk_e] % 2, 0)
    per_block = jnp.stack([slot, jnp.where(has_next, nxt[blk_e], 0), first, last - first]).astype(jnp.int32)
    first_expert = jnp.stack([slot[0], blk_e[0], 0, MOE_WEIGHT_PIECES]).astype(jnp.int32)[:, None]
    return dest.reshape(-1), jnp.concatenate([per_block, first_expert], axis=1), n_used, n_blocks


def _pad_lanes(v, fill=0.0):
    return jnp.pad(v.astype(F32), (0, LANE - v.shape[0]), constant_values=fill).reshape(1, LANE)


def kernel(x_prompt, x_sample, state_ssm, state_ssd_conv, state_short_conv, norm_mixer, w_in, ssd_conv_w,
           ssd_conv_b, ssd_dt_bias, ssd_a_log, ssd_d, ssd_norm, sc_conv_w, w_branch_out, w_out, norm_ffn,
           w_router_coarse, w_router_fine, w_expert_gate, w_expert_up, w_expert_down, norm_final):
    depth = w_in.shape[0]
    assert depth == 1
    n_p, seq_p, d = x_prompt.shape
    n_s, seq_s, _ = x_sample.shape
    d_inner = 2 * d
    n_heads = d_inner // HEAD_DIM
    n_groups = d_inner // GROUP_W
    gn = n_groups * STATE_DIM
    conv_dim = d_inner + 2 * gn
    assert conv_dim == 3 * d and n_heads <= LANE and ssd_conv_w.shape[2] == conv_dim
    t_p, t_s = n_p * seq_p, n_s * seq_s
    n_coarse = w_router_coarse.shape[2]
    n_experts = w_router_fine.shape[2]
    assert n_coarse + n_experts <= LANE

    off_dt = 2 * d + d_inner + conv_dim
    off_sc = off_dt + n_heads
    w_in_t = jnp.swapaxes(w_in, 1, 2)
    w_tail_t = w_in_t[0, off_sc:, :]
    w_dt = jnp.pad(w_in_t[0, off_dt:off_sc, :].T, ((0, 0), (0, LANE - n_heads))).astype(BF16)
    head_of_col = jnp.arange(d_inner, dtype=jnp.int32) // HEAD_DIM
    expand = (jnp.arange(LANE, dtype=jnp.int32)[:, None] == head_of_col[None, :]).astype(BF16)
    group_of_n = jnp.arange(gn, dtype=jnp.int32) // STATE_DIM
    group_of_head = jnp.arange(LANE, dtype=jnp.int32) // HEADS_PER_GROUP
    gsel = ((group_of_n[:, None] == group_of_head[None, :])
            & (jnp.arange(LANE)[None, :] < n_heads)).astype(BF16)
    consts = (
        ssd_conv_w[0], ssd_conv_b[0].reshape(1, conv_dim), _pad_lanes(ssd_dt_bias[0]), _pad_lanes(ssd_a_log[0]),
        jnp.repeat(ssd_d[0].astype(F32), HEAD_DIM).reshape(1, d_inner), ssd_norm[0].reshape(1, d_inner),
        sc_conv_w[0], expand,
    )
    wbo = w_branch_out[0].astype(BF16)
    wa, wb = wbo[:d_inner], wbo[d_inner:]
    wo = w_out[0].astype(BF16)
    w_router = jnp.pad(jnp.concatenate([w_router_coarse[0], w_router_fine[0]], axis=1),
                       ((0, 0), (0, LANE - n_coarse - n_experts)))
    r_hi = w_router.astype(BF16)
    r_lo = (w_router - r_hi.astype(F32)).astype(BF16)

    x_p = x_prompt.reshape(t_p, d)
    x_s = x_sample.reshape(t_s, d)
    h, dt_raw = _prenorm(x_p, x_s, norm_mixer[0].reshape(1, d), w_dt)
    p, tails = _inproj(h, w_in_t, w_tail_t, off_dt, consts[0], consts[1], t_p, seq_p)
    ya_p, yb_p, p_ssm, p_sc_tail = _ssd_prompt(p, dt_raw, consts, n_p, seq_p, d)
    ya_s, yb_s, s_ssm, s_conv, s_sc = _ssd_sample(
        p, dt_raw, state_ssm[0].reshape(n_s, n_groups, GROUP_W, STATE_DIM), state_ssd_conv[0],
        state_short_conv[0], consts, gsel, t_p, seq_s, d)
    merged = _branch_out(ya_p, yb_p, ya_s, yb_s, p, wa, wb)
    x1, h2, route, wts, counts = _mix_route(merged, x_p, x_s, wo, norm_ffn[0].reshape(1, d), r_hi, r_lo,
                                            n_coarse, n_experts // n_coarse)

    n_assign = (t_p + t_s) * MOE_TOP_K
    dest, sched, n_used, n_blocks = _route_rows(
        route[:, 0:MOE_TOP_K], route[:, MOE_TOP_K:2 * MOE_TOP_K],
        counts[0, n_coarse:n_coarse + n_experts].astype(jnp.int32), n_assign)
    xs = _moe_scatter(dest, h2, jnp.zeros((n_blocks * MOE_ROWS, d), F32))
    yrows = _moe_ffn(sched, n_used, xs, w_expert_gate, w_expert_up, w_expert_down)
    out_p, out_s = _moe_combine(dest, x1, wts, norm_final.reshape(1, d), yrows, t_p)

    kw = ssd_conv_w.shape[1]
    kw2 = sc_conv_w.shape[1]
    tiles_per_seq = (tails.shape[0] // SUBLANE) * seq_p // (t_p + t_s)
    seq_tails = tails.reshape(-1, SUBLANE, tails.shape[1])[tiles_per_seq - 1:n_p * tiles_per_seq:tiles_per_seq]
    p_conv = seq_tails[:, SUBLANE - (kw - 1):, 4 * d:7 * d]
    return (
        out_p.reshape(n_p, seq_p, d),
        out_s.reshape(n_s, seq_s, d),
        p_ssm.reshape(1, n_p, n_heads, HEAD_DIM, STATE_DIM),
        p_conv[None],
        p_sc_tail[:, SUBLANE - (kw2 - 1):, :][None],
        s_ssm.reshape(1, n_s, n_heads, HEAD_DIM, STATE_DIM),
        s_conv[None],
        s_sc[None],
    )
```

```python
import functools
import math

import jax
import jax.numpy as jnp
from jax import lax
from jax.experimental import pallas as pl
from jax.experimental.pallas import tpu as pltpu

F32 = jnp.float32
BF16 = jnp.bfloat16

NORM_EPS = 1e-6
SSD_NORM_EPS = 1e-5
HEAD_DIM = 64
STATE_DIM = 128
HEADS_PER_GROUP = 8
GROUP_W = HEADS_PER_GROUP * HEAD_DIM
SSD_CHUNK = 128
MOE_TOP_K = 2
MOE_ROWS = 256

LANE = 128
SUBLANE = 8
VMEM_LIMIT = 56 * 1024 * 1024

NT_DIMS = (((1,), (1,)), ((), ()))
TN_DIMS = (((0,), (0,)), ((), ()))


def _tile(n, target, align):
    best = None
    for t in range(align, min(n, target) + 1, align):
        if n % t == 0:
            best = t
    assert best is not None, (n, target, align)
    return best


def _split2(v):
    hi = v.astype(BF16)
    lo = (v - hi.astype(F32)).astype(BF16)
    return hi, lo


def _split3(v):
    hi = v.astype(BF16)
    r = v - hi.astype(F32)
    mid = r.astype(BF16)
    lo = (r - mid.astype(F32)).astype(BF16)
    return hi, mid, lo


def _softplus(x):
    return jnp.maximum(x, 0.0) + jnp.log1p(jnp.exp(-jnp.abs(x)))


def _silu(x):
    return x * jax.nn.sigmoid(x)


def _by_stream(tile, prompt_tiles, fn, prompt_ref, sample_ref):
    @pl.when(tile < prompt_tiles)
    def _():
        fn(prompt_ref)

    @pl.when(tile >= prompt_tiles)
    def _():
        fn(sample_ref)


def _stream_specs(block, prompt_tiles, **kwargs):
    pad = (0,) * (len(block) - 1)
    prompt = lambda i, *_: (jnp.minimum(i, prompt_tiles - 1),) + pad
    sample = lambda i, *_: (jnp.maximum(i - prompt_tiles, 0),) + pad
    return pl.BlockSpec(block, prompt, **kwargs), pl.BlockSpec(block, sample, **kwargs)


def _prenorm_body(xp_ref, xs_ref, nw_ref, wdt_ref, h_ref, dt_ref, *, prompt_tiles):
    def run(x_ref):
        x = x_ref[...]
        h = x * lax.rsqrt(jnp.mean(x * x, axis=-1, keepdims=True) + NORM_EPS) * nw_ref[...]
        hb = h.astype(BF16)
        h_ref[...] = hb
        dt_ref[...] = jnp.dot(hb, wdt_ref[...], preferred_element_type=F32)

    _by_stream(pl.program_id(0), prompt_tiles, run, xp_ref, xs_ref)


def _prenorm(x_p, x_s, norm_w, w_dt):
    t_p, D = x_p.shape
    T = t_p + x_s.shape[0]
    tm = _tile(math.gcd(t_p, x_s.shape[0]), 512, 16)
    npt = t_p // tm
    xp_spec, xs_spec = _stream_specs((tm, D), npt)
    return pl.pallas_call(
        functools.partial(_prenorm_body, prompt_tiles=npt),
        grid=(T // tm,),
        in_specs=[xp_spec, xs_spec, pl.BlockSpec((1, D), lambda i: (0, 0)),
                  pl.BlockSpec((D, LANE), lambda i: (0, 0))],
        out_specs=[pl.BlockSpec((tm, D), lambda i: (i, 0)), pl.BlockSpec((tm, LANE), lambda i: (i, 0))],
        out_shape=[jax.ShapeDtypeStruct((T, D), BF16), jax.ShapeDtypeStruct((T, LANE), F32)],
        compiler_params=pltpu.CompilerParams(
            dimension_semantics=("arbitrary",), vmem_limit_bytes=VMEM_LIMIT),
        name="prenorm",
    )(x_p, x_s, norm_w, w_dt)


INPROJ_CONV_SLABS = 4


def _inproj_body(h_ref, wa_ref, wb_ref, cw_ref, cb_ref, p_ref, tail_ref, w_scr, cext,
                 *, main_tiles, blocks_per_d, prompt_tiles, tiles_per_seq):
    j = pl.program_id(0)
    i = pl.program_id(1)
    tm, tn = p_ref.shape
    kw = cw_ref.shape[0]

    @pl.when(i == 0)
    def _():
        def cast(w_ref):
            w_scr[...] = w_ref[...].astype(BF16)

        _by_stream(j, main_tiles, cast, wa_ref, wb_ref)

    is_conv = (j >= 4 * blocks_per_d) & (j < 7 * blocks_per_d) & (i < prompt_tiles)

    @pl.when(jnp.logical_not(is_conv))
    def _():
        raw = lax.dot_general(h_ref[...], w_scr[...], NT_DIMS, preferred_element_type=F32)
        p_ref[...] = raw
        tail_ref[...] = raw[tm - SUBLANE:tm, :]

    @pl.when(is_conv)
    def _():
        @pl.when(i % tiles_per_seq == 0)
        def _():
            cext[:, 0:SUBLANE, :] = jnp.zeros((cext.shape[0], SUBLANE, LANE), F32)

        ws = tn // INPROJ_CONV_SLABS
        first = SUBLANE - (kw - 1)
        for k in range(INPROJ_CONV_SLABS):
            raw = lax.dot_general(h_ref[...], w_scr[k * ws:(k + 1) * ws, :], NT_DIMS, preferred_element_type=F32)
            for s in range(k * ws // LANE, (k + 1) * ws // LANE):
                lo, hi = s * LANE, (s + 1) * LANE
                cext[s, SUBLANE:SUBLANE + tm, :] = raw[:, lo - k * ws:hi - k * ws]
                acc = cb_ref[:, lo:hi]
                for t in range(kw):
                    acc = acc + cw_ref[t:t + 1, lo:hi] * cext[s, first + t:first + t + tm, :]
                p_ref[:, lo:hi] = _silu(acc)
                tail = cext[s, tm:tm + SUBLANE, :]
                tail_ref[:, lo:hi] = tail
                cext[s, 0:SUBLANE, :] = tail


def _inproj(h, w_in_t, w_tail_t, n_head_cols, cw, cb, t_p, seq_p):
    T, D = h.shape
    tm = _tile(math.gcd(seq_p, T - t_p), 1024, 16)
    tn = _tile(math.gcd(D, w_tail_t.shape[0]), 1024, LANE)
    main_tiles = n_head_cols // tn
    n_out = n_head_cols + w_tail_t.shape[0]
    bpd = D // tn
    conv_block = lambda j, i: (0, jnp.clip(j - 4 * bpd, 0, 3 * bpd - 1))
    return pl.pallas_call(
        functools.partial(_inproj_body, main_tiles=main_tiles, blocks_per_d=bpd,
                          prompt_tiles=t_p // tm, tiles_per_seq=seq_p // tm),
        grid=(n_out // tn, T // tm),
        in_specs=[
            pl.BlockSpec((tm, D), lambda j, i: (i, 0)),
            pl.BlockSpec((None, tn, D), lambda j, i: (0, jnp.minimum(j, main_tiles - 1), 0)),
            pl.BlockSpec((tn, D), lambda j, i: (jnp.maximum(j - main_tiles, 0), 0),
                         pipeline_mode=pl.Buffered(1)),
            pl.BlockSpec((cw.shape[0], tn), conv_block),
            pl.BlockSpec((1, tn), conv_block),
        ],
        out_specs=[pl.BlockSpec((tm, tn), lambda j, i: (i, j)),
                   pl.BlockSpec((SUBLANE, tn), lambda j, i: (i, j))],
        out_shape=[jax.ShapeDtypeStruct((T, n_out), F32),
                   jax.ShapeDtypeStruct((T // tm * SUBLANE, n_out), F32)],
        scratch_shapes=[pltpu.VMEM((tn, D), BF16), pltpu.VMEM((tn // LANE, tm + SUBLANE, LANE), F32)],
        compiler_params=pltpu.CompilerParams(
            dimension_semantics=("arbitrary", "arbitrary"), vmem_limit_bytes=VMEM_LIMIT),
        name="inproj",
    )(h, w_in_t, w_tail_t, cw, cb)


def _slab_store(ext, row0, value, col0=0):
    for s in range(value.shape[1] // LANE):
        ext[col0 // LANE + s, row0:row0 + value.shape[0], :] = value[:, s * LANE:(s + 1) * LANE]


def _slab_load(ext, row0, rows, lo, hi):
    return jnp.concatenate([ext[s, row0:row0 + rows, :] for s in range(lo // LANE, hi // LANE)], axis=1)


def _conv(ext, lo, hi, q, width, w_ref, first):
    parts = []
    for s in range(lo // LANE, hi // LANE):
        acc = None
        for k in range(width):
            term = w_ref[k:k + 1, s * LANE:(s + 1) * LANE] * ext[s, first + k:first + k + q, :]
            acc = term if acc is None else acc + term
        parts.append(acc)
    return jnp.concatenate(parts, axis=1)


def _gated_norm(y, z, nw):
    g = y * _silu(z)
    return g * lax.rsqrt(jnp.mean(g * g, axis=-1, keepdims=True) + SSD_NORM_EPS) * nw


def _decay_col(cs_last_row, g):
    d = jnp.exp(cs_last_row)
    parts = [
        jnp.broadcast_to(d[0:1, g * HEADS_PER_GROUP + j:g * HEADS_PER_GROUP + j + 1], (HEAD_DIM, STATE_DIM))
        for j in range(HEADS_PER_GROUP)
    ]
    return jnp.concatenate(parts, axis=0)


def _ssd_prompt_body(x4, x5, x6, z2, z3, scb, scc, sch, dtr,
                     dtb, alog, dskip, nw, scw, expand,
                     ya_ref, yb_ref, st_ref, sct_ref, ext2):
    q, d = x4.shape
    n_groups = st_ref.shape[1]
    gn = n_groups * STATE_DIM
    kw2 = scw.shape[0]

    @pl.when(pl.program_id(1) == 0)
    def _():
        st_ref[...] = jnp.zeros(st_ref.shape, F32)
        ext2[:, 0:SUBLANE, :] = jnp.zeros((ext2.shape[0], SUBLANE, LANE), F32)

    _slab_store(ext2, SUBLANE, scc[...] * sch[...])
    v = _conv(ext2, 0, d, q, kw2, scw, SUBLANE - (kw2 - 1))
    yb_ref[...] = (scb[...] * v).astype(yb_ref.dtype)
    tail2 = _slab_load(ext2, q, SUBLANE, 0, d)
    _slab_store(ext2, 0, tail2)
    sct_ref[0] = tail2

    dt = _softplus(dtr[...] + dtb[...])
    a = dt * (-jnp.exp(alog[...]))
    row = lax.broadcasted_iota(jnp.int32, (q, q), 0)
    col = lax.broadcasted_iota(jnp.int32, (q, q), 1)
    causal = row >= col
    tri = causal.astype(BF16)
    cs = sum(jnp.dot(tri, part, preferred_element_type=F32) for part in _split3(a))
    cs_t = cs.T
    cs_last = cs[q - 1:q, :]
    dend = jnp.exp(cs_last - cs)
    ecs = jnp.exp(cs)
    stacked = jnp.concatenate([dt, dend, ecs], axis=0)
    st_b = stacked.astype(BF16)

    lane = lax.broadcasted_iota(jnp.int32, (q, LANE), 1)
    groups_per_block = d // GROUP_W

    for g in range(n_groups):
        c0 = g * GROUP_W
        xref = x4 if g < groups_per_block else x5
        bc = (g % groups_per_block) * GROUP_W
        xs = xref[:, bc:bc + GROUP_W]
        e_g = expand[:, c0:c0 + GROUP_W]
        ex = jnp.dot(st_b, e_g, preferred_element_type=F32)
        xdt = xs * ex[0:q]
        xdt_b = xdt.astype(BF16)
        xdd_b = (xdt * ex[q:2 * q]).astype(BF16)
        bg = x6[:, g * STATE_DIM:(g + 1) * STATE_DIM].astype(BF16)
        cg = x6[:, gn + g * STATE_DIM:gn + (g + 1) * STATE_DIM].astype(BF16)
        cbm = lax.dot_general(cg, bg, NT_DIMS, preferred_element_type=F32)
        state = st_ref[0, g]
        y_off = lax.dot_general(cg, state.astype(BF16), NT_DIMS, preferred_element_type=F32)
        y_parts = []
        for j in range(HEADS_PER_GROUP // 2):
            scores = []
            for h in (g * HEADS_PER_GROUP + 2 * j, g * HEADS_PER_GROUP + 2 * j + 1):
                seg = cs[:, h:h + 1] - cs_t[h:h + 1, :]
                dec = jnp.exp(jnp.where(causal, seg, -jnp.inf))
                scores.append((cbm * dec).astype(BF16))
            xp = xdt_b[:, j * LANE:(j + 1) * LANE]
            zero = jnp.zeros_like(xp)
            rhs = jnp.concatenate(
                [jnp.where(lane < HEAD_DIM, xp, zero), jnp.where(lane >= HEAD_DIM, xp, zero)], axis=0)
            y_parts.append(jnp.dot(jnp.concatenate(scores, axis=1), rhs, preferred_element_type=F32))
        y = jnp.concatenate(y_parts, axis=1) + y_off * ex[2 * q:3 * q] + xs * dskip[:, c0:c0 + GROUP_W]
        zref = z2 if g < groups_per_block else z3
        ya_ref[:, c0:c0 + GROUP_W] = _gated_norm(
            y, zref[:, bc:bc + GROUP_W], nw[:, c0:c0 + GROUP_W]).astype(ya_ref.dtype)
        st_ref[0, g] = state * _decay_col(cs_last, g) + lax.dot_general(
            xdd_b, bg, TN_DIMS, preferred_element_type=F32)


def _ssd_prompt(p, dt_raw, consts, n_seq, seq_len, d):
    (_, _, dtb, alog, dskip, nw, scw, expand) = consts
    q = SSD_CHUNK if seq_len % SSD_CHUNK == 0 else seq_len
    nc = seq_len // q
    n_groups = 2 * d // GROUP_W
    t_p = n_seq * seq_len
    assert 2 * n_groups * STATE_DIM == d

    def blk(cidx):
        return pl.BlockSpec((q, d), lambda b, c, cidx=cidx: (b * nc + c, cidx))

    def const(arr):
        return pl.BlockSpec(arr.shape, lambda b, c: (0,) * arr.ndim)

    return pl.pallas_call(
        _ssd_prompt_body,
        grid=(n_seq, nc),
        in_specs=[blk(4), blk(5), blk(6), blk(2), blk(3), blk(7), blk(8), blk(9),
                  pl.BlockSpec((q, LANE), lambda b, c: (b * nc + c, 0)),
                  const(dtb), const(alog), const(dskip), const(nw), const(scw), const(expand)],
        out_specs=[
            pl.BlockSpec((q, 2 * d), lambda b, c: (b * nc + c, 0)),
            pl.BlockSpec((q, d), lambda b, c: (b * nc + c, 0)),
            pl.BlockSpec((1, n_groups, GROUP_W, STATE_DIM), lambda b, c: (b, 0, 0, 0)),
            pl.BlockSpec((1, SUBLANE, d), lambda b, c: (b, 0, 0)),
        ],
        out_shape=[
            jax.ShapeDtypeStruct((t_p, 2 * d), BF16),
            jax.ShapeDtypeStruct((t_p, d), BF16),
            jax.ShapeDtypeStruct((n_seq, n_groups, GROUP_W, STATE_DIM), F32),
            jax.ShapeDtypeStruct((n_seq, SUBLANE, d), F32),
        ],
        scratch_shapes=[pltpu.VMEM((d // LANE, q + SUBLANE, LANE), F32)],
        compiler_params=pltpu.CompilerParams(
            dimension_semantics=("arbitrary", "arbitrary"), vmem_limit_bytes=VMEM_LIMIT),
        name="ssd_prompt",
    )(p, p, p, p, p, p, p, p, dt_raw, dtb, alog, dskip, nw, scw, expand)


SAMPLE_SEQS_PER_STEP = 2


def _ssd_sample_body(x4, x5, x6, z2, z3, scb, scc, sch, dtr, ssm_in, conv_in, sc_in,
                     cw, cb, dtb, alog, dskip, nw, scw, expand, gsel,
                     ya_ref, yb_ref, ssm_out, conv_out, sc_out, exts, ext2s):
    d = x4.shape[1]
    n_groups = ssm_in.shape[1]
    gn = n_groups * STATE_DIM
    d_inner = n_groups * GROUP_W
    kw = cw.shape[0]
    kw2 = scw.shape[0]
    q = x4.shape[0] // SAMPLE_SEQS_PER_STEP
    groups_per_block = d // GROUP_W
    first = SUBLANE - (kw - 1)
    first2 = SUBLANE - (kw2 - 1)
    nrep = q * q

    rep_t = lax.broadcasted_iota(jnp.int32, (nrep, LANE), 0) % q
    rep_s = lax.broadcasted_iota(jnp.int32, (nrep, LANE), 0) // q
    rep_causal = rep_t >= rep_s
    row_q = lax.broadcasted_iota(jnp.int32, (q, LANE), 0)

    def rep_rows(m):
        return jnp.concatenate([jnp.broadcast_to(m[s:s + 1], (q, m.shape[1])) for s in range(q)], axis=0)

    def tile_rows(m):
        return jnp.concatenate([m] * q, axis=0)

    yb_rows = []
    ya_rows = [[] for _ in range(n_groups)]
    for sidx in range(SAMPLE_SEQS_PER_STEP):
        r0 = sidx * q
        ext, ext2 = exts.at[sidx], ext2s.at[sidx]
        _slab_store(ext, first, conv_in[sidx])
        _slab_store(ext, SUBLANE, x4[r0:r0 + q, :])
        _slab_store(ext, SUBLANE, x5[r0:r0 + q, :], d)
        _slab_store(ext, SUBLANE, x6[r0:r0 + q, :], 2 * d)
        conv_out[sidx] = _slab_load(ext, SUBLANE + q - (kw - 1), kw - 1, 0, 3 * d)

        _slab_store(ext2, first2, sc_in[sidx])
        _slab_store(ext2, SUBLANE, scc[r0:r0 + q, :] * sch[r0:r0 + q, :])
        yb_rows.append(scb[r0:r0 + q, :] * _conv(ext2, 0, d, q, kw2, scw, first2))
        sc_out[sidx] = _slab_load(ext2, SUBLANE + q - (kw2 - 1), kw2 - 1, 0, d)

        dt = _softplus(dtr[r0:r0 + q, :] + dtb[...])
        a = dt * (-jnp.exp(alog[...]))
        cs = jnp.zeros((q, LANE), F32)
        for r in range(q):
            cs = cs + jnp.where(row_q >= r, jnp.broadcast_to(a[r:r + 1], (q, LANE)), 0.0)
        cs_last = cs[q - 1:q, :]
        dend = jnp.exp(cs_last - cs)
        ecs = jnp.exp(cs)

        bmat = _silu(_conv(ext, 2 * d, 2 * d + gn, q, kw, cw, first) + cb[:, 2 * d:2 * d + gn])
        cmat = _silu(_conv(ext, 2 * d + gn, 3 * d, q, kw, cw, first) + cb[:, 2 * d + gn:3 * d])

        cb_hi, cb_lo = _split2(tile_rows(cmat) * rep_rows(bmat))
        cbh = (jnp.dot(cb_hi, gsel[...], preferred_element_type=F32)
               + jnp.dot(cb_lo, gsel[...], preferred_element_type=F32))
        dec = jnp.exp(jnp.where(rep_causal, tile_rows(cs) - rep_rows(cs), -jnp.inf))
        stacked = jnp.concatenate([dt, dend, ecs], axis=0)
        st_hi = stacked.astype(BF16).astype(F32)
        pad = jnp.zeros((LANE - nrep - 6 * q, LANE), F32)
        lhs = jnp.concatenate([cbh * dec, st_hi, stacked - st_hi, pad], axis=0).astype(BF16)

        xdd_parts = []
        for g in range(n_groups):
            c0 = g * GROUP_W
            xs = _silu(_conv(ext, c0, c0 + GROUP_W, q, kw, cw, first) + cb[:, c0:c0 + GROUP_W])
            ex = jnp.dot(lhs, expand[:, c0:c0 + GROUP_W], preferred_element_type=F32)
            o = nrep
            dtx = ex[o:o + q] + ex[o + 3 * q:o + 4 * q]
            dendx = ex[o + q:o + 2 * q] + ex[o + 4 * q:o + 5 * q]
            ecsx = ex[o + 2 * q:o + 3 * q] + ex[o + 5 * q:o + 6 * q]
            xdt = xs * dtx
            xdd = xdt * dendx
            y = xs * dskip[:, c0:c0 + GROUP_W]
            for s in range(q):
                y = y + ex[s * q:(s + 1) * q] * jnp.broadcast_to(xdt[s:s + 1], (q, GROUP_W))
            state = ssm_in[sidx, g]
            cg = cmat[:, g * STATE_DIM:(g + 1) * STATE_DIM]
            y = y + lax.dot_general(cg, state, NT_DIMS, preferred_element_type=F32) * ecsx
            zref = z2 if g < groups_per_block else z3
            zc = (g % groups_per_block) * GROUP_W
            ya_rows[g].append(_gated_norm(y, zref[r0:r0 + q, zc:zc + GROUP_W], nw[:, c0:c0 + GROUP_W]))
            xdd_parts.append(xdd)

        zrows = jnp.zeros((LANE - q, d_inner), F32)
        xdd_t = jnp.concatenate([jnp.concatenate(xdd_parts, axis=1), zrows], axis=0).T
        for g in range(n_groups):
            b_pad = jnp.concatenate(
                [bmat[:, g * STATE_DIM:(g + 1) * STATE_DIM], jnp.zeros((LANE - q, STATE_DIM), F32)], axis=0)
            ssm_out[sidx, g] = ssm_in[sidx, g] * _decay_col(cs_last, g) + jnp.dot(
                xdd_t[g * GROUP_W:(g + 1) * GROUP_W, :], b_pad, preferred_element_type=F32)

    yb_ref[...] = jnp.concatenate(yb_rows, axis=0).astype(yb_ref.dtype)
    for g in range(n_groups):
        ya_ref[:, g * GROUP_W:(g + 1) * GROUP_W] = jnp.concatenate(ya_rows[g], axis=0).astype(ya_ref.dtype)


def _ssd_sample(p, dt_raw, ssm, conv_state, sc_state, consts, gsel, row0, seq_len, d):
    (cw, cb, dtb, alog, dskip, nw, scw, expand) = consts
    n_seq = ssm.shape[0]
    sp = SAMPLE_SEQS_PER_STEP
    rows = sp * seq_len
    assert seq_len == SUBLANE and n_seq % sp == 0 and row0 % rows == 0
    b0 = row0 // rows

    def blk(cidx):
        return pl.BlockSpec((rows, d), lambda i, cidx=cidx: (b0 + i, cidx))

    def const(arr):
        return pl.BlockSpec(arr.shape, lambda i: (0,) * arr.ndim)

    def per_seq(arr):
        return pl.BlockSpec((sp,) + arr.shape[1:], lambda i: (i,) + (0,) * (arr.ndim - 1))

    return pl.pallas_call(
        _ssd_sample_body,
        grid=(n_seq // sp,),
        in_specs=[blk(4), blk(5), blk(6), blk(2), blk(3), blk(7), blk(8), blk(9),
                  pl.BlockSpec((rows, LANE), lambda i: (b0 + i, 0)),
                  per_seq(ssm), per_seq(conv_state), per_seq(sc_state),
                  const(cw), const(cb), const(dtb), const(alog), const(dskip), const(nw), const(scw),
                  const(expand), const(gsel)],
        out_specs=[
            pl.BlockSpec((rows, 2 * d), lambda i: (i, 0)),
            pl.BlockSpec((rows, d), lambda i: (i, 0)),
            per_seq(ssm), per_seq(conv_state), per_seq(sc_state),
        ],
        out_shape=[
            jax.ShapeDtypeStruct((n_seq * seq_len, 2 * d), BF16),
            jax.ShapeDtypeStruct((n_seq * seq_len, d), BF16),
            jax.ShapeDtypeStruct(ssm.shape, F32),
            jax.ShapeDtypeStruct(conv_state.shape, F32),
            jax.ShapeDtypeStruct(sc_state.shape, F32),
        ],
        scratch_shapes=[pltpu.VMEM((sp, 3 * d // LANE, 2 * SUBLANE, LANE), F32),
                        pltpu.VMEM((sp, d // LANE, 2 * SUBLANE, LANE), F32)],
        compiler_params=pltpu.CompilerParams(
            dimension_semantics=("arbitrary",), vmem_limit_bytes=VMEM_LIMIT),
        name="ssd_sample",
    )(p, p, p, p, p, p, p, p, dt_raw, ssm, conv_state, sc_state,
      cw, cb, dtb, alog, dskip, nw, scw, expand, gsel)


def _branch_out_body(yap_ref, ybp_ref, yas_ref, ybs_ref, ga_ref, gb_ref, wa_ref, wb_ref, o_ref, *, prompt_tiles):
    def run(refs):
        ya_ref, yb_ref = refs
        pa = jnp.dot(ya_ref[...], wa_ref[...], preferred_element_type=F32)
        pb = jnp.dot(yb_ref[...], wb_ref[...], preferred_element_type=F32)
        merged = jax.nn.sigmoid(ga_ref[...]) * pa + jax.nn.sigmoid(gb_ref[...]) * pb
        o_ref[...] = merged.astype(o_ref.dtype)

    _by_stream(pl.program_id(0), prompt_tiles, run, (yap_ref, ybp_ref), (yas_ref, ybs_ref))


def _branch_out(ya_p, yb_p, ya_s, yb_s, p, wa, wb):
    t_p, d = yb_p.shape
    t_s = yb_s.shape[0]
    tm = _tile(math.gcd(t_p, t_s), 512, 16)
    tn = _tile(d, 512, LANE)
    nj = d // tn
    npt = t_p // tm
    yap_spec, yas_spec = _stream_specs((tm, 2 * d), npt)
    ybp_spec, ybs_spec = _stream_specs((tm, d), npt)
    return pl.pallas_call(
        functools.partial(_branch_out_body, prompt_tiles=npt),
        grid=((t_p + t_s) // tm, nj),
        in_specs=[
            yap_spec, ybp_spec, yas_spec, ybs_spec,
            pl.BlockSpec((tm, tn), lambda i, j: (i, j)),
            pl.BlockSpec((tm, tn), lambda i, j: (i, nj + j)),
            pl.BlockSpec((2 * d, tn), lambda i, j: (0, j)),
            pl.BlockSpec((d, tn), lambda i, j: (0, j)),
        ],
        out_specs=pl.BlockSpec((tm, tn), lambda i, j: (i, j)),
        out_shape=jax.ShapeDtypeStruct((t_p + t_s, d), BF16),
        compiler_params=pltpu.CompilerParams(
            dimension_semantics=("arbitrary", "arbitrary"), vmem_limit_bytes=VMEM_LIMIT),
        name="branch_out",
    )(ya_p, yb_p, ya_s, yb_s, p, p, wa, wb)


def _mix_route_body(m_ref, xp_ref, xs_ref, wo_ref, nw_ref, rhi_ref, rlo_ref,
                    x1_ref, h2_ref, route_ref, wts_ref, cnt_ref, *, n_coarse, per_group, prompt_tiles):
    @pl.when(pl.program_id(0) == 0)
    def _():
        cnt_ref[...] = jnp.zeros(cnt_ref.shape, F32)

    def run(x_ref):
        x1 = x_ref[...] + jnp.dot(m_ref[...], wo_ref[...], preferred_element_type=F32)
        x1_ref[...] = x1
        h2 = x1 * lax.rsqrt(jnp.mean(x1 * x1, axis=-1, keepdims=True) + NORM_EPS) * nw_ref[...]
        h2_ref[...] = h2
        h_hi, h_lo = _split2(h2)
        logits = (jnp.dot(h_hi, rhi_ref[...], preferred_element_type=F32)
                  + jnp.dot(h_hi, rlo_ref[...], preferred_element_type=F32)
                  + jnp.dot(h_lo, rhi_ref[...], preferred_element_type=F32))

        tm = logits.shape[0]
        n_fine = n_coarse * per_group
        lane = lax.broadcasted_iota(jnp.int32, logits.shape, 1)
        big = jnp.int32(LANE)
        neg = -jnp.inf
        is_c = lane < n_coarse
        lc = jnp.where(is_c, logits, neg)
        mc = jnp.max(lc, axis=-1, keepdims=True)
        grp = jnp.min(jnp.where(is_c & (lc == mc), lane, big), axis=-1, keepdims=True)
        p_grp = 1.0 / jnp.sum(jnp.where(is_c, jnp.exp(lc - mc), 0.0), axis=-1, keepdims=True)
        eidx = lane - n_coarse
        sel = (eidx >= 0) & (eidx < n_fine) & ((eidx // per_group) == grp)
        lf = jnp.where(sel, logits, neg)
        v1 = jnp.max(lf, axis=-1, keepdims=True)
        i1 = jnp.min(jnp.where(sel & (lf == v1), eidx, big), axis=-1, keepdims=True)
        sel2 = sel & (eidx != i1)
        lf2 = jnp.where(sel2, logits, neg)
        v2 = jnp.max(lf2, axis=-1, keepdims=True)
        i2 = jnp.min(jnp.where(sel2 & (lf2 == v2), eidx, big), axis=-1, keepdims=True)
        e2 = jnp.exp(v2 - v1)
        w1 = p_grp / (1.0 + e2)
        w2 = p_grp * e2 / (1.0 + e2)
        wts_ref[...] = jnp.where(lane == 0, w1, jnp.where(lane == 1, w2, 0.0))

        hit1 = eidx == i1
        hit2 = eidx == i2
        hits = hit1.astype(F32) + hit2.astype(F32)
        earlier = (lax.broadcasted_iota(jnp.int32, (tm, tm), 0)
                   > lax.broadcasted_iota(jnp.int32, (tm, tm), 1)).astype(BF16)
        before = jnp.dot(earlier, hits.astype(BF16), preferred_element_type=F32) + cnt_ref[...]
        r1 = jnp.sum(jnp.where(hit1, before, 0.0), axis=-1, keepdims=True).astype(jnp.int32)
        r2 = jnp.sum(jnp.where(hit2, before, 0.0), axis=-1, keepdims=True).astype(jnp.int32)
        cnt_ref[...] += jnp.sum(hits, axis=0, keepdims=True)
        route_ref[...] = jnp.where(lane == 0, i1, jnp.where(lane == 1, i2, jnp.where(
            lane == 2, r1, jnp.where(lane == 3, r2, 0))))

    _by_stream(pl.program_id(0), prompt_tiles, run, xp_ref, xs_ref)


def _mix_route(merged, x_p, x_s, wo, norm_w, r_hi, r_lo, n_coarse, per_group):
    t_p, d = x_p.shape
    T = t_p + x_s.shape[0]
    tm = _tile(math.gcd(t_p, x_s.shape[0]), 256, 16)
    npt = t_p // tm
    row = lambda i: (i, 0)
    fixed = lambda i: (0, 0)
    xp_spec, xs_spec = _stream_specs((tm, d), npt)
    resident = dict(pipeline_mode=pl.Buffered(1))
    return pl.pallas_call(
        functools.partial(_mix_route_body, n_coarse=n_coarse, per_group=per_group, prompt_tiles=npt),
        grid=(T // tm,),
        in_specs=[
            pl.BlockSpec((tm, d), row), xp_spec, xs_spec, pl.BlockSpec((d, d), fixed, **resident),
            pl.BlockSpec((1, d), fixed), pl.BlockSpec((d, LANE), fixed, **resident),
            pl.BlockSpec((d, LANE), fixed, **resident),
        ],
        out_specs=[pl.BlockSpec((tm, d), row), pl.BlockSpec((tm, d), row),
                   pl.BlockSpec((tm, LANE), row), pl.BlockSpec((tm, LANE), row),
                   pl.BlockSpec((1, LANE), fixed)],
        out_shape=[
            jax.ShapeDtypeStruct((T, d), F32), jax.ShapeDtypeStruct((T, d), F32),
            jax.ShapeDtypeStruct((T, LANE), jnp.int32), jax.ShapeDtypeStruct((T, LANE), F32),
            jax.ShapeDtypeStruct((1, LANE), F32),
        ],
        compiler_params=pltpu.CompilerParams(
            dimension_semantics=("arbitrary",), vmem_limit_bytes=VMEM_LIMIT),
        name="mix_route",
    )(merged, x_p, x_s, wo, norm_w, r_hi, r_lo)


MOE_MOVE_TOKENS = 128
MOE_MOVE_UNROLL = 8


def _moe_scatter_body(dest_ref, fill_ref, h_ref, xs_hbm, zeros, sem, zsem):
    tt = h_ref.shape[0]
    t0 = pl.program_id(0) * tt
    n_experts = fill_ref.shape[0] - 1
    n_blocks = xs_hbm.shape[0] // MOE_ROWS

    @pl.when(pl.program_id(0) == 0)
    def _():
        zeros[...] = jnp.zeros(zeros.shape, zeros.dtype)

        def fill(row):
            return pltpu.make_async_copy(zeros, xs_hbm.at[pl.ds(pl.multiple_of(row, MOE_ROWS), MOE_ROWS)], zsem)

        def expert_fill(action):
            def body(e, carry):
                @pl.when(fill_ref[e] >= 0)
                def _():
                    action(fill(fill_ref[e]))
                return carry
            lax.fori_loop(0, n_experts, body, 0)

        def tail_fill(action):
            def body(blk, carry):
                action(fill(blk * MOE_ROWS))
                return carry
            lax.fori_loop(fill_ref[n_experts], n_blocks, body, 0)

        expert_fill(lambda c: c.start())
        tail_fill(lambda c: c.start())
        expert_fill(lambda c: c.wait())
        tail_fill(lambda c: c.wait())

    def start(i, carry):
        for k in range(MOE_TOP_K):
            pltpu.make_async_copy(
                h_ref.at[pl.ds(i, 1)], xs_hbm.at[pl.ds(dest_ref[(t0 + i) * MOE_TOP_K + k], 1)], sem
            ).start(priority=k % 2)
        return carry

    lax.fori_loop(0, tt, start, 0, unroll=MOE_MOVE_UNROLL)
    for _ in range(MOE_TOP_K):
        pltpu.make_async_copy(h_ref, xs_hbm.at[pl.ds(0, tt)], sem).wait()


def _moe_scatter(dest, fill_rows, h2, n_rows):
    T, d = h2.shape
    tt = MOE_MOVE_TOKENS
    assert T % tt == 0
    return pl.pallas_call(
        _moe_scatter_body,
        grid_spec=pltpu.PrefetchScalarGridSpec(
            num_scalar_prefetch=2,
            grid=(T // tt,),
            in_specs=[pl.BlockSpec((tt, d), lambda i, dr, fr: (i, 0))],
            out_specs=pl.BlockSpec(memory_space=pl.ANY),
            scratch_shapes=[pltpu.VMEM((MOE_ROWS, d), h2.dtype), pltpu.SemaphoreType.DMA(()),
                            pltpu.SemaphoreType.DMA(())],
        ),
        out_shape=jax.ShapeDtypeStruct((n_rows, d), h2.dtype),
        compiler_params=pltpu.CompilerParams(dimension_semantics=("arbitrary",)),
        name="moe_scatter",
    )(dest, fill_rows, h2)


MOE_WEIGHT_PIECES = 8
MOE_WEIGHT_RING = 5


def _moe_ffn_body(sched_ref, n_used_ref, x_ref, wg_hbm, wu_hbm, wd_hbm, o_ref,
                  wg_b, wu_b, wd_b, sg, su, sd, sems):
    b = pl.program_id(0)
    hbm = (wg_hbm, wu_hbm, wd_hbm)
    stage = (sg, su, sd)
    resident = (wg_b, wu_b, wd_b)

    def copies(e, piece, ring):
        out = []
        for k in range(3):
            rows = stage[k].shape[1]
            out.append(pltpu.make_async_copy(
                hbm[k].at[0, e, pl.ds(pl.multiple_of(piece * rows, rows), rows), :],
                stage[k].at[ring], sems.at[ring, k]))
        return out

    def start(e, piece, ring):
        for c in copies(e, piece, ring):
            c.start()

    def land(e, piece, ring, slot):
        for k, c in enumerate(copies(e, piece, ring)):
            c.wait()
            rows = stage[k].shape[1]
            resident[k][slot, pl.ds(pl.multiple_of(piece * rows, rows), rows), :] = stage[k][ring].astype(BF16)

    def stream(e, first, count, slot):
        def body(c, carry):
            ring = c % MOE_WEIGHT_RING
            land(e, first + c, ring, slot)

            @pl.when(c + MOE_WEIGHT_RING < count)
            def _():
                start(e, first + c + MOE_WEIGHT_RING, ring)
            return carry

        lax.fori_loop(0, count, body, 0)

    def prime(e, first, count):
        for r in range(MOE_WEIGHT_RING):
            @pl.when(r < count)
            def _():
                start(e, first + r, r)

    @pl.when(b < n_used_ref[0])
    def _():
        slot = sched_ref[0, b]

        @pl.when(b == 0)
        def _():
            first_expert = sched_ref[1, pl.num_programs(0)]
            prime(first_expert, 0, MOE_WEIGHT_PIECES)
            stream(first_expert, 0, MOE_WEIGHT_PIECES, slot)

        nxt, first, count = sched_ref[1, b], sched_ref[2, b], sched_ref[3, b]
        prime(nxt, first, count)
        xb = x_ref[...].astype(BF16)
        gate = jnp.dot(xb, wg_b[slot], preferred_element_type=F32)
        up = jnp.dot(xb, wu_b[slot], preferred_element_type=F32)
        act = (_silu(gate) * up).astype(BF16)
        o_ref[...] = jnp.dot(act, wd_b[slot], preferred_element_type=F32)
        stream(nxt, first, count, 1 - slot)

    @pl.when(b >= n_used_ref[0])
    def _():
        o_ref[...] = jnp.zeros(o_ref.shape, o_ref.dtype)


def _moe_ffn(sched, n_used, xs, wg, wu, wd):
    R, d = xs.shape
    f = wg.shape[3]
    nb = R // MOE_ROWS
    np_ = MOE_WEIGHT_PIECES
    assert d % np_ == 0 and f % np_ == 0
    any_spec = pl.BlockSpec(memory_space=pl.ANY)
    return pl.pallas_call(
        _moe_ffn_body,
        grid_spec=pltpu.PrefetchScalarGridSpec(
            num_scalar_prefetch=2,
            grid=(nb,),
            in_specs=[pl.BlockSpec((MOE_ROWS, d), lambda b, sc, nu: (b, 0)), any_spec, any_spec, any_spec],
            out_specs=pl.BlockSpec((MOE_ROWS, d), lambda b, sc, nu: (b, 0)),
            scratch_shapes=[
                pltpu.VMEM((2, d, f), BF16), pltpu.VMEM((2, d, f), BF16), pltpu.VMEM((2, f, d), BF16),
                pltpu.VMEM((MOE_WEIGHT_RING, d // np_, f), F32), pltpu.VMEM((MOE_WEIGHT_RING, d // np_, f), F32),
                pltpu.VMEM((MOE_WEIGHT_RING, f // np_, d), F32),
                pltpu.SemaphoreType.DMA((MOE_WEIGHT_RING, 3)),
            ],
        ),
        out_shape=jax.ShapeDtypeStruct((R, d), F32),
        compiler_params=pltpu.CompilerParams(
            dimension_semantics=("arbitrary",), vmem_limit_bytes=VMEM_LIMIT),
        name="moe_ffn",
    )(sched, n_used, xs, wg, wu, wd)


def _moe_combine_body(dest_ref, x1_ref, wts_ref, nw_ref, yb_hbm, op_ref, os_ref, buf, sems, *, prompt_tiles):
    tt = x1_ref.shape[0]
    i = pl.program_id(0)

    def gather(tile, slot):
        def start(r, carry):
            for k in range(MOE_TOP_K):
                pltpu.make_async_copy(
                    yb_hbm.at[pl.ds(dest_ref[(tile * tt + r) * MOE_TOP_K + k], 1)],
                    buf.at[slot, k, pl.ds(r, 1)], sems.at[slot]).start(priority=k % 2)
            return carry

        lax.fori_loop(0, tt, start, 0, unroll=MOE_MOVE_UNROLL)

    @pl.when(i == 0)
    def _():
        gather(0, 0)

    @pl.when(i + 1 < pl.num_programs(0))
    def _():
        gather(i + 1, (i + 1) % 2)

    slot = i % 2
    for k in range(MOE_TOP_K):
        pltpu.make_async_copy(yb_hbm.at[pl.ds(0, tt)], buf.at[slot, k], sems.at[slot]).wait()

    def finish(o_ref):
        w = wts_ref[...]
        x2 = x1_ref[...] + (buf[slot, 0] * w[:, 0:1] + buf[slot, 1] * w[:, 1:2])
        o_ref[...] = x2 * lax.rsqrt(jnp.mean(x2 * x2, axis=-1, keepdims=True) + NORM_EPS) * nw_ref[...]

    _by_stream(i, prompt_tiles, finish, op_ref, os_ref)


def _moe_combine(dest, x1, wts, norm_w, yb, t_p):
    T, d = x1.shape
    tt = MOE_MOVE_TOKENS
    assert t_p % tt == 0 and T % tt == 0
    npt = t_p // tt
    op_spec, os_spec = _stream_specs((tt, d), npt)
    return pl.pallas_call(
        functools.partial(_moe_combine_body, prompt_tiles=npt),
        grid_spec=pltpu.PrefetchScalarGridSpec(
            num_scalar_prefetch=1,
            grid=(T // tt,),
            in_specs=[
                pl.BlockSpec((tt, d), lambda i, dr: (i, 0)),
                pl.BlockSpec((tt, LANE), lambda i, dr: (i, 0)),
                pl.BlockSpec((1, d), lambda i, dr: (0, 0)),
                pl.BlockSpec(memory_space=pl.ANY),
            ],
            out_specs=[op_spec, os_spec],
            scratch_shapes=[pltpu.VMEM((2, MOE_TOP_K, tt, d), F32), pltpu.SemaphoreType.DMA((2,))],
        ),
        out_shape=[jax.ShapeDtypeStruct((t_p, d), F32), jax.ShapeDtypeStruct((T - t_p, d), F32)],
        compiler_params=pltpu.CompilerParams(dimension_semantics=("arbitrary",)),
        name="moe_combine",
    )(dest, x1, wts, norm_w, yb)


def _route_rows(eid, rank, counts, n_assign):
    n_experts = counts.shape[0]
    padded = (counts + MOE_ROWS - 1) // MOE_ROWS * MOE_ROWS
    pend = jnp.cumsum(padded)
    pstart = pend - padded
    experts = jnp.arange(n_experts, dtype=jnp.int32)
    onehot = eid[:, :, None] == experts[None, None, :]
    dest = (jnp.sum(jnp.where(onehot, pstart[None, None, :], 0), axis=-1) + rank).astype(jnp.int32)
    n_blocks = -(-n_assign // MOE_ROWS) + n_experts
    blk_start = jnp.arange(n_blocks, dtype=jnp.int32) * MOE_ROWS
    used = blk_start < pend[-1]
    blk_e = jnp.minimum(jnp.sum((blk_start[:, None] >= pend[None, :]).astype(jnp.int32), axis=1), n_experts - 1)
    n_used = (pend[-1] // MOE_ROWS).astype(jnp.int32).reshape(1)

    nonempty = counts > 0
    ordinal = jnp.cumsum(nonempty.astype(jnp.int32)) - 1
    later = (experts[None, :] > experts[:, None]) & nonempty[None, :]
    nxt = jnp.min(jnp.where(later, experts[None, :], n_experts), axis=1)
    pos = blk_start // MOE_ROWS - (pstart // MOE_ROWS)[blk_e]
    nblk = jnp.maximum((padded // MOE_ROWS)[blk_e], 1)
    has_next = used & (nxt[blk_e] < n_experts)
    first = jnp.where(has_next, MOE_WEIGHT_PIECES * pos // nblk, 0)
    last = jnp.where(has_next, MOE_WEIGHT_PIECES * (pos + 1) // nblk, 0)
    slot = jnp.where(used, ordinal[blk_e] % 2, 0)
    per_block = jnp.stack([slot, jnp.where(has_next, nxt[blk_e], 0), first, last - first]).astype(jnp.int32)
    first_expert = jnp.stack([slot[0], blk_e[0], 0, MOE_WEIGHT_PIECES]).astype(jnp.int32)[:, None]
    sched = jnp.concatenate([per_block, first_expert], axis=1)
    fill_rows = jnp.concatenate([jnp.where(nonempty, pend - MOE_ROWS, -1), n_used]).astype(jnp.int32)
    return dest.reshape(-1), sched, fill_rows, n_used, n_blocks


def _pad_lanes(v, fill=0.0):
    return jnp.pad(v.astype(F32), (0, LANE - v.shape[0]), constant_values=fill).reshape(1, LANE)


def kernel(x_prompt, x_sample, state_ssm, state_ssd_conv, state_short_conv, norm_mixer, w_in, ssd_conv_w,
           ssd_conv_b, ssd_dt_bias, ssd_a_log, ssd_d, ssd_norm, sc_conv_w, w_branch_out, w_out, norm_ffn,
           w_router_coarse, w_router_fine, w_expert_gate, w_expert_up, w_expert_down, norm_final):
    depth = w_in.shape[0]
    assert depth == 1
    n_p, seq_p, d = x_prompt.shape
    n_s, seq_s, _ = x_sample.shape
    d_inner = 2 * d
    n_heads = d_inner // HEAD_DIM
    n_groups = d_inner // GROUP_W
    gn = n_groups * STATE_DIM
    conv_dim = d_inner + 2 * gn
    assert conv_dim == 3 * d and n_heads <= LANE and ssd_conv_w.shape[2] == conv_dim
    t_p, t_s = n_p * seq_p, n_s * seq_s
    n_coarse = w_router_coarse.shape[2]
    n_experts = w_router_fine.shape[2]
    assert n_coarse + n_experts <= LANE

    off_dt = 2 * d + d_inner + conv_dim
    off_sc = off_dt + n_heads
    w_in_t = jnp.swapaxes(w_in, 1, 2)
    w_tail_t = w_in_t[0, off_sc:, :]
    w_dt = jnp.pad(w_in_t[0, off_dt:off_sc, :].T, ((0, 0), (0, LANE - n_heads))).astype(BF16)
    head_of_col = jnp.arange(d_inner, dtype=jnp.int32) // HEAD_DIM
    expand = (jnp.arange(LANE, dtype=jnp.int32)[:, None] == head_of_col[None, :]).astype(BF16)
    group_of_n = jnp.arange(gn, dtype=jnp.int32) // STATE_DIM
    group_of_head = jnp.arange(LANE, dtype=jnp.int32) // HEADS_PER_GROUP
    gsel = ((group_of_n[:, None] == group_of_head[None, :])
            & (jnp.arange(LANE)[None, :] < n_heads)).astype(BF16)
    consts = (
        ssd_conv_w[0], ssd_conv_b[0].reshape(1, conv_dim), _pad_lanes(ssd_dt_bias[0]), _pad_lanes(ssd_a_log[0]),
        jnp.repeat(ssd_d[0].astype(F32), HEAD_DIM).reshape(1, d_inner), ssd_norm[0].reshape(1, d_inner),
        sc_conv_w[0], expand,
    )
    wbo = w_branch_out[0].astype(BF16)
    wa, wb = wbo[:d_inner], wbo[d_inner:]
    wo = w_out[0].astype(BF16)
    w_router = jnp.pad(jnp.concatenate([w_router_coarse[0], w_router_fine[0]], axis=1),
                       ((0, 0), (0, LANE - n_coarse - n_experts)))
    r_hi = w_router.astype(BF16)
    r_lo = (w_router - r_hi.astype(F32)).astype(BF16)

    x_p = x_prompt.reshape(t_p, d)
    x_s = x_sample.reshape(t_s, d)
    h, dt_raw = _prenorm(x_p, x_s, norm_mixer[0].reshape(1, d), w_dt)
    p, tails = _inproj(h, w_in_t, w_tail_t, off_dt, consts[0], consts[1], t_p, seq_p)
    ya_p, yb_p, p_ssm, p_sc_tail = _ssd_prompt(p, dt_raw, consts, n_p, seq_p, d)
    ya_s, yb_s, s_ssm, s_conv, s_sc = _ssd_sample(
        p, dt_raw, state_ssm[0].reshape(n_s, n_groups, GROUP_W, STATE_DIM), state_ssd_conv[0],
        state_short_conv[0], consts, gsel, t_p, seq_s, d)
    merged = _branch_out(ya_p, yb_p, ya_s, yb_s, p, wa, wb)
    x1, h2, route, wts, counts = _mix_route(merged, x_p, x_s, wo, norm_ffn[0].reshape(1, d), r_hi, r_lo,
                                            n_coarse, n_experts // n_coarse)

    n_assign = (t_p + t_s) * MOE_TOP_K
    dest, sched, fill_rows, n_used, n_blocks = _route_rows(
        route[:, 0:MOE_TOP_K], route[:, MOE_TOP_K:2 * MOE_TOP_K],
        counts[0, n_coarse:n_coarse + n_experts].astype(jnp.int32), n_assign)
    xs = _moe_scatter(dest, fill_rows, h2, n_blocks * MOE_ROWS)
    yrows = _moe_ffn(sched, n_used, xs, w_expert_gate, w_expert_up, w_expert_down)
    out_p, out_s = _moe_combine(dest, x1, wts, norm_final.reshape(1, d), yrows, t_p)

    kw = ssd_conv_w.shape[1]
    kw2 = sc_conv_w.shape[1]
    tiles_per_seq = (tails.shape[0] // SUBLANE) * seq_p // (t_p + t_s)
    seq_tails = tails.reshape(-1, SUBLANE, tails.shape[1])[tiles_per_seq - 1:n_p * tiles_per_seq:tiles_per_seq]
    p_conv = seq_tails[:, SUBLANE - (kw - 1):, 4 * d:7 * d]
    return (
        out_p.reshape(n_p, seq_p, d),
        out_s.reshape(n_s, seq_s, d),
        p_ssm.reshape(1, n_p, n_heads, HEAD_DIM, STATE_DIM),
        p_conv[None],
        p_sc_tail[:, SUBLANE - (kw2 - 1):, :][None],
        s_ssm.reshape(1, n_s, n_heads, HEAD_DIM, STATE_DIM),
        s_conv[None],
        s_sc[None],
    )
```

```python
import functools
import math

import jax
import jax.numpy as jnp
from jax import lax
from jax.experimental import pallas as pl
from jax.experimental.pallas import tpu as pltpu

F32 = jnp.float32
BF16 = jnp.bfloat16

NORM_EPS = 1e-6
SSD_NORM_EPS = 1e-5
HEAD_DIM = 64
STATE_DIM = 128
HEADS_PER_GROUP = 8
GROUP_W = HEADS_PER_GROUP * HEAD_DIM
SSD_CHUNK = 128
MOE_TOP_K = 2
MOE_ROWS = 256

LANE = 128
SUBLANE = 8
VMEM_LIMIT = 56 * 1024 * 1024

NT_DIMS = (((1,), (1,)), ((), ()))
TN_DIMS = (((0,), (0,)), ((), ()))


def _tile(n, target, align):
    best = None
    for t in range(align, min(n, target) + 1, align):
        if n % t == 0:
            best = t
    assert best is not None, (n, target, align)
    return best


def _split2(v):
    hi = v.astype(BF16)
    lo = (v - hi.astype(F32)).astype(BF16)
    return hi, lo


def _split3(v):
    hi = v.astype(BF16)
    r = v - hi.astype(F32)
    mid = r.astype(BF16)
    lo = (r - mid.astype(F32)).astype(BF16)
    return hi, mid, lo


def _softplus(x):
    return jnp.maximum(x, 0.0) + jnp.log1p(jnp.exp(-jnp.abs(x)))


def _silu(x):
    return x * jax.nn.sigmoid(x)


def _by_stream(tile, prompt_tiles, fn, prompt_ref, sample_ref):
    @pl.when(tile < prompt_tiles)
    def _():
        fn(prompt_ref)

    @pl.when(tile >= prompt_tiles)
    def _():
        fn(sample_ref)


def _stream_specs(block, prompt_tiles, **kwargs):
    pad = (0,) * (len(block) - 1)
    prompt = lambda i, *_: (jnp.minimum(i, prompt_tiles - 1),) + pad
    sample = lambda i, *_: (jnp.maximum(i - prompt_tiles, 0),) + pad
    return pl.BlockSpec(block, prompt, **kwargs), pl.BlockSpec(block, sample, **kwargs)


def _prenorm_body(xp_ref, xs_ref, nw_ref, wdt_ref, h_ref, dt_ref, *, prompt_tiles):
    def run(x_ref):
        x = x_ref[...]
        h = x * lax.rsqrt(jnp.mean(x * x, axis=-1, keepdims=True) + NORM_EPS) * nw_ref[...]
        hb = h.astype(BF16)
        h_ref[...] = hb
        dt_ref[...] = jnp.dot(hb, wdt_ref[...], preferred_element_type=F32)

    _by_stream(pl.program_id(0), prompt_tiles, run, xp_ref, xs_ref)


def _prenorm(x_p, x_s, norm_w, w_dt):
    t_p, D = x_p.shape
    T = t_p + x_s.shape[0]
    tm = _tile(math.gcd(t_p, x_s.shape[0]), 512, 16)
    npt = t_p // tm
    xp_spec, xs_spec = _stream_specs((tm, D), npt)
    return pl.pallas_call(
        functools.partial(_prenorm_body, prompt_tiles=npt),
        grid=(T // tm,),
        in_specs=[xp_spec, xs_spec, pl.BlockSpec((1, D), lambda i: (0, 0)),
                  pl.BlockSpec((D, LANE), lambda i: (0, 0))],
        out_specs=[pl.BlockSpec((tm, D), lambda i: (i, 0)), pl.BlockSpec((tm, LANE), lambda i: (i, 0))],
        out_shape=[jax.ShapeDtypeStruct((T, D), BF16), jax.ShapeDtypeStruct((T, LANE), F32)],
        compiler_params=pltpu.CompilerParams(
            dimension_semantics=("arbitrary",), vmem_limit_bytes=VMEM_LIMIT),
        name="prenorm",
    )(x_p, x_s, norm_w, w_dt)


INPROJ_CONV_SLABS = 4


def _inproj_body(h_ref, wa_ref, wb_ref, cw_ref, cb_ref, p_ref, tail_ref, w_scr, cext,
                 *, main_tiles, blocks_per_d, prompt_tiles, tiles_per_seq):
    j = pl.program_id(0)
    i = pl.program_id(1)
    tm, tn = p_ref.shape
    kw = cw_ref.shape[0]

    @pl.when(i == 0)
    def _():
        def cast(w_ref):
            w_scr[...] = w_ref[...].astype(BF16)

        _by_stream(j, main_tiles, cast, wa_ref, wb_ref)

    is_conv = (j >= 4 * blocks_per_d) & (j < 7 * blocks_per_d) & (i < prompt_tiles)

    @pl.when(jnp.logical_not(is_conv))
    def _():
        raw = lax.dot_general(h_ref[...], w_scr[...], NT_DIMS, preferred_element_type=F32)
        p_ref[...] = raw
        tail_ref[...] = raw[tm - SUBLANE:tm, :]

    @pl.when(is_conv)
    def _():
        @pl.when(i % tiles_per_seq == 0)
        def _():
            cext[:, 0:SUBLANE, :] = jnp.zeros((cext.shape[0], SUBLANE, LANE), F32)

        ws = tn // INPROJ_CONV_SLABS
        first = SUBLANE - (kw - 1)
        for k in range(INPROJ_CONV_SLABS):
            raw = lax.dot_general(h_ref[...], w_scr[k * ws:(k + 1) * ws, :], NT_DIMS, preferred_element_type=F32)
            for s in range(k * ws // LANE, (k + 1) * ws // LANE):
                lo, hi = s * LANE, (s + 1) * LANE
                cext[s, SUBLANE:SUBLANE + tm, :] = raw[:, lo - k * ws:hi - k * ws]
                acc = cb_ref[:, lo:hi]
                for t in range(kw):
                    acc = acc + cw_ref[t:t + 1, lo:hi] * cext[s, first + t:first + t + tm, :]
                p_ref[:, lo:hi] = _silu(acc)
                tail = cext[s, tm:tm + SUBLANE, :]
                tail_ref[:, lo:hi] = tail
                cext[s, 0:SUBLANE, :] = tail


def _inproj(h, w_in_t, w_tail_t, n_head_cols, cw, cb, t_p, seq_p):
    T, D = h.shape
    tm = _tile(math.gcd(seq_p, T - t_p), 1024, 16)
    tn = _tile(math.gcd(D, w_tail_t.shape[0]), 1024, LANE)
    main_tiles = n_head_cols // tn
    n_out = n_head_cols + w_tail_t.shape[0]
    bpd = D // tn
    conv_block = lambda j, i: (0, jnp.clip(j - 4 * bpd, 0, 3 * bpd - 1))
    return pl.pallas_call(
        functools.partial(_inproj_body, main_tiles=main_tiles, blocks_per_d=bpd,
                          prompt_tiles=t_p // tm, tiles_per_seq=seq_p // tm),
        grid=(n_out // tn, T // tm),
        in_specs=[
            pl.BlockSpec((tm, D), lambda j, i: (i, 0)),
            pl.BlockSpec((None, tn, D), lambda j, i: (0, jnp.minimum(j, main_tiles - 1), 0)),
            pl.BlockSpec((tn, D), lambda j, i: (jnp.maximum(j - main_tiles, 0), 0),
                         pipeline_mode=pl.Buffered(1)),
            pl.BlockSpec((cw.shape[0], tn), conv_block),
            pl.BlockSpec((1, tn), conv_block),
        ],
        out_specs=[pl.BlockSpec((tm, tn), lambda j, i: (i, j)),
                   pl.BlockSpec((SUBLANE, tn), lambda j, i: (i, j))],
        out_shape=[jax.ShapeDtypeStruct((T, n_out), F32),
                   jax.ShapeDtypeStruct((T // tm * SUBLANE, n_out), F32)],
        scratch_shapes=[pltpu.VMEM((tn, D), BF16), pltpu.VMEM((tn // LANE, tm + SUBLANE, LANE), F32)],
        compiler_params=pltpu.CompilerParams(
            dimension_semantics=("arbitrary", "arbitrary"), vmem_limit_bytes=VMEM_LIMIT),
        name="inproj",
    )(h, w_in_t, w_tail_t, cw, cb)


def _slab_store(ext, row0, value, col0=0):
    for s in range(value.shape[1] // LANE):
        ext[col0 // LANE + s, row0:row0 + value.shape[0], :] = value[:, s * LANE:(s + 1) * LANE]


def _slab_load(ext, row0, rows, lo, hi):
    return jnp.concatenate([ext[s, row0:row0 + rows, :] for s in range(lo // LANE, hi // LANE)], axis=1)


def _conv(ext, lo, hi, q, width, w_ref, first):
    parts = []
    for s in range(lo // LANE, hi // LANE):
        acc = None
        for k in range(width):
            term = w_ref[k:k + 1, s * LANE:(s + 1) * LANE] * ext[s, first + k:first + k + q, :]
            acc = term if acc is None else acc + term
        parts.append(acc)
    return jnp.concatenate(parts, axis=1)


def _gated_norm(y, z, nw):
    g = y * _silu(z)
    return g * lax.rsqrt(jnp.mean(g * g, axis=-1, keepdims=True) + SSD_NORM_EPS) * nw


def _decay_col(cs_last_row, g):
    d = jnp.exp(cs_last_row)
    parts = [
        jnp.broadcast_to(d[0:1, g * HEADS_PER_GROUP + j:g * HEADS_PER_GROUP + j + 1], (HEAD_DIM, STATE_DIM))
        for j in range(HEADS_PER_GROUP)
    ]
    return jnp.concatenate(parts, axis=0)


def _ssd_prompt_body(x4, x5, x6, z2, z3, scb, scc, sch, dtr,
                     dtb, alog, dskip, nw, scw, expand,
                     ya_ref, yb_ref, st_ref, sct_ref, ext2):
    q, d = x4.shape
    n_groups = st_ref.shape[1]
    gn = n_groups * STATE_DIM
    kw2 = scw.shape[0]

    @pl.when(pl.program_id(1) == 0)
    def _():
        st_ref[...] = jnp.zeros(st_ref.shape, F32)
        ext2[:, 0:SUBLANE, :] = jnp.zeros((ext2.shape[0], SUBLANE, LANE), F32)

    _slab_store(ext2, SUBLANE, scc[...] * sch[...])
    v = _conv(ext2, 0, d, q, kw2, scw, SUBLANE - (kw2 - 1))
    yb_ref[...] = (scb[...] * v).astype(yb_ref.dtype)
    tail2 = _slab_load(ext2, q, SUBLANE, 0, d)
    _slab_store(ext2, 0, tail2)
    sct_ref[0] = tail2

    dt = _softplus(dtr[...] + dtb[...])
    a = dt * (-jnp.exp(alog[...]))
    row = lax.broadcasted_iota(jnp.int32, (q, q), 0)
    col = lax.broadcasted_iota(jnp.int32, (q, q), 1)
    causal = row >= col
    tri = causal.astype(BF16)
    cs = sum(jnp.dot(tri, part, preferred_element_type=F32) for part in _split3(a))
    cs_t = cs.T
    cs_last = cs[q - 1:q, :]
    dend = jnp.exp(cs_last - cs)
    ecs = jnp.exp(cs)
    stacked = jnp.concatenate([dt, dend, ecs], axis=0)
    st_b = stacked.astype(BF16)

    lane = lax.broadcasted_iota(jnp.int32, (q, LANE), 1)
    groups_per_block = d // GROUP_W

    for g in range(n_groups):
        c0 = g * GROUP_W
        xref = x4 if g < groups_per_block else x5
        bc = (g % groups_per_block) * GROUP_W
        xs = xref[:, bc:bc + GROUP_W]
        e_g = expand[:, c0:c0 + GROUP_W]
        ex = jnp.dot(st_b, e_g, preferred_element_type=F32)
        xdt = xs * ex[0:q]
        xdt_b = xdt.astype(BF16)
        xdd_b = (xdt * ex[q:2 * q]).astype(BF16)
        bg = x6[:, g * STATE_DIM:(g + 1) * STATE_DIM].astype(BF16)
        cg = x6[:, gn + g * STATE_DIM:gn + (g + 1) * STATE_DIM].astype(BF16)
        cbm = lax.dot_general(cg, bg, NT_DIMS, preferred_element_type=F32)
        state = st_ref[0, g]
        y_off = lax.dot_general(cg, state.astype(BF16), NT_DIMS, preferred_element_type=F32)
        y_parts = []
        for j in range(HEADS_PER_GROUP // 2):
            scores = []
            for h in (g * HEADS_PER_GROUP + 2 * j, g * HEADS_PER_GROUP + 2 * j + 1):
                seg = cs[:, h:h + 1] - cs_t[h:h + 1, :]
                dec = jnp.exp(jnp.where(causal, seg, -jnp.inf))
                scores.append((cbm * dec).astype(BF16))
            xp = xdt_b[:, j * LANE:(j + 1) * LANE]
            zero = jnp.zeros_like(xp)
            rhs = jnp.concatenate(
                [jnp.where(lane < HEAD_DIM, xp, zero), jnp.where(lane >= HEAD_DIM, xp, zero)], axis=0)
            y_parts.append(jnp.dot(jnp.concatenate(scores, axis=1), rhs, preferred_element_type=F32))
        y = jnp.concatenate(y_parts, axis=1) + y_off * ex[2 * q:3 * q] + xs * dskip[:, c0:c0 + GROUP_W]
        zref = z2 if g < groups_per_block else z3
        ya_ref[:, c0:c0 + GROUP_W] = _gated_norm(
            y, zref[:, bc:bc + GROUP_W], nw[:, c0:c0 + GROUP_W]).astype(ya_ref.dtype)
        st_ref[0, g] = state * _decay_col(cs_last, g) + lax.dot_general(
            xdd_b, bg, TN_DIMS, preferred_element_type=F32)


def _ssd_prompt(p, dt_raw, consts, n_seq, seq_len, d):
    (_, _, dtb, alog, dskip, nw, scw, expand) = consts
    q = SSD_CHUNK if seq_len % SSD_CHUNK == 0 else seq_len
    nc = seq_len // q
    n_groups = 2 * d // GROUP_W
    t_p = n_seq * seq_len
    assert 2 * n_groups * STATE_DIM == d

    def blk(cidx):
        return pl.BlockSpec((q, d), lambda b, c, cidx=cidx: (b * nc + c, cidx))

    def const(arr):
        return pl.BlockSpec(arr.shape, lambda b, c: (0,) * arr.ndim)

    return pl.pallas_call(
        _ssd_prompt_body,
        grid=(n_seq, nc),
        in_specs=[blk(4), blk(5), blk(6), blk(2), blk(3), blk(7), blk(8), blk(9),
                  pl.BlockSpec((q, LANE), lambda b, c: (b * nc + c, 0)),
                  const(dtb), const(alog), const(dskip), const(nw), const(scw), const(expand)],
        out_specs=[
            pl.BlockSpec((q, 2 * d), lambda b, c: (b * nc + c, 0)),
            pl.BlockSpec((q, d), lambda b, c: (b * nc + c, 0)),
            pl.BlockSpec((1, n_groups, GROUP_W, STATE_DIM), lambda b, c: (b, 0, 0, 0)),
            pl.BlockSpec((1, SUBLANE, d), lambda b, c: (b, 0, 0)),
        ],
        out_shape=[
            jax.ShapeDtypeStruct((t_p, 2 * d), BF16),
            jax.ShapeDtypeStruct((t_p, d), BF16),
            jax.ShapeDtypeStruct((n_seq, n_groups, GROUP_W, STATE_DIM), F32),
            jax.ShapeDtypeStruct((n_seq, SUBLANE, d), F32),
        ],
        scratch_shapes=[pltpu.VMEM((d // LANE, q + SUBLANE, LANE), F32)],
        compiler_params=pltpu.CompilerParams(
            dimension_semantics=("arbitrary", "arbitrary"), vmem_limit_bytes=VMEM_LIMIT),
        name="ssd_prompt",
    )(p, p, p, p, p, p, p, p, dt_raw, dtb, alog, dskip, nw, scw, expand)


SAMPLE_SEQS_PER_STEP = 2


def _ssd_sample_body(x4, x5, x6, z2, z3, scb, scc, sch, dtr, ssm_in, conv_in, sc_in,
                     cw, cb, dtb, alog, dskip, nw, scw, expand, gsel,
                     ya_ref, yb_ref, ssm_out, conv_out, sc_out, exts, ext2s):
    d = x4.shape[1]
    n_groups = ssm_in.shape[1]
    gn = n_groups * STATE_DIM
    d_inner = n_groups * GROUP_W
    kw = cw.shape[0]
    kw2 = scw.shape[0]
    q = x4.shape[0] // SAMPLE_SEQS_PER_STEP
    groups_per_block = d // GROUP_W
    first = SUBLANE - (kw - 1)
    first2 = SUBLANE - (kw2 - 1)
    nrep = q * q

    rep_t = lax.broadcasted_iota(jnp.int32, (nrep, LANE), 0) % q
    rep_s = lax.broadcasted_iota(jnp.int32, (nrep, LANE), 0) // q
    rep_causal = rep_t >= rep_s
    row_q = lax.broadcasted_iota(jnp.int32, (q, LANE), 0)

    def rep_rows(m):
        return jnp.concatenate([jnp.broadcast_to(m[s:s + 1], (q, m.shape[1])) for s in range(q)], axis=0)

    def tile_rows(m):
        return jnp.concatenate([m] * q, axis=0)

    yb_rows = []
    ya_rows = [[] for _ in range(n_groups)]
    for sidx in range(SAMPLE_SEQS_PER_STEP):
        r0 = sidx * q
        ext, ext2 = exts.at[sidx], ext2s.at[sidx]
        _slab_store(ext, first, conv_in[sidx])
        _slab_store(ext, SUBLANE, x4[r0:r0 + q, :])
        _slab_store(ext, SUBLANE, x5[r0:r0 + q, :], d)
        _slab_store(ext, SUBLANE, x6[r0:r0 + q, :], 2 * d)
        conv_out[sidx] = _slab_load(ext, SUBLANE + q - (kw - 1), kw - 1, 0, 3 * d)

        _slab_store(ext2, first2, sc_in[sidx])
        _slab_store(ext2, SUBLANE, scc[r0:r0 + q, :] * sch[r0:r0 + q, :])
        yb_rows.append(scb[r0:r0 + q, :] * _conv(ext2, 0, d, q, kw2, scw, first2))
        sc_out[sidx] = _slab_load(ext2, SUBLANE + q - (kw2 - 1), kw2 - 1, 0, d)

        dt = _softplus(dtr[r0:r0 + q, :] + dtb[...])
        a = dt * (-jnp.exp(alog[...]))
        cs = jnp.zeros((q, LANE), F32)
        for r in range(q):
            cs = cs + jnp.where(row_q >= r, jnp.broadcast_to(a[r:r + 1], (q, LANE)), 0.0)
        cs_last = cs[q - 1:q, :]
        dend = jnp.exp(cs_last - cs)
        ecs = jnp.exp(cs)

        bmat = _silu(_conv(ext, 2 * d, 2 * d + gn, q, kw, cw, first) + cb[:, 2 * d:2 * d + gn])
        cmat = _silu(_conv(ext, 2 * d + gn, 3 * d, q, kw, cw, first) + cb[:, 2 * d + gn:3 * d])

        cb_hi, cb_lo = _split2(tile_rows(cmat) * rep_rows(bmat))
        cbh = (jnp.dot(cb_hi, gsel[...], preferred_element_type=F32)
               + jnp.dot(cb_lo, gsel[...], preferred_element_type=F32))
        dec = jnp.exp(jnp.where(rep_causal, tile_rows(cs) - rep_rows(cs), -jnp.inf))
        stacked = jnp.concatenate([dt, dend, ecs], axis=0)
        st_hi = stacked.astype(BF16).astype(F32)
        pad = jnp.zeros((LANE - nrep - 6 * q, LANE), F32)
        lhs = jnp.concatenate([cbh * dec, st_hi, stacked - st_hi, pad], axis=0).astype(BF16)

        xdd_parts = []
        for g in range(n_groups):
            c0 = g * GROUP_W
            xs = _silu(_conv(ext, c0, c0 + GROUP_W, q, kw, cw, first) + cb[:, c0:c0 + GROUP_W])
            ex = jnp.dot(lhs, expand[:, c0:c0 + GROUP_W], preferred_element_type=F32)
            o = nrep
            dtx = ex[o:o + q] + ex[o + 3 * q:o + 4 * q]
            dendx = ex[o + q:o + 2 * q] + ex[o + 4 * q:o + 5 * q]
            ecsx = ex[o + 2 * q:o + 3 * q] + ex[o + 5 * q:o + 6 * q]
            xdt = xs * dtx
            xdd = xdt * dendx
            y = xs * dskip[:, c0:c0 + GROUP_W]
            for s in range(q):
                y = y + ex[s * q:(s + 1) * q] * jnp.broadcast_to(xdt[s:s + 1], (q, GROUP_W))
            state = ssm_in[sidx, g]
            cg = cmat[:, g * STATE_DIM:(g + 1) * STATE_DIM]
            y = y + lax.dot_general(cg, state, NT_DIMS, preferred_element_type=F32) * ecsx
            zref = z2 if g < groups_per_block else z3
            zc = (g % groups_per_block) * GROUP_W
            ya_rows[g].append(_gated_norm(y, zref[r0:r0 + q, zc:zc + GROUP_W], nw[:, c0:c0 + GROUP_W]))
            xdd_parts.append(xdd)

        zrows = jnp.zeros((LANE - q, d_inner), F32)
        xdd_t = jnp.concatenate([jnp.concatenate(xdd_parts, axis=1), zrows], axis=0).T
        for g in range(n_groups):
            b_pad = jnp.concatenate(
                [bmat[:, g * STATE_DIM:(g + 1) * STATE_DIM], jnp.zeros((LANE - q, STATE_DIM), F32)], axis=0)
            ssm_out[sidx, g] = ssm_in[sidx, g] * _decay_col(cs_last, g) + jnp.dot(
                xdd_t[g * GROUP_W:(g + 1) * GROUP_W, :], b_pad, preferred_element_type=F32)

    yb_ref[...] = jnp.concatenate(yb_rows, axis=0).astype(yb_ref.dtype)
    for g in range(n_groups):
        ya_ref[:, g * GROUP_W:(g + 1) * GROUP_W] = jnp.concatenate(ya_rows[g], axis=0).astype(ya_ref.dtype)


def _ssd_sample(p, dt_raw, ssm, conv_state, sc_state, consts, gsel, row0, seq_len, d):
    (cw, cb, dtb, alog, dskip, nw, scw, expand) = consts
    n_seq = ssm.shape[0]
    sp = SAMPLE_SEQS_PER_STEP
    rows = sp * seq_len
    assert seq_len == SUBLANE and n_seq % sp == 0 and row0 % rows == 0
    b0 = row0 // rows

    def blk(cidx):
        return pl.BlockSpec((rows, d), lambda i, cidx=cidx: (b0 + i, cidx))

    def const(arr):
        return pl.BlockSpec(arr.shape, lambda i: (0,) * arr.ndim)

    def per_seq(arr):
        return pl.BlockSpec((sp,) + arr.shape[1:], lambda i: (i,) + (0,) * (arr.ndim - 1))

    return pl.pallas_call(
        _ssd_sample_body,
        grid=(n_seq // sp,),
        in_specs=[blk(4), blk(5), blk(6), blk(2), blk(3), blk(7), blk(8), blk(9),
                  pl.BlockSpec((rows, LANE), lambda i: (b0 + i, 0)),
                  per_seq(ssm), per_seq(conv_state), per_seq(sc_state),
                  const(cw), const(cb), const(dtb), const(alog), const(dskip), const(nw), const(scw),
                  const(expand), const(gsel)],
        out_specs=[
            pl.BlockSpec((rows, 2 * d), lambda i: (i, 0)),
            pl.BlockSpec((rows, d), lambda i: (i, 0)),
            per_seq(ssm), per_seq(conv_state), per_seq(sc_state),
        ],
        out_shape=[
            jax.ShapeDtypeStruct((n_seq * seq_len, 2 * d), BF16),
            jax.ShapeDtypeStruct((n_seq * seq_len, d), BF16),
            jax.ShapeDtypeStruct(ssm.shape, F32),
            jax.ShapeDtypeStruct(conv_state.shape, F32),
            jax.ShapeDtypeStruct(sc_state.shape, F32),
        ],
        scratch_shapes=[pltpu.VMEM((sp, 3 * d // LANE, 2 * SUBLANE, LANE), F32),
                        pltpu.VMEM((sp, d // LANE, 2 * SUBLANE, LANE), F32)],
        compiler_params=pltpu.CompilerParams(
            dimension_semantics=("arbitrary",), vmem_limit_bytes=VMEM_LIMIT),
        name="ssd_sample",
    )(p, p, p, p, p, p, p, p, dt_raw, ssm, conv_state, sc_state,
      cw, cb, dtb, alog, dskip, nw, scw, expand, gsel)


def _branch_out_body(yap_ref, ybp_ref, yas_ref, ybs_ref, ga_ref, gb_ref, wa_ref, wb_ref, o_ref, *, prompt_tiles):
    def run(refs):
        ya_ref, yb_ref = refs
        pa = jnp.dot(ya_ref[...], wa_ref[...], preferred_element_type=F32)
        pb = jnp.dot(yb_ref[...], wb_ref[...], preferred_element_type=F32)
        merged = jax.nn.sigmoid(ga_ref[...]) * pa + jax.nn.sigmoid(gb_ref[...]) * pb
        o_ref[...] = merged.astype(o_ref.dtype)

    _by_stream(pl.program_id(0), prompt_tiles, run, (yap_ref, ybp_ref), (yas_ref, ybs_ref))


def _branch_out(ya_p, yb_p, ya_s, yb_s, p, wa, wb):
    t_p, d = yb_p.shape
    t_s = yb_s.shape[0]
    tm = _tile(math.gcd(t_p, t_s), 512, 16)
    tn = _tile(d, 512, LANE)
    nj = d // tn
    npt = t_p // tm
    yap_spec, yas_spec = _stream_specs((tm, 2 * d), npt)
    ybp_spec, ybs_spec = _stream_specs((tm, d), npt)
    return pl.pallas_call(
        functools.partial(_branch_out_body, prompt_tiles=npt),
        grid=((t_p + t_s) // tm, nj),
        in_specs=[
            yap_spec, ybp_spec, yas_spec, ybs_spec,
            pl.BlockSpec((tm, tn), lambda i, j: (i, j)),
            pl.BlockSpec((tm, tn), lambda i, j: (i, nj + j)),
            pl.BlockSpec((2 * d, tn), lambda i, j: (0, j)),
            pl.BlockSpec((d, tn), lambda i, j: (0, j)),
        ],
        out_specs=pl.BlockSpec((tm, tn), lambda i, j: (i, j)),
        out_shape=jax.ShapeDtypeStruct((t_p + t_s, d), BF16),
        compiler_params=pltpu.CompilerParams(
            dimension_semantics=("arbitrary", "arbitrary"), vmem_limit_bytes=VMEM_LIMIT),
        name="branch_out",
    )(ya_p, yb_p, ya_s, yb_s, p, p, wa, wb)


def _mix_route_body(m_ref, xp_ref, xs_ref, wo_ref, nw_ref, rhi_ref, rlo_ref,
                    x1_ref, h2_ref, route_ref, wts_ref, cnt_ref, *, n_coarse, per_group, prompt_tiles):
    @pl.when(pl.program_id(0) == 0)
    def _():
        cnt_ref[...] = jnp.zeros(cnt_ref.shape, F32)

    def run(x_ref):
        x1 = x_ref[...] + jnp.dot(m_ref[...], wo_ref[...], preferred_element_type=F32)
        x1_ref[...] = x1
        h2 = x1 * lax.rsqrt(jnp.mean(x1 * x1, axis=-1, keepdims=True) + NORM_EPS) * nw_ref[...]
        h2_ref[...] = h2
        h_hi, h_lo = _split2(h2)
        logits = (jnp.dot(h_hi, rhi_ref[...], preferred_element_type=F32)
                  + jnp.dot(h_hi, rlo_ref[...], preferred_element_type=F32)
                  + jnp.dot(h_lo, rhi_ref[...], preferred_element_type=F32))

        tm = logits.shape[0]
        n_fine = n_coarse * per_group
        lane = lax.broadcasted_iota(jnp.int32, logits.shape, 1)
        big = jnp.int32(LANE)
        neg = -jnp.inf
        is_c = lane < n_coarse
        lc = jnp.where(is_c, logits, neg)
        mc = jnp.max(lc, axis=-1, keepdims=True)
        grp = jnp.min(jnp.where(is_c & (lc == mc), lane, big), axis=-1, keepdims=True)
        p_grp = 1.0 / jnp.sum(jnp.where(is_c, jnp.exp(lc - mc), 0.0), axis=-1, keepdims=True)
        eidx = lane - n_coarse
        sel = (eidx >= 0) & (eidx < n_fine) & ((eidx // per_group) == grp)
        lf = jnp.where(sel, logits, neg)
        v1 = jnp.max(lf, axis=-1, keepdims=True)
        i1 = jnp.min(jnp.where(sel & (lf == v1), eidx, big), axis=-1, keepdims=True)
        sel2 = sel & (eidx != i1)
        lf2 = jnp.where(sel2, logits, neg)
        v2 = jnp.max(lf2, axis=-1, keepdims=True)
        i2 = jnp.min(jnp.where(sel2 & (lf2 == v2), eidx, big), axis=-1, keepdims=True)
        e2 = jnp.exp(v2 - v1)
        w1 = p_grp / (1.0 + e2)
        w2 = p_grp * e2 / (1.0 + e2)
        wts_ref[...] = jnp.where(lane == 0, w1, jnp.where(lane == 1, w2, 0.0))

        hit1 = eidx == i1
        hit2 = eidx == i2
        hits = hit1.astype(F32) + hit2.astype(F32)
        earlier = (lax.broadcasted_iota(jnp.int32, (tm, tm), 0)
                   > lax.broadcasted_iota(jnp.int32, (tm, tm), 1)).astype(BF16)
        before = jnp.dot(earlier, hits.astype(BF16), preferred_element_type=F32) + cnt_ref[...]
        r1 = jnp.sum(jnp.where(hit1, before, 0.0), axis=-1, keepdims=True).astype(jnp.int32)
        r2 = jnp.sum(jnp.where(hit2, before, 0.0), axis=-1, keepdims=True).astype(jnp.int32)
        cnt_ref[...] += jnp.sum(hits, axis=0, keepdims=True)
        route_ref[...] = jnp.where(lane == 0, i1, jnp.where(lane == 1, i2, jnp.where(
            lane == 2, r1, jnp.where(lane == 3, r2, 0))))

    _by_stream(pl.program_id(0), prompt_tiles, run, xp_ref, xs_ref)


def _mix_route(merged, x_p, x_s, wo, norm_w, r_hi, r_lo, n_coarse, per_group):
    t_p, d = x_p.shape
    T = t_p + x_s.shape[0]
    tm = _tile(math.gcd(t_p, x_s.shape[0]), 256, 16)
    npt = t_p // tm
    row = lambda i: (i, 0)
    fixed = lambda i: (0, 0)
    xp_spec, xs_spec = _stream_specs((tm, d), npt)
    resident = dict(pipeline_mode=pl.Buffered(1))
    return pl.pallas_call(
        functools.partial(_mix_route_body, n_coarse=n_coarse, per_group=per_group, prompt_tiles=npt),
        grid=(T // tm,),
        in_specs=[
            pl.BlockSpec((tm, d), row), xp_spec, xs_spec, pl.BlockSpec((d, d), fixed, **resident),
            pl.BlockSpec((1, d), fixed), pl.BlockSpec((d, LANE), fixed, **resident),
            pl.BlockSpec((d, LANE), fixed, **resident),
        ],
        out_specs=[pl.BlockSpec((tm, d), row), pl.BlockSpec((tm, d), row),
                   pl.BlockSpec((tm, LANE), row), pl.BlockSpec((tm, LANE), row),
                   pl.BlockSpec((1, LANE), fixed)],
        out_shape=[
            jax.ShapeDtypeStruct((T, d), F32), jax.ShapeDtypeStruct((T, d), F32),
            jax.ShapeDtypeStruct((T, LANE), jnp.int32), jax.ShapeDtypeStruct((T, LANE), F32),
            jax.ShapeDtypeStruct((1, LANE), F32),
        ],
        compiler_params=pltpu.CompilerParams(
            dimension_semantics=("arbitrary",), vmem_limit_bytes=VMEM_LIMIT),
        name="mix_route",
    )(merged, x_p, x_s, wo, norm_w, r_hi, r_lo)


MOE_MOVE_TOKENS = 128
MOE_MOVE_UNROLL = 8


def _moe_scatter_body(dest_ref, fill_ref, h_ref, xs_hbm, zeros, sem, zsem):
    tt = h_ref.shape[0]
    t0 = pl.program_id(0) * tt
    n_experts = fill_ref.shape[0] - 1
    n_blocks = xs_hbm.shape[0] // MOE_ROWS

    @pl.when(pl.program_id(0) == 0)
    def _():
        zeros[...] = jnp.zeros(zeros.shape, zeros.dtype)

        def fill(row):
            return pltpu.make_async_copy(zeros, xs_hbm.at[pl.ds(pl.multiple_of(row, MOE_ROWS), MOE_ROWS)], zsem)

        def expert_fill(action):
            def body(e, carry):
                @pl.when(fill_ref[e] >= 0)
                def _():
                    action(fill(fill_ref[e]))
                return carry
            lax.fori_loop(0, n_experts, body, 0)

        def tail_fill(action):
            def body(blk, carry):
                action(fill(blk * MOE_ROWS))
                return carry
            lax.fori_loop(fill_ref[n_experts], n_blocks, body, 0)

        expert_fill(lambda c: c.start())
        tail_fill(lambda c: c.start())
        expert_fill(lambda c: c.wait())
        tail_fill(lambda c: c.wait())

    def start(i, carry):
        for k in range(MOE_TOP_K):
            pltpu.make_async_copy(
                h_ref.at[pl.ds(i, 1)], xs_hbm.at[pl.ds(dest_ref[(t0 + i) * MOE_TOP_K + k], 1)], sem
            ).start(priority=k % 2)
        return carry

    lax.fori_loop(0, tt, start, 0, unroll=MOE_MOVE_UNROLL)
    for _ in range(MOE_TOP_K):
        pltpu.make_async_copy(h_ref, xs_hbm.at[pl.ds(0, tt)], sem).wait()


def _moe_scatter(dest, fill_rows, h2, n_rows):
    T, d = h2.shape
    tt = MOE_MOVE_TOKENS
    assert T % tt == 0
    return pl.pallas_call(
        _moe_scatter_body,
        grid_spec=pltpu.PrefetchScalarGridSpec(
            num_scalar_prefetch=2,
            grid=(T // tt,),
            in_specs=[pl.BlockSpec((tt, d), lambda i, dr, fr: (i, 0))],
            out_specs=pl.BlockSpec(memory_space=pl.ANY),
            scratch_shapes=[pltpu.VMEM((MOE_ROWS, d), h2.dtype), pltpu.SemaphoreType.DMA(()),
                            pltpu.SemaphoreType.DMA(())],
        ),
        out_shape=jax.ShapeDtypeStruct((n_rows, d), h2.dtype),
        compiler_params=pltpu.CompilerParams(dimension_semantics=("arbitrary",)),
        name="moe_scatter",
    )(dest, fill_rows, h2)


MOE_WEIGHT_PIECES = 8
MOE_WEIGHT_RING = 3


def _moe_ffn_body(sched_ref, order_ref, n_used_ref, x_ref, wg_hbm, wu_hbm, wd_hbm, o_ref,
                  wg_b, wu_b, wd_b, sg, su, sd, sems):
    b = pl.program_id(0)
    hbm = (wg_hbm, wu_hbm, wd_hbm)
    stage = (sg, su, sd)
    resident = (wg_b, wu_b, wd_b)
    n_total = order_ref[order_ref.shape[0] - 1]

    def copies(c):
        e = order_ref[c // MOE_WEIGHT_PIECES]
        piece = c % MOE_WEIGHT_PIECES
        ring = c % MOE_WEIGHT_RING
        out = []
        for k in range(3):
            rows = stage[k].shape[1]
            out.append(pltpu.make_async_copy(
                hbm[k].at[0, e, pl.ds(pl.multiple_of(piece * rows, rows), rows), :],
                stage[k].at[ring], sems.at[ring, k]))
        return out

    def start(c):
        @pl.when(c < n_total)
        def _():
            for cp in copies(c):
                cp.start()

    def land(lo, hi):
        def body(c, carry):
            slot = (c // MOE_WEIGHT_PIECES) % 2
            piece = c % MOE_WEIGHT_PIECES
            ring = c % MOE_WEIGHT_RING
            for k, cp in enumerate(copies(c)):
                cp.wait()
                rows = stage[k].shape[1]
                resident[k][slot, pl.ds(pl.multiple_of(piece * rows, rows), rows), :] = stage[k][ring].astype(BF16)
            start(c + MOE_WEIGHT_RING)
            return carry

        lax.fori_loop(lo, hi, body, 0)

    @pl.when(b < n_used_ref[0])
    def _():
        @pl.when(b == 0)
        def _():
            for c in range(MOE_WEIGHT_RING):
                start(c)
            land(0, MOE_WEIGHT_PIECES)

        slot = sched_ref[0, b]
        xb = x_ref[...].astype(BF16)
        gate = jnp.dot(xb, wg_b[slot], preferred_element_type=F32)
        up = jnp.dot(xb, wu_b[slot], preferred_element_type=F32)
        act = (_silu(gate) * up).astype(BF16)
        o_ref[...] = jnp.dot(act, wd_b[slot], preferred_element_type=F32)
        land(sched_ref[1, b], sched_ref[2, b])

    @pl.when(b >= n_used_ref[0])
    def _():
        o_ref[...] = jnp.zeros(o_ref.shape, o_ref.dtype)


def _moe_ffn(sched, order, n_used, xs, wg, wu, wd):
    R, d = xs.shape
    f = wg.shape[3]
    nb = R // MOE_ROWS
    np_ = MOE_WEIGHT_PIECES
    assert d % np_ == 0 and f % np_ == 0
    any_spec = pl.BlockSpec(memory_space=pl.ANY)
    return pl.pallas_call(
        _moe_ffn_body,
        grid_spec=pltpu.PrefetchScalarGridSpec(
            num_scalar_prefetch=3,
            grid=(nb,),
            in_specs=[pl.BlockSpec((MOE_ROWS, d), lambda b, sc, od, nu: (b, 0)), any_spec, any_spec, any_spec],
            out_specs=pl.BlockSpec((MOE_ROWS, d), lambda b, sc, od, nu: (b, 0)),
            scratch_shapes=[
                pltpu.VMEM((2, d, f), BF16), pltpu.VMEM((2, d, f), BF16), pltpu.VMEM((2, f, d), BF16),
                pltpu.VMEM((MOE_WEIGHT_RING, d // np_, f), F32), pltpu.VMEM((MOE_WEIGHT_RING, d // np_, f), F32),
                pltpu.VMEM((MOE_WEIGHT_RING, f // np_, d), F32),
                pltpu.SemaphoreType.DMA((MOE_WEIGHT_RING, 3)),
            ],
        ),
        out_shape=jax.ShapeDtypeStruct((R, d), F32),
        compiler_params=pltpu.CompilerParams(
            dimension_semantics=("arbitrary",), vmem_limit_bytes=VMEM_LIMIT),
        name="moe_ffn",
    )(sched, order, n_used, xs, wg, wu, wd)


def _moe_combine_body(dest_ref, x1_ref, wts_ref, nw_ref, yb_hbm, op_ref, os_ref, buf, sems, *, prompt_tiles):
    tt = x1_ref.shape[0]
    i = pl.program_id(0)

    def gather(tile, slot):
        def start(r, carry):
            for k in range(MOE_TOP_K):
                pltpu.make_async_copy(
                    yb_hbm.at[pl.ds(dest_ref[(tile * tt + r) * MOE_TOP_K + k], 1)],
                    buf.at[slot, k, pl.ds(r, 1)], sems.at[slot]).start(priority=k % 2)
            return carry

        lax.fori_loop(0, tt, start, 0, unroll=MOE_MOVE_UNROLL)

    @pl.when(i == 0)
    def _():
        gather(0, 0)

    @pl.when(i + 1 < pl.num_programs(0))
    def _():
        gather(i + 1, (i + 1) % 2)

    slot = i % 2
    for k in range(MOE_TOP_K):
        pltpu.make_async_copy(yb_hbm.at[pl.ds(0, tt)], buf.at[slot, k], sems.at[slot]).wait()

    def finish(o_ref):
        w = wts_ref[...]
        x2 = x1_ref[...] + (buf[slot, 0] * w[:, 0:1] + buf[slot, 1] * w[:, 1:2])
        o_ref[...] = x2 * lax.rsqrt(jnp.mean(x2 * x2, axis=-1, keepdims=True) + NORM_EPS) * nw_ref[...]

    _by_stream(i, prompt_tiles, finish, op_ref, os_ref)


def _moe_combine(dest, x1, wts, norm_w, yb, t_p):
    T, d = x1.shape
    tt = MOE_MOVE_TOKENS
    assert t_p % tt == 0 and T % tt == 0
    npt = t_p // tt
    op_spec, os_spec = _stream_specs((tt, d), npt)
    return pl.pallas_call(
        functools.partial(_moe_combine_body, prompt_tiles=npt),
        grid_spec=pltpu.PrefetchScalarGridSpec(
            num_scalar_prefetch=1,
            grid=(T // tt,),
            in_specs=[
                pl.BlockSpec((tt, d), lambda i, dr: (i, 0)),
                pl.BlockSpec((tt, LANE), lambda i, dr: (i, 0)),
                pl.BlockSpec((1, d), lambda i, dr: (0, 0)),
                pl.BlockSpec(memory_space=pl.ANY),
            ],
            out_specs=[op_spec, os_spec],
            scratch_shapes=[pltpu.VMEM((2, MOE_TOP_K, tt, d), F32), pltpu.SemaphoreType.DMA((2,))],
        ),
        out_shape=[jax.ShapeDtypeStruct((t_p, d), F32), jax.ShapeDtypeStruct((T - t_p, d), F32)],
        compiler_params=pltpu.CompilerParams(dimension_semantics=("arbitrary",)),
        name="moe_combine",
    )(dest, x1, wts, norm_w, yb)


def _route_rows(eid, rank, counts, n_assign):
    n_experts = counts.shape[0]
    padded = (counts + MOE_ROWS - 1) // MOE_ROWS * MOE_ROWS
    pend = jnp.cumsum(padded)
    pstart = pend - padded
    experts = jnp.arange(n_experts, dtype=jnp.int32)
    onehot = eid[:, :, None] == experts[None, None, :]
    dest = (jnp.sum(jnp.where(onehot, pstart[None, None, :], 0), axis=-1) + rank).astype(jnp.int32)
    n_blocks = -(-n_assign // MOE_ROWS) + n_experts
    blk_start = jnp.arange(n_blocks, dtype=jnp.int32) * MOE_ROWS
    used = blk_start < pend[-1]
    blk_e = jnp.minimum(jnp.sum((blk_start[:, None] >= pend[None, :]).astype(jnp.int32), axis=1), n_experts - 1)
    n_used = (pend[-1] // MOE_ROWS).astype(jnp.int32).reshape(1)

    nonempty = counts > 0
    ordinal = jnp.cumsum(nonempty.astype(jnp.int32)) - 1
    n_nonempty = jnp.sum(nonempty.astype(jnp.int32))
    order = jnp.argsort(jnp.where(nonempty, experts, n_experts + experts)).astype(jnp.int32)
    n_slices = MOE_WEIGHT_PIECES * n_nonempty
    pos = blk_start // MOE_ROWS - (pstart // MOE_ROWS)[blk_e]
    nblk = jnp.maximum((padded // MOE_ROWS)[blk_e], 1)
    has_next = used & (ordinal[blk_e] + 1 < n_nonempty)
    base = MOE_WEIGHT_PIECES * (ordinal[blk_e] + 1)
    first = jnp.where(has_next, base + MOE_WEIGHT_PIECES * pos // nblk, n_slices)
    last = jnp.where(has_next, base + MOE_WEIGHT_PIECES * (pos + 1) // nblk, n_slices)
    slot = jnp.where(used, ordinal[blk_e] % 2, 0)
    sched = jnp.stack([slot, first, last]).astype(jnp.int32)
    order = jnp.concatenate([order, n_slices.reshape(1)]).astype(jnp.int32)
    fill_rows = jnp.concatenate([jnp.where(nonempty, pend - MOE_ROWS, -1), n_used]).astype(jnp.int32)
    return dest.reshape(-1), sched, order, fill_rows, n_used, n_blocks


def _pad_lanes(v, fill=0.0):
    return jnp.pad(v.astype(F32), (0, LANE - v.shape[0]), constant_values=fill).reshape(1, LANE)


def kernel(x_prompt, x_sample, state_ssm, state_ssd_conv, state_short_conv, norm_mixer, w_in, ssd_conv_w,
           ssd_conv_b, ssd_dt_bias, ssd_a_log, ssd_d, ssd_norm, sc_conv_w, w_branch_out, w_out, norm_ffn,
           w_router_coarse, w_router_fine, w_expert_gate, w_expert_up, w_expert_down, norm_final):
    depth = w_in.shape[0]
    assert depth == 1
    n_p, seq_p, d = x_prompt.shape
    n_s, seq_s, _ = x_sample.shape
    d_inner = 2 * d
    n_heads = d_inner // HEAD_DIM
    n_groups = d_inner // GROUP_W
    gn = n_groups * STATE_DIM
    conv_dim = d_inner + 2 * gn
    assert conv_dim == 3 * d and n_heads <= LANE and ssd_conv_w.shape[2] == conv_dim
    t_p, t_s = n_p * seq_p, n_s * seq_s
    n_coarse = w_router_coarse.shape[2]
    n_experts = w_router_fine.shape[2]
    assert n_coarse + n_experts <= LANE

    off_dt = 2 * d + d_inner + conv_dim
    off_sc = off_dt + n_heads
    w_in_t = jnp.swapaxes(w_in, 1, 2)
    w_tail_t = w_in_t[0, off_sc:, :]
    w_dt = jnp.pad(w_in_t[0, off_dt:off_sc, :].T, ((0, 0), (0, LANE - n_heads))).astype(BF16)
    head_of_col = jnp.arange(d_inner, dtype=jnp.int32) // HEAD_DIM
    expand = (jnp.arange(LANE, dtype=jnp.int32)[:, None] == head_of_col[None, :]).astype(BF16)
    group_of_n = jnp.arange(gn, dtype=jnp.int32) // STATE_DIM
    group_of_head = jnp.arange(LANE, dtype=jnp.int32) // HEADS_PER_GROUP
    gsel = ((group_of_n[:, None] == group_of_head[None, :])
            & (jnp.arange(LANE)[None, :] < n_heads)).astype(BF16)
    consts = (
        ssd_conv_w[0], ssd_conv_b[0].reshape(1, conv_dim), _pad_lanes(ssd_dt_bias[0]), _pad_lanes(ssd_a_log[0]),
        jnp.repeat(ssd_d[0].astype(F32), HEAD_DIM).reshape(1, d_inner), ssd_norm[0].reshape(1, d_inner),
        sc_conv_w[0], expand,
    )
    wbo = w_branch_out[0].astype(BF16)
    wa, wb = wbo[:d_inner], wbo[d_inner:]
    wo = w_out[0].astype(BF16)
    w_router = jnp.pad(jnp.concatenate([w_router_coarse[0], w_router_fine[0]], axis=1),
                       ((0, 0), (0, LANE - n_coarse - n_experts)))
    r_hi = w_router.astype(BF16)
    r_lo = (w_router - r_hi.astype(F32)).astype(BF16)

    x_p = x_prompt.reshape(t_p, d)
    x_s = x_sample.reshape(t_s, d)
    h, dt_raw = _prenorm(x_p, x_s, norm_mixer[0].reshape(1, d), w_dt)
    p, tails = _inproj(h, w_in_t, w_tail_t, off_dt, consts[0], consts[1], t_p, seq_p)
    ya_p, yb_p, p_ssm, p_sc_tail = _ssd_prompt(p, dt_raw, consts, n_p, seq_p, d)
    ya_s, yb_s, s_ssm, s_conv, s_sc = _ssd_sample(
        p, dt_raw, state_ssm[0].reshape(n_s, n_groups, GROUP_W, STATE_DIM), state_ssd_conv[0],
        state_short_conv[0], consts, gsel, t_p, seq_s, d)
    merged = _branch_out(ya_p, yb_p, ya_s, yb_s, p, wa, wb)
    x1, h2, route, wts, counts = _mix_route(merged, x_p, x_s, wo, norm_ffn[0].reshape(1, d), r_hi, r_lo,
                                            n_coarse, n_experts // n_coarse)

    n_assign = (t_p + t_s) * MOE_TOP_K
    dest, sched, order, fill_rows, n_used, n_blocks = _route_rows(
        route[:, 0:MOE_TOP_K], route[:, MOE_TOP_K:2 * MOE_TOP_K],
        counts[0, n_coarse:n_coarse + n_experts].astype(jnp.int32), n_assign)
    xs = _moe_scatter(dest, fill_rows, h2, n_blocks * MOE_ROWS)
    yrows = _moe_ffn(sched, order, n_used, xs, w_expert_gate, w_expert_up, w_expert_down)
    out_p, out_s = _moe_combine(dest, x1, wts, norm_final.reshape(1, d), yrows, t_p)

    kw = ssd_conv_w.shape[1]
    kw2 = sc_conv_w.shape[1]
    tiles_per_seq = (tails.shape[0] // SUBLANE) * seq_p // (t_p + t_s)
    seq_tails = tails.reshape(-1, SUBLANE, tails.shape[1])[tiles_per_seq - 1:n_p * tiles_per_seq:tiles_per_seq]
    p_conv = seq_tails[:, SUBLANE - (kw - 1):, 4 * d:7 * d]
    return (
        out_p.reshape(n_p, seq_p, d),
        out_s.reshape(n_s, seq_s, d),
        p_ssm.reshape(1, n_p, n_heads, HEAD_DIM, STATE_DIM),
        p_conv[None],
        p_sc_tail[:, SUBLANE - (kw2 - 1):, :][None],
        s_ssm.reshape(1, n_s, n_heads, HEAD_DIM, STATE_DIM),
        s_conv[None],
        s_sc[None],
    )
```

```python
import functools
import math

import jax
import jax.numpy as jnp
from jax import lax
from jax.experimental import pallas as pl
from jax.experimental.pallas import tpu as pltpu

F32 = jnp.float32
BF16 = jnp.bfloat16

NORM_EPS = 1e-6
SSD_NORM_EPS = 1e-5
HEAD_DIM = 64
STATE_DIM = 128
HEADS_PER_GROUP = 8
GROUP_W = HEADS_PER_GROUP * HEAD_DIM
SSD_CHUNK = 128
MOE_TOP_K = 2
MOE_ROWS = 256

LANE = 128
SUBLANE = 8
VMEM_LIMIT = 56 * 1024 * 1024

NT_DIMS = (((1,), (1,)), ((), ()))
TN_DIMS = (((0,), (0,)), ((), ()))


def _tile(n, target, align):
    best = None
    for t in range(align, min(n, target) + 1, align):
        if n % t == 0:
            best = t
    assert best is not None, (n, target, align)
    return best


def _split2(v):
    hi = v.astype(BF16)
    lo = (v - hi.astype(F32)).astype(BF16)
    return hi, lo


def _split3(v):
    hi = v.astype(BF16)
    r = v - hi.astype(F32)
    mid = r.astype(BF16)
    lo = (r - mid.astype(F32)).astype(BF16)
    return hi, mid, lo


def _softplus(x):
    return jnp.maximum(x, 0.0) + jnp.log1p(jnp.exp(-jnp.abs(x)))


def _silu(x):
    return x * jax.nn.sigmoid(x)


def _by_stream(tile, prompt_tiles, fn, prompt_ref, sample_ref):
    @pl.when(tile < prompt_tiles)
    def _():
        fn(prompt_ref)

    @pl.when(tile >= prompt_tiles)
    def _():
        fn(sample_ref)


def _stream_specs(block, prompt_tiles, **kwargs):
    pad = (0,) * (len(block) - 1)
    prompt = lambda i, *_: (jnp.minimum(i, prompt_tiles - 1),) + pad
    sample = lambda i, *_: (jnp.maximum(i - prompt_tiles, 0),) + pad
    return pl.BlockSpec(block, prompt, **kwargs), pl.BlockSpec(block, sample, **kwargs)


def _prenorm_body(xp_ref, xs_ref, nw_ref, wdt_ref, h_ref, dt_ref, *, prompt_tiles):
    def run(x_ref):
        x = x_ref[...]
        h = x * lax.rsqrt(jnp.mean(x * x, axis=-1, keepdims=True) + NORM_EPS) * nw_ref[...]
        hb = h.astype(BF16)
        h_ref[...] = hb
        dt_ref[...] = jnp.dot(hb, wdt_ref[...], preferred_element_type=F32)

    _by_stream(pl.program_id(0), prompt_tiles, run, xp_ref, xs_ref)


def _prenorm(x_p, x_s, norm_w, w_dt):
    t_p, D = x_p.shape
    T = t_p + x_s.shape[0]
    tm = _tile(math.gcd(t_p, x_s.shape[0]), 512, 16)
    npt = t_p // tm
    xp_spec, xs_spec = _stream_specs((tm, D), npt)
    return pl.pallas_call(
        functools.partial(_prenorm_body, prompt_tiles=npt),
        grid=(T // tm,),
        in_specs=[xp_spec, xs_spec, pl.BlockSpec((1, D), lambda i: (0, 0)),
                  pl.BlockSpec((D, LANE), lambda i: (0, 0))],
        out_specs=[pl.BlockSpec((tm, D), lambda i: (i, 0)), pl.BlockSpec((tm, LANE), lambda i: (i, 0))],
        out_shape=[jax.ShapeDtypeStruct((T, D), BF16), jax.ShapeDtypeStruct((T, LANE), F32)],
        compiler_params=pltpu.CompilerParams(
            dimension_semantics=("arbitrary",), vmem_limit_bytes=VMEM_LIMIT),
        name="prenorm",
    )(x_p, x_s, norm_w, w_dt)


INPROJ_CONV_SLABS = 4


def _inproj_body(h_ref, w_ref, cw_ref, cb_ref, p_ref, tail_ref, w_scr, cext,
                 *, blocks_per_d, prompt_tiles, tiles_per_seq):
    j = pl.program_id(0)
    i = pl.program_id(1)
    tm, tn = p_ref.shape
    kw = cw_ref.shape[0]

    @pl.when(i == 0)
    def _():
        w_scr[...] = w_ref[...].astype(BF16)

    is_conv = (j >= 4 * blocks_per_d) & (j < 7 * blocks_per_d) & (i < prompt_tiles)

    @pl.when(jnp.logical_not(is_conv))
    def _():
        raw = lax.dot_general(h_ref[...], w_scr[...], NT_DIMS, preferred_element_type=F32)
        p_ref[...] = raw
        tail_ref[...] = raw[tm - SUBLANE:tm, :]

    @pl.when(is_conv)
    def _():
        @pl.when(i % tiles_per_seq == 0)
        def _():
            cext[:, 0:SUBLANE, :] = jnp.zeros((cext.shape[0], SUBLANE, LANE), F32)

        ws = tn // INPROJ_CONV_SLABS
        first = SUBLANE - (kw - 1)
        for k in range(INPROJ_CONV_SLABS):
            raw = lax.dot_general(h_ref[...], w_scr[k * ws:(k + 1) * ws, :], NT_DIMS, preferred_element_type=F32)
            for s in range(k * ws // LANE, (k + 1) * ws // LANE):
                lo, hi = s * LANE, (s + 1) * LANE
                cext[s, SUBLANE:SUBLANE + tm, :] = raw[:, lo - k * ws:hi - k * ws]
                acc = cb_ref[:, lo:hi]
                for t in range(kw):
                    acc = acc + cw_ref[t:t + 1, lo:hi] * cext[s, first + t:first + t + tm, :]
                p_ref[:, lo:hi] = _silu(acc)
                tail = cext[s, tm:tm + SUBLANE, :]
                tail_ref[:, lo:hi] = tail
                cext[s, 0:SUBLANE, :] = tail


def _inproj(h, w_in_t, n_head_cols, tail_start, cw, cb, t_p, seq_p):
    T, D = h.shape
    n_tail = w_in_t.shape[0] - tail_start
    tm = _tile(math.gcd(seq_p, T - t_p), 1024, 16)
    tn = _tile(math.gcd(D, n_tail), 1024, LANE)
    main_tiles = n_head_cols // tn
    n_out = n_head_cols + n_tail
    bpd = D // tn
    assert tail_start % SUBLANE == 0 and n_head_cols % tn == 0
    conv_block = lambda j, i: (0, jnp.clip(j - 4 * bpd, 0, 3 * bpd - 1))
    w_row = lambda j, i: (pl.multiple_of(
        jnp.where(j < main_tiles, j * tn, tail_start + (j - main_tiles) * tn), SUBLANE), 0)
    return pl.pallas_call(
        functools.partial(_inproj_body, blocks_per_d=bpd, prompt_tiles=t_p // tm, tiles_per_seq=seq_p // tm),
        grid=(n_out // tn, T // tm),
        in_specs=[
            pl.BlockSpec((tm, D), lambda j, i: (i, 0)),
            pl.BlockSpec((pl.Element(tn), pl.Element(D)), w_row),
            pl.BlockSpec((cw.shape[0], tn), conv_block),
            pl.BlockSpec((1, tn), conv_block),
        ],
        out_specs=[pl.BlockSpec((tm, tn), lambda j, i: (i, j)),
                   pl.BlockSpec((SUBLANE, tn), lambda j, i: (i, j))],
        out_shape=[jax.ShapeDtypeStruct((T, n_out), F32),
                   jax.ShapeDtypeStruct((T // tm * SUBLANE, n_out), F32)],
        scratch_shapes=[pltpu.VMEM((tn, D), BF16), pltpu.VMEM((tn // LANE, tm + SUBLANE, LANE), F32)],
        compiler_params=pltpu.CompilerParams(
            dimension_semantics=("arbitrary", "arbitrary"), vmem_limit_bytes=VMEM_LIMIT),
        name="inproj",
    )(h, w_in_t, cw, cb)


def _slab_store(ext, row0, value, col0=0):
    for s in range(value.shape[1] // LANE):
        ext[col0 // LANE + s, row0:row0 + value.shape[0], :] = value[:, s * LANE:(s + 1) * LANE]


def _slab_load(ext, row0, rows, lo, hi):
    return jnp.concatenate([ext[s, row0:row0 + rows, :] for s in range(lo // LANE, hi // LANE)], axis=1)


def _conv(ext, lo, hi, q, width, w_ref, first):
    parts = []
    for s in range(lo // LANE, hi // LANE):
        acc = None
        for k in range(width):
            term = w_ref[k:k + 1, s * LANE:(s + 1) * LANE] * ext[s, first + k:first + k + q, :]
            acc = term if acc is None else acc + term
        parts.append(acc)
    return jnp.concatenate(parts, axis=1)


def _gated_norm(y, z, nw):
    g = y * _silu(z)
    return g * lax.rsqrt(jnp.mean(g * g, axis=-1, keepdims=True) + SSD_NORM_EPS) * nw


def _decay_col(cs_last_row, g):
    d = jnp.exp(cs_last_row)
    parts = [
        jnp.broadcast_to(d[0:1, g * HEADS_PER_GROUP + j:g * HEADS_PER_GROUP + j + 1], (HEAD_DIM, STATE_DIM))
        for j in range(HEADS_PER_GROUP)
    ]
    return jnp.concatenate(parts, axis=0)


def _ssd_prompt_body(x4, x5, x6, z2, z3, scb, scc, sch, dtr,
                     dtb, alog, dskip, nw, scw, expand,
                     ya_ref, yb_ref, st_ref, sct_ref, ext2):
    q, d = x4.shape
    n_groups = st_ref.shape[1]
    gn = n_groups * STATE_DIM
    kw2 = scw.shape[0]

    @pl.when(pl.program_id(1) == 0)
    def _():
        st_ref[...] = jnp.zeros(st_ref.shape, F32)
        ext2[:, 0:SUBLANE, :] = jnp.zeros((ext2.shape[0], SUBLANE, LANE), F32)

    _slab_store(ext2, SUBLANE, scc[...] * sch[...])
    v = _conv(ext2, 0, d, q, kw2, scw, SUBLANE - (kw2 - 1))
    yb_ref[...] = (scb[...] * v).astype(yb_ref.dtype)
    tail2 = _slab_load(ext2, q, SUBLANE, 0, d)
    _slab_store(ext2, 0, tail2)
    sct_ref[0] = tail2

    dt = _softplus(dtr[...] + dtb[...])
    a = dt * (-jnp.exp(alog[...]))
    row = lax.broadcasted_iota(jnp.int32, (q, q), 0)
    col = lax.broadcasted_iota(jnp.int32, (q, q), 1)
    causal = row >= col
    tri = causal.astype(BF16)
    cs = sum(jnp.dot(tri, part, preferred_element_type=F32) for part in _split3(a))
    cs_t = cs.T
    cs_last = cs[q - 1:q, :]
    dend = jnp.exp(cs_last - cs)
    ecs = jnp.exp(cs)
    stacked = jnp.concatenate([dt, dend, ecs], axis=0)
    st_b = stacked.astype(BF16)

    lane = lax.broadcasted_iota(jnp.int32, (q, LANE), 1)
    groups_per_block = d // GROUP_W

    for g in range(n_groups):
        c0 = g * GROUP_W
        xref = x4 if g < groups_per_block else x5
        bc = (g % groups_per_block) * GROUP_W
        xs = xref[:, bc:bc + GROUP_W]
        e_g = expand[:, c0:c0 + GROUP_W]
        ex = jnp.dot(st_b, e_g, preferred_element_type=F32)
        xdt = xs * ex[0:q]
        xdt_b = xdt.astype(BF16)
        xdd_b = (xdt * ex[q:2 * q]).astype(BF16)
        bg = x6[:, g * STATE_DIM:(g + 1) * STATE_DIM].astype(BF16)
        cg = x6[:, gn + g * STATE_DIM:gn + (g + 1) * STATE_DIM].astype(BF16)
        cbm = lax.dot_general(cg, bg, NT_DIMS, preferred_element_type=F32)
        state = st_ref[0, g]
        y_off = lax.dot_general(cg, state.astype(BF16), NT_DIMS, preferred_element_type=F32)
        y_parts = []
        for j in range(HEADS_PER_GROUP // 2):
            scores = []
            for h in (g * HEADS_PER_GROUP + 2 * j, g * HEADS_PER_GROUP + 2 * j + 1):
                seg = cs[:, h:h + 1] - cs_t[h:h + 1, :]
                dec = jnp.exp(jnp.where(causal, seg, -jnp.inf))
                scores.append((cbm * dec).astype(BF16))
            xp = xdt_b[:, j * LANE:(j + 1) * LANE]
            zero = jnp.zeros_like(xp)
            rhs = jnp.concatenate(
                [jnp.where(lane < HEAD_DIM, xp, zero), jnp.where(lane >= HEAD_DIM, xp, zero)], axis=0)
            y_parts.append(jnp.dot(jnp.concatenate(scores, axis=1), rhs, preferred_element_type=F32))
        y = jnp.concatenate(y_parts, axis=1) + y_off * ex[2 * q:3 * q] + xs * dskip[:, c0:c0 + GROUP_W]
        zref = z2 if g < groups_per_block else z3
        ya_ref[:, c0:c0 + GROUP_W] = _gated_norm(
            y, zref[:, bc:bc + GROUP_W], nw[:, c0:c0 + GROUP_W]).astype(ya_ref.dtype)
        st_ref[0, g] = state * _decay_col(cs_last, g) + lax.dot_general(
            xdd_b, bg, TN_DIMS, preferred_element_type=F32)


def _ssd_prompt(p, dt_raw, consts, n_seq, seq_len, d):
    (_, _, dtb, alog, dskip, nw, scw, expand) = consts
    q = SSD_CHUNK if seq_len % SSD_CHUNK == 0 else seq_len
    nc = seq_len // q
    n_groups = 2 * d // GROUP_W
    t_p = n_seq * seq_len
    assert 2 * n_groups * STATE_DIM == d

    def blk(cidx):
        return pl.BlockSpec((q, d), lambda b, c, cidx=cidx: (b * nc + c, cidx))

    def const(arr):
        return pl.BlockSpec(arr.shape, lambda b, c: (0,) * arr.ndim)

    return pl.pallas_call(
        _ssd_prompt_body,
        grid=(n_seq, nc),
        in_specs=[blk(4), blk(5), blk(6), blk(2), blk(3), blk(7), blk(8), blk(9),
                  pl.BlockSpec((q, LANE), lambda b, c: (b * nc + c, 0)),
                  const(dtb), const(alog), const(dskip), const(nw), const(scw), const(expand)],
        out_specs=[
            pl.BlockSpec((q, 2 * d), lambda b, c: (b * nc + c, 0)),
            pl.BlockSpec((q, d), lambda b, c: (b * nc + c, 0)),
            pl.BlockSpec((1, n_groups, GROUP_W, STATE_DIM), lambda b, c: (b, 0, 0, 0)),
            pl.BlockSpec((1, SUBLANE, d), lambda b, c: (b, 0, 0)),
        ],
        out_shape=[
            jax.ShapeDtypeStruct((t_p, 2 * d), BF16),
            jax.ShapeDtypeStruct((t_p, d), BF16),
            jax.ShapeDtypeStruct((n_seq, n_groups, GROUP_W, STATE_DIM), F32),
            jax.ShapeDtypeStruct((n_seq, SUBLANE, d), F32),
        ],
        scratch_shapes=[pltpu.VMEM((d // LANE, q + SUBLANE, LANE), F32)],
        compiler_params=pltpu.CompilerParams(
            dimension_semantics=("arbitrary", "arbitrary"), vmem_limit_bytes=VMEM_LIMIT),
        name="ssd_prompt",
    )(p, p, p, p, p, p, p, p, dt_raw, dtb, alog, dskip, nw, scw, expand)


SAMPLE_SEQS_PER_STEP = 2


SAMPLE_STATE_SLOTS = 3


def _ssd_sample_body(x4, x5, x6, z2, z3, scb, scc, sch, dtr, ssm_hbm, conv_in, sc_in,
                     cw, cb, dtb, alog, dskip, nw, scw, expand, gsel,
                     ya_ref, yb_ref, ssm_out, conv_out, sc_out, exts, ext2s, ssm_buf, ssm_sems):
    d = x4.shape[1]
    n_groups = ssm_out.shape[1]
    step = pl.program_id(0)
    n_steps = pl.num_programs(0)

    def fetch(t):
        slot = t % SAMPLE_STATE_SLOTS
        return pltpu.make_async_copy(
            ssm_hbm.at[pl.ds(t * SAMPLE_SEQS_PER_STEP, SAMPLE_SEQS_PER_STEP)], ssm_buf.at[slot], ssm_sems.at[slot])

    @pl.when(step == 0)
    def _():
        for t in range(SAMPLE_STATE_SLOTS - 1):
            @pl.when(t < n_steps)
            def _():
                fetch(t).start()

    @pl.when(step + SAMPLE_STATE_SLOTS - 1 < n_steps)
    def _():
        fetch(step + SAMPLE_STATE_SLOTS - 1).start()

    fetch(step).wait()
    ssm_in = ssm_buf.at[step % SAMPLE_STATE_SLOTS]
    gn = n_groups * STATE_DIM
    d_inner = n_groups * GROUP_W
    kw = cw.shape[0]
    kw2 = scw.shape[0]
    q = x4.shape[0] // SAMPLE_SEQS_PER_STEP
    groups_per_block = d // GROUP_W
    first = SUBLANE - (kw - 1)
    first2 = SUBLANE - (kw2 - 1)
    nrep = q * q

    rep_t = lax.broadcasted_iota(jnp.int32, (nrep, LANE), 0) % q
    rep_s = lax.broadcasted_iota(jnp.int32, (nrep, LANE), 0) // q
    rep_causal = rep_t >= rep_s
    row_q = lax.broadcasted_iota(jnp.int32, (q, LANE), 0)

    def rep_rows(m):
        return jnp.concatenate([jnp.broadcast_to(m[s:s + 1], (q, m.shape[1])) for s in range(q)], axis=0)

    def tile_rows(m):
        return jnp.concatenate([m] * q, axis=0)

    yb_rows = []
    ya_rows = [[] for _ in range(n_groups)]
    for sidx in range(SAMPLE_SEQS_PER_STEP):
        r0 = sidx * q
        ext, ext2 = exts.at[sidx], ext2s.at[sidx]
        _slab_store(ext, first, conv_in[sidx])
        _slab_store(ext, SUBLANE, x4[r0:r0 + q, :])
        _slab_store(ext, SUBLANE, x5[r0:r0 + q, :], d)
        _slab_store(ext, SUBLANE, x6[r0:r0 + q, :], 2 * d)
        conv_out[sidx] = _slab_load(ext, SUBLANE + q - (kw - 1), kw - 1, 0, 3 * d)

        _slab_store(ext2, first2, sc_in[sidx])
        _slab_store(ext2, SUBLANE, scc[r0:r0 + q, :] * sch[r0:r0 + q, :])
        yb_rows.append(scb[r0:r0 + q, :] * _conv(ext2, 0, d, q, kw2, scw, first2))
        sc_out[sidx] = _slab_load(ext2, SUBLANE + q - (kw2 - 1), kw2 - 1, 0, d)

        dt = _softplus(dtr[r0:r0 + q, :] + dtb[...])
        a = dt * (-jnp.exp(alog[...]))
        cs = jnp.zeros((q, LANE), F32)
        for r in range(q):
            cs = cs + jnp.where(row_q >= r, jnp.broadcast_to(a[r:r + 1], (q, LANE)), 0.0)
        cs_last = cs[q - 1:q, :]
        dend = jnp.exp(cs_last - cs)
        ecs = jnp.exp(cs)

        bmat = _silu(_conv(ext, 2 * d, 2 * d + gn, q, kw, cw, first) + cb[:, 2 * d:2 * d + gn])
        cmat = _silu(_conv(ext, 2 * d + gn, 3 * d, q, kw, cw, first) + cb[:, 2 * d + gn:3 * d])

        cb_hi, cb_lo = _split2(tile_rows(cmat) * rep_rows(bmat))
        cbh = (jnp.dot(cb_hi, gsel[...], preferred_element_type=F32)
               + jnp.dot(cb_lo, gsel[...], preferred_element_type=F32))
        dec = jnp.exp(jnp.where(rep_causal, tile_rows(cs) - rep_rows(cs), -jnp.inf))
        stacked = jnp.concatenate([dt, dend, ecs], axis=0)
        st_hi = stacked.astype(BF16).astype(F32)
        pad = jnp.zeros((LANE - nrep - 6 * q, LANE), F32)
        lhs = jnp.concatenate([cbh * dec, st_hi, stacked - st_hi, pad], axis=0).astype(BF16)

        xdd_parts = []
        for g in range(n_groups):
            c0 = g * GROUP_W
            xs = _silu(_conv(ext, c0, c0 + GROUP_W, q, kw, cw, first) + cb[:, c0:c0 + GROUP_W])
            ex = jnp.dot(lhs, expand[:, c0:c0 + GROUP_W], preferred_element_type=F32)
            o = nrep
            dtx = ex[o:o + q] + ex[o + 3 * q:o + 4 * q]
            dendx = ex[o + q:o + 2 * q] + ex[o + 4 * q:o + 5 * q]
            ecsx = ex[o + 2 * q:o + 3 * q] + ex[o + 5 * q:o + 6 * q]
            xdt = xs * dtx
            xdd = xdt * dendx
            y = xs * dskip[:, c0:c0 + GROUP_W]
            for s in range(q):
                y = y + ex[s * q:(s + 1) * q] * jnp.broadcast_to(xdt[s:s + 1], (q, GROUP_W))
            state = ssm_in[sidx, g]
            cg = cmat[:, g * STATE_DIM:(g + 1) * STATE_DIM]
            y = y + lax.dot_general(cg, state, NT_DIMS, preferred_element_type=F32) * ecsx
            zref = z2 if g < groups_per_block else z3
            zc = (g % groups_per_block) * GROUP_W
            ya_rows[g].append(_gated_norm(y, zref[r0:r0 + q, zc:zc + GROUP_W], nw[:, c0:c0 + GROUP_W]))
            xdd_parts.append(xdd)

        zrows = jnp.zeros((LANE - q, d_inner), F32)
        xdd_t = jnp.concatenate([jnp.concatenate(xdd_parts, axis=1), zrows], axis=0).T
        for g in range(n_groups):
            b_pad = jnp.concatenate(
                [bmat[:, g * STATE_DIM:(g + 1) * STATE_DIM], jnp.zeros((LANE - q, STATE_DIM), F32)], axis=0)
            ssm_out[sidx, g] = ssm_in[sidx, g] * _decay_col(cs_last, g) + jnp.dot(
                xdd_t[g * GROUP_W:(g + 1) * GROUP_W, :], b_pad, preferred_element_type=F32)

    yb_ref[...] = jnp.concatenate(yb_rows, axis=0).astype(yb_ref.dtype)
    for g in range(n_groups):
        ya_ref[:, g * GROUP_W:(g + 1) * GROUP_W] = jnp.concatenate(ya_rows[g], axis=0).astype(ya_ref.dtype)


def _ssd_sample(p, dt_raw, ssm, conv_state, sc_state, consts, gsel, row0, seq_len, d):
    (cw, cb, dtb, alog, dskip, nw, scw, expand) = consts
    n_seq = ssm.shape[0]
    sp = SAMPLE_SEQS_PER_STEP
    rows = sp * seq_len
    assert seq_len == SUBLANE and n_seq % sp == 0 and row0 % rows == 0
    b0 = row0 // rows

    def blk(cidx):
        return pl.BlockSpec((rows, d), lambda i, cidx=cidx: (b0 + i, cidx))

    def const(arr):
        return pl.BlockSpec(arr.shape, lambda i: (0,) * arr.ndim)

    def per_seq(arr):
        return pl.BlockSpec((sp,) + arr.shape[1:], lambda i: (i,) + (0,) * (arr.ndim - 1))

    return pl.pallas_call(
        _ssd_sample_body,
        grid=(n_seq // sp,),
        in_specs=[blk(4), blk(5), blk(6), blk(2), blk(3), blk(7), blk(8), blk(9),
                  pl.BlockSpec((rows, LANE), lambda i: (b0 + i, 0)),
                  pl.BlockSpec(memory_space=pl.ANY), per_seq(conv_state), per_seq(sc_state),
                  const(cw), const(cb), const(dtb), const(alog), const(dskip), const(nw), const(scw),
                  const(expand), const(gsel)],
        out_specs=[
            pl.BlockSpec((rows, 2 * d), lambda i: (i, 0)),
            pl.BlockSpec((rows, d), lambda i: (i, 0)),
            per_seq(ssm), per_seq(conv_state), per_seq(sc_state),
        ],
        out_shape=[
            jax.ShapeDtypeStruct((n_seq * seq_len, 2 * d), BF16),
            jax.ShapeDtypeStruct((n_seq * seq_len, d), BF16),
            jax.ShapeDtypeStruct(ssm.shape, F32),
            jax.ShapeDtypeStruct(conv_state.shape, F32),
            jax.ShapeDtypeStruct(sc_state.shape, F32),
        ],
        scratch_shapes=[pltpu.VMEM((sp, 3 * d // LANE, 2 * SUBLANE, LANE), F32),
                        pltpu.VMEM((sp, d // LANE, 2 * SUBLANE, LANE), F32),
                        pltpu.VMEM((SAMPLE_STATE_SLOTS, sp) + ssm.shape[1:], F32),
                        pltpu.SemaphoreType.DMA((SAMPLE_STATE_SLOTS,))],
        compiler_params=pltpu.CompilerParams(
            dimension_semantics=("arbitrary",), vmem_limit_bytes=VMEM_LIMIT),
        name="ssd_sample",
    )(p, p, p, p, p, p, p, p, dt_raw, ssm, conv_state, sc_state,
      cw, cb, dtb, alog, dskip, nw, scw, expand, gsel)


def _branch_out_body(yap_ref, ybp_ref, yas_ref, ybs_ref, ga_ref, gb_ref, wa_ref, wb_ref, o_ref, *, prompt_tiles):
    def run(refs):
        ya_ref, yb_ref = refs
        pa = jnp.dot(ya_ref[...], wa_ref[...], preferred_element_type=F32)
        pb = jnp.dot(yb_ref[...], wb_ref[...], preferred_element_type=F32)
        merged = jax.nn.sigmoid(ga_ref[...]) * pa + jax.nn.sigmoid(gb_ref[...]) * pb
        o_ref[...] = merged.astype(o_ref.dtype)

    _by_stream(pl.program_id(0), prompt_tiles, run, (yap_ref, ybp_ref), (yas_ref, ybs_ref))


def _branch_out(ya_p, yb_p, ya_s, yb_s, p, wa, wb):
    t_p, d = yb_p.shape
    t_s = yb_s.shape[0]
    tm = _tile(math.gcd(t_p, t_s), 512, 16)
    tn = _tile(d, 512, LANE)
    nj = d // tn
    npt = t_p // tm
    yap_spec, yas_spec = _stream_specs((tm, 2 * d), npt)
    ybp_spec, ybs_spec = _stream_specs((tm, d), npt)
    return pl.pallas_call(
        functools.partial(_branch_out_body, prompt_tiles=npt),
        grid=((t_p + t_s) // tm, nj),
        in_specs=[
            yap_spec, ybp_spec, yas_spec, ybs_spec,
            pl.BlockSpec((tm, tn), lambda i, j: (i, j)),
            pl.BlockSpec((tm, tn), lambda i, j: (i, nj + j)),
            pl.BlockSpec((2 * d, tn), lambda i, j: (0, j)),
            pl.BlockSpec((d, tn), lambda i, j: (0, j)),
        ],
        out_specs=pl.BlockSpec((tm, tn), lambda i, j: (i, j)),
        out_shape=jax.ShapeDtypeStruct((t_p + t_s, d), BF16),
        compiler_params=pltpu.CompilerParams(
            dimension_semantics=("arbitrary", "arbitrary"), vmem_limit_bytes=VMEM_LIMIT),
        name="branch_out",
    )(ya_p, yb_p, ya_s, yb_s, p, p, wa, wb)


def _mix_route_body(m_ref, xp_ref, xs_ref, wo_ref, nw_ref, rhi_ref, rlo_ref,
                    x1_ref, h2_ref, route_ref, wts_ref, cnt_ref, *, n_coarse, per_group, prompt_tiles):
    @pl.when(pl.program_id(0) == 0)
    def _():
        cnt_ref[...] = jnp.zeros(cnt_ref.shape, F32)

    def run(x_ref):
        x1 = x_ref[...] + jnp.dot(m_ref[...], wo_ref[...], preferred_element_type=F32)
        x1_ref[...] = x1
        h2 = x1 * lax.rsqrt(jnp.mean(x1 * x1, axis=-1, keepdims=True) + NORM_EPS) * nw_ref[...]
        h2_ref[...] = h2
        h_hi, h_lo = _split2(h2)
        logits = (jnp.dot(h_hi, rhi_ref[...], preferred_element_type=F32)
                  + jnp.dot(h_hi, rlo_ref[...], preferred_element_type=F32)
                  + jnp.dot(h_lo, rhi_ref[...], preferred_element_type=F32))

        tm = logits.shape[0]
        n_fine = n_coarse * per_group
        lane = lax.broadcasted_iota(jnp.int32, logits.shape, 1)
        big = jnp.int32(LANE)
        neg = -jnp.inf
        is_c = lane < n_coarse
        lc = jnp.where(is_c, logits, neg)
        mc = jnp.max(lc, axis=-1, keepdims=True)
        grp = jnp.min(jnp.where(is_c & (lc == mc), lane, big), axis=-1, keepdims=True)
        p_grp = 1.0 / jnp.sum(jnp.where(is_c, jnp.exp(lc - mc), 0.0), axis=-1, keepdims=True)
        eidx = lane - n_coarse
        sel = (eidx >= 0) & (eidx < n_fine) & ((eidx // per_group) == grp)
        lf = jnp.where(sel, logits, neg)
        v1 = jnp.max(lf, axis=-1, keepdims=True)
        i1 = jnp.min(jnp.where(sel & (lf == v1), eidx, big), axis=-1, keepdims=True)
        sel2 = sel & (eidx != i1)
        lf2 = jnp.where(sel2, logits, neg)
        v2 = jnp.max(lf2, axis=-1, keepdims=True)
        i2 = jnp.min(jnp.where(sel2 & (lf2 == v2), eidx, big), axis=-1, keepdims=True)
        e2 = jnp.exp(v2 - v1)
        w1 = p_grp / (1.0 + e2)
        w2 = p_grp * e2 / (1.0 + e2)
        wts_ref[...] = jnp.where(lane == 0, w1, jnp.where(lane == 1, w2, 0.0))

        hit1 = eidx == i1
        hit2 = eidx == i2
        hits = hit1.astype(F32) + hit2.astype(F32)
        earlier = (lax.broadcasted_iota(jnp.int32, (tm, tm), 0)
                   > lax.broadcasted_iota(jnp.int32, (tm, tm), 1)).astype(BF16)
        before = jnp.dot(earlier, hits.astype(BF16), preferred_element_type=F32) + cnt_ref[...]
        r1 = jnp.sum(jnp.where(hit1, before, 0.0), axis=-1, keepdims=True).astype(jnp.int32)
        r2 = jnp.sum(jnp.where(hit2, before, 0.0), axis=-1, keepdims=True).astype(jnp.int32)
        cnt_ref[...] += jnp.sum(hits, axis=0, keepdims=True)
        route_ref[...] = jnp.where(lane == 0, i1, jnp.where(lane == 1, i2, jnp.where(
            lane == 2, r1, jnp.where(lane == 3, r2, 0))))

    _by_stream(pl.program_id(0), prompt_tiles, run, xp_ref, xs_ref)


def _mix_route(merged, x_p, x_s, wo, norm_w, r_hi, r_lo, n_coarse, per_group):
    t_p, d = x_p.shape
    T = t_p + x_s.shape[0]
    tm = _tile(math.gcd(t_p, x_s.shape[0]), 256, 16)
    npt = t_p // tm
    row = lambda i: (i, 0)
    fixed = lambda i: (0, 0)
    xp_spec, xs_spec = _stream_specs((tm, d), npt)
    resident = dict(pipeline_mode=pl.Buffered(1))
    return pl.pallas_call(
        functools.partial(_mix_route_body, n_coarse=n_coarse, per_group=per_group, prompt_tiles=npt),
        grid=(T // tm,),
        in_specs=[
            pl.BlockSpec((tm, d), row), xp_spec, xs_spec, pl.BlockSpec((d, d), fixed, **resident),
            pl.BlockSpec((1, d), fixed), pl.BlockSpec((d, LANE), fixed, **resident),
            pl.BlockSpec((d, LANE), fixed, **resident),
        ],
        out_specs=[pl.BlockSpec((tm, d), row), pl.BlockSpec((tm, d), row),
                   pl.BlockSpec((tm, LANE), row), pl.BlockSpec((tm, LANE), row),
                   pl.BlockSpec((1, LANE), fixed)],
        out_shape=[
            jax.ShapeDtypeStruct((T, d), F32), jax.ShapeDtypeStruct((T, d), F32),
            jax.ShapeDtypeStruct((T, LANE), jnp.int32), jax.ShapeDtypeStruct((T, LANE), F32),
            jax.ShapeDtypeStruct((1, LANE), F32),
        ],
        compiler_params=pltpu.CompilerParams(
            dimension_semantics=("arbitrary",), vmem_limit_bytes=VMEM_LIMIT),
        name="mix_route",
    )(merged, x_p, x_s, wo, norm_w, r_hi, r_lo)


MOE_MOVE_TOKENS = 128
MOE_MOVE_UNROLL = 8


def _moe_scatter_body(dest_ref, fill_ref, h_ref, xs_hbm, zeros, sem, zsem):
    tt = h_ref.shape[0]
    t0 = pl.program_id(0) * tt
    n_experts = fill_ref.shape[0] - 1
    n_blocks = xs_hbm.shape[0] // MOE_ROWS

    @pl.when(pl.program_id(0) == 0)
    def _():
        zeros[...] = jnp.zeros(zeros.shape, zeros.dtype)

        def fill(row):
            return pltpu.make_async_copy(zeros, xs_hbm.at[pl.ds(pl.multiple_of(row, MOE_ROWS), MOE_ROWS)], zsem)

        def expert_fill(action):
            def body(e, carry):
                @pl.when(fill_ref[e] >= 0)
                def _():
                    action(fill(fill_ref[e]))
                return carry
            lax.fori_loop(0, n_experts, body, 0)

        def tail_fill(action):
            def body(blk, carry):
                action(fill(blk * MOE_ROWS))
                return carry
            lax.fori_loop(fill_ref[n_experts], n_blocks, body, 0)

        expert_fill(lambda c: c.start())
        tail_fill(lambda c: c.start())
        expert_fill(lambda c: c.wait())
        tail_fill(lambda c: c.wait())

    def start(i, carry):
        for k in range(MOE_TOP_K):
            pltpu.make_async_copy(
                h_ref.at[pl.ds(i, 1)], xs_hbm.at[pl.ds(dest_ref[(t0 + i) * MOE_TOP_K + k], 1)], sem
            ).start(priority=k % 2)
        return carry

    lax.fori_loop(0, tt, start, 0, unroll=MOE_MOVE_UNROLL)
    for _ in range(MOE_TOP_K):
        pltpu.make_async_copy(h_ref, xs_hbm.at[pl.ds(0, tt)], sem).wait()


def _moe_scatter(dest, fill_rows, h2, n_rows):
    T, d = h2.shape
    tt = MOE_MOVE_TOKENS
    assert T % tt == 0
    return pl.pallas_call(
        _moe_scatter_body,
        grid_spec=pltpu.PrefetchScalarGridSpec(
            num_scalar_prefetch=2,
            grid=(T // tt,),
            in_specs=[pl.BlockSpec((tt, d), lambda i, dr, fr: (i, 0))],
            out_specs=pl.BlockSpec(memory_space=pl.ANY),
            scratch_shapes=[pltpu.VMEM((MOE_ROWS, d), h2.dtype), pltpu.SemaphoreType.DMA(()),
                            pltpu.SemaphoreType.DMA(())],
        ),
        out_shape=jax.ShapeDtypeStruct((n_rows, d), h2.dtype),
        compiler_params=pltpu.CompilerParams(dimension_semantics=("arbitrary",)),
        name="moe_scatter",
    )(dest, fill_rows, h2)


MOE_WEIGHT_PIECES = 8
MOE_WEIGHT_RING = 3


def _moe_ffn_body(sched_ref, order_ref, n_used_ref, x_ref, wg_hbm, wu_hbm, wd_hbm, o_ref,
                  wg_b, wu_b, wd_b, sg, su, sd, sems):
    b = pl.program_id(0)
    hbm = (wg_hbm, wu_hbm, wd_hbm)
    stage = (sg, su, sd)
    resident = (wg_b, wu_b, wd_b)
    n_total = order_ref[order_ref.shape[0] - 1]

    def copies(c):
        e = order_ref[c // MOE_WEIGHT_PIECES]
        piece = c % MOE_WEIGHT_PIECES
        ring = c % MOE_WEIGHT_RING
        out = []
        for k in range(3):
            rows = stage[k].shape[1]
            out.append(pltpu.make_async_copy(
                hbm[k].at[0, e, pl.ds(pl.multiple_of(piece * rows, rows), rows), :],
                stage[k].at[ring], sems.at[ring, k]))
        return out

    def start(c):
        @pl.when(c < n_total)
        def _():
            for cp in copies(c):
                cp.start()

    def land(lo, hi):
        def body(c, carry):
            slot = (c // MOE_WEIGHT_PIECES) % 2
            piece = c % MOE_WEIGHT_PIECES
            ring = c % MOE_WEIGHT_RING
            for k, cp in enumerate(copies(c)):
                cp.wait()
                rows = stage[k].shape[1]
                resident[k][slot, pl.ds(pl.multiple_of(piece * rows, rows), rows), :] = stage[k][ring].astype(BF16)
            start(c + MOE_WEIGHT_RING)
            return carry

        lax.fori_loop(lo, hi, body, 0)

    @pl.when(b < n_used_ref[0])
    def _():
        @pl.when(b == 0)
        def _():
            for c in range(MOE_WEIGHT_RING):
                start(c)
            land(0, MOE_WEIGHT_PIECES)

        slot = sched_ref[0, b]
        xb = x_ref[...].astype(BF16)
        gate = jnp.dot(xb, wg_b[slot], preferred_element_type=F32)
        up = jnp.dot(xb, wu_b[slot], preferred_element_type=F32)
        act = (_silu(gate) * up).astype(BF16)
        o_ref[...] = jnp.dot(act, wd_b[slot], preferred_element_type=F32)
        land(sched_ref[1, b], sched_ref[2, b])

    @pl.when(b >= n_used_ref[0])
    def _():
        o_ref[...] = jnp.zeros(o_ref.shape, o_ref.dtype)


def _moe_ffn(sched, order, n_used, xs, wg, wu, wd):
    R, d = xs.shape
    f = wg.shape[3]
    nb = R // MOE_ROWS
    np_ = MOE_WEIGHT_PIECES
    assert d % np_ == 0 and f % np_ == 0
    any_spec = pl.BlockSpec(memory_space=pl.ANY)
    return pl.pallas_call(
        _moe_ffn_body,
        grid_spec=pltpu.PrefetchScalarGridSpec(
            num_scalar_prefetch=3,
            grid=(nb,),
            in_specs=[pl.BlockSpec((MOE_ROWS, d), lambda b, sc, od, nu: (b, 0)), any_spec, any_spec, any_spec],
            out_specs=pl.BlockSpec((MOE_ROWS, d), lambda b, sc, od, nu: (b, 0)),
            scratch_shapes=[
                pltpu.VMEM((2, d, f), BF16), pltpu.VMEM((2, d, f), BF16), pltpu.VMEM((2, f, d), BF16),
                pltpu.VMEM((MOE_WEIGHT_RING, d // np_, f), F32), pltpu.VMEM((MOE_WEIGHT_RING, d // np_, f), F32),
                pltpu.VMEM((MOE_WEIGHT_RING, f // np_, d), F32),
                pltpu.SemaphoreType.DMA((MOE_WEIGHT_RING, 3)),
            ],
        ),
        out_shape=jax.ShapeDtypeStruct((R, d), F32),
        compiler_params=pltpu.CompilerParams(
            dimension_semantics=("arbitrary",), vmem_limit_bytes=VMEM_LIMIT),
        name="moe_ffn",
    )(sched, order, n_used, xs, wg, wu, wd)


def _moe_combine_body(dest_ref, x1_ref, wts_ref, nw_ref, yb_hbm, op_ref, os_ref, buf, sems, *, prompt_tiles):
    tt = x1_ref.shape[0]
    i = pl.program_id(0)

    def gather(tile, slot):
        def start(r, carry):
            for k in range(MOE_TOP_K):
                pltpu.make_async_copy(
                    yb_hbm.at[pl.ds(dest_ref[(tile * tt + r) * MOE_TOP_K + k], 1)],
                    buf.at[slot, k, pl.ds(r, 1)], sems.at[slot]).start(priority=k % 2)
            return carry

        lax.fori_loop(0, tt, start, 0, unroll=MOE_MOVE_UNROLL)

    @pl.when(i == 0)
    def _():
        gather(0, 0)

    @pl.when(i + 1 < pl.num_programs(0))
    def _():
        gather(i + 1, (i + 1) % 2)

    slot = i % 2
    for k in range(MOE_TOP_K):
        pltpu.make_async_copy(yb_hbm.at[pl.ds(0, tt)], buf.at[slot, k], sems.at[slot]).wait()

    def finish(o_ref):
        w = wts_ref[...]
        x2 = x1_ref[...] + (buf[slot, 0] * w[:, 0:1] + buf[slot, 1] * w[:, 1:2])
        o_ref[...] = x2 * lax.rsqrt(jnp.mean(x2 * x2, axis=-1, keepdims=True) + NORM_EPS) * nw_ref[...]

    _by_stream(i, prompt_tiles, finish, op_ref, os_ref)


def _moe_combine(dest, x1, wts, norm_w, yb, t_p):
    T, d = x1.shape
    tt = MOE_MOVE_TOKENS
    assert t_p % tt == 0 and T % tt == 0
    npt = t_p // tt
    op_spec, os_spec = _stream_specs((tt, d), npt)
    return pl.pallas_call(
        functools.partial(_moe_combine_body, prompt_tiles=npt),
        grid_spec=pltpu.PrefetchScalarGridSpec(
            num_scalar_prefetch=1,
            grid=(T // tt,),
            in_specs=[
                pl.BlockSpec((tt, d), lambda i, dr: (i, 0)),
                pl.BlockSpec((tt, LANE), lambda i, dr: (i, 0)),
                pl.BlockSpec((1, d), lambda i, dr: (0, 0)),
                pl.BlockSpec(memory_space=pl.ANY),
            ],
            out_specs=[op_spec, os_spec],
            scratch_shapes=[pltpu.VMEM((2, MOE_TOP_K, tt, d), F32), pltpu.SemaphoreType.DMA((2,))],
        ),
        out_shape=[jax.ShapeDtypeStruct((t_p, d), F32), jax.ShapeDtypeStruct((T - t_p, d), F32)],
        compiler_params=pltpu.CompilerParams(dimension_semantics=("arbitrary",)),
        name="moe_combine",
    )(dest, x1, wts, norm_w, yb)


def _route_rows(eid, rank, counts, n_assign):
    n_experts = counts.shape[0]
    padded = (counts + MOE_ROWS - 1) // MOE_ROWS * MOE_ROWS
    pend = jnp.cumsum(padded)
    pstart = pend - padded
    experts = jnp.arange(n_experts, dtype=jnp.int32)
    onehot = eid[:, :, None] == experts[None, None, :]
    dest = (jnp.sum(jnp.where(onehot, pstart[None, None, :], 0), axis=-1) + rank).astype(jnp.int32)
    n_blocks = -(-n_assign // MOE_ROWS) + n_experts
    blk_start = jnp.arange(n_blocks, dtype=jnp.int32) * MOE_ROWS
    used = blk_start < pend[-1]
    blk_e = jnp.minimum(jnp.sum((blk_start[:, None] >= pend[None, :]).astype(jnp.int32), axis=1), n_experts - 1)
    n_used = (pend[-1] // MOE_ROWS).astype(jnp.int32).reshape(1)

    nonempty = counts > 0
    ordinal = jnp.cumsum(nonempty.astype(jnp.int32)) - 1
    n_nonempty = jnp.sum(nonempty.astype(jnp.int32))
    order = jnp.argsort(jnp.where(nonempty, experts, n_experts + experts)).astype(jnp.int32)
    n_slices = MOE_WEIGHT_PIECES * n_nonempty
    pos = blk_start // MOE_ROWS - (pstart // MOE_ROWS)[blk_e]
    nblk = jnp.maximum((padded // MOE_ROWS)[blk_e], 1)
    has_next = used & (ordinal[blk_e] + 1 < n_nonempty)
    base = MOE_WEIGHT_PIECES * (ordinal[blk_e] + 1)
    first = jnp.where(has_next, base + MOE_WEIGHT_PIECES * pos // nblk, n_slices)
    last = jnp.where(has_next, base + MOE_WEIGHT_PIECES * (pos + 1) // nblk, n_slices)
    slot = jnp.where(used, ordinal[blk_e] % 2, 0)
    sched = jnp.stack([slot, first, last]).astype(jnp.int32)
    order = jnp.concatenate([order, n_slices.reshape(1)]).astype(jnp.int32)
    fill_rows = jnp.concatenate([jnp.where(nonempty, pend - MOE_ROWS, -1), n_used]).astype(jnp.int32)
    return dest.reshape(-1), sched, order, fill_rows, n_used, n_blocks


def _pad_lanes(v, fill=0.0):
    return jnp.pad(v.astype(F32), (0, LANE - v.shape[0]), constant_values=fill).reshape(1, LANE)


def kernel(x_prompt, x_sample, state_ssm, state_ssd_conv, state_short_conv, norm_mixer, w_in, ssd_conv_w,
           ssd_conv_b, ssd_dt_bias, ssd_a_log, ssd_d, ssd_norm, sc_conv_w, w_branch_out, w_out, norm_ffn,
           w_router_coarse, w_router_fine, w_expert_gate, w_expert_up, w_expert_down, norm_final):
    depth = w_in.shape[0]
    assert depth == 1
    n_p, seq_p, d = x_prompt.shape
    n_s, seq_s, _ = x_sample.shape
    d_inner = 2 * d
    n_heads = d_inner // HEAD_DIM
    n_groups = d_inner // GROUP_W
    gn = n_groups * STATE_DIM
    conv_dim = d_inner + 2 * gn
    assert conv_dim == 3 * d and n_heads <= LANE and ssd_conv_w.shape[2] == conv_dim
    t_p, t_s = n_p * seq_p, n_s * seq_s
    n_coarse = w_router_coarse.shape[2]
    n_experts = w_router_fine.shape[2]
    assert n_coarse + n_experts <= LANE

    off_dt = 2 * d + d_inner + conv_dim
    off_sc = off_dt + n_heads
    w_in_t = jnp.swapaxes(w_in, 1, 2)
    w_dt = jnp.pad(w_in_t[0, off_dt:off_sc, :].T, ((0, 0), (0, LANE - n_heads))).astype(BF16)
    head_of_col = jnp.arange(d_inner, dtype=jnp.int32) // HEAD_DIM
    expand = (jnp.arange(LANE, dtype=jnp.int32)[:, None] == head_of_col[None, :]).astype(BF16)
    group_of_n = jnp.arange(gn, dtype=jnp.int32) // STATE_DIM
    group_of_head = jnp.arange(LANE, dtype=jnp.int32) // HEADS_PER_GROUP
    gsel = ((group_of_n[:, None] == group_of_head[None, :])
            & (jnp.arange(LANE)[None, :] < n_heads)).astype(BF16)
    consts = (
        ssd_conv_w[0], ssd_conv_b[0].reshape(1, conv_dim), _pad_lanes(ssd_dt_bias[0]), _pad_lanes(ssd_a_log[0]),
        jnp.repeat(ssd_d[0].astype(F32), HEAD_DIM).reshape(1, d_inner), ssd_norm[0].reshape(1, d_inner),
        sc_conv_w[0], expand,
    )
    wbo = w_branch_out[0].astype(BF16)
    wa, wb = wbo[:d_inner], wbo[d_inner:]
    wo = w_out[0].astype(BF16)
    w_router = jnp.pad(jnp.concatenate([w_router_coarse[0], w_router_fine[0]], axis=1),
                       ((0, 0), (0, LANE - n_coarse - n_experts)))
    r_hi = w_router.astype(BF16)
    r_lo = (w_router - r_hi.astype(F32)).astype(BF16)

    x_p = x_prompt.reshape(t_p, d)
    x_s = x_sample.reshape(t_s, d)
    h, dt_raw = _prenorm(x_p, x_s, norm_mixer[0].reshape(1, d), w_dt)
    p, tails = _inproj(h, w_in_t[0], off_dt, off_sc, consts[0], consts[1], t_p, seq_p)
    ya_p, yb_p, p_ssm, p_sc_tail = _ssd_prompt(p, dt_raw, consts, n_p, seq_p, d)
    ya_s, yb_s, s_ssm, s_conv, s_sc = _ssd_sample(
        p, dt_raw, state_ssm[0].reshape(n_s, n_groups, GROUP_W, STATE_DIM), state_ssd_conv[0],
        state_short_conv[0], consts, gsel, t_p, seq_s, d)
    merged = _branch_out(ya_p, yb_p, ya_s, yb_s, p, wa, wb)
    x1, h2, route, wts, counts = _mix_route(merged, x_p, x_s, wo, norm_ffn[0].reshape(1, d), r_hi, r_lo,
                                            n_coarse, n_experts // n_coarse)

    n_assign = (t_p + t_s) * MOE_TOP_K
    dest, sched, order, fill_rows, n_used, n_blocks = _route_rows(
        route[:, 0:MOE_TOP_K], route[:, MOE_TOP_K:2 * MOE_TOP_K],
        counts[0, n_coarse:n_coarse + n_experts].astype(jnp.int32), n_assign)
    xs = _moe_scatter(dest, fill_rows, h2, n_blocks * MOE_ROWS)
    yrows = _moe_ffn(sched, order, n_used, xs, w_expert_gate, w_expert_up, w_expert_down)
    out_p, out_s = _moe_combine(dest, x1, wts, norm_final.reshape(1, d), yrows, t_p)

    kw = ssd_conv_w.shape[1]
    kw2 = sc_conv_w.shape[1]
    tiles_per_seq = (tails.shape[0] // SUBLANE) * seq_p // (t_p + t_s)
    seq_tails = tails.reshape(-1, SUBLANE, tails.shape[1])[tiles_per_seq - 1:n_p * tiles_per_seq:tiles_per_seq]
    p_conv = seq_tails[:, SUBLANE - (kw - 1):, 4 * d:7 * d]
    return (
        out_p.reshape(n_p, seq_p, d),
        out_s.reshape(n_s, seq_s, d),
        p_ssm.reshape(1, n_p, n_heads, HEAD_DIM, STATE_DIM),
        p_conv[None],
        p_sc_tail[:, SUBLANE - (kw2 - 1):, :][None],
        s_ssm.reshape(1, n_s, n_heads, HEAD_DIM, STATE_DIM),
        s_conv[None],
        s_sc[None],
    )
```

```python
import functools
import math

import jax
import jax.numpy as jnp
from jax import lax
from jax.experimental import pallas as pl
from jax.experimental.pallas import tpu as pltpu

F32 = jnp.float32
BF16 = jnp.bfloat16

NORM_EPS = 1e-6
SSD_NORM_EPS = 1e-5
HEAD_DIM = 64
STATE_DIM = 128
HEADS_PER_GROUP = 8
GROUP_W = HEADS_PER_GROUP * HEAD_DIM
SSD_CHUNK = 128
MOE_TOP_K = 2
MOE_ROWS = 256

LANE = 128
SUBLANE = 8
VMEM_LIMIT = 56 * 1024 * 1024

NT_DIMS = (((1,), (1,)), ((), ()))
TN_DIMS = (((0,), (0,)), ((), ()))


def _tile(n, target, align):
    best = None
    for t in range(align, min(n, target) + 1, align):
        if n % t == 0:
            best = t
    assert best is not None, (n, target, align)
    return best


def _split2(v):
    hi = v.astype(BF16)
    lo = (v - hi.astype(F32)).astype(BF16)
    return hi, lo


def _split3(v):
    hi = v.astype(BF16)
    r = v - hi.astype(F32)
    mid = r.astype(BF16)
    lo = (r - mid.astype(F32)).astype(BF16)
    return hi, mid, lo


def _softplus(x):
    return jnp.maximum(x, 0.0) + jnp.log1p(jnp.exp(-jnp.abs(x)))


def _silu(x):
    return x * jax.nn.sigmoid(x)


def _by_stream(tile, prompt_tiles, fn, prompt_ref, sample_ref):
    @pl.when(tile < prompt_tiles)
    def _():
        fn(prompt_ref)

    @pl.when(tile >= prompt_tiles)
    def _():
        fn(sample_ref)


def _stream_specs(block, prompt_tiles, **kwargs):
    pad = (0,) * (len(block) - 1)
    prompt = lambda i, *_: (jnp.minimum(i, prompt_tiles - 1),) + pad
    sample = lambda i, *_: (jnp.maximum(i - prompt_tiles, 0),) + pad
    return pl.BlockSpec(block, prompt, **kwargs), pl.BlockSpec(block, sample, **kwargs)


def _prenorm_body(xp_ref, xs_ref, nw_ref, wdt_ref, h_ref, dt_ref, *, prompt_tiles):
    def run(x_ref):
        x = x_ref[...]
        h = x * lax.rsqrt(jnp.mean(x * x, axis=-1, keepdims=True) + NORM_EPS) * nw_ref[...]
        hb = h.astype(BF16)
        h_ref[...] = hb
        dt_ref[...] = jnp.dot(hb, wdt_ref[...], preferred_element_type=F32)

    _by_stream(pl.program_id(0), prompt_tiles, run, xp_ref, xs_ref)


def _prenorm(x_p, x_s, norm_w, w_dt):
    t_p, D = x_p.shape
    T = t_p + x_s.shape[0]
    tm = _tile(math.gcd(t_p, x_s.shape[0]), 512, 16)
    npt = t_p // tm
    xp_spec, xs_spec = _stream_specs((tm, D), npt)
    return pl.pallas_call(
        functools.partial(_prenorm_body, prompt_tiles=npt),
        grid=(T // tm,),
        in_specs=[xp_spec, xs_spec, pl.BlockSpec((1, D), lambda i: (0, 0)),
                  pl.BlockSpec((D, LANE), lambda i: (0, 0))],
        out_specs=[pl.BlockSpec((tm, D), lambda i: (i, 0)), pl.BlockSpec((tm, LANE), lambda i: (i, 0))],
        out_shape=[jax.ShapeDtypeStruct((T, D), BF16), jax.ShapeDtypeStruct((T, LANE), F32)],
        compiler_params=pltpu.CompilerParams(
            dimension_semantics=("arbitrary",), vmem_limit_bytes=VMEM_LIMIT),
        name="prenorm",
    )(x_p, x_s, norm_w, w_dt)


INPROJ_CONV_SLABS = 4


def _inproj_body(h_ref, w_ref, cw_ref, cb_ref, p_ref, tail_ref, w_scr, cext,
                 *, blocks_per_d, prompt_tiles, tiles_per_seq):
    j = pl.program_id(0)
    i = pl.program_id(1)
    tm, tn = p_ref.shape
    kw = cw_ref.shape[0]

    @pl.when(i == 0)
    def _():
        w_scr[...] = w_ref[...].astype(BF16)

    is_conv = (j >= 4 * blocks_per_d) & (j < 7 * blocks_per_d) & (i < prompt_tiles)

    @pl.when(jnp.logical_not(is_conv))
    def _():
        raw = lax.dot_general(h_ref[...], w_scr[...], NT_DIMS, preferred_element_type=F32)
        p_ref[...] = raw
        tail_ref[...] = raw[tm - SUBLANE:tm, :]

    @pl.when(is_conv)
    def _():
        @pl.when(i % tiles_per_seq == 0)
        def _():
            cext[:, 0:SUBLANE, :] = jnp.zeros((cext.shape[0], SUBLANE, LANE), F32)

        ws = tn // INPROJ_CONV_SLABS
        first = SUBLANE - (kw - 1)
        for k in range(INPROJ_CONV_SLABS):
            raw = lax.dot_general(h_ref[...], w_scr[k * ws:(k + 1) * ws, :], NT_DIMS, preferred_element_type=F32)
            for s in range(k * ws // LANE, (k + 1) * ws // LANE):
                lo, hi = s * LANE, (s + 1) * LANE
                cext[s, SUBLANE:SUBLANE + tm, :] = raw[:, lo - k * ws:hi - k * ws]
                acc = cb_ref[:, lo:hi]
                for t in range(kw):
                    acc = acc + cw_ref[t:t + 1, lo:hi] * cext[s, first + t:first + t + tm, :]
                p_ref[:, lo:hi] = _silu(acc)
                tail = cext[s, tm:tm + SUBLANE, :]
                tail_ref[:, lo:hi] = tail
                cext[s, 0:SUBLANE, :] = tail


def _inproj(h, w_in_t, n_head_cols, tail_start, cw, cb, t_p, seq_p):
    T, D = h.shape
    n_tail = w_in_t.shape[0] - tail_start
    tm = _tile(math.gcd(seq_p, T - t_p), 1024, 16)
    tn = _tile(math.gcd(D, n_tail), 1024, LANE)
    main_tiles = n_head_cols // tn
    n_out = n_head_cols + n_tail
    bpd = D // tn
    assert tail_start % SUBLANE == 0 and n_head_cols % tn == 0
    conv_block = lambda j, i: (0, jnp.clip(j - 4 * bpd, 0, 3 * bpd - 1))
    w_row = lambda j, i: (pl.multiple_of(
        jnp.where(j < main_tiles, j * tn, tail_start + (j - main_tiles) * tn), SUBLANE), 0)
    return pl.pallas_call(
        functools.partial(_inproj_body, blocks_per_d=bpd, prompt_tiles=t_p // tm, tiles_per_seq=seq_p // tm),
        grid=(n_out // tn, T // tm),
        in_specs=[
            pl.BlockSpec((tm, D), lambda j, i: (i, 0)),
            pl.BlockSpec((pl.Element(tn), pl.Element(D)), w_row),
            pl.BlockSpec((cw.shape[0], tn), conv_block),
            pl.BlockSpec((1, tn), conv_block),
        ],
        out_specs=[pl.BlockSpec((tm, tn), lambda j, i: (i, j)),
                   pl.BlockSpec((SUBLANE, tn), lambda j, i: (i, j))],
        out_shape=[jax.ShapeDtypeStruct((T, n_out), F32),
                   jax.ShapeDtypeStruct((T // tm * SUBLANE, n_out), F32)],
        scratch_shapes=[pltpu.VMEM((tn, D), BF16), pltpu.VMEM((tn // LANE, tm + SUBLANE, LANE), F32)],
        compiler_params=pltpu.CompilerParams(
            dimension_semantics=("arbitrary", "arbitrary"), vmem_limit_bytes=VMEM_LIMIT),
        name="inproj",
    )(h, w_in_t, cw, cb)


def _slab_store(ext, row0, value, col0=0):
    for s in range(value.shape[1] // LANE):
        ext[col0 // LANE + s, row0:row0 + value.shape[0], :] = value[:, s * LANE:(s + 1) * LANE]


def _slab_load(ext, row0, rows, lo, hi):
    return jnp.concatenate([ext[s, row0:row0 + rows, :] for s in range(lo // LANE, hi // LANE)], axis=1)


def _conv(ext, lo, hi, q, width, w_ref, first):
    parts = []
    for s in range(lo // LANE, hi // LANE):
        acc = None
        for k in range(width):
            term = w_ref[k:k + 1, s * LANE:(s + 1) * LANE] * ext[s, first + k:first + k + q, :]
            acc = term if acc is None else acc + term
        parts.append(acc)
    return jnp.concatenate(parts, axis=1)


def _gated_norm(y, z, nw):
    g = y * _silu(z)
    return g * lax.rsqrt(jnp.mean(g * g, axis=-1, keepdims=True) + SSD_NORM_EPS) * nw


def _decay_col(cs_last_row, g):
    d = jnp.exp(cs_last_row)
    parts = [
        jnp.broadcast_to(d[0:1, g * HEADS_PER_GROUP + j:g * HEADS_PER_GROUP + j + 1], (HEAD_DIM, STATE_DIM))
        for j in range(HEADS_PER_GROUP)
    ]
    return jnp.concatenate(parts, axis=0)


def _ssd_prompt_body(x4, x5, x6, z2, z3, scb, scc, sch, dtr,
                     dtb, alog, dskip, nw, scw, expand,
                     ya_ref, yb_ref, st_ref, sct_ref, ext2):
    q, d = x4.shape
    n_groups = st_ref.shape[1]
    gn = n_groups * STATE_DIM
    kw2 = scw.shape[0]

    @pl.when(pl.program_id(1) == 0)
    def _():
        st_ref[...] = jnp.zeros(st_ref.shape, F32)
        ext2[:, 0:SUBLANE, :] = jnp.zeros((ext2.shape[0], SUBLANE, LANE), F32)

    _slab_store(ext2, SUBLANE, scc[...] * sch[...])
    v = _conv(ext2, 0, d, q, kw2, scw, SUBLANE - (kw2 - 1))
    yb_ref[...] = (scb[...] * v).astype(yb_ref.dtype)
    tail2 = _slab_load(ext2, q, SUBLANE, 0, d)
    _slab_store(ext2, 0, tail2)
    sct_ref[0] = tail2

    dt = _softplus(dtr[...] + dtb[...])
    a = dt * (-jnp.exp(alog[...]))
    row = lax.broadcasted_iota(jnp.int32, (q, q), 0)
    col = lax.broadcasted_iota(jnp.int32, (q, q), 1)
    causal = row >= col
    tri = causal.astype(BF16)
    cs = sum(jnp.dot(tri, part, preferred_element_type=F32) for part in _split3(a))
    cs_t = cs.T
    cs_last = cs[q - 1:q, :]
    dend = jnp.exp(cs_last - cs)
    ecs = jnp.exp(cs)
    stacked = jnp.concatenate([dt, dend, ecs], axis=0)
    st_b = stacked.astype(BF16)

    lane = lax.broadcasted_iota(jnp.int32, (q, LANE), 1)
    groups_per_block = d // GROUP_W

    for g in range(n_groups):
        c0 = g * GROUP_W
        xref = x4 if g < groups_per_block else x5
        bc = (g % groups_per_block) * GROUP_W
        xs = xref[:, bc:bc + GROUP_W]
        e_g = expand[:, c0:c0 + GROUP_W]
        ex = jnp.dot(st_b, e_g, preferred_element_type=F32)
        xdt = xs * ex[0:q]
        xdt_b = xdt.astype(BF16)
        xdd_b = (xdt * ex[q:2 * q]).astype(BF16)
        bg = x6[:, g * STATE_DIM:(g + 1) * STATE_DIM].astype(BF16)
        cg = x6[:, gn + g * STATE_DIM:gn + (g + 1) * STATE_DIM].astype(BF16)
        cbm = lax.dot_general(cg, bg, NT_DIMS, preferred_element_type=F32)
        state = st_ref[0, g]
        y_off = lax.dot_general(cg, state.astype(BF16), NT_DIMS, preferred_element_type=F32)
        y_parts = []
        for j in range(HEADS_PER_GROUP // 2):
            scores = []
            for h in (g * HEADS_PER_GROUP + 2 * j, g * HEADS_PER_GROUP + 2 * j + 1):
                seg = cs[:, h:h + 1] - cs_t[h:h + 1, :]
                dec = jnp.exp(jnp.where(causal, seg, -jnp.inf))
                scores.append((cbm * dec).astype(BF16))
            xp = xdt_b[:, j * LANE:(j + 1) * LANE]
            zero = jnp.zeros_like(xp)
            rhs = jnp.concatenate(
                [jnp.where(lane < HEAD_DIM, xp, zero), jnp.where(lane >= HEAD_DIM, xp, zero)], axis=0)
            y_parts.append(jnp.dot(jnp.concatenate(scores, axis=1), rhs, preferred_element_type=F32))
        y = jnp.concatenate(y_parts, axis=1) + y_off * ex[2 * q:3 * q] + xs * dskip[:, c0:c0 + GROUP_W]
        zref = z2 if g < groups_per_block else z3
        ya_ref[:, c0:c0 + GROUP_W] = _gated_norm(
            y, zref[:, bc:bc + GROUP_W], nw[:, c0:c0 + GROUP_W]).astype(ya_ref.dtype)
        st_ref[0, g] = state * _decay_col(cs_last, g) + lax.dot_general(
            xdd_b, bg, TN_DIMS, preferred_element_type=F32)


def _ssd_prompt(p, dt_raw, consts, n_seq, seq_len, d):
    (_, _, dtb, alog, dskip, nw, scw, expand) = consts
    q = SSD_CHUNK if seq_len % SSD_CHUNK == 0 else seq_len
    nc = seq_len // q
    n_groups = 2 * d // GROUP_W
    t_p = n_seq * seq_len
    assert 2 * n_groups * STATE_DIM == d

    def blk(cidx):
        return pl.BlockSpec((q, d), lambda b, c, cidx=cidx: (b * nc + c, cidx))

    def const(arr):
        return pl.BlockSpec(arr.shape, lambda b, c: (0,) * arr.ndim)

    return pl.pallas_call(
        _ssd_prompt_body,
        grid=(n_seq, nc),
        in_specs=[blk(4), blk(5), blk(6), blk(2), blk(3), blk(7), blk(8), blk(9),
                  pl.BlockSpec((q, LANE), lambda b, c: (b * nc + c, 0)),
                  const(dtb), const(alog), const(dskip), const(nw), const(scw), const(expand)],
        out_specs=[
            pl.BlockSpec((q, 2 * d), lambda b, c: (b * nc + c, 0)),
            pl.BlockSpec((q, d), lambda b, c: (b * nc + c, 0)),
            pl.BlockSpec((1, n_groups, GROUP_W, STATE_DIM), lambda b, c: (b, 0, 0, 0)),
            pl.BlockSpec((1, SUBLANE, d), lambda b, c: (b, 0, 0)),
        ],
        out_shape=[
            jax.ShapeDtypeStruct((t_p, 2 * d), BF16),
            jax.ShapeDtypeStruct((t_p, d), BF16),
            jax.ShapeDtypeStruct((n_seq, n_groups, GROUP_W, STATE_DIM), F32),
            jax.ShapeDtypeStruct((n_seq, SUBLANE, d), F32),
        ],
        scratch_shapes=[pltpu.VMEM((d // LANE, q + SUBLANE, LANE), F32)],
        compiler_params=pltpu.CompilerParams(
            dimension_semantics=("arbitrary", "arbitrary"), vmem_limit_bytes=VMEM_LIMIT),
        name="ssd_prompt",
    )(p, p, p, p, p, p, p, p, dt_raw, dtb, alog, dskip, nw, scw, expand)


SAMPLE_SEQS_PER_STEP = 2


SAMPLE_STATE_SLOTS = 3


def _ssd_sample_body(x4, x5, x6, z2, z3, scb, scc, sch, dtr, ssm_hbm, conv_in, sc_in,
                     cw, cb, dtb, alog, dskip, nw, scw, expand, gsel,
                     ya_ref, yb_ref, ssm_out, conv_out, sc_out, exts, ext2s, ssm_buf, ssm_sems):
    d = x4.shape[1]
    n_groups = ssm_out.shape[1]
    step = pl.program_id(0)
    n_steps = pl.num_programs(0)

    def fetch(t):
        slot = t % SAMPLE_STATE_SLOTS
        return pltpu.make_async_copy(
            ssm_hbm.at[pl.ds(t * SAMPLE_SEQS_PER_STEP, SAMPLE_SEQS_PER_STEP)], ssm_buf.at[slot], ssm_sems.at[slot])

    @pl.when(step == 0)
    def _():
        for t in range(SAMPLE_STATE_SLOTS - 1):
            @pl.when(t < n_steps)
            def _():
                fetch(t).start()

    @pl.when(step + SAMPLE_STATE_SLOTS - 1 < n_steps)
    def _():
        fetch(step + SAMPLE_STATE_SLOTS - 1).start()

    fetch(step).wait()
    ssm_in = ssm_buf.at[step % SAMPLE_STATE_SLOTS]
    gn = n_groups * STATE_DIM
    d_inner = n_groups * GROUP_W
    kw = cw.shape[0]
    kw2 = scw.shape[0]
    q = x4.shape[0] // SAMPLE_SEQS_PER_STEP
    groups_per_block = d // GROUP_W
    first = SUBLANE - (kw - 1)
    first2 = SUBLANE - (kw2 - 1)
    nrep = q * q

    rep_t = lax.broadcasted_iota(jnp.int32, (nrep, LANE), 0) % q
    rep_s = lax.broadcasted_iota(jnp.int32, (nrep, LANE), 0) // q
    rep_causal = rep_t >= rep_s
    row_q = lax.broadcasted_iota(jnp.int32, (q, LANE), 0)

    def rep_rows(m):
        return jnp.concatenate([jnp.broadcast_to(m[s:s + 1], (q, m.shape[1])) for s in range(q)], axis=0)

    def tile_rows(m):
        return jnp.concatenate([m] * q, axis=0)

    yb_rows = []
    ya_rows = [[] for _ in range(n_groups)]
    for sidx in range(SAMPLE_SEQS_PER_STEP):
        r0 = sidx * q
        ext, ext2 = exts.at[sidx], ext2s.at[sidx]
        _slab_store(ext, first, conv_in[sidx])
        _slab_store(ext, SUBLANE, x4[r0:r0 + q, :])
        _slab_store(ext, SUBLANE, x5[r0:r0 + q, :], d)
        _slab_store(ext, SUBLANE, x6[r0:r0 + q, :], 2 * d)
        conv_out[sidx] = _slab_load(ext, SUBLANE + q - (kw - 1), kw - 1, 0, 3 * d)

        _slab_store(ext2, first2, sc_in[sidx])
        _slab_store(ext2, SUBLANE, scc[r0:r0 + q, :] * sch[r0:r0 + q, :])
        yb_rows.append(scb[r0:r0 + q, :] * _conv(ext2, 0, d, q, kw2, scw, first2))
        sc_out[sidx] = _slab_load(ext2, SUBLANE + q - (kw2 - 1), kw2 - 1, 0, d)

        dt = _softplus(dtr[r0:r0 + q, :] + dtb[...])
        a = dt * (-jnp.exp(alog[...]))
        cs = jnp.zeros((q, LANE), F32)
        for r in range(q):
            cs = cs + jnp.where(row_q >= r, jnp.broadcast_to(a[r:r + 1], (q, LANE)), 0.0)
        cs_last = cs[q - 1:q, :]
        dend = jnp.exp(cs_last - cs)
        ecs = jnp.exp(cs)

        bmat = _silu(_conv(ext, 2 * d, 2 * d + gn, q, kw, cw, first) + cb[:, 2 * d:2 * d + gn])
        cmat = _silu(_conv(ext, 2 * d + gn, 3 * d, q, kw, cw, first) + cb[:, 2 * d + gn:3 * d])

        cb_hi, cb_lo = _split2(tile_rows(cmat) * rep_rows(bmat))
        cbh = (jnp.dot(cb_hi, gsel[...], preferred_element_type=F32)
               + jnp.dot(cb_lo, gsel[...], preferred_element_type=F32))
        dec = jnp.exp(jnp.where(rep_causal, tile_rows(cs) - rep_rows(cs), -jnp.inf))
        stacked = jnp.concatenate([dt, dend, ecs], axis=0)
        st_hi = stacked.astype(BF16).astype(F32)
        pad = jnp.zeros((LANE - nrep - 6 * q, LANE), F32)
        lhs = jnp.concatenate([cbh * dec, st_hi, stacked - st_hi, pad], axis=0).astype(BF16)

        xdd_parts = []
        for g in range(n_groups):
            c0 = g * GROUP_W
            xs = _silu(_conv(ext, c0, c0 + GROUP_W, q, kw, cw, first) + cb[:, c0:c0 + GROUP_W])
            ex = jnp.dot(lhs, expand[:, c0:c0 + GROUP_W], preferred_element_type=F32)
            o = nrep
            dtx = ex[o:o + q] + ex[o + 3 * q:o + 4 * q]
            dendx = ex[o + q:o + 2 * q] + ex[o + 4 * q:o + 5 * q]
            ecsx = ex[o + 2 * q:o + 3 * q] + ex[o + 5 * q:o + 6 * q]
            xdt = xs * dtx
            xdd = xdt * dendx
            y = xs * dskip[:, c0:c0 + GROUP_W]
            for s in range(q):
                y = y + ex[s * q:(s + 1) * q] * jnp.broadcast_to(xdt[s:s + 1], (q, GROUP_W))
            state = ssm_in[sidx, g]
            cg = cmat[:, g * STATE_DIM:(g + 1) * STATE_DIM]
            y = y + lax.dot_general(cg, state, NT_DIMS, preferred_element_type=F32) * ecsx
            zref = z2 if g < groups_per_block else z3
            zc = (g % groups_per_block) * GROUP_W
            ya_rows[g].append(_gated_norm(y, zref[r0:r0 + q, zc:zc + GROUP_W], nw[:, c0:c0 + GROUP_W]))
            xdd_parts.append(xdd)

        zrows = jnp.zeros((LANE - q, d_inner), F32)
        xdd_t = jnp.concatenate([jnp.concatenate(xdd_parts, axis=1), zrows], axis=0).T
        for g in range(n_groups):
            b_pad = jnp.concatenate(
                [bmat[:, g * STATE_DIM:(g + 1) * STATE_DIM], jnp.zeros((LANE - q, STATE_DIM), F32)], axis=0)
            ssm_out[sidx, g] = ssm_in[sidx, g] * _decay_col(cs_last, g) + jnp.dot(
                xdd_t[g * GROUP_W:(g + 1) * GROUP_W, :], b_pad, preferred_element_type=F32)

    yb_ref[...] = jnp.concatenate(yb_rows, axis=0).astype(yb_ref.dtype)
    for g in range(n_groups):
        ya_ref[:, g * GROUP_W:(g + 1) * GROUP_W] = jnp.concatenate(ya_rows[g], axis=0).astype(ya_ref.dtype)


def _ssd_sample(p, dt_raw, ssm, conv_state, sc_state, consts, gsel, row0, seq_len, d):
    (cw, cb, dtb, alog, dskip, nw, scw, expand) = consts
    n_seq = ssm.shape[0]
    sp = SAMPLE_SEQS_PER_STEP
    rows = sp * seq_len
    assert seq_len == SUBLANE and n_seq % sp == 0 and row0 % rows == 0
    b0 = row0 // rows

    def blk(cidx):
        return pl.BlockSpec((rows, d), lambda i, cidx=cidx: (b0 + i, cidx))

    def const(arr):
        return pl.BlockSpec(arr.shape, lambda i: (0,) * arr.ndim)

    def per_seq(arr):
        return pl.BlockSpec((sp,) + arr.shape[1:], lambda i: (i,) + (0,) * (arr.ndim - 1))

    return pl.pallas_call(
        _ssd_sample_body,
        grid=(n_seq // sp,),
        in_specs=[blk(4), blk(5), blk(6), blk(2), blk(3), blk(7), blk(8), blk(9),
                  pl.BlockSpec((rows, LANE), lambda i: (b0 + i, 0)),
                  pl.BlockSpec(memory_space=pl.ANY), per_seq(conv_state), per_seq(sc_state),
                  const(cw), const(cb), const(dtb), const(alog), const(dskip), const(nw), const(scw),
                  const(expand), const(gsel)],
        out_specs=[
            pl.BlockSpec((rows, 2 * d), lambda i: (i, 0)),
            pl.BlockSpec((rows, d), lambda i: (i, 0)),
            per_seq(ssm), per_seq(conv_state), per_seq(sc_state),
        ],
        out_shape=[
            jax.ShapeDtypeStruct((n_seq * seq_len, 2 * d), BF16),
            jax.ShapeDtypeStruct((n_seq * seq_len, d), BF16),
            jax.ShapeDtypeStruct(ssm.shape, F32),
            jax.ShapeDtypeStruct(conv_state.shape, F32),
            jax.ShapeDtypeStruct(sc_state.shape, F32),
        ],
        scratch_shapes=[pltpu.VMEM((sp, 3 * d // LANE, 2 * SUBLANE, LANE), F32),
                        pltpu.VMEM((sp, d // LANE, 2 * SUBLANE, LANE), F32),
                        pltpu.VMEM((SAMPLE_STATE_SLOTS, sp) + ssm.shape[1:], F32),
                        pltpu.SemaphoreType.DMA((SAMPLE_STATE_SLOTS,))],
        compiler_params=pltpu.CompilerParams(
            dimension_semantics=("arbitrary",), vmem_limit_bytes=VMEM_LIMIT),
        name="ssd_sample",
    )(p, p, p, p, p, p, p, p, dt_raw, ssm, conv_state, sc_state,
      cw, cb, dtb, alog, dskip, nw, scw, expand, gsel)


def _branch_out_body(yap_ref, ybp_ref, yas_ref, ybs_ref, ga_ref, gb_ref, wa_ref, wb_ref, o_ref,
                     wa_scr, wb_scr, *, prompt_tiles):
    i = pl.program_id(1)

    @pl.when(i == 0)
    def _():
        wa_scr[...] = wa_ref[...].astype(BF16)
        wb_scr[...] = wb_ref[...].astype(BF16)

    def run(refs):
        ya_ref, yb_ref = refs
        pa = jnp.dot(ya_ref[...], wa_scr[...], preferred_element_type=F32)
        pb = jnp.dot(yb_ref[...], wb_scr[...], preferred_element_type=F32)
        merged = jax.nn.sigmoid(ga_ref[...]) * pa + jax.nn.sigmoid(gb_ref[...]) * pb
        o_ref[...] = merged.astype(o_ref.dtype)

    _by_stream(i, prompt_tiles, run, (yap_ref, ybp_ref), (yas_ref, ybs_ref))


def _branch_out(ya_p, yb_p, ya_s, yb_s, p, w_branch_out):
    t_p, d = yb_p.shape
    t_s = yb_s.shape[0]
    tm = _tile(math.gcd(t_p, t_s), 512, 16)
    tn = _tile(d, 512, LANE)
    nj = d // tn
    npt = t_p // tm
    prompt = lambda j, i: (jnp.minimum(i, npt - 1), 0)
    sample = lambda j, i: (jnp.maximum(i - npt, 0), 0)
    once = dict(pipeline_mode=pl.Buffered(1))
    return pl.pallas_call(
        functools.partial(_branch_out_body, prompt_tiles=npt),
        grid=(nj, (t_p + t_s) // tm),
        in_specs=[
            pl.BlockSpec((tm, 2 * d), prompt), pl.BlockSpec((tm, d), prompt),
            pl.BlockSpec((tm, 2 * d), sample), pl.BlockSpec((tm, d), sample),
            pl.BlockSpec((tm, tn), lambda j, i: (i, j)),
            pl.BlockSpec((tm, tn), lambda j, i: (i, nj + j)),
            pl.BlockSpec((None, 2 * d, tn), lambda j, i: (0, 0, j), **once),
            pl.BlockSpec((None, d, tn), lambda j, i: (0, 2, j), **once),
        ],
        out_specs=pl.BlockSpec((tm, tn), lambda j, i: (i, j)),
        out_shape=jax.ShapeDtypeStruct((t_p + t_s, d), BF16),
        scratch_shapes=[pltpu.VMEM((2 * d, tn), BF16), pltpu.VMEM((d, tn), BF16)],
        compiler_params=pltpu.CompilerParams(
            dimension_semantics=("arbitrary", "arbitrary"), vmem_limit_bytes=VMEM_LIMIT),
        name="branch_out",
    )(ya_p, yb_p, ya_s, yb_s, p, p, w_branch_out, w_branch_out)


def _mix_route_body(m_ref, xp_ref, xs_ref, wo_ref, nw_ref, rhi_ref, rlo_ref,
                    x1_ref, h2_ref, route_ref, wts_ref, cnt_ref, wo_scr, *, n_coarse, per_group, prompt_tiles):
    @pl.when(pl.program_id(0) == 0)
    def _():
        cnt_ref[...] = jnp.zeros(cnt_ref.shape, F32)
        wo_scr[...] = wo_ref[...].astype(BF16)

    def run(x_ref):
        x1 = x_ref[...] + jnp.dot(m_ref[...], wo_scr[...], preferred_element_type=F32)
        x1_ref[...] = x1
        h2 = x1 * lax.rsqrt(jnp.mean(x1 * x1, axis=-1, keepdims=True) + NORM_EPS) * nw_ref[...]
        h2_ref[...] = h2
        h_hi, h_lo = _split2(h2)
        logits = (jnp.dot(h_hi, rhi_ref[...], preferred_element_type=F32)
                  + jnp.dot(h_hi, rlo_ref[...], preferred_element_type=F32)
                  + jnp.dot(h_lo, rhi_ref[...], preferred_element_type=F32))

        tm = logits.shape[0]
        n_fine = n_coarse * per_group
        lane = lax.broadcasted_iota(jnp.int32, logits.shape, 1)
        big = jnp.int32(LANE)
        neg = -jnp.inf
        is_c = lane < n_coarse
        lc = jnp.where(is_c, logits, neg)
        mc = jnp.max(lc, axis=-1, keepdims=True)
        grp = jnp.min(jnp.where(is_c & (lc == mc), lane, big), axis=-1, keepdims=True)
        p_grp = 1.0 / jnp.sum(jnp.where(is_c, jnp.exp(lc - mc), 0.0), axis=-1, keepdims=True)
        eidx = lane - n_coarse
        sel = (eidx >= 0) & (eidx < n_fine) & ((eidx // per_group) == grp)
        lf = jnp.where(sel, logits, neg)
        v1 = jnp.max(lf, axis=-1, keepdims=True)
        i1 = jnp.min(jnp.where(sel & (lf == v1), eidx, big), axis=-1, keepdims=True)
        sel2 = sel & (eidx != i1)
        lf2 = jnp.where(sel2, logits, neg)
        v2 = jnp.max(lf2, axis=-1, keepdims=True)
        i2 = jnp.min(jnp.where(sel2 & (lf2 == v2), eidx, big), axis=-1, keepdims=True)
        e2 = jnp.exp(v2 - v1)
        w1 = p_grp / (1.0 + e2)
        w2 = p_grp * e2 / (1.0 + e2)
        wts_ref[...] = jnp.where(lane == 0, w1, jnp.where(lane == 1, w2, 0.0))

        hit1 = eidx == i1
        hit2 = eidx == i2
        hits = hit1.astype(F32) + hit2.astype(F32)
        earlier = (lax.broadcasted_iota(jnp.int32, (tm, tm), 0)
                   > lax.broadcasted_iota(jnp.int32, (tm, tm), 1)).astype(BF16)
        before = jnp.dot(earlier, hits.astype(BF16), preferred_element_type=F32) + cnt_ref[...]
        r1 = jnp.sum(jnp.where(hit1, before, 0.0), axis=-1, keepdims=True).astype(jnp.int32)
        r2 = jnp.sum(jnp.where(hit2, before, 0.0), axis=-1, keepdims=True).astype(jnp.int32)
        cnt_ref[...] += jnp.sum(hits, axis=0, keepdims=True)
        route_ref[...] = jnp.where(lane == 0, i1, jnp.where(lane == 1, i2, jnp.where(
            lane == 2, r1, jnp.where(lane == 3, r2, 0))))

    _by_stream(pl.program_id(0), prompt_tiles, run, xp_ref, xs_ref)


def _mix_route(merged, x_p, x_s, wo, norm_w, r_hi, r_lo, n_coarse, per_group):
    t_p, d = x_p.shape
    T = t_p + x_s.shape[0]
    tm = _tile(math.gcd(t_p, x_s.shape[0]), 256, 16)
    npt = t_p // tm
    row = lambda i: (i, 0)
    fixed = lambda i: (0, 0)
    xp_spec, xs_spec = _stream_specs((tm, d), npt)
    resident = dict(pipeline_mode=pl.Buffered(1))
    return pl.pallas_call(
        functools.partial(_mix_route_body, n_coarse=n_coarse, per_group=per_group, prompt_tiles=npt),
        grid=(T // tm,),
        in_specs=[
            pl.BlockSpec((tm, d), row), xp_spec, xs_spec, pl.BlockSpec((None, d, d), lambda i: (0, 0, 0), **resident),
            pl.BlockSpec((1, d), fixed), pl.BlockSpec((d, LANE), fixed, **resident),
            pl.BlockSpec((d, LANE), fixed, **resident),
        ],
        out_specs=[pl.BlockSpec((tm, d), row), pl.BlockSpec((tm, d), row),
                   pl.BlockSpec((tm, LANE), row), pl.BlockSpec((tm, LANE), row),
                   pl.BlockSpec((1, LANE), fixed)],
        out_shape=[
            jax.ShapeDtypeStruct((T, d), F32), jax.ShapeDtypeStruct((T, d), F32),
            jax.ShapeDtypeStruct((T, LANE), jnp.int32), jax.ShapeDtypeStruct((T, LANE), F32),
            jax.ShapeDtypeStruct((1, LANE), F32),
        ],
        scratch_shapes=[pltpu.VMEM((d, d), BF16)],
        compiler_params=pltpu.CompilerParams(
            dimension_semantics=("arbitrary",), vmem_limit_bytes=VMEM_LIMIT),
        name="mix_route",
    )(merged, x_p, x_s, wo, norm_w, r_hi, r_lo)


MOE_MOVE_TOKENS = 128
MOE_MOVE_UNROLL = 8


def _moe_scatter_body(dest_ref, fill_ref, h_ref, xs_hbm, zeros, sem, zsem):
    tt = h_ref.shape[0]
    t0 = pl.program_id(0) * tt
    n_experts = fill_ref.shape[0] - 1
    n_blocks = xs_hbm.shape[0] // MOE_ROWS

    @pl.when(pl.program_id(0) == 0)
    def _():
        zeros[...] = jnp.zeros(zeros.shape, zeros.dtype)

        def fill(row):
            return pltpu.make_async_copy(zeros, xs_hbm.at[pl.ds(pl.multiple_of(row, MOE_ROWS), MOE_ROWS)], zsem)

        def expert_fill(action):
            def body(e, carry):
                @pl.when(fill_ref[e] >= 0)
                def _():
                    action(fill(fill_ref[e]))
                return carry
            lax.fori_loop(0, n_experts, body, 0)

        def tail_fill(action):
            def body(blk, carry):
                action(fill(blk * MOE_ROWS))
                return carry
            lax.fori_loop(fill_ref[n_experts], n_blocks, body, 0)

        expert_fill(lambda c: c.start())
        tail_fill(lambda c: c.start())
        expert_fill(lambda c: c.wait())
        tail_fill(lambda c: c.wait())

    def start(i, carry):
        for k in range(MOE_TOP_K):
            pltpu.make_async_copy(
                h_ref.at[pl.ds(i, 1)], xs_hbm.at[pl.ds(dest_ref[(t0 + i) * MOE_TOP_K + k], 1)], sem
            ).start(priority=k % 2)
        return carry

    lax.fori_loop(0, tt, start, 0, unroll=MOE_MOVE_UNROLL)
    for _ in range(MOE_TOP_K):
        pltpu.make_async_copy(h_ref, xs_hbm.at[pl.ds(0, tt)], sem).wait()


def _moe_scatter(dest, fill_rows, h2, n_rows):
    T, d = h2.shape
    tt = MOE_MOVE_TOKENS
    assert T % tt == 0
    return pl.pallas_call(
        _moe_scatter_body,
        grid_spec=pltpu.PrefetchScalarGridSpec(
            num_scalar_prefetch=2,
            grid=(T // tt,),
            in_specs=[pl.BlockSpec((tt, d), lambda i, dr, fr: (i, 0))],
            out_specs=pl.BlockSpec(memory_space=pl.ANY),
            scratch_shapes=[pltpu.VMEM((MOE_ROWS, d), h2.dtype), pltpu.SemaphoreType.DMA(()),
                            pltpu.SemaphoreType.DMA(())],
        ),
        out_shape=jax.ShapeDtypeStruct((n_rows, d), h2.dtype),
        compiler_params=pltpu.CompilerParams(dimension_semantics=("arbitrary",)),
        name="moe_scatter",
    )(dest, fill_rows, h2)


MOE_WEIGHT_PIECES = 8
MOE_WEIGHT_RING = 3


def _moe_ffn_body(sched_ref, order_ref, n_used_ref, x_ref, wg_hbm, wu_hbm, wd_hbm, o_ref,
                  wg_b, wu_b, wd_b, sg, su, sd, sems):
    b = pl.program_id(0)
    hbm = (wg_hbm, wu_hbm, wd_hbm)
    stage = (sg, su, sd)
    resident = (wg_b, wu_b, wd_b)
    n_total = order_ref[order_ref.shape[0] - 1]

    def copies(c):
        e = order_ref[c // MOE_WEIGHT_PIECES]
        piece = c % MOE_WEIGHT_PIECES
        ring = c % MOE_WEIGHT_RING
        out = []
        for k in range(3):
            rows = stage[k].shape[1]
            out.append(pltpu.make_async_copy(
                hbm[k].at[0, e, pl.ds(pl.multiple_of(piece * rows, rows), rows), :],
                stage[k].at[ring], sems.at[ring, k]))
        return out

    def start(c):
        @pl.when(c < n_total)
        def _():
            for cp in copies(c):
                cp.start()

    def land(lo, hi):
        def body(c, carry):
            slot = (c // MOE_WEIGHT_PIECES) % 2
            piece = c % MOE_WEIGHT_PIECES
            ring = c % MOE_WEIGHT_RING
            for k, cp in enumerate(copies(c)):
                cp.wait()
                rows = stage[k].shape[1]
                resident[k][slot, pl.ds(pl.multiple_of(piece * rows, rows), rows), :] = stage[k][ring].astype(BF16)
            start(c + MOE_WEIGHT_RING)
            return carry

        lax.fori_loop(lo, hi, body, 0)

    @pl.when(b < n_used_ref[0])
    def _():
        @pl.when(b == 0)
        def _():
            for c in range(MOE_WEIGHT_RING):
                start(c)
            land(0, MOE_WEIGHT_PIECES)

        slot = sched_ref[0, b]
        xb = x_ref[...].astype(BF16)
        gate = jnp.dot(xb, wg_b[slot], preferred_element_type=F32)
        up = jnp.dot(xb, wu_b[slot], preferred_element_type=F32)
        act = (_silu(gate) * up).astype(BF16)
        o_ref[...] = jnp.dot(act, wd_b[slot], preferred_element_type=F32)
        land(sched_ref[1, b], sched_ref[2, b])

    @pl.when(b >= n_used_ref[0])
    def _():
        o_ref[...] = jnp.zeros(o_ref.shape, o_ref.dtype)


def _moe_ffn(sched, order, n_used, xs, wg, wu, wd):
    R, d = xs.shape
    f = wg.shape[3]
    nb = R // MOE_ROWS
    np_ = MOE_WEIGHT_PIECES
    assert d % np_ == 0 and f % np_ == 0
    any_spec = pl.BlockSpec(memory_space=pl.ANY)
    return pl.pallas_call(
        _moe_ffn_body,
        grid_spec=pltpu.PrefetchScalarGridSpec(
            num_scalar_prefetch=3,
            grid=(nb,),
            in_specs=[pl.BlockSpec((MOE_ROWS, d), lambda b, sc, od, nu: (b, 0)), any_spec, any_spec, any_spec],
            out_specs=pl.BlockSpec((MOE_ROWS, d), lambda b, sc, od, nu: (b, 0)),
            scratch_shapes=[
                pltpu.VMEM((2, d, f), BF16), pltpu.VMEM((2, d, f), BF16), pltpu.VMEM((2, f, d), BF16),
                pltpu.VMEM((MOE_WEIGHT_RING, d // np_, f), F32), pltpu.VMEM((MOE_WEIGHT_RING, d // np_, f), F32),
                pltpu.VMEM((MOE_WEIGHT_RING, f // np_, d), F32),
                pltpu.SemaphoreType.DMA((MOE_WEIGHT_RING, 3)),
            ],
        ),
        out_shape=jax.ShapeDtypeStruct((R, d), F32),
        compiler_params=pltpu.CompilerParams(
            dimension_semantics=("arbitrary",), vmem_limit_bytes=VMEM_LIMIT),
        name="moe_ffn",
    )(sched, order, n_used, xs, wg, wu, wd)


def _moe_combine_body(dest_ref, x1_ref, wts_ref, nw_ref, yb_hbm, op_ref, os_ref, buf, sems, *, prompt_tiles):
    tt = x1_ref.shape[0]
    i = pl.program_id(0)

    def gather(tile, slot):
        def start(r, carry):
            for k in range(MOE_TOP_K):
                pltpu.make_async_copy(
                    yb_hbm.at[pl.ds(dest_ref[(tile * tt + r) * MOE_TOP_K + k], 1)],
                    buf.at[slot, k, pl.ds(r, 1)], sems.at[slot]).start(priority=k % 2)
            return carry

        lax.fori_loop(0, tt, start, 0, unroll=MOE_MOVE_UNROLL)

    @pl.when(i == 0)
    def _():
        gather(0, 0)

    @pl.when(i + 1 < pl.num_programs(0))
    def _():
        gather(i + 1, (i + 1) % 2)

    slot = i % 2
    for k in range(MOE_TOP_K):
        pltpu.make_async_copy(yb_hbm.at[pl.ds(0, tt)], buf.at[slot, k], sems.at[slot]).wait()

    def finish(o_ref):
        w = wts_ref[...]
        x2 = x1_ref[...] + (buf[slot, 0] * w[:, 0:1] + buf[slot, 1] * w[:, 1:2])
        o_ref[...] = x2 * lax.rsqrt(jnp.mean(x2 * x2, axis=-1, keepdims=True) + NORM_EPS) * nw_ref[...]

    _by_stream(i, prompt_tiles, finish, op_ref, os_ref)


def _moe_combine(dest, x1, wts, norm_w, yb, t_p):
    T, d = x1.shape
    tt = MOE_MOVE_TOKENS
    assert t_p % tt == 0 and T % tt == 0
    npt = t_p // tt
    op_spec, os_spec = _stream_specs((tt, d), npt)
    return pl.pallas_call(
        functools.partial(_moe_combine_body, prompt_tiles=npt),
        grid_spec=pltpu.PrefetchScalarGridSpec(
            num_scalar_prefetch=1,
            grid=(T // tt,),
            in_specs=[
                pl.BlockSpec((tt, d), lambda i, dr: (i, 0)),
                pl.BlockSpec((tt, LANE), lambda i, dr: (i, 0)),
                pl.BlockSpec((1, d), lambda i, dr: (0, 0)),
                pl.BlockSpec(memory_space=pl.ANY),
            ],
            out_specs=[op_spec, os_spec],
            scratch_shapes=[pltpu.VMEM((2, MOE_TOP_K, tt, d), F32), pltpu.SemaphoreType.DMA((2,))],
        ),
        out_shape=[jax.ShapeDtypeStruct((t_p, d), F32), jax.ShapeDtypeStruct((T - t_p, d), F32)],
        compiler_params=pltpu.CompilerParams(dimension_semantics=("arbitrary",)),
        name="moe_combine",
    )(dest, x1, wts, norm_w, yb)


def _route_rows(eid, rank, counts, n_assign):
    n_experts = counts.shape[0]
    padded = (counts + MOE_ROWS - 1) // MOE_ROWS * MOE_ROWS
    pend = jnp.cumsum(padded)
    pstart = pend - padded
    experts = jnp.arange(n_experts, dtype=jnp.int32)
    onehot = eid[:, :, None] == experts[None, None, :]
    dest = (jnp.sum(jnp.where(onehot, pstart[None, None, :], 0), axis=-1) + rank).astype(jnp.int32)
    n_blocks = -(-n_assign // MOE_ROWS) + n_experts
    blk_start = jnp.arange(n_blocks, dtype=jnp.int32) * MOE_ROWS
    used = blk_start < pend[-1]
    blk_e = jnp.minimum(jnp.sum((blk_start[:, None] >= pend[None, :]).astype(jnp.int32), axis=1), n_experts - 1)
    n_used = (pend[-1] // MOE_ROWS).astype(jnp.int32).reshape(1)

    nonempty = counts > 0
    ordinal = jnp.cumsum(nonempty.astype(jnp.int32)) - 1
    n_nonempty = jnp.sum(nonempty.astype(jnp.int32))
    order = jnp.argsort(jnp.where(nonempty, experts, n_experts + experts)).astype(jnp.int32)
    n_slices = MOE_WEIGHT_PIECES * n_nonempty
    pos = blk_start // MOE_ROWS - (pstart // MOE_ROWS)[blk_e]
    nblk = jnp.maximum((padded // MOE_ROWS)[blk_e], 1)
    has_next = used & (ordinal[blk_e] + 1 < n_nonempty)
    base = MOE_WEIGHT_PIECES * (ordinal[blk_e] + 1)
    first = jnp.where(has_next, base + MOE_WEIGHT_PIECES * pos // nblk, n_slices)
    last = jnp.where(has_next, base + MOE_WEIGHT_PIECES * (pos + 1) // nblk, n_slices)
    slot = jnp.where(used, ordinal[blk_e] % 2, 0)
    sched = jnp.stack([slot, first, last]).astype(jnp.int32)
    order = jnp.concatenate([order, n_slices.reshape(1)]).astype(jnp.int32)
    fill_rows = jnp.concatenate([jnp.where(nonempty, pend - MOE_ROWS, -1), n_used]).astype(jnp.int32)
    return dest.reshape(-1), sched, order, fill_rows, n_used, n_blocks


def _pad_lanes(v, fill=0.0):
    return jnp.pad(v.astype(F32), (0, LANE - v.shape[0]), constant_values=fill).reshape(1, LANE)


def kernel(x_prompt, x_sample, state_ssm, state_ssd_conv, state_short_conv, norm_mixer, w_in, ssd_conv_w,
           ssd_conv_b, ssd_dt_bias, ssd_a_log, ssd_d, ssd_norm, sc_conv_w, w_branch_out, w_out, norm_ffn,
           w_router_coarse, w_router_fine, w_expert_gate, w_expert_up, w_expert_down, norm_final):
    depth = w_in.shape[0]
    assert depth == 1
    n_p, seq_p, d = x_prompt.shape
    n_s, seq_s, _ = x_sample.shape
    d_inner = 2 * d
    n_heads = d_inner // HEAD_DIM
    n_groups = d_inner // GROUP_W
    gn = n_groups * STATE_DIM
    conv_dim = d_inner + 2 * gn
    assert conv_dim == 3 * d and n_heads <= LANE and ssd_conv_w.shape[2] == conv_dim
    t_p, t_s = n_p * seq_p, n_s * seq_s
    n_coarse = w_router_coarse.shape[2]
    n_experts = w_router_fine.shape[2]
    assert n_coarse + n_experts <= LANE

    off_dt = 2 * d + d_inner + conv_dim
    off_sc = off_dt + n_heads
    w_in_t = jnp.swapaxes(w_in, 1, 2)
    w_dt = jnp.pad(w_in_t[0, off_dt:off_sc, :].T, ((0, 0), (0, LANE - n_heads))).astype(BF16)
    head_of_col = jnp.arange(d_inner, dtype=jnp.int32) // HEAD_DIM
    expand = (jnp.arange(LANE, dtype=jnp.int32)[:, None] == head_of_col[None, :]).astype(BF16)
    group_of_n = jnp.arange(gn, dtype=jnp.int32) // STATE_DIM
    group_of_head = jnp.arange(LANE, dtype=jnp.int32) // HEADS_PER_GROUP
    gsel = ((group_of_n[:, None] == group_of_head[None, :])
            & (jnp.arange(LANE)[None, :] < n_heads)).astype(BF16)
    consts = (
        ssd_conv_w[0], ssd_conv_b[0].reshape(1, conv_dim), _pad_lanes(ssd_dt_bias[0]), _pad_lanes(ssd_a_log[0]),
        jnp.repeat(ssd_d[0].astype(F32), HEAD_DIM).reshape(1, d_inner), ssd_norm[0].reshape(1, d_inner),
        sc_conv_w[0], expand,
    )
    w_router = jnp.pad(jnp.concatenate([w_router_coarse[0], w_router_fine[0]], axis=1),
                       ((0, 0), (0, LANE - n_coarse - n_experts)))
    r_hi = w_router.astype(BF16)
    r_lo = (w_router - r_hi.astype(F32)).astype(BF16)

    x_p = x_prompt.reshape(t_p, d)
    x_s = x_sample.reshape(t_s, d)
    h, dt_raw = _prenorm(x_p, x_s, norm_mixer[0].reshape(1, d), w_dt)
    p, tails = _inproj(h, w_in_t[0], off_dt, off_sc, consts[0], consts[1], t_p, seq_p)
    ya_p, yb_p, p_ssm, p_sc_tail = _ssd_prompt(p, dt_raw, consts, n_p, seq_p, d)
    ya_s, yb_s, s_ssm, s_conv, s_sc = _ssd_sample(
        p, dt_raw, state_ssm[0].reshape(n_s, n_groups, GROUP_W, STATE_DIM), state_ssd_conv[0],
        state_short_conv[0], consts, gsel, t_p, seq_s, d)
    merged = _branch_out(ya_p, yb_p, ya_s, yb_s, p, w_branch_out)
    x1, h2, route, wts, counts = _mix_route(merged, x_p, x_s, w_out, norm_ffn[0].reshape(1, d), r_hi, r_lo,
                                            n_coarse, n_experts // n_coarse)

    n_assign = (t_p + t_s) * MOE_TOP_K
    dest, sched, order, fill_rows, n_used, n_blocks = _route_rows(
        route[:, 0:MOE_TOP_K], route[:, MOE_TOP_K:2 * MOE_TOP_K],
        counts[0, n_coarse:n_coarse + n_experts].astype(jnp.int32), n_assign)
    xs = _moe_scatter(dest, fill_rows, h2, n_blocks * MOE_ROWS)
    yrows = _moe_ffn(sched, order, n_used, xs, w_expert_gate, w_expert_up, w_expert_down)
    out_p, out_s = _moe_combine(dest, x1, wts, norm_final.reshape(1, d), yrows, t_p)

    kw = ssd_conv_w.shape[1]
    kw2 = sc_conv_w.shape[1]
    tiles_per_seq = (tails.shape[0] // SUBLANE) * seq_p // (t_p + t_s)
    seq_tails = tails.reshape(-1, SUBLANE, tails.shape[1])[tiles_per_seq - 1:n_p * tiles_per_seq:tiles_per_seq]
    p_conv = seq_tails[:, SUBLANE - (kw - 1):, 4 * d:7 * d]
    return (
        out_p.reshape(n_p, seq_p, d),
        out_s.reshape(n_s, seq_s, d),
        p_ssm.reshape(1, n_p, n_heads, HEAD_DIM, STATE_DIM),
        p_conv[None],
        p_sc_tail[:, SUBLANE - (kw2 - 1):, :][None],
        s_ssm.reshape(1, n_s, n_heads, HEAD_DIM, STATE_DIM),
        s_conv[None],
        s_sc[None],
    )
```

```python
import functools
import math

import jax
import jax.numpy as jnp
from jax import lax
from jax.experimental import pallas as pl
from jax.experimental.pallas import tpu as pltpu

F32 = jnp.float32
BF16 = jnp.bfloat16

NORM_EPS = 1e-6
SSD_NORM_EPS = 1e-5
HEAD_DIM = 64
STATE_DIM = 128
HEADS_PER_GROUP = 8
GROUP_W = HEADS_PER_GROUP * HEAD_DIM
SSD_CHUNK = 128
MOE_TOP_K = 2
MOE_ROWS = 256

LANE = 128
SUBLANE = 8
VMEM_LIMIT = 56 * 1024 * 1024

NT_DIMS = (((1,), (1,)), ((), ()))
TN_DIMS = (((0,), (0,)), ((), ()))


def _tile(n, target, align):
    best = None
    for t in range(align, min(n, target) + 1, align):
        if n % t == 0:
            best = t
    assert best is not None, (n, target, align)
    return best


def _split2(v):
    hi = v.astype(BF16)
    lo = (v - hi.astype(F32)).astype(BF16)
    return hi, lo


def _split3(v):
    hi = v.astype(BF16)
    r = v - hi.astype(F32)
    mid = r.astype(BF16)
    lo = (r - mid.astype(F32)).astype(BF16)
    return hi, mid, lo


def _softplus(x):
    return jnp.maximum(x, 0.0) + jnp.log1p(jnp.exp(-jnp.abs(x)))


def _silu(x):
    return x * jax.nn.sigmoid(x)


def _by_stream(tile, prompt_tiles, fn, prompt_ref, sample_ref):
    @pl.when(tile < prompt_tiles)
    def _():
        fn(prompt_ref)

    @pl.when(tile >= prompt_tiles)
    def _():
        fn(sample_ref)


def _stream_specs(block, prompt_tiles, **kwargs):
    pad = (0,) * (len(block) - 1)
    prompt = lambda i, *_: (jnp.minimum(i, prompt_tiles - 1),) + pad
    sample = lambda i, *_: (jnp.maximum(i - prompt_tiles, 0),) + pad
    return pl.BlockSpec(block, prompt, **kwargs), pl.BlockSpec(block, sample, **kwargs)


def _prenorm_body(xp_ref, xs_ref, nw_ref, wdt_ref, h_ref, dt_ref, *, prompt_tiles):
    def run(x_ref):
        x = x_ref[...]
        h = x * lax.rsqrt(jnp.mean(x * x, axis=-1, keepdims=True) + NORM_EPS) * nw_ref[...]
        hb = h.astype(BF16)
        h_ref[...] = hb
        dt_ref[...] = jnp.dot(hb, wdt_ref[...], preferred_element_type=F32)

    _by_stream(pl.program_id(0), prompt_tiles, run, xp_ref, xs_ref)


def _prenorm(x_p, x_s, norm_w, w_dt):
    t_p, D = x_p.shape
    T = t_p + x_s.shape[0]
    tm = _tile(math.gcd(t_p, x_s.shape[0]), 512, 16)
    npt = t_p // tm
    xp_spec, xs_spec = _stream_specs((tm, D), npt)
    return pl.pallas_call(
        functools.partial(_prenorm_body, prompt_tiles=npt),
        grid=(T // tm,),
        in_specs=[xp_spec, xs_spec, pl.BlockSpec((1, D), lambda i: (0, 0)),
                  pl.BlockSpec((D, LANE), lambda i: (0, 0))],
        out_specs=[pl.BlockSpec((tm, D), lambda i: (i, 0)), pl.BlockSpec((tm, LANE), lambda i: (i, 0))],
        out_shape=[jax.ShapeDtypeStruct((T, D), BF16), jax.ShapeDtypeStruct((T, LANE), F32)],
        compiler_params=pltpu.CompilerParams(
            dimension_semantics=("arbitrary",), vmem_limit_bytes=VMEM_LIMIT),
        name="prenorm",
    )(x_p, x_s, norm_w, w_dt)


INPROJ_CONV_SLABS = 4


def _inproj_body(h_ref, w_ref, cw_ref, cb_ref, p_ref, tail_ref, w_scr, cext,
                 *, blocks_per_d, prompt_tiles, tiles_per_seq):
    j = pl.program_id(0)
    i = pl.program_id(1)
    tm, tn = p_ref.shape
    kw = cw_ref.shape[0]

    @pl.when(i == 0)
    def _():
        w_scr[...] = w_ref[...].astype(BF16)

    is_conv = (j >= 4 * blocks_per_d) & (j < 7 * blocks_per_d) & (i < prompt_tiles)

    @pl.when(jnp.logical_not(is_conv))
    def _():
        raw = lax.dot_general(h_ref[...], w_scr[...], NT_DIMS, preferred_element_type=F32)
        p_ref[...] = raw
        tail_ref[...] = raw[tm - SUBLANE:tm, :]

    @pl.when(is_conv)
    def _():
        @pl.when(i % tiles_per_seq == 0)
        def _():
            cext[:, 0:SUBLANE, :] = jnp.zeros((cext.shape[0], SUBLANE, LANE), F32)

        ws = tn // INPROJ_CONV_SLABS
        first = SUBLANE - (kw - 1)
        for k in range(INPROJ_CONV_SLABS):
            raw = lax.dot_general(h_ref[...], w_scr[k * ws:(k + 1) * ws, :], NT_DIMS, preferred_element_type=F32)
            for s in range(k * ws // LANE, (k + 1) * ws // LANE):
                lo, hi = s * LANE, (s + 1) * LANE
                cext[s, SUBLANE:SUBLANE + tm, :] = raw[:, lo - k * ws:hi - k * ws]
                acc = cb_ref[:, lo:hi]
                for t in range(kw):
                    acc = acc + cw_ref[t:t + 1, lo:hi] * cext[s, first + t:first + t + tm, :]
                p_ref[:, lo:hi] = _silu(acc)
                tail = cext[s, tm:tm + SUBLANE, :]
                tail_ref[:, lo:hi] = tail
                cext[s, 0:SUBLANE, :] = tail


def _inproj(h, w_in_t, n_head_cols, tail_start, cw, cb, t_p, seq_p):
    T, D = h.shape
    n_tail = w_in_t.shape[0] - tail_start
    tm = _tile(math.gcd(seq_p, T - t_p), 1024, 16)
    tn = _tile(math.gcd(D, n_tail), 1024, LANE)
    main_tiles = n_head_cols // tn
    n_out = n_head_cols + n_tail
    bpd = D // tn
    assert tail_start % SUBLANE == 0 and n_head_cols % tn == 0
    conv_block = lambda j, i: (0, jnp.clip(j - 4 * bpd, 0, 3 * bpd - 1))
    w_row = lambda j, i: (pl.multiple_of(
        jnp.where(j < main_tiles, j * tn, tail_start + (j - main_tiles) * tn), SUBLANE), 0)
    return pl.pallas_call(
        functools.partial(_inproj_body, blocks_per_d=bpd, prompt_tiles=t_p // tm, tiles_per_seq=seq_p // tm),
        grid=(n_out // tn, T // tm),
        in_specs=[
            pl.BlockSpec((tm, D), lambda j, i: (i, 0)),
            pl.BlockSpec((pl.Element(tn), pl.Element(D)), w_row),
            pl.BlockSpec((cw.shape[0], tn), conv_block),
            pl.BlockSpec((1, tn), conv_block),
        ],
        out_specs=[pl.BlockSpec((tm, tn), lambda j, i: (i, j)),
                   pl.BlockSpec((SUBLANE, tn), lambda j, i: (i, j))],
        out_shape=[jax.ShapeDtypeStruct((T, n_out), F32),
                   jax.ShapeDtypeStruct((T // tm * SUBLANE, n_out), F32)],
        scratch_shapes=[pltpu.VMEM((tn, D), BF16), pltpu.VMEM((tn // LANE, tm + SUBLANE, LANE), F32)],
        compiler_params=pltpu.CompilerParams(
            dimension_semantics=("arbitrary", "arbitrary"), vmem_limit_bytes=VMEM_LIMIT),
        name="inproj",
    )(h, w_in_t, cw, cb)


def _slab_store(ext, row0, value, col0=0):
    for s in range(value.shape[1] // LANE):
        ext[col0 // LANE + s, row0:row0 + value.shape[0], :] = value[:, s * LANE:(s + 1) * LANE]


def _slab_load(ext, row0, rows, lo, hi):
    return jnp.concatenate([ext[s, row0:row0 + rows, :] for s in range(lo // LANE, hi // LANE)], axis=1)


def _conv(ext, lo, hi, q, width, w_ref, first):
    parts = []
    for s in range(lo // LANE, hi // LANE):
        acc = None
        for k in range(width):
            term = w_ref[k:k + 1, s * LANE:(s + 1) * LANE] * ext[s, first + k:first + k + q, :]
            acc = term if acc is None else acc + term
        parts.append(acc)
    return jnp.concatenate(parts, axis=1)


def _gated_norm(y, z, nw):
    g = y * _silu(z)
    return g * lax.rsqrt(jnp.mean(g * g, axis=-1, keepdims=True) + SSD_NORM_EPS) * nw


def _decay_col(cs_last_row, g):
    d = jnp.exp(cs_last_row)
    parts = [
        jnp.broadcast_to(d[0:1, g * HEADS_PER_GROUP + j:g * HEADS_PER_GROUP + j + 1], (HEAD_DIM, STATE_DIM))
        for j in range(HEADS_PER_GROUP)
    ]
    return jnp.concatenate(parts, axis=0)


def _ssd_prompt_body(x4, x5, x6, z2, z3, scb, scc, sch, dtr,
                     dtb, alog, dskip, nw, scw, expand,
                     ya_ref, yb_ref, st_ref, sct_ref, ext2):
    q, d = x4.shape
    n_groups = st_ref.shape[1]
    gn = n_groups * STATE_DIM
    kw2 = scw.shape[0]

    @pl.when(pl.program_id(1) == 0)
    def _():
        st_ref[...] = jnp.zeros(st_ref.shape, F32)
        ext2[:, 0:SUBLANE, :] = jnp.zeros((ext2.shape[0], SUBLANE, LANE), F32)

    _slab_store(ext2, SUBLANE, scc[...] * sch[...])
    v = _conv(ext2, 0, d, q, kw2, scw, SUBLANE - (kw2 - 1))
    yb_ref[...] = (scb[...] * v).astype(yb_ref.dtype)
    tail2 = _slab_load(ext2, q, SUBLANE, 0, d)
    _slab_store(ext2, 0, tail2)
    sct_ref[0] = tail2

    dt = _softplus(dtr[...] + dtb[...])
    a = dt * (-jnp.exp(alog[...]))
    row = lax.broadcasted_iota(jnp.int32, (q, q), 0)
    col = lax.broadcasted_iota(jnp.int32, (q, q), 1)
    causal = row >= col
    tri = causal.astype(BF16)
    cs = sum(jnp.dot(tri, part, preferred_element_type=F32) for part in _split3(a))
    cs_t = cs.T
    cs_last = cs[q - 1:q, :]
    dend = jnp.exp(cs_last - cs)
    ecs = jnp.exp(cs)
    stacked = jnp.concatenate([dt, dend, ecs], axis=0)
    st_b = stacked.astype(BF16)

    lane = lax.broadcasted_iota(jnp.int32, (q, LANE), 1)
    groups_per_block = d // GROUP_W

    for g in range(n_groups):
        c0 = g * GROUP_W
        xref = x4 if g < groups_per_block else x5
        bc = (g % groups_per_block) * GROUP_W
        xs = xref[:, bc:bc + GROUP_W]
        e_g = expand[:, c0:c0 + GROUP_W]
        ex = jnp.dot(st_b, e_g, preferred_element_type=F32)
        xdt = xs * ex[0:q]
        xdt_b = xdt.astype(BF16)
        xdd_b = (xdt * ex[q:2 * q]).astype(BF16)
        bg = x6[:, g * STATE_DIM:(g + 1) * STATE_DIM].astype(BF16)
        cg = x6[:, gn + g * STATE_DIM:gn + (g + 1) * STATE_DIM].astype(BF16)
        cbm = lax.dot_general(cg, bg, NT_DIMS, preferred_element_type=F32)
        state = st_ref[0, g]
        y_off = lax.dot_general(cg, state.astype(BF16), NT_DIMS, preferred_element_type=F32)
        y_parts = []
        for j in range(HEADS_PER_GROUP // 2):
            scores = []
            for h in (g * HEADS_PER_GROUP + 2 * j, g * HEADS_PER_GROUP + 2 * j + 1):
                seg = cs[:, h:h + 1] - cs_t[h:h + 1, :]
                dec = jnp.exp(jnp.where(causal, seg, -jnp.inf))
                scores.append((cbm * dec).astype(BF16))
            xp = xdt_b[:, j * LANE:(j + 1) * LANE]
            zero = jnp.zeros_like(xp)
            rhs = jnp.concatenate(
                [jnp.where(lane < HEAD_DIM, xp, zero), jnp.where(lane >= HEAD_DIM, xp, zero)], axis=0)
            y_parts.append(jnp.dot(jnp.concatenate(scores, axis=1), rhs, preferred_element_type=F32))
        y = jnp.concatenate(y_parts, axis=1) + y_off * ex[2 * q:3 * q] + xs * dskip[:, c0:c0 + GROUP_W]
        zref = z2 if g < groups_per_block else z3
        ya_ref[:, c0:c0 + GROUP_W] = _gated_norm(
            y, zref[:, bc:bc + GROUP_W], nw[:, c0:c0 + GROUP_W]).astype(ya_ref.dtype)
        st_ref[0, g] = state * _decay_col(cs_last, g) + lax.dot_general(
            xdd_b, bg, TN_DIMS, preferred_element_type=F32)


def _ssd_prompt(p, dt_raw, consts, n_seq, seq_len, d):
    (_, _, dtb, alog, dskip, nw, scw, expand) = consts
    q = SSD_CHUNK if seq_len % SSD_CHUNK == 0 else seq_len
    nc = seq_len // q
    n_groups = 2 * d // GROUP_W
    t_p = n_seq * seq_len
    assert 2 * n_groups * STATE_DIM == d

    def blk(cidx):
        return pl.BlockSpec((q, d), lambda b, c, cidx=cidx: (b * nc + c, cidx))

    def const(arr):
        return pl.BlockSpec(arr.shape, lambda b, c: (0,) * arr.ndim)

    return pl.pallas_call(
        _ssd_prompt_body,
        grid=(n_seq, nc),
        in_specs=[blk(4), blk(5), blk(6), blk(2), blk(3), blk(7), blk(8), blk(9),
                  pl.BlockSpec((q, LANE), lambda b, c: (b * nc + c, 0)),
                  const(dtb), const(alog), const(dskip), const(nw), const(scw), const(expand)],
        out_specs=[
            pl.BlockSpec((q, 2 * d), lambda b, c: (b * nc + c, 0)),
            pl.BlockSpec((q, d), lambda b, c: (b * nc + c, 0)),
            pl.BlockSpec((1, n_groups, GROUP_W, STATE_DIM), lambda b, c: (b, 0, 0, 0)),
            pl.BlockSpec((1, SUBLANE, d), lambda b, c: (b, 0, 0)),
        ],
        out_shape=[
            jax.ShapeDtypeStruct((t_p, 2 * d), BF16),
            jax.ShapeDtypeStruct((t_p, d), BF16),
            jax.ShapeDtypeStruct((n_seq, n_groups, GROUP_W, STATE_DIM), F32),
            jax.ShapeDtypeStruct((n_seq, SUBLANE, d), F32),
        ],
        scratch_shapes=[pltpu.VMEM((d // LANE, q + SUBLANE, LANE), F32)],
        compiler_params=pltpu.CompilerParams(
            dimension_semantics=("arbitrary", "arbitrary"), vmem_limit_bytes=VMEM_LIMIT),
        name="ssd_prompt",
    )(p, p, p, p, p, p, p, p, dt_raw, dtb, alog, dskip, nw, scw, expand)


SAMPLE_SEQS_PER_STEP = 2


SAMPLE_STATE_SLOTS = 3


def _ssd_sample_body(x4, x5, x6, z2, z3, scb, scc, sch, dtr, ssm_hbm, conv_in, sc_in,
                     cw, cb, dtb, alog, dskip, nw, scw, expand, gsel,
                     ya_ref, yb_ref, ssm_out, conv_out, sc_out, exts, ext2s, ssm_buf, ssm_sems):
    d = x4.shape[1]
    n_groups = ssm_out.shape[1]
    step = pl.program_id(0)
    n_steps = pl.num_programs(0)

    def fetch(t):
        slot = t % SAMPLE_STATE_SLOTS
        return pltpu.make_async_copy(
            ssm_hbm.at[pl.ds(t * SAMPLE_SEQS_PER_STEP, SAMPLE_SEQS_PER_STEP)], ssm_buf.at[slot], ssm_sems.at[slot])

    @pl.when(step == 0)
    def _():
        for t in range(SAMPLE_STATE_SLOTS - 1):
            @pl.when(t < n_steps)
            def _():
                fetch(t).start()

    @pl.when(step + SAMPLE_STATE_SLOTS - 1 < n_steps)
    def _():
        fetch(step + SAMPLE_STATE_SLOTS - 1).start()

    fetch(step).wait()
    ssm_in = ssm_buf.at[step % SAMPLE_STATE_SLOTS]
    gn = n_groups * STATE_DIM
    d_inner = n_groups * GROUP_W
    kw = cw.shape[0]
    kw2 = scw.shape[0]
    q = x4.shape[0] // SAMPLE_SEQS_PER_STEP
    groups_per_block = d // GROUP_W
    first = SUBLANE - (kw - 1)
    first2 = SUBLANE - (kw2 - 1)
    nrep = q * q

    rep_t = lax.broadcasted_iota(jnp.int32, (nrep, LANE), 0) % q
    rep_s = lax.broadcasted_iota(jnp.int32, (nrep, LANE), 0) // q
    rep_causal = rep_t >= rep_s
    row_q = lax.broadcasted_iota(jnp.int32, (q, LANE), 0)

    def rep_rows(m):
        return jnp.concatenate([jnp.broadcast_to(m[s:s + 1], (q, m.shape[1])) for s in range(q)], axis=0)

    def tile_rows(m):
        return jnp.concatenate([m] * q, axis=0)

    yb_rows = []
    ya_rows = [[] for _ in range(n_groups)]
    for sidx in range(SAMPLE_SEQS_PER_STEP):
        r0 = sidx * q
        ext, ext2 = exts.at[sidx], ext2s.at[sidx]
        _slab_store(ext, first, conv_in[sidx])
        _slab_store(ext, SUBLANE, x4[r0:r0 + q, :])
        _slab_store(ext, SUBLANE, x5[r0:r0 + q, :], d)
        _slab_store(ext, SUBLANE, x6[r0:r0 + q, :], 2 * d)
        conv_out[sidx] = _slab_load(ext, SUBLANE + q - (kw - 1), kw - 1, 0, 3 * d)

        _slab_store(ext2, first2, sc_in[sidx])
        _slab_store(ext2, SUBLANE, scc[r0:r0 + q, :] * sch[r0:r0 + q, :])
        yb_rows.append(scb[r0:r0 + q, :] * _conv(ext2, 0, d, q, kw2, scw, first2))
        sc_out[sidx] = _slab_load(ext2, SUBLANE + q - (kw2 - 1), kw2 - 1, 0, d)

        dt = _softplus(dtr[r0:r0 + q, :] + dtb[...])
        a = dt * (-jnp.exp(alog[...]))
        cs = jnp.zeros((q, LANE), F32)
        for r in range(q):
            cs = cs + jnp.where(row_q >= r, jnp.broadcast_to(a[r:r + 1], (q, LANE)), 0.0)
        cs_last = cs[q - 1:q, :]
        dend = jnp.exp(cs_last - cs)
        ecs = jnp.exp(cs)

        bmat = _silu(_conv(ext, 2 * d, 2 * d + gn, q, kw, cw, first) + cb[:, 2 * d:2 * d + gn])
        cmat = _silu(_conv(ext, 2 * d + gn, 3 * d, q, kw, cw, first) + cb[:, 2 * d + gn:3 * d])

        cb_hi, cb_lo = _split2(tile_rows(cmat) * rep_rows(bmat))
        cbh = (jnp.dot(cb_hi, gsel[...], preferred_element_type=F32)
               + jnp.dot(cb_lo, gsel[...], preferred_element_type=F32))
        dec = jnp.exp(jnp.where(rep_causal, tile_rows(cs) - rep_rows(cs), -jnp.inf))
        stacked = jnp.concatenate([dt, dend, ecs], axis=0)
        st_hi = stacked.astype(BF16).astype(F32)
        pad = jnp.zeros((LANE - nrep - 6 * q, LANE), F32)
        lhs = jnp.concatenate([cbh * dec, st_hi, stacked - st_hi, pad], axis=0).astype(BF16)

        xdd_parts = []
        for g in range(n_groups):
            c0 = g * GROUP_W
            xs = _silu(_conv(ext, c0, c0 + GROUP_W, q, kw, cw, first) + cb[:, c0:c0 + GROUP_W])
            ex = jnp.dot(lhs, expand[:, c0:c0 + GROUP_W], preferred_element_type=F32)
            o = nrep
            dtx = ex[o:o + q] + ex[o + 3 * q:o + 4 * q]
            dendx = ex[o + q:o + 2 * q] + ex[o + 4 * q:o + 5 * q]
            ecsx = ex[o + 2 * q:o + 3 * q] + ex[o + 5 * q:o + 6 * q]
            xdt = xs * dtx
            xdd = xdt * dendx
            y = xs * dskip[:, c0:c0 + GROUP_W]
            for s in range(q):
                y = y + ex[s * q:(s + 1) * q] * jnp.broadcast_to(xdt[s:s + 1], (q, GROUP_W))
            state = ssm_in[sidx, g]
            cg = cmat[:, g * STATE_DIM:(g + 1) * STATE_DIM]
            y = y + lax.dot_general(cg, state, NT_DIMS, preferred_element_type=F32) * ecsx
            zref = z2 if g < groups_per_block else z3
            zc = (g % groups_per_block) * GROUP_W
            ya_rows[g].append(_gated_norm(y, zref[r0:r0 + q, zc:zc + GROUP_W], nw[:, c0:c0 + GROUP_W]))
            xdd_parts.append(xdd)

        zrows = jnp.zeros((LANE - q, d_inner), F32)
        xdd_t = jnp.concatenate([jnp.concatenate(xdd_parts, axis=1), zrows], axis=0).T
        for g in range(n_groups):
            b_pad = jnp.concatenate(
                [bmat[:, g * STATE_DIM:(g + 1) * STATE_DIM], jnp.zeros((LANE - q, STATE_DIM), F32)], axis=0)
            ssm_out[sidx, g] = ssm_in[sidx, g] * _decay_col(cs_last, g) + jnp.dot(
                xdd_t[g * GROUP_W:(g + 1) * GROUP_W, :], b_pad, preferred_element_type=F32)

    yb_ref[...] = jnp.concatenate(yb_rows, axis=0).astype(yb_ref.dtype)
    for g in range(n_groups):
        ya_ref[:, g * GROUP_W:(g + 1) * GROUP_W] = jnp.concatenate(ya_rows[g], axis=0).astype(ya_ref.dtype)


def _ssd_sample(p, dt_raw, ssm, conv_state, sc_state, consts, gsel, row0, seq_len, d):
    (cw, cb, dtb, alog, dskip, nw, scw, expand) = consts
    n_seq = ssm.shape[0]
    sp = SAMPLE_SEQS_PER_STEP
    rows = sp * seq_len
    assert seq_len == SUBLANE and n_seq % sp == 0 and row0 % rows == 0
    b0 = row0 // rows

    def blk(cidx):
        return pl.BlockSpec((rows, d), lambda i, cidx=cidx: (b0 + i, cidx))

    def const(arr):
        return pl.BlockSpec(arr.shape, lambda i: (0,) * arr.ndim)

    def per_seq(arr):
        return pl.BlockSpec((sp,) + arr.shape[1:], lambda i: (i,) + (0,) * (arr.ndim - 1))

    return pl.pallas_call(
        _ssd_sample_body,
        grid=(n_seq // sp,),
        in_specs=[blk(4), blk(5), blk(6), blk(2), blk(3), blk(7), blk(8), blk(9),
                  pl.BlockSpec((rows, LANE), lambda i: (b0 + i, 0)),
                  pl.BlockSpec(memory_space=pl.ANY), per_seq(conv_state), per_seq(sc_state),
                  const(cw), const(cb), const(dtb), const(alog), const(dskip), const(nw), const(scw),
                  const(expand), const(gsel)],
        out_specs=[
            pl.BlockSpec((rows, 2 * d), lambda i: (i, 0)),
            pl.BlockSpec((rows, d), lambda i: (i, 0)),
            per_seq(ssm), per_seq(conv_state), per_seq(sc_state),
        ],
        out_shape=[
            jax.ShapeDtypeStruct((n_seq * seq_len, 2 * d), BF16),
            jax.ShapeDtypeStruct((n_seq * seq_len, d), BF16),
            jax.ShapeDtypeStruct(ssm.shape, F32),
            jax.ShapeDtypeStruct(conv_state.shape, F32),
            jax.ShapeDtypeStruct(sc_state.shape, F32),
        ],
        scratch_shapes=[pltpu.VMEM((sp, 3 * d // LANE, 2 * SUBLANE, LANE), F32),
                        pltpu.VMEM((sp, d // LANE, 2 * SUBLANE, LANE), F32),
                        pltpu.VMEM((SAMPLE_STATE_SLOTS, sp) + ssm.shape[1:], F32),
                        pltpu.SemaphoreType.DMA((SAMPLE_STATE_SLOTS,))],
        compiler_params=pltpu.CompilerParams(
            dimension_semantics=("arbitrary",), vmem_limit_bytes=VMEM_LIMIT),
        name="ssd_sample",
    )(p, p, p, p, p, p, p, p, dt_raw, ssm, conv_state, sc_state,
      cw, cb, dtb, alog, dskip, nw, scw, expand, gsel)


def _branch_out_body(yap_ref, ybp_ref, yas_ref, ybs_ref, ga_ref, gb_ref, wa_ref, wb_ref, o_ref,
                     wa_scr, wb_scr, *, prompt_tiles):
    i = pl.program_id(1)

    @pl.when(i == 0)
    def _():
        wa_scr[...] = wa_ref[...].astype(BF16)
        wb_scr[...] = wb_ref[...].astype(BF16)

    def run(refs):
        ya_ref, yb_ref = refs
        pa = jnp.dot(ya_ref[...], wa_scr[...], preferred_element_type=F32)
        pb = jnp.dot(yb_ref[...], wb_scr[...], preferred_element_type=F32)
        merged = jax.nn.sigmoid(ga_ref[...]) * pa + jax.nn.sigmoid(gb_ref[...]) * pb
        o_ref[...] = merged.astype(o_ref.dtype)

    _by_stream(i, prompt_tiles, run, (yap_ref, ybp_ref), (yas_ref, ybs_ref))


def _branch_out(ya_p, yb_p, ya_s, yb_s, p, w_branch_out):
    t_p, d = yb_p.shape
    t_s = yb_s.shape[0]
    tm = _tile(math.gcd(t_p, t_s), 512, 16)
    tn = _tile(d, 512, LANE)
    nj = d // tn
    npt = t_p // tm
    prompt = lambda j, i: (jnp.minimum(i, npt - 1), 0)
    sample = lambda j, i: (jnp.maximum(i - npt, 0), 0)
    once = dict(pipeline_mode=pl.Buffered(1))
    return pl.pallas_call(
        functools.partial(_branch_out_body, prompt_tiles=npt),
        grid=(nj, (t_p + t_s) // tm),
        in_specs=[
            pl.BlockSpec((tm, 2 * d), prompt), pl.BlockSpec((tm, d), prompt),
            pl.BlockSpec((tm, 2 * d), sample), pl.BlockSpec((tm, d), sample),
            pl.BlockSpec((tm, tn), lambda j, i: (i, j)),
            pl.BlockSpec((tm, tn), lambda j, i: (i, nj + j)),
            pl.BlockSpec((None, 2 * d, tn), lambda j, i: (0, 0, j), **once),
            pl.BlockSpec((None, d, tn), lambda j, i: (0, 2, j), **once),
        ],
        out_specs=pl.BlockSpec((tm, tn), lambda j, i: (i, j)),
        out_shape=jax.ShapeDtypeStruct((t_p + t_s, d), BF16),
        scratch_shapes=[pltpu.VMEM((2 * d, tn), BF16), pltpu.VMEM((d, tn), BF16)],
        compiler_params=pltpu.CompilerParams(
            dimension_semantics=("arbitrary", "arbitrary"), vmem_limit_bytes=VMEM_LIMIT),
        name="branch_out",
    )(ya_p, yb_p, ya_s, yb_s, p, p, w_branch_out, w_branch_out)


def _mix_route_body(m_ref, xp_ref, xs_ref, wo_ref, nw_ref, rhi_ref, rlo_ref,
                    x1_ref, h2_ref, route_ref, wts_ref, cnt_ref, wo_scr, *, n_coarse, per_group, prompt_tiles):
    @pl.when(pl.program_id(0) == 0)
    def _():
        cnt_ref[...] = jnp.zeros(cnt_ref.shape, F32)
        wo_scr[...] = wo_ref[...].astype(BF16)

    def run(x_ref):
        x1 = x_ref[...] + jnp.dot(m_ref[...], wo_scr[...], preferred_element_type=F32)
        x1_ref[...] = x1
        h2 = x1 * lax.rsqrt(jnp.mean(x1 * x1, axis=-1, keepdims=True) + NORM_EPS) * nw_ref[...]
        h2_ref[...] = h2
        h_hi, h_lo = _split2(h2)
        logits = (jnp.dot(h_hi, rhi_ref[...], preferred_element_type=F32)
                  + jnp.dot(h_hi, rlo_ref[...], preferred_element_type=F32)
                  + jnp.dot(h_lo, rhi_ref[...], preferred_element_type=F32))

        tm = logits.shape[0]
        n_fine = n_coarse * per_group
        lane = lax.broadcasted_iota(jnp.int32, logits.shape, 1)
        big = jnp.int32(LANE)
        neg = -jnp.inf
        is_c = lane < n_coarse
        lc = jnp.where(is_c, logits, neg)
        mc = jnp.max(lc, axis=-1, keepdims=True)
        grp = jnp.min(jnp.where(is_c & (lc == mc), lane, big), axis=-1, keepdims=True)
        p_grp = 1.0 / jnp.sum(jnp.where(is_c, jnp.exp(lc - mc), 0.0), axis=-1, keepdims=True)
        eidx = lane - n_coarse
        sel = (eidx >= 0) & (eidx < n_fine) & ((eidx // per_group) == grp)
        lf = jnp.where(sel, logits, neg)
        v1 = jnp.max(lf, axis=-1, keepdims=True)
        i1 = jnp.min(jnp.where(sel & (lf == v1), eidx, big), axis=-1, keepdims=True)
        sel2 = sel & (eidx != i1)
        lf2 = jnp.where(sel2, logits, neg)
        v2 = jnp.max(lf2, axis=-1, keepdims=True)
        i2 = jnp.min(jnp.where(sel2 & (lf2 == v2), eidx, big), axis=-1, keepdims=True)
        e2 = jnp.exp(v2 - v1)
        w1 = p_grp / (1.0 + e2)
        w2 = p_grp * e2 / (1.0 + e2)
        wts_ref[...] = jnp.where(lane == 0, w1, jnp.where(lane == 1, w2, 0.0))

        hit1 = eidx == i1
        hit2 = eidx == i2
        hits = hit1.astype(F32) + hit2.astype(F32)
        earlier = (lax.broadcasted_iota(jnp.int32, (tm, tm), 0)
                   > lax.broadcasted_iota(jnp.int32, (tm, tm), 1)).astype(BF16)
        before = jnp.dot(earlier, hits.astype(BF16), preferred_element_type=F32) + cnt_ref[...]
        r1 = jnp.sum(jnp.where(hit1, before, 0.0), axis=-1, keepdims=True).astype(jnp.int32)
        r2 = jnp.sum(jnp.where(hit2, before, 0.0), axis=-1, keepdims=True).astype(jnp.int32)
        cnt_ref[...] += jnp.sum(hits, axis=0, keepdims=True)
        route_ref[...] = jnp.where(lane == 0, i1, jnp.where(lane == 1, i2, jnp.where(
            lane == 2, r1, jnp.where(lane == 3, r2, 0))))

    _by_stream(pl.program_id(0), prompt_tiles, run, xp_ref, xs_ref)


def _mix_route(merged, x_p, x_s, wo, norm_w, r_hi, r_lo, n_coarse, per_group):
    t_p, d = x_p.shape
    T = t_p + x_s.shape[0]
    tm = _tile(math.gcd(t_p, x_s.shape[0]), 256, 16)
    npt = t_p // tm
    row = lambda i: (i, 0)
    fixed = lambda i: (0, 0)
    xp_spec, xs_spec = _stream_specs((tm, d), npt)
    resident = dict(pipeline_mode=pl.Buffered(1))
    return pl.pallas_call(
        functools.partial(_mix_route_body, n_coarse=n_coarse, per_group=per_group, prompt_tiles=npt),
        grid=(T // tm,),
        in_specs=[
            pl.BlockSpec((tm, d), row), xp_spec, xs_spec, pl.BlockSpec((None, d, d), lambda i: (0, 0, 0), **resident),
            pl.BlockSpec((1, d), fixed), pl.BlockSpec((d, LANE), fixed, **resident),
            pl.BlockSpec((d, LANE), fixed, **resident),
        ],
        out_specs=[pl.BlockSpec((tm, d), row), pl.BlockSpec((tm, d), row),
                   pl.BlockSpec((tm, LANE), row), pl.BlockSpec((tm, LANE), row),
                   pl.BlockSpec((1, LANE), fixed)],
        out_shape=[
            jax.ShapeDtypeStruct((T, d), F32), jax.ShapeDtypeStruct((T, d), F32),
            jax.ShapeDtypeStruct((T, LANE), jnp.int32), jax.ShapeDtypeStruct((T, LANE), F32),
            jax.ShapeDtypeStruct((1, LANE), F32),
        ],
        scratch_shapes=[pltpu.VMEM((d, d), BF16)],
        compiler_params=pltpu.CompilerParams(
            dimension_semantics=("arbitrary",), vmem_limit_bytes=VMEM_LIMIT),
        name="mix_route",
    )(merged, x_p, x_s, wo, norm_w, r_hi, r_lo)


MOE_MOVE_TOKENS = 256
MOE_MOVE_UNROLL = 8


def _moe_scatter_body(dest_ref, fill_ref, h_ref, xs_hbm, zeros, sem, zsem):
    tt = h_ref.shape[0]
    step = pl.program_id(0)
    t0 = step * tt
    n_experts = (fill_ref.shape[0] - 1) // 2
    n_blocks = xs_hbm.shape[0] // MOE_ROWS
    pad_bits = [1 << k for k in reversed(range(3, MOE_ROWS.bit_length() - 1))]

    def fills(action):
        def expert(e, carry):
            row, n_pad = fill_ref[e], fill_ref[n_experts + e]
            head = (-row) & (SUBLANE - 1)
            for r in range(SUBLANE - 1):
                @pl.when(r < head)
                def _():
                    action(pltpu.make_async_copy(zeros.at[pl.ds(0, 1)], xs_hbm.at[pl.ds(row + r, 1)], zsem))
            row, n_pad = row + head, n_pad - head
            for bit in pad_bits:
                @pl.when((n_pad & bit) != 0)
                def _():
                    action(pltpu.make_async_copy(
                        zeros.at[pl.ds(0, bit)], xs_hbm.at[pl.ds(pl.multiple_of(row, SUBLANE), bit)], zsem))
                row = row + (n_pad & bit)
            return carry

        def tail(blk, carry):
            action(pltpu.make_async_copy(
                zeros, xs_hbm.at[pl.ds(pl.multiple_of(blk * MOE_ROWS, MOE_ROWS), MOE_ROWS)], zsem))
            return carry

        lax.fori_loop(0, n_experts, expert, 0)
        lax.fori_loop(fill_ref[2 * n_experts], n_blocks, tail, 0)

    @pl.when(step == 0)
    def _():
        zeros[...] = jnp.zeros(zeros.shape, zeros.dtype)
        fills(lambda c: c.start())

    def start(i, carry):
        for k in range(MOE_TOP_K):
            pltpu.make_async_copy(
                h_ref.at[pl.ds(i, 1)], xs_hbm.at[pl.ds(dest_ref[(t0 + i) * MOE_TOP_K + k], 1)], sem
            ).start(priority=k % 2)
        return carry

    lax.fori_loop(0, tt, start, 0, unroll=MOE_MOVE_UNROLL)
    for _ in range(MOE_TOP_K):
        pltpu.make_async_copy(h_ref, xs_hbm.at[pl.ds(0, tt)], sem).wait()

    @pl.when(step == pl.num_programs(0) - 1)
    def _():
        fills(lambda c: c.wait())


def _moe_scatter(dest, fill_rows, h2, n_rows):
    T, d = h2.shape
    tt = _tile(T, MOE_MOVE_TOKENS, SUBLANE)
    return pl.pallas_call(
        _moe_scatter_body,
        grid_spec=pltpu.PrefetchScalarGridSpec(
            num_scalar_prefetch=2,
            grid=(T // tt,),
            in_specs=[pl.BlockSpec((tt, d), lambda i, dr, fr: (i, 0))],
            out_specs=pl.BlockSpec(memory_space=pl.ANY),
            scratch_shapes=[pltpu.VMEM((MOE_ROWS, d), h2.dtype), pltpu.SemaphoreType.DMA(()),
                            pltpu.SemaphoreType.DMA(())],
        ),
        out_shape=jax.ShapeDtypeStruct((n_rows, d), h2.dtype),
        compiler_params=pltpu.CompilerParams(dimension_semantics=("arbitrary",)),
        name="moe_scatter",
    )(dest, fill_rows, h2)


MOE_WEIGHT_PIECES = 8
MOE_WEIGHT_RING = 4


def _moe_ffn_body(sched_ref, order_ref, n_used_ref, x_ref, wg_hbm, wu_hbm, wd_hbm, o_ref,
                  wg_b, wu_b, wd_b, sg, su, sd, sems):
    b = pl.program_id(0)
    hbm = (wg_hbm, wu_hbm, wd_hbm)
    stage = (sg, su, sd)
    resident = (wg_b, wu_b, wd_b)
    n_total = order_ref[order_ref.shape[0] - 1]

    def copies(c):
        e = order_ref[c // MOE_WEIGHT_PIECES]
        piece = c % MOE_WEIGHT_PIECES
        ring = c % MOE_WEIGHT_RING
        out = []
        for k in range(3):
            rows = stage[k].shape[1]
            out.append(pltpu.make_async_copy(
                hbm[k].at[0, e, pl.ds(pl.multiple_of(piece * rows, rows), rows), :],
                stage[k].at[ring], sems.at[ring, k]))
        return out

    def start(c):
        @pl.when(c < n_total)
        def _():
            for cp in copies(c):
                cp.start()

    def land(lo, hi):
        def body(c, carry):
            slot = (c // MOE_WEIGHT_PIECES) % 2
            piece = c % MOE_WEIGHT_PIECES
            ring = c % MOE_WEIGHT_RING
            for k, cp in enumerate(copies(c)):
                cp.wait()
                rows = stage[k].shape[1]
                resident[k][slot, pl.ds(pl.multiple_of(piece * rows, rows), rows), :] = stage[k][ring].astype(BF16)
            start(c + MOE_WEIGHT_RING)
            return carry

        lax.fori_loop(lo, hi, body, 0)

    @pl.when(b < n_used_ref[0])
    def _():
        @pl.when(b == 0)
        def _():
            for c in range(MOE_WEIGHT_RING):
                start(c)
            land(0, MOE_WEIGHT_PIECES)

        slot = sched_ref[0, b]
        xb = x_ref[...].astype(BF16)
        gate = jnp.dot(xb, wg_b[slot], preferred_element_type=F32)
        up = jnp.dot(xb, wu_b[slot], preferred_element_type=F32)
        act = (_silu(gate) * up).astype(BF16)
        o_ref[...] = jnp.dot(act, wd_b[slot], preferred_element_type=F32)
        land(sched_ref[1, b], sched_ref[2, b])

    @pl.when(b >= n_used_ref[0])
    def _():
        o_ref[...] = jnp.zeros(o_ref.shape, o_ref.dtype)


def _moe_ffn(sched, order, n_used, xs, wg, wu, wd):
    R, d = xs.shape
    f = wg.shape[3]
    nb = R // MOE_ROWS
    np_ = MOE_WEIGHT_PIECES
    assert d % np_ == 0 and f % np_ == 0
    any_spec = pl.BlockSpec(memory_space=pl.ANY)
    return pl.pallas_call(
        _moe_ffn_body,
        grid_spec=pltpu.PrefetchScalarGridSpec(
            num_scalar_prefetch=3,
            grid=(nb,),
            in_specs=[pl.BlockSpec((MOE_ROWS, d), lambda b, sc, od, nu: (b, 0)), any_spec, any_spec, any_spec],
            out_specs=pl.BlockSpec((MOE_ROWS, d), lambda b, sc, od, nu: (b, 0)),
            scratch_shapes=[
                pltpu.VMEM((2, d, f), BF16), pltpu.VMEM((2, d, f), BF16), pltpu.VMEM((2, f, d), BF16),
                pltpu.VMEM((MOE_WEIGHT_RING, d // np_, f), F32), pltpu.VMEM((MOE_WEIGHT_RING, d // np_, f), F32),
                pltpu.VMEM((MOE_WEIGHT_RING, f // np_, d), F32),
                pltpu.SemaphoreType.DMA((MOE_WEIGHT_RING, 3)),
            ],
        ),
        out_shape=jax.ShapeDtypeStruct((R, d), F32),
        compiler_params=pltpu.CompilerParams(
            dimension_semantics=("arbitrary",), vmem_limit_bytes=VMEM_LIMIT),
        name="moe_ffn",
    )(sched, order, n_used, xs, wg, wu, wd)


def _moe_combine_body(dest_ref, x1_ref, wts_ref, nw_ref, yb_hbm, op_ref, os_ref, buf, sems, *, prompt_tiles):
    tt = x1_ref.shape[0]
    i = pl.program_id(0)

    def gather(tile, slot):
        def start(r, carry):
            for k in range(MOE_TOP_K):
                pltpu.make_async_copy(
                    yb_hbm.at[pl.ds(dest_ref[(tile * tt + r) * MOE_TOP_K + k], 1)],
                    buf.at[slot, k, pl.ds(r, 1)], sems.at[slot]).start(priority=k % 2)
            return carry

        lax.fori_loop(0, tt, start, 0, unroll=MOE_MOVE_UNROLL)

    @pl.when(i == 0)
    def _():
        gather(0, 0)

    @pl.when(i + 1 < pl.num_programs(0))
    def _():
        gather(i + 1, (i + 1) % 2)

    slot = i % 2
    for k in range(MOE_TOP_K):
        pltpu.make_async_copy(yb_hbm.at[pl.ds(0, tt)], buf.at[slot, k], sems.at[slot]).wait()

    def finish(o_ref):
        w = wts_ref[...]
        x2 = x1_ref[...] + (buf[slot, 0] * w[:, 0:1] + buf[slot, 1] * w[:, 1:2])
        o_ref[...] = x2 * lax.rsqrt(jnp.mean(x2 * x2, axis=-1, keepdims=True) + NORM_EPS) * nw_ref[...]

    _by_stream(i, prompt_tiles, finish, op_ref, os_ref)


def _moe_combine(dest, x1, wts, norm_w, yb, t_p):
    T, d = x1.shape
    tt = _tile(math.gcd(T, t_p), MOE_MOVE_TOKENS, SUBLANE)
    npt = t_p // tt
    op_spec, os_spec = _stream_specs((tt, d), npt)
    return pl.pallas_call(
        functools.partial(_moe_combine_body, prompt_tiles=npt),
        grid_spec=pltpu.PrefetchScalarGridSpec(
            num_scalar_prefetch=1,
            grid=(T // tt,),
            in_specs=[
                pl.BlockSpec((tt, d), lambda i, dr: (i, 0)),
                pl.BlockSpec((tt, LANE), lambda i, dr: (i, 0)),
                pl.BlockSpec((1, d), lambda i, dr: (0, 0)),
                pl.BlockSpec(memory_space=pl.ANY),
            ],
            out_specs=[op_spec, os_spec],
            scratch_shapes=[pltpu.VMEM((2, MOE_TOP_K, tt, d), F32), pltpu.SemaphoreType.DMA((2,))],
        ),
        out_shape=[jax.ShapeDtypeStruct((t_p, d), F32), jax.ShapeDtypeStruct((T - t_p, d), F32)],
        compiler_params=pltpu.CompilerParams(dimension_semantics=("arbitrary",)),
        name="moe_combine",
    )(dest, x1, wts, norm_w, yb)


def _route_rows(eid, rank, counts, n_assign):
    n_experts = counts.shape[0]
    padded = (counts + MOE_ROWS - 1) // MOE_ROWS * MOE_ROWS
    pend = jnp.cumsum(padded)
    pstart = pend - padded
    experts = jnp.arange(n_experts, dtype=jnp.int32)
    onehot = eid[:, :, None] == experts[None, None, :]
    dest = (jnp.sum(jnp.where(onehot, pstart[None, None, :], 0), axis=-1) + rank).astype(jnp.int32)
    n_blocks = -(-n_assign // MOE_ROWS) + n_experts
    blk_start = jnp.arange(n_blocks, dtype=jnp.int32) * MOE_ROWS
    used = blk_start < pend[-1]
    blk_e = jnp.minimum(jnp.sum((blk_start[:, None] >= pend[None, :]).astype(jnp.int32), axis=1), n_experts - 1)
    n_used = (pend[-1] // MOE_ROWS).astype(jnp.int32).reshape(1)

    nonempty = counts > 0
    ordinal = jnp.cumsum(nonempty.astype(jnp.int32)) - 1
    n_nonempty = jnp.sum(nonempty.astype(jnp.int32))
    order = jnp.argsort(jnp.where(nonempty, experts, n_experts + experts)).astype(jnp.int32)
    n_slices = MOE_WEIGHT_PIECES * n_nonempty
    pos = blk_start // MOE_ROWS - (pstart // MOE_ROWS)[blk_e]
    nblk = jnp.maximum((padded // MOE_ROWS)[blk_e], 1)
    has_next = used & (ordinal[blk_e] + 1 < n_nonempty)
    base = MOE_WEIGHT_PIECES * (ordinal[blk_e] + 1)
    first = jnp.where(has_next, base + MOE_WEIGHT_PIECES * pos // nblk, n_slices)
    last = jnp.where(has_next, base + MOE_WEIGHT_PIECES * (pos + 1) // nblk, n_slices)
    slot = jnp.where(used, ordinal[blk_e] % 2, 0)
    sched = jnp.stack([slot, first, last]).astype(jnp.int32)
    order = jnp.concatenate([order, n_slices.reshape(1)]).astype(jnp.int32)
    fill_rows = jnp.concatenate([pstart + counts, padded - counts, n_used]).astype(jnp.int32)
    return dest.reshape(-1), sched, order, fill_rows, n_used, n_blocks


def _pad_lanes(v, fill=0.0):
    return jnp.pad(v.astype(F32), (0, LANE - v.shape[0]), constant_values=fill).reshape(1, LANE)


def kernel(x_prompt, x_sample, state_ssm, state_ssd_conv, state_short_conv, norm_mixer, w_in, ssd_conv_w,
           ssd_conv_b, ssd_dt_bias, ssd_a_log, ssd_d, ssd_norm, sc_conv_w, w_branch_out, w_out, norm_ffn,
           w_router_coarse, w_router_fine, w_expert_gate, w_expert_up, w_expert_down, norm_final):
    depth = w_in.shape[0]
    assert depth == 1
    n_p, seq_p, d = x_prompt.shape
    n_s, seq_s, _ = x_sample.shape
    d_inner = 2 * d
    n_heads = d_inner // HEAD_DIM
    n_groups = d_inner // GROUP_W
    gn = n_groups * STATE_DIM
    conv_dim = d_inner + 2 * gn
    assert conv_dim == 3 * d and n_heads <= LANE and ssd_conv_w.shape[2] == conv_dim
    t_p, t_s = n_p * seq_p, n_s * seq_s
    n_coarse = w_router_coarse.shape[2]
    n_experts = w_router_fine.shape[2]
    assert n_coarse + n_experts <= LANE

    off_dt = 2 * d + d_inner + conv_dim
    off_sc = off_dt + n_heads
    w_in_t = jnp.swapaxes(w_in, 1, 2)
    w_dt = jnp.pad(w_in_t[0, off_dt:off_sc, :].T, ((0, 0), (0, LANE - n_heads))).astype(BF16)
    head_of_col = jnp.arange(d_inner, dtype=jnp.int32) // HEAD_DIM
    expand = (jnp.arange(LANE, dtype=jnp.int32)[:, None] == head_of_col[None, :]).astype(BF16)
    group_of_n = jnp.arange(gn, dtype=jnp.int32) // STATE_DIM
    group_of_head = jnp.arange(LANE, dtype=jnp.int32) // HEADS_PER_GROUP
    gsel = ((group_of_n[:, None] == group_of_head[None, :])
            & (jnp.arange(LANE)[None, :] < n_heads)).astype(BF16)
    consts = (
        ssd_conv_w[0], ssd_conv_b[0].reshape(1, conv_dim), _pad_lanes(ssd_dt_bias[0]), _pad_lanes(ssd_a_log[0]),
        jnp.repeat(ssd_d[0].astype(F32), HEAD_DIM).reshape(1, d_inner), ssd_norm[0].reshape(1, d_inner),
        sc_conv_w[0], expand,
    )
    w_router = jnp.pad(jnp.concatenate([w_router_coarse[0], w_router_fine[0]], axis=1),
                       ((0, 0), (0, LANE - n_coarse - n_experts)))
    r_hi = w_router.astype(BF16)
    r_lo = (w_router - r_hi.astype(F32)).astype(BF16)

    x_p = x_prompt.reshape(t_p, d)
    x_s = x_sample.reshape(t_s, d)
    h, dt_raw = _prenorm(x_p, x_s, norm_mixer[0].reshape(1, d), w_dt)
    p, tails = _inproj(h, w_in_t[0], off_dt, off_sc, consts[0], consts[1], t_p, seq_p)
    ya_p, yb_p, p_ssm, p_sc_tail = _ssd_prompt(p, dt_raw, consts, n_p, seq_p, d)
    ya_s, yb_s, s_ssm, s_conv, s_sc = _ssd_sample(
        p, dt_raw, state_ssm[0].reshape(n_s, n_groups, GROUP_W, STATE_DIM), state_ssd_conv[0],
        state_short_conv[0], consts, gsel, t_p, seq_s, d)
    merged = _branch_out(ya_p, yb_p, ya_s, yb_s, p, w_branch_out)
    x1, h2, route, wts, counts = _mix_route(merged, x_p, x_s, w_out, norm_ffn[0].reshape(1, d), r_hi, r_lo,
                                            n_coarse, n_experts // n_coarse)

    n_assign = (t_p + t_s) * MOE_TOP_K
    dest, sched, order, fill_rows, n_used, n_blocks = _route_rows(
        route[:, 0:MOE_TOP_K], route[:, MOE_TOP_K:2 * MOE_TOP_K],
        counts[0, n_coarse:n_coarse + n_experts].astype(jnp.int32), n_assign)
    xs = _moe_scatter(dest, fill_rows, h2, n_blocks * MOE_ROWS)
    yrows = _moe_ffn(sched, order, n_used, xs, w_expert_gate, w_expert_up, w_expert_down)
    out_p, out_s = _moe_combine(dest, x1, wts, norm_final.reshape(1, d), yrows, t_p)

    kw = ssd_conv_w.shape[1]
    kw2 = sc_conv_w.shape[1]
    tiles_per_seq = (tails.shape[0] // SUBLANE) * seq_p // (t_p + t_s)
    seq_tails = tails.reshape(-1, SUBLANE, tails.shape[1])[tiles_per_seq - 1:n_p * tiles_per_seq:tiles_per_seq]
    p_conv = seq_tails[:, SUBLANE - (kw - 1):, 4 * d:7 * d]
    return (
        out_p.reshape(n_p, seq_p, d),
        out_s.reshape(n_s, seq_s, d),
        p_ssm.reshape(1, n_p, n_heads, HEAD_DIM, STATE_DIM),
        p_conv[None],
        p_sc_tail[:, SUBLANE - (kw2 - 1):, :][None],
        s_ssm.reshape(1, n_s, n_heads, HEAD_DIM, STATE_DIM),
        s_conv[None],
        s_sc[None],
    )
```

```python
import functools
import math

import jax
import jax.numpy as jnp
from jax import lax
from jax.experimental import pallas as pl
from jax.experimental.pallas import tpu as pltpu

F32 = jnp.float32
BF16 = jnp.bfloat16

NORM_EPS = 1e-6
SSD_NORM_EPS = 1e-5
HEAD_DIM = 64
STATE_DIM = 128
HEADS_PER_GROUP = 8
GROUP_W = HEADS_PER_GROUP * HEAD_DIM
SSD_CHUNK = 128
MOE_TOP_K = 2
MOE_ROWS = 256

LANE = 128
SUBLANE = 8
VMEM_LIMIT = 56 * 1024 * 1024

NT_DIMS = (((1,), (1,)), ((), ()))
TN_DIMS = (((0,), (0,)), ((), ()))


def _tile(n, target, align):
    best = None
    for t in range(align, min(n, target) + 1, align):
        if n % t == 0:
            best = t
    assert best is not None, (n, target, align)
    return best


def _split2(v):
    hi = v.astype(BF16)
    lo = (v - hi.astype(F32)).astype(BF16)
    return hi, lo


def _split3(v):
    hi = v.astype(BF16)
    r = v - hi.astype(F32)
    mid = r.astype(BF16)
    lo = (r - mid.astype(F32)).astype(BF16)
    return hi, mid, lo


def _softplus(x):
    return jnp.maximum(x, 0.0) + jnp.log1p(jnp.exp(-jnp.abs(x)))


def _silu(x):
    return x * jax.nn.sigmoid(x)


def _by_stream(tile, prompt_tiles, fn, prompt_ref, sample_ref):
    @pl.when(tile < prompt_tiles)
    def _():
        fn(prompt_ref)

    @pl.when(tile >= prompt_tiles)
    def _():
        fn(sample_ref)


def _stream_specs(block, prompt_tiles, **kwargs):
    pad = (0,) * (len(block) - 1)
    prompt = lambda i, *_: (jnp.minimum(i, prompt_tiles - 1),) + pad
    sample = lambda i, *_: (jnp.maximum(i - prompt_tiles, 0),) + pad
    return pl.BlockSpec(block, prompt, **kwargs), pl.BlockSpec(block, sample, **kwargs)


def _prenorm_body(xp_ref, xs_ref, nw_ref, wdt_ref, h_ref, dt_ref, *, prompt_tiles):
    def run(x_ref):
        x = x_ref[...]
        h = x * lax.rsqrt(jnp.mean(x * x, axis=-1, keepdims=True) + NORM_EPS) * nw_ref[...]
        hb = h.astype(BF16)
        h_ref[...] = hb
        dt_ref[...] = jnp.dot(hb, wdt_ref[...], preferred_element_type=F32)

    _by_stream(pl.program_id(0), prompt_tiles, run, xp_ref, xs_ref)


def _prenorm(x_p, x_s, norm_w, w_dt):
    t_p, D = x_p.shape
    T = t_p + x_s.shape[0]
    tm = _tile(math.gcd(t_p, x_s.shape[0]), 512, 16)
    npt = t_p // tm
    xp_spec, xs_spec = _stream_specs((tm, D), npt)
    return pl.pallas_call(
        functools.partial(_prenorm_body, prompt_tiles=npt),
        grid=(T // tm,),
        in_specs=[xp_spec, xs_spec, pl.BlockSpec((1, D), lambda i: (0, 0)),
                  pl.BlockSpec((D, LANE), lambda i: (0, 0))],
        out_specs=[pl.BlockSpec((tm, D), lambda i: (i, 0)), pl.BlockSpec((tm, LANE), lambda i: (i, 0))],
        out_shape=[jax.ShapeDtypeStruct((T, D), BF16), jax.ShapeDtypeStruct((T, LANE), F32)],
        compiler_params=pltpu.CompilerParams(
            dimension_semantics=("arbitrary",), vmem_limit_bytes=VMEM_LIMIT),
        name="prenorm",
    )(x_p, x_s, norm_w, w_dt)


INPROJ_CONV_SLABS = 4


def _inproj_body(h_ref, w_ref, cw_ref, cb_ref, p_ref, tail_ref, w_scr, cext,
                 *, blocks_per_d, prompt_tiles, tiles_per_seq):
    j = pl.program_id(0)
    i = pl.program_id(1)
    tm, tn = p_ref.shape
    kw = cw_ref.shape[0]

    @pl.when(i == 0)
    def _():
        w_scr[...] = w_ref[...].astype(BF16)

    is_conv = (j >= 4 * blocks_per_d) & (j < 7 * blocks_per_d) & (i < prompt_tiles)

    @pl.when(jnp.logical_not(is_conv))
    def _():
        raw = lax.dot_general(h_ref[...], w_scr[...], NT_DIMS, preferred_element_type=F32)
        p_ref[...] = raw
        tail_ref[...] = raw[tm - SUBLANE:tm, :]

    @pl.when(is_conv)
    def _():
        @pl.when(i % tiles_per_seq == 0)
        def _():
            cext[:, 0:SUBLANE, :] = jnp.zeros((cext.shape[0], SUBLANE, LANE), F32)

        ws = tn // INPROJ_CONV_SLABS
        first = SUBLANE - (kw - 1)
        for k in range(INPROJ_CONV_SLABS):
            raw = lax.dot_general(h_ref[...], w_scr[k * ws:(k + 1) * ws, :], NT_DIMS, preferred_element_type=F32)
            for s in range(k * ws // LANE, (k + 1) * ws // LANE):
                lo, hi = s * LANE, (s + 1) * LANE
                cext[s, SUBLANE:SUBLANE + tm, :] = raw[:, lo - k * ws:hi - k * ws]
                acc = cb_ref[:, lo:hi]
                for t in range(kw):
                    acc = acc + cw_ref[t:t + 1, lo:hi] * cext[s, first + t:first + t + tm, :]
                p_ref[:, lo:hi] = _silu(acc)
                tail = cext[s, tm:tm + SUBLANE, :]
                tail_ref[:, lo:hi] = tail
                cext[s, 0:SUBLANE, :] = tail


def _inproj(h, w_in_t, n_head_cols, tail_start, cw, cb, t_p, seq_p):
    T, D = h.shape
    n_tail = w_in_t.shape[0] - tail_start
    tm = _tile(math.gcd(seq_p, T - t_p), 1024, 16)
    tn = _tile(math.gcd(D, n_tail), 1024, LANE)
    main_tiles = n_head_cols // tn
    n_out = n_head_cols + n_tail
    bpd = D // tn
    assert tail_start % SUBLANE == 0 and n_head_cols % tn == 0
    conv_block = lambda j, i: (0, jnp.clip(j - 4 * bpd, 0, 3 * bpd - 1))
    w_row = lambda j, i: (pl.multiple_of(
        jnp.where(j < main_tiles, j * tn, tail_start + (j - main_tiles) * tn), SUBLANE), 0)
    return pl.pallas_call(
        functools.partial(_inproj_body, blocks_per_d=bpd, prompt_tiles=t_p // tm, tiles_per_seq=seq_p // tm),
        grid=(n_out // tn, T // tm),
        in_specs=[
            pl.BlockSpec((tm, D), lambda j, i: (i, 0)),
            pl.BlockSpec((pl.Element(tn), pl.Element(D)), w_row),
            pl.BlockSpec((cw.shape[0], tn), conv_block),
            pl.BlockSpec((1, tn), conv_block),
        ],
        out_specs=[pl.BlockSpec((tm, tn), lambda j, i: (i, j)),
                   pl.BlockSpec((SUBLANE, tn), lambda j, i: (i, j))],
        out_shape=[jax.ShapeDtypeStruct((T, n_out), F32),
                   jax.ShapeDtypeStruct((T // tm * SUBLANE, n_out), F32)],
        scratch_shapes=[pltpu.VMEM((tn, D), BF16), pltpu.VMEM((tn // LANE, tm + SUBLANE, LANE), F32)],
        compiler_params=pltpu.CompilerParams(
            dimension_semantics=("arbitrary", "arbitrary"), vmem_limit_bytes=VMEM_LIMIT),
        name="inproj",
    )(h, w_in_t, cw, cb)


def _slab_store(ext, row0, value, col0=0):
    for s in range(value.shape[1] // LANE):
        ext[col0 // LANE + s, row0:row0 + value.shape[0], :] = value[:, s * LANE:(s + 1) * LANE]


def _slab_load(ext, row0, rows, lo, hi):
    return jnp.concatenate([ext[s, row0:row0 + rows, :] for s in range(lo // LANE, hi // LANE)], axis=1)


def _conv(ext, lo, hi, q, width, w_ref, first):
    parts = []
    for s in range(lo // LANE, hi // LANE):
        acc = None
        for k in range(width):
            term = w_ref[k:k + 1, s * LANE:(s + 1) * LANE] * ext[s, first + k:first + k + q, :]
            acc = term if acc is None else acc + term
        parts.append(acc)
    return jnp.concatenate(parts, axis=1)


def _gated_norm(y, z, nw):
    g = y * _silu(z)
    return g * lax.rsqrt(jnp.mean(g * g, axis=-1, keepdims=True) + SSD_NORM_EPS) * nw


def _decay_col(cs_last_row, g):
    d = jnp.exp(cs_last_row)
    parts = [
        jnp.broadcast_to(d[0:1, g * HEADS_PER_GROUP + j:g * HEADS_PER_GROUP + j + 1], (HEAD_DIM, STATE_DIM))
        for j in range(HEADS_PER_GROUP)
    ]
    return jnp.concatenate(parts, axis=0)


def _ssd_prompt_body(x4, x5, x6, z2, z3, scb, scc, sch, dtr,
                     dtb, alog, dskip, nw, scw, expand,
                     ya_ref, yb_ref, st_ref, sct_ref, ext2):
    q, d = x4.shape
    n_groups = st_ref.shape[1]
    gn = n_groups * STATE_DIM
    kw2 = scw.shape[0]

    @pl.when(pl.program_id(1) == 0)
    def _():
        st_ref[...] = jnp.zeros(st_ref.shape, F32)
        ext2[:, 0:SUBLANE, :] = jnp.zeros((ext2.shape[0], SUBLANE, LANE), F32)

    _slab_store(ext2, SUBLANE, scc[...] * sch[...])
    v = _conv(ext2, 0, d, q, kw2, scw, SUBLANE - (kw2 - 1))
    yb_ref[...] = (scb[...] * v).astype(yb_ref.dtype)
    tail2 = _slab_load(ext2, q, SUBLANE, 0, d)
    _slab_store(ext2, 0, tail2)
    sct_ref[0] = tail2

    dt = _softplus(dtr[...] + dtb[...])
    a = dt * (-jnp.exp(alog[...]))
    row = lax.broadcasted_iota(jnp.int32, (q, q), 0)
    col = lax.broadcasted_iota(jnp.int32, (q, q), 1)
    causal = row >= col
    tri = causal.astype(BF16)
    cs = sum(jnp.dot(tri, part, preferred_element_type=F32) for part in _split3(a))
    cs_t = cs.T
    cs_last = cs[q - 1:q, :]
    dend = jnp.exp(cs_last - cs)
    ecs = jnp.exp(cs)
    stacked = jnp.concatenate([dt, dend, ecs], axis=0)
    st_b = stacked.astype(BF16)

    lane = lax.broadcasted_iota(jnp.int32, (q, LANE), 1)
    groups_per_block = d // GROUP_W

    for g in range(n_groups):
        c0 = g * GROUP_W
        xref = x4 if g < groups_per_block else x5
        bc = (g % groups_per_block) * GROUP_W
        xs = xref[:, bc:bc + GROUP_W]
        e_g = expand[:, c0:c0 + GROUP_W]
        ex = jnp.dot(st_b, e_g, preferred_element_type=F32)
        xdt = xs * ex[0:q]
        xdt_b = xdt.astype(BF16)
        xdd_b = (xdt * ex[q:2 * q]).astype(BF16)
        bg = x6[:, g * STATE_DIM:(g + 1) * STATE_DIM].astype(BF16)
        cg = x6[:, gn + g * STATE_DIM:gn + (g + 1) * STATE_DIM].astype(BF16)
        cbm = lax.dot_general(cg, bg, NT_DIMS, preferred_element_type=F32)
        state = st_ref[0, g]
        y_off = lax.dot_general(cg, state.astype(BF16), NT_DIMS, preferred_element_type=F32)
        y_parts = []
        for j in range(HEADS_PER_GROUP // 2):
            scores = []
            for h in (g * HEADS_PER_GROUP + 2 * j, g * HEADS_PER_GROUP + 2 * j + 1):
                seg = cs[:, h:h + 1] - cs_t[h:h + 1, :]
                dec = jnp.exp(jnp.where(causal, seg, -jnp.inf))
                scores.append((cbm * dec).astype(BF16))
            xp = xdt_b[:, j * LANE:(j + 1) * LANE]
            zero = jnp.zeros_like(xp)
            rhs = jnp.concatenate(
                [jnp.where(lane < HEAD_DIM, xp, zero), jnp.where(lane >= HEAD_DIM, xp, zero)], axis=0)
            y_parts.append(jnp.dot(jnp.concatenate(scores, axis=1), rhs, preferred_element_type=F32))
        y = jnp.concatenate(y_parts, axis=1) + y_off * ex[2 * q:3 * q] + xs * dskip[:, c0:c0 + GROUP_W]
        zref = z2 if g < groups_per_block else z3
        ya_ref[:, c0:c0 + GROUP_W] = _gated_norm(
            y, zref[:, bc:bc + GROUP_W], nw[:, c0:c0 + GROUP_W]).astype(ya_ref.dtype)
        st_ref[0, g] = state * _decay_col(cs_last, g) + lax.dot_general(
            xdd_b, bg, TN_DIMS, preferred_element_type=F32)


def _ssd_prompt(p, dt_raw, consts, n_seq, seq_len, d):
    (_, _, dtb, alog, dskip, nw, scw, expand) = consts
    q = SSD_CHUNK if seq_len % SSD_CHUNK == 0 else seq_len
    nc = seq_len // q
    n_groups = 2 * d // GROUP_W
    t_p = n_seq * seq_len
    assert 2 * n_groups * STATE_DIM == d

    def blk(cidx):
        return pl.BlockSpec((q, d), lambda b, c, cidx=cidx: (b * nc + c, cidx))

    def const(arr):
        return pl.BlockSpec(arr.shape, lambda b, c: (0,) * arr.ndim)

    return pl.pallas_call(
        _ssd_prompt_body,
        grid=(n_seq, nc),
        in_specs=[blk(4), blk(5), blk(6), blk(2), blk(3), blk(7), blk(8), blk(9),
                  pl.BlockSpec((q, LANE), lambda b, c: (b * nc + c, 0)),
                  const(dtb), const(alog), const(dskip), const(nw), const(scw), const(expand)],
        out_specs=[
            pl.BlockSpec((q, 2 * d), lambda b, c: (b * nc + c, 0)),
            pl.BlockSpec((q, d), lambda b, c: (b * nc + c, 0)),
            pl.BlockSpec((1, n_groups, GROUP_W, STATE_DIM), lambda b, c: (b, 0, 0, 0)),
            pl.BlockSpec((1, SUBLANE, d), lambda b, c: (b, 0, 0)),
        ],
        out_shape=[
            jax.ShapeDtypeStruct((t_p, 2 * d), BF16),
            jax.ShapeDtypeStruct((t_p, d), BF16),
            jax.ShapeDtypeStruct((n_seq, n_groups, GROUP_W, STATE_DIM), F32),
            jax.ShapeDtypeStruct((n_seq, SUBLANE, d), F32),
        ],
        scratch_shapes=[pltpu.VMEM((d // LANE, q + SUBLANE, LANE), F32)],
        compiler_params=pltpu.CompilerParams(
            dimension_semantics=("arbitrary", "arbitrary"), vmem_limit_bytes=VMEM_LIMIT),
        name="ssd_prompt",
    )(p, p, p, p, p, p, p, p, dt_raw, dtb, alog, dskip, nw, scw, expand)


SAMPLE_SEQS_PER_STEP = 4


SAMPLE_STATE_SLOTS = 3


def _ssd_sample_body(x4, x5, x6, z2, z3, scb, scc, sch, dtr, ssm_hbm, conv_in, sc_in,
                     cw, cb, dtb, alog, dskip, nw, scw, expand, gsel,
                     ya_ref, yb_ref, ssm_out, conv_out, sc_out, exts, ext2s, ssm_buf, ssm_sems):
    d = x4.shape[1]
    n_groups = ssm_out.shape[1]
    step = pl.program_id(0)
    n_steps = pl.num_programs(0)

    def fetch(t):
        slot = t % SAMPLE_STATE_SLOTS
        return pltpu.make_async_copy(
            ssm_hbm.at[pl.ds(t * SAMPLE_SEQS_PER_STEP, SAMPLE_SEQS_PER_STEP)], ssm_buf.at[slot], ssm_sems.at[slot])

    @pl.when(step == 0)
    def _():
        for t in range(SAMPLE_STATE_SLOTS - 1):
            @pl.when(t < n_steps)
            def _():
                fetch(t).start()

    @pl.when(step + SAMPLE_STATE_SLOTS - 1 < n_steps)
    def _():
        fetch(step + SAMPLE_STATE_SLOTS - 1).start()

    fetch(step).wait()
    ssm_in = ssm_buf.at[step % SAMPLE_STATE_SLOTS]
    gn = n_groups * STATE_DIM
    d_inner = n_groups * GROUP_W
    kw = cw.shape[0]
    kw2 = scw.shape[0]
    q = x4.shape[0] // SAMPLE_SEQS_PER_STEP
    groups_per_block = d // GROUP_W
    first = SUBLANE - (kw - 1)
    first2 = SUBLANE - (kw2 - 1)
    nrep = q * q

    rep_t = lax.broadcasted_iota(jnp.int32, (nrep, LANE), 0) % q
    rep_s = lax.broadcasted_iota(jnp.int32, (nrep, LANE), 0) // q
    rep_causal = rep_t >= rep_s
    row_q = lax.broadcasted_iota(jnp.int32, (q, LANE), 0)

    def rep_rows(m):
        return jnp.concatenate([jnp.broadcast_to(m[s:s + 1], (q, m.shape[1])) for s in range(q)], axis=0)

    def tile_rows(m):
        return jnp.concatenate([m] * q, axis=0)

    yb_rows = []
    ya_rows = [[] for _ in range(n_groups)]
    for sidx in range(SAMPLE_SEQS_PER_STEP):
        r0 = sidx * q
        ext, ext2 = exts.at[sidx], ext2s.at[sidx]
        _slab_store(ext, first, conv_in[sidx])
        _slab_store(ext, SUBLANE, x4[r0:r0 + q, :])
        _slab_store(ext, SUBLANE, x5[r0:r0 + q, :], d)
        _slab_store(ext, SUBLANE, x6[r0:r0 + q, :], 2 * d)
        conv_out[sidx] = _slab_load(ext, SUBLANE + q - (kw - 1), kw - 1, 0, 3 * d)

        _slab_store(ext2, first2, sc_in[sidx])
        _slab_store(ext2, SUBLANE, scc[r0:r0 + q, :] * sch[r0:r0 + q, :])
        yb_rows.append(scb[r0:r0 + q, :] * _conv(ext2, 0, d, q, kw2, scw, first2))
        sc_out[sidx] = _slab_load(ext2, SUBLANE + q - (kw2 - 1), kw2 - 1, 0, d)

        dt = _softplus(dtr[r0:r0 + q, :] + dtb[...])
        a = dt * (-jnp.exp(alog[...]))
        cs = jnp.zeros((q, LANE), F32)
        for r in range(q):
            cs = cs + jnp.where(row_q >= r, jnp.broadcast_to(a[r:r + 1], (q, LANE)), 0.0)
        cs_last = cs[q - 1:q, :]
        dend = jnp.exp(cs_last - cs)
        ecs = jnp.exp(cs)

        bmat = _silu(_conv(ext, 2 * d, 2 * d + gn, q, kw, cw, first) + cb[:, 2 * d:2 * d + gn])
        cmat = _silu(_conv(ext, 2 * d + gn, 3 * d, q, kw, cw, first) + cb[:, 2 * d + gn:3 * d])

        cb_hi, cb_lo = _split2(tile_rows(cmat) * rep_rows(bmat))
        cbh = (jnp.dot(cb_hi, gsel[...], preferred_element_type=F32)
               + jnp.dot(cb_lo, gsel[...], preferred_element_type=F32))
        dec = jnp.exp(jnp.where(rep_causal, tile_rows(cs) - rep_rows(cs), -jnp.inf))
        stacked = jnp.concatenate([dt, dend, ecs], axis=0)
        st_hi = stacked.astype(BF16).astype(F32)
        pad = jnp.zeros((LANE - nrep - 6 * q, LANE), F32)
        lhs = jnp.concatenate([cbh * dec, st_hi, stacked - st_hi, pad], axis=0).astype(BF16)

        xdd_parts = []
        for g in range(n_groups):
            c0 = g * GROUP_W
            xs = _silu(_conv(ext, c0, c0 + GROUP_W, q, kw, cw, first) + cb[:, c0:c0 + GROUP_W])
            ex = jnp.dot(lhs, expand[:, c0:c0 + GROUP_W], preferred_element_type=F32)
            o = nrep
            dtx = ex[o:o + q] + ex[o + 3 * q:o + 4 * q]
            dendx = ex[o + q:o + 2 * q] + ex[o + 4 * q:o + 5 * q]
            ecsx = ex[o + 2 * q:o + 3 * q] + ex[o + 5 * q:o + 6 * q]
            xdt = xs * dtx
            xdd = xdt * dendx
            y = xs * dskip[:, c0:c0 + GROUP_W]
            for s in range(q):
                y = y + ex[s * q:(s + 1) * q] * jnp.broadcast_to(xdt[s:s + 1], (q, GROUP_W))
            state = ssm_in[sidx, g]
            cg = cmat[:, g * STATE_DIM:(g + 1) * STATE_DIM]
            y = y + lax.dot_general(cg, state, NT_DIMS, preferred_element_type=F32) * ecsx
            zref = z2 if g < groups_per_block else z3
            zc = (g % groups_per_block) * GROUP_W
            ya_rows[g].append(_gated_norm(y, zref[r0:r0 + q, zc:zc + GROUP_W], nw[:, c0:c0 + GROUP_W]))
            xdd_parts.append(xdd)

        zrows = jnp.zeros((LANE - q, d_inner), F32)
        xdd_t = jnp.concatenate([jnp.concatenate(xdd_parts, axis=1), zrows], axis=0).T
        for g in range(n_groups):
            b_pad = jnp.concatenate(
                [bmat[:, g * STATE_DIM:(g + 1) * STATE_DIM], jnp.zeros((LANE - q, STATE_DIM), F32)], axis=0)
            ssm_out[sidx, g] = ssm_in[sidx, g] * _decay_col(cs_last, g) + jnp.dot(
                xdd_t[g * GROUP_W:(g + 1) * GROUP_W, :], b_pad, preferred_element_type=F32)

    yb_ref[...] = jnp.concatenate(yb_rows, axis=0).astype(yb_ref.dtype)
    for g in range(n_groups):
        ya_ref[:, g * GROUP_W:(g + 1) * GROUP_W] = jnp.concatenate(ya_rows[g], axis=0).astype(ya_ref.dtype)


def _ssd_sample(p, dt_raw, ssm, conv_state, sc_state, consts, gsel, row0, seq_len, d):
    (cw, cb, dtb, alog, dskip, nw, scw, expand) = consts
    n_seq = ssm.shape[0]
    sp = SAMPLE_SEQS_PER_STEP
    rows = sp * seq_len
    assert seq_len == SUBLANE and n_seq % sp == 0 and row0 % rows == 0
    b0 = row0 // rows

    def blk(cidx):
        return pl.BlockSpec((rows, d), lambda i, cidx=cidx: (b0 + i, cidx))

    def const(arr):
        return pl.BlockSpec(arr.shape, lambda i: (0,) * arr.ndim)

    def per_seq(arr):
        return pl.BlockSpec((sp,) + arr.shape[1:], lambda i: (i,) + (0,) * (arr.ndim - 1))

    return pl.pallas_call(
        _ssd_sample_body,
        grid=(n_seq // sp,),
        in_specs=[blk(4), blk(5), blk(6), blk(2), blk(3), blk(7), blk(8), blk(9),
                  pl.BlockSpec((rows, LANE), lambda i: (b0 + i, 0)),
                  pl.BlockSpec(memory_space=pl.ANY), per_seq(conv_state), per_seq(sc_state),
                  const(cw), const(cb), const(dtb), const(alog), const(dskip), const(nw), const(scw),
                  const(expand), const(gsel)],
        out_specs=[
            pl.BlockSpec((rows, 2 * d), lambda i: (i, 0)),
            pl.BlockSpec((rows, d), lambda i: (i, 0)),
            per_seq(ssm), per_seq(conv_state), per_seq(sc_state),
        ],
        out_shape=[
            jax.ShapeDtypeStruct((n_seq * seq_len, 2 * d), BF16),
            jax.ShapeDtypeStruct((n_seq * seq_len, d), BF16),
            jax.ShapeDtypeStruct(ssm.shape, F32),
            jax.ShapeDtypeStruct(conv_state.shape, F32),
            jax.ShapeDtypeStruct(sc_state.shape, F32),
        ],
        scratch_shapes=[pltpu.VMEM((sp, 3 * d // LANE, 2 * SUBLANE, LANE), F32),
                        pltpu.VMEM((sp, d // LANE, 2 * SUBLANE, LANE), F32),
                        pltpu.VMEM((SAMPLE_STATE_SLOTS, sp) + ssm.shape[1:], F32),
                        pltpu.SemaphoreType.DMA((SAMPLE_STATE_SLOTS,))],
        compiler_params=pltpu.CompilerParams(
            dimension_semantics=("arbitrary",), vmem_limit_bytes=VMEM_LIMIT),
        name="ssd_sample",
    )(p, p, p, p, p, p, p, p, dt_raw, ssm, conv_state, sc_state,
      cw, cb, dtb, alog, dskip, nw, scw, expand, gsel)


def _branch_out_body(yap_ref, ybp_ref, yas_ref, ybs_ref, ga_ref, gb_ref, wa_ref, wb_ref, o_ref,
                     wa_scr, wb_scr, *, prompt_tiles):
    i = pl.program_id(1)

    @pl.when(i == 0)
    def _():
        wa_scr[...] = wa_ref[...].astype(BF16)
        wb_scr[...] = wb_ref[...].astype(BF16)

    def run(refs):
        ya_ref, yb_ref = refs
        pa = jnp.dot(ya_ref[...], wa_scr[...], preferred_element_type=F32)
        pb = jnp.dot(yb_ref[...], wb_scr[...], preferred_element_type=F32)
        merged = jax.nn.sigmoid(ga_ref[...]) * pa + jax.nn.sigmoid(gb_ref[...]) * pb
        o_ref[...] = merged.astype(o_ref.dtype)

    _by_stream(i, prompt_tiles, run, (yap_ref, ybp_ref), (yas_ref, ybs_ref))


def _branch_out(ya_p, yb_p, ya_s, yb_s, p, w_branch_out):
    t_p, d = yb_p.shape
    t_s = yb_s.shape[0]
    tm = _tile(math.gcd(t_p, t_s), 512, 16)
    tn = _tile(d, 512, LANE)
    nj = d // tn
    npt = t_p // tm
    prompt = lambda j, i: (jnp.minimum(i, npt - 1), 0)
    sample = lambda j, i: (jnp.maximum(i - npt, 0), 0)
    once = dict(pipeline_mode=pl.Buffered(1))
    return pl.pallas_call(
        functools.partial(_branch_out_body, prompt_tiles=npt),
        grid=(nj, (t_p + t_s) // tm),
        in_specs=[
            pl.BlockSpec((tm, 2 * d), prompt), pl.BlockSpec((tm, d), prompt),
            pl.BlockSpec((tm, 2 * d), sample), pl.BlockSpec((tm, d), sample),
            pl.BlockSpec((tm, tn), lambda j, i: (i, j)),
            pl.BlockSpec((tm, tn), lambda j, i: (i, nj + j)),
            pl.BlockSpec((None, 2 * d, tn), lambda j, i: (0, 0, j), **once),
            pl.BlockSpec((None, d, tn), lambda j, i: (0, 2, j), **once),
        ],
        out_specs=pl.BlockSpec((tm, tn), lambda j, i: (i, j)),
        out_shape=jax.ShapeDtypeStruct((t_p + t_s, d), BF16),
        scratch_shapes=[pltpu.VMEM((2 * d, tn), BF16), pltpu.VMEM((d, tn), BF16)],
        compiler_params=pltpu.CompilerParams(
            dimension_semantics=("arbitrary", "arbitrary"), vmem_limit_bytes=VMEM_LIMIT),
        name="branch_out",
    )(ya_p, yb_p, ya_s, yb_s, p, p, w_branch_out, w_branch_out)


def _mix_route_body(m_ref, xp_ref, xs_ref, wo_ref, nw_ref, rhi_ref, rlo_ref,
                    x1_ref, h2_ref, route_ref, wts_ref, cnt_ref, wo_scr, *, n_coarse, per_group, prompt_tiles):
    @pl.when(pl.program_id(0) == 0)
    def _():
        cnt_ref[...] = jnp.zeros(cnt_ref.shape, F32)
        wo_scr[...] = wo_ref[...].astype(BF16)

    def run(x_ref):
        x1 = x_ref[...] + jnp.dot(m_ref[...], wo_scr[...], preferred_element_type=F32)
        x1_ref[...] = x1
        h2 = x1 * lax.rsqrt(jnp.mean(x1 * x1, axis=-1, keepdims=True) + NORM_EPS) * nw_ref[...]
        h2_ref[...] = h2
        h_hi, h_lo = _split2(h2)
        logits = (jnp.dot(h_hi, rhi_ref[...], preferred_element_type=F32)
                  + jnp.dot(h_hi, rlo_ref[...], preferred_element_type=F32)
                  + jnp.dot(h_lo, rhi_ref[...], preferred_element_type=F32))

        tm = logits.shape[0]
        n_fine = n_coarse * per_group
        lane = lax.broadcasted_iota(jnp.int32, logits.shape, 1)
        big = jnp.int32(LANE)
        neg = -jnp.inf
        is_c = lane < n_coarse
        lc = jnp.where(is_c, logits, neg)
        mc = jnp.max(lc, axis=-1, keepdims=True)
        grp = jnp.min(jnp.where(is_c & (lc == mc), lane, big), axis=-1, keepdims=True)
        p_grp = 1.0 / jnp.sum(jnp.where(is_c, jnp.exp(lc - mc), 0.0), axis=-1, keepdims=True)
        eidx = lane - n_coarse
        sel = (eidx >= 0) & (eidx < n_fine) & ((eidx // per_group) == grp)
        lf = jnp.where(sel, logits, neg)
        v1 = jnp.max(lf, axis=-1, keepdims=True)
        i1 = jnp.min(jnp.where(sel & (lf == v1), eidx, big), axis=-1, keepdims=True)
        sel2 = sel & (eidx != i1)
        lf2 = jnp.where(sel2, logits, neg)
        v2 = jnp.max(lf2, axis=-1, keepdims=True)
        i2 = jnp.min(jnp.where(sel2 & (lf2 == v2), eidx, big), axis=-1, keepdims=True)
        e2 = jnp.exp(v2 - v1)
        w1 = p_grp / (1.0 + e2)
        w2 = p_grp * e2 / (1.0 + e2)
        wts_ref[...] = jnp.where(lane == 0, w1, jnp.where(lane == 1, w2, 0.0))

        hit1 = eidx == i1
        hit2 = eidx == i2
        hits = hit1.astype(F32) + hit2.astype(F32)
        earlier = (lax.broadcasted_iota(jnp.int32, (tm, tm), 0)
                   > lax.broadcasted_iota(jnp.int32, (tm, tm), 1)).astype(BF16)
        before = jnp.dot(earlier, hits.astype(BF16), preferred_element_type=F32) + cnt_ref[...]
        r1 = jnp.sum(jnp.where(hit1, before, 0.0), axis=-1, keepdims=True).astype(jnp.int32)
        r2 = jnp.sum(jnp.where(hit2, before, 0.0), axis=-1, keepdims=True).astype(jnp.int32)
        cnt_ref[...] += jnp.sum(hits, axis=0, keepdims=True)
        route_ref[...] = jnp.where(lane == 0, i1, jnp.where(lane == 1, i2, jnp.where(
            lane == 2, r1, jnp.where(lane == 3, r2, 0))))

    _by_stream(pl.program_id(0), prompt_tiles, run, xp_ref, xs_ref)


def _mix_route(merged, x_p, x_s, wo, norm_w, r_hi, r_lo, n_coarse, per_group):
    t_p, d = x_p.shape
    T = t_p + x_s.shape[0]
    tm = _tile(math.gcd(t_p, x_s.shape[0]), 256, 16)
    npt = t_p // tm
    row = lambda i: (i, 0)
    fixed = lambda i: (0, 0)
    xp_spec, xs_spec = _stream_specs((tm, d), npt)
    resident = dict(pipeline_mode=pl.Buffered(1))
    return pl.pallas_call(
        functools.partial(_mix_route_body, n_coarse=n_coarse, per_group=per_group, prompt_tiles=npt),
        grid=(T // tm,),
        in_specs=[
            pl.BlockSpec((tm, d), row), xp_spec, xs_spec, pl.BlockSpec((None, d, d), lambda i: (0, 0, 0), **resident),
            pl.BlockSpec((1, d), fixed), pl.BlockSpec((d, LANE), fixed, **resident),
            pl.BlockSpec((d, LANE), fixed, **resident),
        ],
        out_specs=[pl.BlockSpec((tm, d), row), pl.BlockSpec((tm, d), row),
                   pl.BlockSpec((tm, LANE), row), pl.BlockSpec((tm, LANE), row),
                   pl.BlockSpec((1, LANE), fixed)],
        out_shape=[
            jax.ShapeDtypeStruct((T, d), F32), jax.ShapeDtypeStruct((T, d), F32),
            jax.ShapeDtypeStruct((T, LANE), jnp.int32), jax.ShapeDtypeStruct((T, LANE), F32),
            jax.ShapeDtypeStruct((1, LANE), F32),
        ],
        scratch_shapes=[pltpu.VMEM((d, d), BF16)],
        compiler_params=pltpu.CompilerParams(
            dimension_semantics=("arbitrary",), vmem_limit_bytes=VMEM_LIMIT),
        name="mix_route",
    )(merged, x_p, x_s, wo, norm_w, r_hi, r_lo)


MOE_MOVE_TOKENS = 256
MOE_MOVE_UNROLL = 8


def _moe_scatter_body(dest_ref, fill_ref, h_ref, xs_hbm, zeros, sem, zsem):
    tt = h_ref.shape[0]
    step = pl.program_id(0)
    t0 = step * tt
    n_experts = (fill_ref.shape[0] - 1) // 2
    n_blocks = xs_hbm.shape[0] // MOE_ROWS
    pad_bits = [1 << k for k in reversed(range(3, MOE_ROWS.bit_length() - 1))]

    def fills(action):
        def expert(e, carry):
            row, n_pad = fill_ref[e], fill_ref[n_experts + e]
            head = (-row) & (SUBLANE - 1)
            for r in range(SUBLANE - 1):
                @pl.when(r < head)
                def _():
                    action(pltpu.make_async_copy(zeros.at[pl.ds(0, 1)], xs_hbm.at[pl.ds(row + r, 1)], zsem))
            row, n_pad = row + head, n_pad - head
            for bit in pad_bits:
                @pl.when((n_pad & bit) != 0)
                def _():
                    action(pltpu.make_async_copy(
                        zeros.at[pl.ds(0, bit)], xs_hbm.at[pl.ds(pl.multiple_of(row, SUBLANE), bit)], zsem))
                row = row + (n_pad & bit)
            return carry

        def tail(blk, carry):
            action(pltpu.make_async_copy(
                zeros, xs_hbm.at[pl.ds(pl.multiple_of(blk * MOE_ROWS, MOE_ROWS), MOE_ROWS)], zsem))
            return carry

        lax.fori_loop(0, n_experts, expert, 0)
        lax.fori_loop(fill_ref[2 * n_experts], n_blocks, tail, 0)

    @pl.when(step == 0)
    def _():
        zeros[...] = jnp.zeros(zeros.shape, zeros.dtype)
        fills(lambda c: c.start())

    def start(i, carry):
        for k in range(MOE_TOP_K):
            pltpu.make_async_copy(
                h_ref.at[pl.ds(i, 1)], xs_hbm.at[pl.ds(dest_ref[(t0 + i) * MOE_TOP_K + k], 1)], sem
            ).start(priority=k % 2)
        return carry

    lax.fori_loop(0, tt, start, 0, unroll=MOE_MOVE_UNROLL)
    for _ in range(MOE_TOP_K):
        pltpu.make_async_copy(h_ref, xs_hbm.at[pl.ds(0, tt)], sem).wait()

    @pl.when(step == pl.num_programs(0) - 1)
    def _():
        fills(lambda c: c.wait())


def _moe_scatter(dest, fill_rows, h2, n_rows):
    T, d = h2.shape
    tt = _tile(T, MOE_MOVE_TOKENS, SUBLANE)
    return pl.pallas_call(
        _moe_scatter_body,
        grid_spec=pltpu.PrefetchScalarGridSpec(
            num_scalar_prefetch=2,
            grid=(T // tt,),
            in_specs=[pl.BlockSpec((tt, d), lambda i, dr, fr: (i, 0))],
            out_specs=pl.BlockSpec(memory_space=pl.ANY),
            scratch_shapes=[pltpu.VMEM((MOE_ROWS, d), h2.dtype), pltpu.SemaphoreType.DMA(()),
                            pltpu.SemaphoreType.DMA(())],
        ),
        out_shape=jax.ShapeDtypeStruct((n_rows, d), h2.dtype),
        compiler_params=pltpu.CompilerParams(dimension_semantics=("arbitrary",)),
        name="moe_scatter",
    )(dest, fill_rows, h2)


MOE_WEIGHT_PIECES = 8
MOE_WEIGHT_RING = 4


def _moe_ffn_body(sched_ref, order_ref, n_used_ref, x_ref, wg_hbm, wu_hbm, wd_hbm, o_ref,
                  wg_b, wu_b, wd_b, sg, su, sd, sems):
    b = pl.program_id(0)
    hbm = (wg_hbm, wu_hbm, wd_hbm)
    stage = (sg, su, sd)
    resident = (wg_b, wu_b, wd_b)
    n_total = order_ref[order_ref.shape[0] - 1]

    def copies(c):
        e = order_ref[c // MOE_WEIGHT_PIECES]
        piece = c % MOE_WEIGHT_PIECES
        ring = c % MOE_WEIGHT_RING
        out = []
        for k in range(3):
            rows = stage[k].shape[1]
            out.append(pltpu.make_async_copy(
                hbm[k].at[0, e, pl.ds(pl.multiple_of(piece * rows, rows), rows), :],
                stage[k].at[ring], sems.at[ring, k]))
        return out

    def start(c):
        @pl.when(c < n_total)
        def _():
            for cp in copies(c):
                cp.start()

    def land(lo, hi):
        def body(c, carry):
            slot = (c // MOE_WEIGHT_PIECES) % 2
            piece = c % MOE_WEIGHT_PIECES
            ring = c % MOE_WEIGHT_RING
            for k, cp in enumerate(copies(c)):
                cp.wait()
                rows = stage[k].shape[1]
                resident[k][slot, pl.ds(pl.multiple_of(piece * rows, rows), rows), :] = stage[k][ring].astype(BF16)
            start(c + MOE_WEIGHT_RING)
            return carry

        lax.fori_loop(lo, hi, body, 0)

    @pl.when(b < n_used_ref[0])
    def _():
        @pl.when(b == 0)
        def _():
            for c in range(MOE_WEIGHT_RING):
                start(c)
            land(0, MOE_WEIGHT_PIECES)

        slot = sched_ref[0, b]
        xb = x_ref[...].astype(BF16)
        gate = jnp.dot(xb, wg_b[slot], preferred_element_type=F32)
        up = jnp.dot(xb, wu_b[slot], preferred_element_type=F32)
        act = (_silu(gate) * up).astype(BF16)
        o_ref[...] = jnp.dot(act, wd_b[slot], preferred_element_type=F32)
        land(sched_ref[1, b], sched_ref[2, b])

    @pl.when(b >= n_used_ref[0])
    def _():
        o_ref[...] = jnp.zeros(o_ref.shape, o_ref.dtype)


def _moe_ffn(sched, order, n_used, xs, wg, wu, wd):
    R, d = xs.shape
    f = wg.shape[3]
    nb = R // MOE_ROWS
    np_ = MOE_WEIGHT_PIECES
    assert d % np_ == 0 and f % np_ == 0
    any_spec = pl.BlockSpec(memory_space=pl.ANY)
    return pl.pallas_call(
        _moe_ffn_body,
        grid_spec=pltpu.PrefetchScalarGridSpec(
            num_scalar_prefetch=3,
            grid=(nb,),
            in_specs=[pl.BlockSpec((MOE_ROWS, d), lambda b, sc, od, nu: (b, 0)), any_spec, any_spec, any_spec],
            out_specs=pl.BlockSpec((MOE_ROWS, d), lambda b, sc, od, nu: (b, 0)),
            scratch_shapes=[
                pltpu.VMEM((2, d, f), BF16), pltpu.VMEM((2, d, f), BF16), pltpu.VMEM((2, f, d), BF16),
                pltpu.VMEM((MOE_WEIGHT_RING, d // np_, f), F32), pltpu.VMEM((MOE_WEIGHT_RING, d // np_, f), F32),
                pltpu.VMEM((MOE_WEIGHT_RING, f // np_, d), F32),
                pltpu.SemaphoreType.DMA((MOE_WEIGHT_RING, 3)),
            ],
        ),
        out_shape=jax.ShapeDtypeStruct((R, d), F32),
        compiler_params=pltpu.CompilerParams(
            dimension_semantics=("arbitrary",), vmem_limit_bytes=VMEM_LIMIT),
        name="moe_ffn",
    )(sched, order, n_used, xs, wg, wu, wd)


def _moe_combine_body(dest_ref, x1_ref, wts_ref, nw_ref, yb_hbm, op_ref, os_ref, buf, sems, *, prompt_tiles):
    tt = x1_ref.shape[0]
    i = pl.program_id(0)

    def gather(tile, slot):
        def start(r, carry):
            for k in range(MOE_TOP_K):
                pltpu.make_async_copy(
                    yb_hbm.at[pl.ds(dest_ref[(tile * tt + r) * MOE_TOP_K + k], 1)],
                    buf.at[slot, k, pl.ds(r, 1)], sems.at[slot]).start(priority=k % 2)
            return carry

        lax.fori_loop(0, tt, start, 0, unroll=MOE_MOVE_UNROLL)

    @pl.when(i == 0)
    def _():
        gather(0, 0)

    @pl.when(i + 1 < pl.num_programs(0))
    def _():
        gather(i + 1, (i + 1) % 2)

    slot = i % 2
    for k in range(MOE_TOP_K):
        pltpu.make_async_copy(yb_hbm.at[pl.ds(0, tt)], buf.at[slot, k], sems.at[slot]).wait()

    def finish(o_ref):
        w = wts_ref[...]
        x2 = x1_ref[...] + (buf[slot, 0] * w[:, 0:1] + buf[slot, 1] * w[:, 1:2])
        o_ref[...] = x2 * lax.rsqrt(jnp.mean(x2 * x2, axis=-1, keepdims=True) + NORM_EPS) * nw_ref[...]

    _by_stream(i, prompt_tiles, finish, op_ref, os_ref)


def _moe_combine(dest, x1, wts, norm_w, yb, t_p):
    T, d = x1.shape
    tt = _tile(math.gcd(T, t_p), MOE_MOVE_TOKENS, SUBLANE)
    npt = t_p // tt
    op_spec, os_spec = _stream_specs((tt, d), npt)
    return pl.pallas_call(
        functools.partial(_moe_combine_body, prompt_tiles=npt),
        grid_spec=pltpu.PrefetchScalarGridSpec(
            num_scalar_prefetch=1,
            grid=(T // tt,),
            in_specs=[
                pl.BlockSpec((tt, d), lambda i, dr: (i, 0)),
                pl.BlockSpec((tt, LANE), lambda i, dr: (i, 0)),
                pl.BlockSpec((1, d), lambda i, dr: (0, 0)),
                pl.BlockSpec(memory_space=pl.ANY),
            ],
            out_specs=[op_spec, os_spec],
            scratch_shapes=[pltpu.VMEM((2, MOE_TOP_K, tt, d), F32), pltpu.SemaphoreType.DMA((2,))],
        ),
        out_shape=[jax.ShapeDtypeStruct((t_p, d), F32), jax.ShapeDtypeStruct((T - t_p, d), F32)],
        compiler_params=pltpu.CompilerParams(dimension_semantics=("arbitrary",)),
        name="moe_combine",
    )(dest, x1, wts, norm_w, yb)


def _route_rows(eid, rank, counts, n_assign):
    n_experts = counts.shape[0]
    padded = (counts + MOE_ROWS - 1) // MOE_ROWS * MOE_ROWS
    pend = jnp.cumsum(padded)
    pstart = pend - padded
    experts = jnp.arange(n_experts, dtype=jnp.int32)
    onehot = eid[:, :, None] == experts[None, None, :]
    dest = (jnp.sum(jnp.where(onehot, pstart[None, None, :], 0), axis=-1) + rank).astype(jnp.int32)
    n_blocks = -(-n_assign // MOE_ROWS) + n_experts
    blk_start = jnp.arange(n_blocks, dtype=jnp.int32) * MOE_ROWS
    used = blk_start < pend[-1]
    blk_e = jnp.minimum(jnp.sum((blk_start[:, None] >= pend[None, :]).astype(jnp.int32), axis=1), n_experts - 1)
    n_used = (pend[-1] // MOE_ROWS).astype(jnp.int32).reshape(1)

    nonempty = counts > 0
    ordinal = jnp.cumsum(nonempty.astype(jnp.int32)) - 1
    n_nonempty = jnp.sum(nonempty.astype(jnp.int32))
    order = jnp.argsort(jnp.where(nonempty, experts, n_experts + experts)).astype(jnp.int32)
    n_slices = MOE_WEIGHT_PIECES * n_nonempty
    pos = blk_start // MOE_ROWS - (pstart // MOE_ROWS)[blk_e]
    nblk = jnp.maximum((padded // MOE_ROWS)[blk_e], 1)
    has_next = used & (ordinal[blk_e] + 1 < n_nonempty)
    base = MOE_WEIGHT_PIECES * (ordinal[blk_e] + 1)
    first = jnp.where(has_next, base + MOE_WEIGHT_PIECES * pos // nblk, n_slices)
    last = jnp.where(has_next, base + MOE_WEIGHT_PIECES * (pos + 1) // nblk, n_slices)
    slot = jnp.where(used, ordinal[blk_e] % 2, 0)
    sched = jnp.stack([slot, first, last]).astype(jnp.int32)
    order = jnp.concatenate([order, n_slices.reshape(1)]).astype(jnp.int32)
    fill_rows = jnp.concatenate([pstart + counts, padded - counts, n_used]).astype(jnp.int32)
    return dest.reshape(-1), sched, order, fill_rows, n_used, n_blocks


def _pad_lanes(v, fill=0.0):
    return jnp.pad(v.astype(F32), (0, LANE - v.shape[0]), constant_values=fill).reshape(1, LANE)


def kernel(x_prompt, x_sample, state_ssm, state_ssd_conv, state_short_conv, norm_mixer, w_in, ssd_conv_w,
           ssd_conv_b, ssd_dt_bias, ssd_a_log, ssd_d, ssd_norm, sc_conv_w, w_branch_out, w_out, norm_ffn,
           w_router_coarse, w_router_fine, w_expert_gate, w_expert_up, w_expert_down, norm_final):
    depth = w_in.shape[0]
    assert depth == 1
    n_p, seq_p, d = x_prompt.shape
    n_s, seq_s, _ = x_sample.shape
    d_inner = 2 * d
    n_heads = d_inner // HEAD_DIM
    n_groups = d_inner // GROUP_W
    gn = n_groups * STATE_DIM
    conv_dim = d_inner + 2 * gn
    assert conv_dim == 3 * d and n_heads <= LANE and ssd_conv_w.shape[2] == conv_dim
    t_p, t_s = n_p * seq_p, n_s * seq_s
    n_coarse = w_router_coarse.shape[2]
    n_experts = w_router_fine.shape[2]
    assert n_coarse + n_experts <= LANE

    off_dt = 2 * d + d_inner + conv_dim
    off_sc = off_dt + n_heads
    w_in_t = jnp.swapaxes(w_in, 1, 2)
    w_dt = jnp.pad(w_in_t[0, off_dt:off_sc, :].T, ((0, 0), (0, LANE - n_heads))).astype(BF16)
    head_of_col = jnp.arange(d_inner, dtype=jnp.int32) // HEAD_DIM
    expand = (jnp.arange(LANE, dtype=jnp.int32)[:, None] == head_of_col[None, :]).astype(BF16)
    group_of_n = jnp.arange(gn, dtype=jnp.int32) // STATE_DIM
    group_of_head = jnp.arange(LANE, dtype=jnp.int32) // HEADS_PER_GROUP
    gsel = ((group_of_n[:, None] == group_of_head[None, :])
            & (jnp.arange(LANE)[None, :] < n_heads)).astype(BF16)
    consts = (
        ssd_conv_w[0], ssd_conv_b[0].reshape(1, conv_dim), _pad_lanes(ssd_dt_bias[0]), _pad_lanes(ssd_a_log[0]),
        jnp.repeat(ssd_d[0].astype(F32), HEAD_DIM).reshape(1, d_inner), ssd_norm[0].reshape(1, d_inner),
        sc_conv_w[0], expand,
    )
    w_router = jnp.pad(jnp.concatenate([w_router_coarse[0], w_router_fine[0]], axis=1),
                       ((0, 0), (0, LANE - n_coarse - n_experts)))
    r_hi = w_router.astype(BF16)
    r_lo = (w_router - r_hi.astype(F32)).astype(BF16)

    x_p = x_prompt.reshape(t_p, d)
    x_s = x_sample.reshape(t_s, d)
    h, dt_raw = _prenorm(x_p, x_s, norm_mixer[0].reshape(1, d), w_dt)
    p, tails = _inproj(h, w_in_t[0], off_dt, off_sc, consts[0], consts[1], t_p, seq_p)
    ya_p, yb_p, p_ssm, p_sc_tail = _ssd_prompt(p, dt_raw, consts, n_p, seq_p, d)
    ya_s, yb_s, s_ssm, s_conv, s_sc = _ssd_sample(
        p, dt_raw, state_ssm[0].reshape(n_s, n_groups, GROUP_W, STATE_DIM), state_ssd_conv[0],
        state_short_conv[0], consts, gsel, t_p, seq_s, d)
    merged = _branch_out(ya_p, yb_p, ya_s, yb_s, p, w_branch_out)
    x1, h2, route, wts, counts = _mix_route(merged, x_p, x_s, w_out, norm_ffn[0].reshape(1, d), r_hi, r_lo,
                                            n_coarse, n_experts // n_coarse)

    n_assign = (t_p + t_s) * MOE_TOP_K
    dest, sched, order, fill_rows, n_used, n_blocks = _route_rows(
        route[:, 0:MOE_TOP_K], route[:, MOE_TOP_K:2 * MOE_TOP_K],
        counts[0, n_coarse:n_coarse + n_experts].astype(jnp.int32), n_assign)
    xs = _moe_scatter(dest, fill_rows, h2, n_blocks * MOE_ROWS)
    yrows = _moe_ffn(sched, order, n_used, xs, w_expert_gate, w_expert_up, w_expert_down)
    out_p, out_s = _moe_combine(dest, x1, wts, norm_final.reshape(1, d), yrows, t_p)

    kw = ssd_conv_w.shape[1]
    kw2 = sc_conv_w.shape[1]
    tiles_per_seq = (tails.shape[0] // SUBLANE) * seq_p // (t_p + t_s)
    seq_tails = tails.reshape(-1, SUBLANE, tails.shape[1])[tiles_per_seq - 1:n_p * tiles_per_seq:tiles_per_seq]
    p_conv = seq_tails[:, SUBLANE - (kw - 1):, 4 * d:7 * d]
    return (
        out_p.reshape(n_p, seq_p, d),
        out_s.reshape(n_s, seq_s, d),
        p_ssm.reshape(1, n_p, n_heads, HEAD_DIM, STATE_DIM),
        p_conv[None],
        p_sc_tail[:, SUBLANE - (kw2 - 1):, :][None],
        s_ssm.reshape(1, n_s, n_heads, HEAD_DIM, STATE_DIM),
        s_conv[None],
        s_sc[None],
    )
```

```python
import functools
import math

import jax
import jax.numpy as jnp
from jax import lax
from jax.experimental import pallas as pl
from jax.experimental.pallas import tpu as pltpu

F32 = jnp.float32
BF16 = jnp.bfloat16

NORM_EPS = 1e-6
SSD_NORM_EPS = 1e-5
HEAD_DIM = 64
STATE_DIM = 128
HEADS_PER_GROUP = 8
GROUP_W = HEADS_PER_GROUP * HEAD_DIM
SSD_CHUNK = 128
LOG2_E = math.log2(math.e)
MOE_TOP_K = 2
MOE_ROWS = 256

LANE = 128
SUBLANE = 8
VMEM_LIMIT = 56 * 1024 * 1024

NT_DIMS = (((1,), (1,)), ((), ()))
TN_DIMS = (((0,), (0,)), ((), ()))


def _tile(n, target, align):
    best = None
    for t in range(align, min(n, target) + 1, align):
        if n % t == 0:
            best = t
    assert best is not None, (n, target, align)
    return best


def _split2(v):
    hi = v.astype(BF16)
    lo = (v - hi.astype(F32)).astype(BF16)
    return hi, lo


def _split3(v):
    hi = v.astype(BF16)
    r = v - hi.astype(F32)
    mid = r.astype(BF16)
    lo = (r - mid.astype(F32)).astype(BF16)
    return hi, mid, lo


def _softplus(x):
    return jnp.maximum(x, 0.0) + jnp.log1p(jnp.exp(-jnp.abs(x)))


def _silu(x):
    return x * jax.nn.sigmoid(x)


def _by_stream(tile, prompt_tiles, fn, prompt_ref, sample_ref):
    @pl.when(tile < prompt_tiles)
    def _():
        fn(prompt_ref)

    @pl.when(tile >= prompt_tiles)
    def _():
        fn(sample_ref)


def _stream_specs(block, prompt_tiles, **kwargs):
    pad = (0,) * (len(block) - 1)
    prompt = lambda i, *_: (jnp.minimum(i, prompt_tiles - 1),) + pad
    sample = lambda i, *_: (jnp.maximum(i - prompt_tiles, 0),) + pad
    return pl.BlockSpec(block, prompt, **kwargs), pl.BlockSpec(block, sample, **kwargs)


def _prenorm_body(xp_ref, xs_ref, nw_ref, wdt_ref, h_ref, dt_ref, *, prompt_tiles):
    def run(x_ref):
        x = x_ref[...]
        h = x * lax.rsqrt(jnp.mean(x * x, axis=-1, keepdims=True) + NORM_EPS) * nw_ref[...]
        hb = h.astype(BF16)
        h_ref[...] = hb
        dt_ref[...] = jnp.dot(hb, wdt_ref[...], preferred_element_type=F32)

    _by_stream(pl.program_id(0), prompt_tiles, run, xp_ref, xs_ref)


def _prenorm(x_p, x_s, norm_w, w_dt):
    t_p, D = x_p.shape
    T = t_p + x_s.shape[0]
    tm = _tile(math.gcd(t_p, x_s.shape[0]), 512, 16)
    npt = t_p // tm
    xp_spec, xs_spec = _stream_specs((tm, D), npt)
    return pl.pallas_call(
        functools.partial(_prenorm_body, prompt_tiles=npt),
        grid=(T // tm,),
        in_specs=[xp_spec, xs_spec, pl.BlockSpec((1, D), lambda i: (0, 0)),
                  pl.BlockSpec((D, LANE), lambda i: (0, 0))],
        out_specs=[pl.BlockSpec((tm, D), lambda i: (i, 0)), pl.BlockSpec((tm, LANE), lambda i: (i, 0))],
        out_shape=[jax.ShapeDtypeStruct((T, D), BF16), jax.ShapeDtypeStruct((T, LANE), F32)],
        compiler_params=pltpu.CompilerParams(
            dimension_semantics=("arbitrary",), vmem_limit_bytes=VMEM_LIMIT),
        name="prenorm",
    )(x_p, x_s, norm_w, w_dt)


INPROJ_CONV_SLABS = 4


def _inproj_body(h_ref, w_ref, cw_ref, cb_ref, p_ref, tail_ref, w_scr, cext,
                 *, blocks_per_d, prompt_tiles, tiles_per_seq):
    j = pl.program_id(0)
    i = pl.program_id(1)
    tm, tn = p_ref.shape
    kw = cw_ref.shape[0]

    @pl.when(i == 0)
    def _():
        w_scr[...] = w_ref[...].astype(BF16)

    is_conv = (j >= 4 * blocks_per_d) & (j < 7 * blocks_per_d) & (i < prompt_tiles)

    @pl.when(jnp.logical_not(is_conv))
    def _():
        raw = lax.dot_general(h_ref[...], w_scr[...], NT_DIMS, preferred_element_type=F32)
        p_ref[...] = raw
        tail_ref[...] = raw[tm - SUBLANE:tm, :]

    @pl.when(is_conv)
    def _():
        @pl.when(i % tiles_per_seq == 0)
        def _():
            cext[:, 0:SUBLANE, :] = jnp.zeros((cext.shape[0], SUBLANE, LANE), F32)

        ws = tn // INPROJ_CONV_SLABS
        first = SUBLANE - (kw - 1)
        for k in range(INPROJ_CONV_SLABS):
            raw = lax.dot_general(h_ref[...], w_scr[k * ws:(k + 1) * ws, :], NT_DIMS, preferred_element_type=F32)
            for s in range(k * ws // LANE, (k + 1) * ws // LANE):
                lo, hi = s * LANE, (s + 1) * LANE
                cext[s, SUBLANE:SUBLANE + tm, :] = raw[:, lo - k * ws:hi - k * ws]
                acc = cb_ref[:, lo:hi]
                for t in range(kw):
                    acc = acc + cw_ref[t:t + 1, lo:hi] * cext[s, first + t:first + t + tm, :]
                p_ref[:, lo:hi] = _silu(acc)
                tail = cext[s, tm:tm + SUBLANE, :]
                tail_ref[:, lo:hi] = tail
                cext[s, 0:SUBLANE, :] = tail


def _inproj(h, w_in_t, n_head_cols, tail_start, cw, cb, t_p, seq_p):
    T, D = h.shape
    n_tail = w_in_t.shape[0] - tail_start
    tm = _tile(math.gcd(seq_p, T - t_p), 1024, 16)
    tn = _tile(math.gcd(D, n_tail), 1024, LANE)
    main_tiles = n_head_cols // tn
    n_out = n_head_cols + n_tail
    bpd = D // tn
    assert tail_start % SUBLANE == 0 and n_head_cols % tn == 0
    conv_block = lambda j, i: (0, jnp.clip(j - 4 * bpd, 0, 3 * bpd - 1))
    w_row = lambda j, i: (pl.multiple_of(
        jnp.where(j < main_tiles, j * tn, tail_start + (j - main_tiles) * tn), SUBLANE), 0)
    return pl.pallas_call(
        functools.partial(_inproj_body, blocks_per_d=bpd, prompt_tiles=t_p // tm, tiles_per_seq=seq_p // tm),
        grid=(n_out // tn, T // tm),
        in_specs=[
            pl.BlockSpec((tm, D), lambda j, i: (i, 0)),
            pl.BlockSpec((pl.Element(tn), pl.Element(D)), w_row),
            pl.BlockSpec((cw.shape[0], tn), conv_block),
            pl.BlockSpec((1, tn), conv_block),
        ],
        out_specs=[pl.BlockSpec((tm, tn), lambda j, i: (i, j)),
                   pl.BlockSpec((SUBLANE, tn), lambda j, i: (i, j))],
        out_shape=[jax.ShapeDtypeStruct((T, n_out), F32),
                   jax.ShapeDtypeStruct((T // tm * SUBLANE, n_out), F32)],
        scratch_shapes=[pltpu.VMEM((tn, D), BF16), pltpu.VMEM((tn // LANE, tm + SUBLANE, LANE), F32)],
        compiler_params=pltpu.CompilerParams(
            dimension_semantics=("arbitrary", "arbitrary"), vmem_limit_bytes=VMEM_LIMIT),
        name="inproj",
    )(h, w_in_t, cw, cb)


def _slab_store(ext, row0, value, col0=0):
    for s in range(value.shape[1] // LANE):
        ext[col0 // LANE + s, row0:row0 + value.shape[0], :] = value[:, s * LANE:(s + 1) * LANE]


def _slab_load(ext, row0, rows, lo, hi):
    return jnp.concatenate([ext[s, row0:row0 + rows, :] for s in range(lo // LANE, hi // LANE)], axis=1)


def _conv(ext, lo, hi, q, width, w_ref, first):
    parts = []
    for s in range(lo // LANE, hi // LANE):
        acc = None
        for k in range(width):
            term = w_ref[k:k + 1, s * LANE:(s + 1) * LANE] * ext[s, first + k:first + k + q, :]
            acc = term if acc is None else acc + term
        parts.append(acc)
    return jnp.concatenate(parts, axis=1)


def _gated_norm(y, z, nw):
    g = y * _silu(z)
    return g * lax.rsqrt(jnp.mean(g * g, axis=-1, keepdims=True) + SSD_NORM_EPS) * nw


def _decay_col(cs_last_row, g):
    d = jnp.exp(cs_last_row)
    parts = [
        jnp.broadcast_to(d[0:1, g * HEADS_PER_GROUP + j:g * HEADS_PER_GROUP + j + 1], (HEAD_DIM, STATE_DIM))
        for j in range(HEADS_PER_GROUP)
    ]
    return jnp.concatenate(parts, axis=0)


def _ssd_prompt_body(x4, x5, x6, z2, z3, scb, scc, sch, dtr,
                     dtb, alog, dskip, nw, scw, expand,
                     ya_ref, yb_ref, st_ref, sct_ref, ext2):
    q, d = x4.shape
    n_groups = st_ref.shape[1]
    gn = n_groups * STATE_DIM
    kw2 = scw.shape[0]

    @pl.when(pl.program_id(1) == 0)
    def _():
        st_ref[...] = jnp.zeros(st_ref.shape, F32)
        ext2[:, 0:SUBLANE, :] = jnp.zeros((ext2.shape[0], SUBLANE, LANE), F32)

    _slab_store(ext2, SUBLANE, scc[...] * sch[...])
    v = _conv(ext2, 0, d, q, kw2, scw, SUBLANE - (kw2 - 1))
    yb_ref[...] = (scb[...] * v).astype(yb_ref.dtype)
    tail2 = _slab_load(ext2, q, SUBLANE, 0, d)
    _slab_store(ext2, 0, tail2)
    sct_ref[0] = tail2

    dt = _softplus(dtr[...] + dtb[...])
    a = dt * (-jnp.exp(alog[...]))
    row = lax.broadcasted_iota(jnp.int32, (q, q), 0)
    col = lax.broadcasted_iota(jnp.int32, (q, q), 1)
    causal = row >= col
    tri = causal.astype(BF16)
    cs = sum(jnp.dot(tri, part, preferred_element_type=F32) for part in _split3(a))
    cs2 = cs * LOG2_E
    cs2_t = cs2.T
    cs_last = cs[q - 1:q, :]
    dend = jnp.exp(cs_last - cs)
    ecs = jnp.exp(cs)
    stacked = jnp.concatenate([dt, dend, ecs], axis=0)
    st_b = stacked.astype(BF16)

    lane = lax.broadcasted_iota(jnp.int32, (q, LANE), 1)
    groups_per_block = d // GROUP_W

    for g in range(n_groups):
        c0 = g * GROUP_W
        xref = x4 if g < groups_per_block else x5
        bc = (g % groups_per_block) * GROUP_W
        xs = xref[:, bc:bc + GROUP_W]
        e_g = expand[:, c0:c0 + GROUP_W]
        ex = jnp.dot(st_b, e_g, preferred_element_type=F32)
        xdt = xs * ex[0:q]
        xdt_b = xdt.astype(BF16)
        xdd_b = (xdt * ex[q:2 * q]).astype(BF16)
        bg = x6[:, g * STATE_DIM:(g + 1) * STATE_DIM].astype(BF16)
        cg = x6[:, gn + g * STATE_DIM:gn + (g + 1) * STATE_DIM].astype(BF16)
        cbm = lax.dot_general(cg, bg, NT_DIMS, preferred_element_type=F32)
        state = st_ref[0, g]
        y_off = lax.dot_general(cg, state.astype(BF16), NT_DIMS, preferred_element_type=F32)
        y_parts = []
        for j in range(HEADS_PER_GROUP // 2):
            scores = []
            for h in (g * HEADS_PER_GROUP + 2 * j, g * HEADS_PER_GROUP + 2 * j + 1):
                seg = cs2[:, h:h + 1] - cs2_t[h:h + 1, :]
                dec = jnp.exp2(jnp.where(causal, seg, -jnp.inf))
                scores.append((cbm * dec).astype(BF16))
            xp = xdt_b[:, j * LANE:(j + 1) * LANE]
            zero = jnp.zeros_like(xp)
            rhs = jnp.concatenate(
                [jnp.where(lane < HEAD_DIM, xp, zero), jnp.where(lane >= HEAD_DIM, xp, zero)], axis=0)
            y_parts.append(jnp.dot(jnp.concatenate(scores, axis=1), rhs, preferred_element_type=F32))
        y = jnp.concatenate(y_parts, axis=1) + y_off * ex[2 * q:3 * q] + xs * dskip[:, c0:c0 + GROUP_W]
        zref = z2 if g < groups_per_block else z3
        ya_ref[:, c0:c0 + GROUP_W] = _gated_norm(
            y, zref[:, bc:bc + GROUP_W], nw[:, c0:c0 + GROUP_W]).astype(ya_ref.dtype)
        st_ref[0, g] = state * _decay_col(cs_last, g) + lax.dot_general(
            xdd_b, bg, TN_DIMS, preferred_element_type=F32)


def _ssd_prompt(p, dt_raw, consts, n_seq, seq_len, d):
    (_, _, dtb, alog, dskip, nw, scw, expand) = consts
    q = SSD_CHUNK if seq_len % SSD_CHUNK == 0 else seq_len
    nc = seq_len // q
    n_groups = 2 * d // GROUP_W
    t_p = n_seq * seq_len
    assert 2 * n_groups * STATE_DIM == d

    def blk(cidx):
        return pl.BlockSpec((q, d), lambda b, c, cidx=cidx: (b * nc + c, cidx))

    def const(arr):
        return pl.BlockSpec(arr.shape, lambda b, c: (0,) * arr.ndim)

    return pl.pallas_call(
        _ssd_prompt_body,
        grid=(n_seq, nc),
        in_specs=[blk(4), blk(5), blk(6), blk(2), blk(3), blk(7), blk(8), blk(9),
                  pl.BlockSpec((q, LANE), lambda b, c: (b * nc + c, 0)),
                  const(dtb), const(alog), const(dskip), const(nw), const(scw), const(expand)],
        out_specs=[
            pl.BlockSpec((q, 2 * d), lambda b, c: (b * nc + c, 0)),
            pl.BlockSpec((q, d), lambda b, c: (b * nc + c, 0)),
            pl.BlockSpec((1, n_groups, GROUP_W, STATE_DIM), lambda b, c: (b, 0, 0, 0)),
            pl.BlockSpec((1, SUBLANE, d), lambda b, c: (b, 0, 0)),
        ],
        out_shape=[
            jax.ShapeDtypeStruct((t_p, 2 * d), BF16),
            jax.ShapeDtypeStruct((t_p, d), BF16),
            jax.ShapeDtypeStruct((n_seq, n_groups, GROUP_W, STATE_DIM), F32),
            jax.ShapeDtypeStruct((n_seq, SUBLANE, d), F32),
        ],
        scratch_shapes=[pltpu.VMEM((d // LANE, q + SUBLANE, LANE), F32)],
        compiler_params=pltpu.CompilerParams(
            dimension_semantics=("arbitrary", "arbitrary"), vmem_limit_bytes=VMEM_LIMIT),
        name="ssd_prompt",
    )(p, p, p, p, p, p, p, p, dt_raw, dtb, alog, dskip, nw, scw, expand)


SAMPLE_SEQS_PER_STEP = 2


SAMPLE_STATE_SLOTS = 3


def _ssd_sample_body(x4, x5, x6, z2, z3, scb, scc, sch, dtr, ssm_hbm, conv_in, sc_in,
                     cw, cb, dtb, alog, dskip, nw, scw, expand, gsel,
                     ya_ref, yb_ref, ssm_out, conv_out, sc_out, exts, ext2s, ssm_buf, ssm_sems):
    d = x4.shape[1]
    n_groups = ssm_out.shape[1]
    step = pl.program_id(0)
    n_steps = pl.num_programs(0)

    def fetch(t):
        slot = t % SAMPLE_STATE_SLOTS
        return pltpu.make_async_copy(
            ssm_hbm.at[pl.ds(t * SAMPLE_SEQS_PER_STEP, SAMPLE_SEQS_PER_STEP)], ssm_buf.at[slot], ssm_sems.at[slot])

    @pl.when(step == 0)
    def _():
        for t in range(SAMPLE_STATE_SLOTS - 1):
            @pl.when(t < n_steps)
            def _():
                fetch(t).start()

    @pl.when(step + SAMPLE_STATE_SLOTS - 1 < n_steps)
    def _():
        fetch(step + SAMPLE_STATE_SLOTS - 1).start()

    fetch(step).wait()
    ssm_in = ssm_buf.at[step % SAMPLE_STATE_SLOTS]
    gn = n_groups * STATE_DIM
    d_inner = n_groups * GROUP_W
    kw = cw.shape[0]
    kw2 = scw.shape[0]
    q = x4.shape[0] // SAMPLE_SEQS_PER_STEP
    groups_per_block = d // GROUP_W
    first = SUBLANE - (kw - 1)
    first2 = SUBLANE - (kw2 - 1)
    nrep = q * q

    rep_t = lax.broadcasted_iota(jnp.int32, (nrep, LANE), 0) % q
    rep_s = lax.broadcasted_iota(jnp.int32, (nrep, LANE), 0) // q
    rep_causal = rep_t >= rep_s
    row_q = lax.broadcasted_iota(jnp.int32, (q, LANE), 0)

    def rep_rows(m):
        return jnp.concatenate([jnp.broadcast_to(m[s:s + 1], (q, m.shape[1])) for s in range(q)], axis=0)

    def tile_rows(m):
        return jnp.concatenate([m] * q, axis=0)

    yb_rows = []
    ya_rows = [[] for _ in range(n_groups)]
    for sidx in range(SAMPLE_SEQS_PER_STEP):
        r0 = sidx * q
        ext, ext2 = exts.at[sidx], ext2s.at[sidx]
        _slab_store(ext, first, conv_in[sidx])
        _slab_store(ext, SUBLANE, x4[r0:r0 + q, :])
        _slab_store(ext, SUBLANE, x5[r0:r0 + q, :], d)
        _slab_store(ext, SUBLANE, x6[r0:r0 + q, :], 2 * d)
        conv_out[sidx] = _slab_load(ext, SUBLANE + q - (kw - 1), kw - 1, 0, 3 * d)

        _slab_store(ext2, first2, sc_in[sidx])
        _slab_store(ext2, SUBLANE, scc[r0:r0 + q, :] * sch[r0:r0 + q, :])
        yb_rows.append(scb[r0:r0 + q, :] * _conv(ext2, 0, d, q, kw2, scw, first2))
        sc_out[sidx] = _slab_load(ext2, SUBLANE + q - (kw2 - 1), kw2 - 1, 0, d)

        dt = _softplus(dtr[r0:r0 + q, :] + dtb[...])
        a = dt * (-jnp.exp(alog[...]))
        cs = jnp.zeros((q, LANE), F32)
        for r in range(q):
            cs = cs + jnp.where(row_q >= r, jnp.broadcast_to(a[r:r + 1], (q, LANE)), 0.0)
        cs_last = cs[q - 1:q, :]
        dend = jnp.exp(cs_last - cs)
        ecs = jnp.exp(cs)

        bmat = _silu(_conv(ext, 2 * d, 2 * d + gn, q, kw, cw, first) + cb[:, 2 * d:2 * d + gn])
        cmat = _silu(_conv(ext, 2 * d + gn, 3 * d, q, kw, cw, first) + cb[:, 2 * d + gn:3 * d])

        cb_hi, cb_lo = _split2(tile_rows(cmat) * rep_rows(bmat))
        cbh = (jnp.dot(cb_hi, gsel[...], preferred_element_type=F32)
               + jnp.dot(cb_lo, gsel[...], preferred_element_type=F32))
        dec = jnp.exp(jnp.where(rep_causal, tile_rows(cs) - rep_rows(cs), -jnp.inf))
        stacked = jnp.concatenate([dt, dend, ecs], axis=0)
        st_hi = stacked.astype(BF16).astype(F32)
        pad = jnp.zeros((LANE - nrep - 6 * q, LANE), F32)
        lhs = jnp.concatenate([cbh * dec, st_hi, stacked - st_hi, pad], axis=0).astype(BF16)

        xdd_parts = []
        for g in range(n_groups):
            c0 = g * GROUP_W
            xs = _silu(_conv(ext, c0, c0 + GROUP_W, q, kw, cw, first) + cb[:, c0:c0 + GROUP_W])
            ex = jnp.dot(lhs, expand[:, c0:c0 + GROUP_W], preferred_element_type=F32)
            o = nrep
            dtx = ex[o:o + q] + ex[o + 3 * q:o + 4 * q]
            dendx = ex[o + q:o + 2 * q] + ex[o + 4 * q:o + 5 * q]
            ecsx = ex[o + 2 * q:o + 3 * q] + ex[o + 5 * q:o + 6 * q]
            xdt = xs * dtx
            xdd = xdt * dendx
            y = xs * dskip[:, c0:c0 + GROUP_W]
            for s in range(q):
                y = y + ex[s * q:(s + 1) * q] * jnp.broadcast_to(xdt[s:s + 1], (q, GROUP_W))
            state = ssm_in[sidx, g]
            cg = cmat[:, g * STATE_DIM:(g + 1) * STATE_DIM]
            y = y + lax.dot_general(cg, state, NT_DIMS, preferred_element_type=F32) * ecsx
            zref = z2 if g < groups_per_block else z3
            zc = (g % groups_per_block) * GROUP_W
            ya_rows[g].append(_gated_norm(y, zref[r0:r0 + q, zc:zc + GROUP_W], nw[:, c0:c0 + GROUP_W]))
            xdd_parts.append(xdd)

        zrows = jnp.zeros((LANE - q, d_inner), F32)
        xdd_t = jnp.concatenate([jnp.concatenate(xdd_parts, axis=1), zrows], axis=0).T
        for g in range(n_groups):
            b_pad = jnp.concatenate(
                [bmat[:, g * STATE_DIM:(g + 1) * STATE_DIM], jnp.zeros((LANE - q, STATE_DIM), F32)], axis=0)
            ssm_out[sidx, g] = ssm_in[sidx, g] * _decay_col(cs_last, g) + jnp.dot(
                xdd_t[g * GROUP_W:(g + 1) * GROUP_W, :], b_pad, preferred_element_type=F32)

    yb_ref[...] = jnp.concatenate(yb_rows, axis=0).astype(yb_ref.dtype)
    for g in range(n_groups):
        ya_ref[:, g * GROUP_W:(g + 1) * GROUP_W] = jnp.concatenate(ya_rows[g], axis=0).astype(ya_ref.dtype)


def _ssd_sample(p, dt_raw, ssm, conv_state, sc_state, consts, gsel, row0, seq_len, d):
    (cw, cb, dtb, alog, dskip, nw, scw, expand) = consts
    n_seq = ssm.shape[0]
    sp = SAMPLE_SEQS_PER_STEP
    rows = sp * seq_len
    assert seq_len == SUBLANE and n_seq % sp == 0 and row0 % rows == 0
    b0 = row0 // rows

    def blk(cidx):
        return pl.BlockSpec((rows, d), lambda i, cidx=cidx: (b0 + i, cidx))

    def const(arr):
        return pl.BlockSpec(arr.shape, lambda i: (0,) * arr.ndim)

    def per_seq(arr):
        return pl.BlockSpec((sp,) + arr.shape[1:], lambda i: (i,) + (0,) * (arr.ndim - 1))

    return pl.pallas_call(
        _ssd_sample_body,
        grid=(n_seq // sp,),
        in_specs=[blk(4), blk(5), blk(6), blk(2), blk(3), blk(7), blk(8), blk(9),
                  pl.BlockSpec((rows, LANE), lambda i: (b0 + i, 0)),
                  pl.BlockSpec(memory_space=pl.ANY), per_seq(conv_state), per_seq(sc_state),
                  const(cw), const(cb), const(dtb), const(alog), const(dskip), const(nw), const(scw),
                  const(expand), const(gsel)],
        out_specs=[
            pl.BlockSpec((rows, 2 * d), lambda i: (i, 0)),
            pl.BlockSpec((rows, d), lambda i: (i, 0)),
            per_seq(ssm), per_seq(conv_state), per_seq(sc_state),
        ],
        out_shape=[
            jax.ShapeDtypeStruct((n_seq * seq_len, 2 * d), BF16),
            jax.ShapeDtypeStruct((n_seq * seq_len, d), BF16),
            jax.ShapeDtypeStruct(ssm.shape, F32),
            jax.ShapeDtypeStruct(conv_state.shape, F32),
            jax.ShapeDtypeStruct(sc_state.shape, F32),
        ],
        scratch_shapes=[pltpu.VMEM((sp, 3 * d // LANE, 2 * SUBLANE, LANE), F32),
                        pltpu.VMEM((sp, d // LANE, 2 * SUBLANE, LANE), F32),
                        pltpu.VMEM((SAMPLE_STATE_SLOTS, sp) + ssm.shape[1:], F32),
                        pltpu.SemaphoreType.DMA((SAMPLE_STATE_SLOTS,))],
        compiler_params=pltpu.CompilerParams(
            dimension_semantics=("arbitrary",), vmem_limit_bytes=VMEM_LIMIT),
        name="ssd_sample",
    )(p, p, p, p, p, p, p, p, dt_raw, ssm, conv_state, sc_state,
      cw, cb, dtb, alog, dskip, nw, scw, expand, gsel)


def _branch_out_body(yap_ref, ybp_ref, yas_ref, ybs_ref, ga_ref, gb_ref, wa_ref, wb_ref, o_ref,
                     wa_scr, wb_scr, *, prompt_tiles):
    i = pl.program_id(1)

    @pl.when(i == 0)
    def _():
        wa_scr[...] = wa_ref[...].astype(BF16)
        wb_scr[...] = wb_ref[...].astype(BF16)

    def run(refs):
        ya_ref, yb_ref = refs
        pa = jnp.dot(ya_ref[...], wa_scr[...], preferred_element_type=F32)
        pb = jnp.dot(yb_ref[...], wb_scr[...], preferred_element_type=F32)
        merged = jax.nn.sigmoid(ga_ref[...]) * pa + jax.nn.sigmoid(gb_ref[...]) * pb
        o_ref[...] = merged.astype(o_ref.dtype)

    _by_stream(i, prompt_tiles, run, (yap_ref, ybp_ref), (yas_ref, ybs_ref))


def _branch_out(ya_p, yb_p, ya_s, yb_s, p, w_branch_out):
    t_p, d = yb_p.shape
    t_s = yb_s.shape[0]
    tm = _tile(math.gcd(t_p, t_s), 512, 16)
    tn = _tile(d, 512, LANE)
    nj = d // tn
    npt = t_p // tm
    prompt = lambda j, i: (jnp.minimum(i, npt - 1), 0)
    sample = lambda j, i: (jnp.maximum(i - npt, 0), 0)
    once = dict(pipeline_mode=pl.Buffered(1))
    return pl.pallas_call(
        functools.partial(_branch_out_body, prompt_tiles=npt),
        grid=(nj, (t_p + t_s) // tm),
        in_specs=[
            pl.BlockSpec((tm, 2 * d), prompt), pl.BlockSpec((tm, d), prompt),
            pl.BlockSpec((tm, 2 * d), sample), pl.BlockSpec((tm, d), sample),
            pl.BlockSpec((tm, tn), lambda j, i: (i, j)),
            pl.BlockSpec((tm, tn), lambda j, i: (i, nj + j)),
            pl.BlockSpec((None, 2 * d, tn), lambda j, i: (0, 0, j), **once),
            pl.BlockSpec((None, d, tn), lambda j, i: (0, 2, j), **once),
        ],
        out_specs=pl.BlockSpec((tm, tn), lambda j, i: (i, j)),
        out_shape=jax.ShapeDtypeStruct((t_p + t_s, d), BF16),
        scratch_shapes=[pltpu.VMEM((2 * d, tn), BF16), pltpu.VMEM((d, tn), BF16)],
        compiler_params=pltpu.CompilerParams(
            dimension_semantics=("arbitrary", "arbitrary"), vmem_limit_bytes=VMEM_LIMIT),
        name="branch_out",
    )(ya_p, yb_p, ya_s, yb_s, p, p, w_branch_out, w_branch_out)


def _mix_route_body(m_ref, xp_ref, xs_ref, wo_ref, nw_ref, rcat_ref,
                    x1_ref, h2_ref, route_ref, wts_ref, cnt_ref, wo_scr, *, n_coarse, per_group, prompt_tiles):
    @pl.when(pl.program_id(0) == 0)
    def _():
        cnt_ref[...] = jnp.zeros(cnt_ref.shape, F32)
        wo_scr[...] = wo_ref[...].astype(BF16)

    def run(x_ref):
        x1 = x_ref[...] + jnp.dot(m_ref[...], wo_scr[...], preferred_element_type=F32)
        x1_ref[...] = x1
        h2 = x1 * lax.rsqrt(jnp.mean(x1 * x1, axis=-1, keepdims=True) + NORM_EPS) * nw_ref[...]
        h2_ref[...] = h2
        h_hi, h_lo = _split2(h2)
        both = jnp.dot(h_hi, rcat_ref[...], preferred_element_type=F32)
        logits = (both[:, :LANE] + both[:, LANE:]
                  + jnp.dot(h_lo, rcat_ref[:, :LANE], preferred_element_type=F32))

        tm = logits.shape[0]
        n_fine = n_coarse * per_group
        lane = lax.broadcasted_iota(jnp.int32, logits.shape, 1)
        big = jnp.int32(LANE)
        neg = -jnp.inf
        is_c = lane < n_coarse
        lc = jnp.where(is_c, logits, neg)
        mc = jnp.max(lc, axis=-1, keepdims=True)
        grp = jnp.min(jnp.where(is_c & (lc == mc), lane, big), axis=-1, keepdims=True)
        p_grp = 1.0 / jnp.sum(jnp.where(is_c, jnp.exp(lc - mc), 0.0), axis=-1, keepdims=True)
        eidx = lane - n_coarse
        sel = (eidx >= 0) & (eidx < n_fine) & ((eidx // per_group) == grp)
        lf = jnp.where(sel, logits, neg)
        v1 = jnp.max(lf, axis=-1, keepdims=True)
        i1 = jnp.min(jnp.where(sel & (lf == v1), eidx, big), axis=-1, keepdims=True)
        sel2 = sel & (eidx != i1)
        lf2 = jnp.where(sel2, logits, neg)
        v2 = jnp.max(lf2, axis=-1, keepdims=True)
        i2 = jnp.min(jnp.where(sel2 & (lf2 == v2), eidx, big), axis=-1, keepdims=True)
        e2 = jnp.exp(v2 - v1)
        w1 = p_grp / (1.0 + e2)
        w2 = p_grp * e2 / (1.0 + e2)
        wts_ref[...] = jnp.where(lane == 0, w1, jnp.where(lane == 1, w2, 0.0))

        hit1 = eidx == i1
        hit2 = eidx == i2
        hits = hit1.astype(F32) + hit2.astype(F32)
        earlier = (lax.broadcasted_iota(jnp.int32, (tm, tm), 0)
                   > lax.broadcasted_iota(jnp.int32, (tm, tm), 1)).astype(BF16)
        before = jnp.dot(earlier, hits.astype(BF16), preferred_element_type=F32) + cnt_ref[...]
        r1 = jnp.sum(jnp.where(hit1, before, 0.0), axis=-1, keepdims=True).astype(jnp.int32)
        r2 = jnp.sum(jnp.where(hit2, before, 0.0), axis=-1, keepdims=True).astype(jnp.int32)
        cnt_ref[...] += jnp.sum(hits, axis=0, keepdims=True)
        route_ref[...] = jnp.where(lane == 0, i1, jnp.where(lane == 1, i2, jnp.where(
            lane == 2, r1, jnp.where(lane == 3, r2, 0))))

    _by_stream(pl.program_id(0), prompt_tiles, run, xp_ref, xs_ref)


def _mix_route(merged, x_p, x_s, wo, norm_w, r_cat, n_coarse, per_group):
    t_p, d = x_p.shape
    T = t_p + x_s.shape[0]
    tm = _tile(math.gcd(t_p, x_s.shape[0]), 256, 16)
    npt = t_p // tm
    row = lambda i: (i, 0)
    fixed = lambda i: (0, 0)
    xp_spec, xs_spec = _stream_specs((tm, d), npt)
    resident = dict(pipeline_mode=pl.Buffered(1))
    return pl.pallas_call(
        functools.partial(_mix_route_body, n_coarse=n_coarse, per_group=per_group, prompt_tiles=npt),
        grid=(T // tm,),
        in_specs=[
            pl.BlockSpec((tm, d), row), xp_spec, xs_spec, pl.BlockSpec((None, d, d), lambda i: (0, 0, 0), **resident),
            pl.BlockSpec((1, d), fixed), pl.BlockSpec((d, 2 * LANE), fixed, **resident),
        ],
        out_specs=[pl.BlockSpec((tm, d), row), pl.BlockSpec((tm, d), row),
                   pl.BlockSpec((tm, LANE), row), pl.BlockSpec((tm, LANE), row),
                   pl.BlockSpec((1, LANE), fixed)],
        out_shape=[
            jax.ShapeDtypeStruct((T, d), F32), jax.ShapeDtypeStruct((T, d), F32),
            jax.ShapeDtypeStruct((T, LANE), jnp.int32), jax.ShapeDtypeStruct((T, LANE), F32),
            jax.ShapeDtypeStruct((1, LANE), F32),
        ],
        scratch_shapes=[pltpu.VMEM((d, d), BF16)],
        compiler_params=pltpu.CompilerParams(
            dimension_semantics=("arbitrary",), vmem_limit_bytes=VMEM_LIMIT),
        name="mix_route",
    )(merged, x_p, x_s, wo, norm_w, r_cat)


MOE_MOVE_TOKENS = 512
MOE_MOVE_UNROLL = 8


def _moe_scatter_body(dest_ref, fill_ref, h_ref, xs_hbm, zeros, sem, zsem):
    tt = h_ref.shape[0]
    step = pl.program_id(0)
    t0 = step * tt
    n_experts = (fill_ref.shape[0] - 1) // 2
    n_blocks = xs_hbm.shape[0] // MOE_ROWS
    pad_bits = [1 << k for k in reversed(range(3, MOE_ROWS.bit_length() - 1))]

    def fills(action):
        def expert(e, carry):
            row, n_pad = fill_ref[e], fill_ref[n_experts + e]
            head = (-row) & (SUBLANE - 1)
            for r in range(SUBLANE - 1):
                @pl.when(r < head)
                def _():
                    action(pltpu.make_async_copy(zeros.at[pl.ds(0, 1)], xs_hbm.at[pl.ds(row + r, 1)], zsem))
            row, n_pad = row + head, n_pad - head
            for bit in pad_bits:
                @pl.when((n_pad & bit) != 0)
                def _():
                    action(pltpu.make_async_copy(
                        zeros.at[pl.ds(0, bit)], xs_hbm.at[pl.ds(pl.multiple_of(row, SUBLANE), bit)], zsem))
                row = row + (n_pad & bit)
            return carry

        def tail(blk, carry):
            action(pltpu.make_async_copy(
                zeros, xs_hbm.at[pl.ds(pl.multiple_of(blk * MOE_ROWS, MOE_ROWS), MOE_ROWS)], zsem))
            return carry

        lax.fori_loop(0, n_experts, expert, 0)
        lax.fori_loop(fill_ref[2 * n_experts], n_blocks, tail, 0)

    @pl.when(step == 0)
    def _():
        zeros[...] = jnp.zeros(zeros.shape, zeros.dtype)
        fills(lambda c: c.start())

    def start(i, carry):
        for k in range(MOE_TOP_K):
            pltpu.make_async_copy(
                h_ref.at[pl.ds(i, 1)], xs_hbm.at[pl.ds(dest_ref[(t0 + i) * MOE_TOP_K + k], 1)], sem
            ).start(priority=k % 2)
        return carry

    lax.fori_loop(0, tt, start, 0, unroll=MOE_MOVE_UNROLL)
    for _ in range(MOE_TOP_K):
        pltpu.make_async_copy(h_ref, xs_hbm.at[pl.ds(0, tt)], sem).wait()

    @pl.when(step == pl.num_programs(0) - 1)
    def _():
        fills(lambda c: c.wait())


def _moe_scatter(dest, fill_rows, h2, n_rows):
    T, d = h2.shape
    tt = _tile(T, MOE_MOVE_TOKENS, SUBLANE)
    return pl.pallas_call(
        _moe_scatter_body,
        grid_spec=pltpu.PrefetchScalarGridSpec(
            num_scalar_prefetch=2,
            grid=(T // tt,),
            in_specs=[pl.BlockSpec((tt, d), lambda i, dr, fr: (i, 0))],
            out_specs=pl.BlockSpec(memory_space=pl.ANY),
            scratch_shapes=[pltpu.VMEM((MOE_ROWS, d), h2.dtype), pltpu.SemaphoreType.DMA(()),
                            pltpu.SemaphoreType.DMA(())],
        ),
        out_shape=jax.ShapeDtypeStruct((n_rows, d), h2.dtype),
        compiler_params=pltpu.CompilerParams(dimension_semantics=("arbitrary",)),
        name="moe_scatter",
    )(dest, fill_rows, h2)


MOE_WEIGHT_PIECES = 8
MOE_WEIGHT_RING = 4


def _moe_ffn_body(sched_ref, order_ref, n_used_ref, x_ref, wg_hbm, wu_hbm, wd_hbm, o_ref,
                  wg_b, wu_b, wd_b, sg, su, sd, sems):
    b = pl.program_id(0)
    hbm = (wg_hbm, wu_hbm, wd_hbm)
    stage = (sg, su, sd)
    resident = (wg_b, wu_b, wd_b)
    n_total = order_ref[order_ref.shape[0] - 1]

    def copies(c):
        e = order_ref[c // MOE_WEIGHT_PIECES]
        piece = c % MOE_WEIGHT_PIECES
        ring = c % MOE_WEIGHT_RING
        out = []
        for k in range(3):
            rows = stage[k].shape[1]
            out.append(pltpu.make_async_copy(
                hbm[k].at[0, e, pl.ds(pl.multiple_of(piece * rows, rows), rows), :],
                stage[k].at[ring], sems.at[ring, k]))
        return out

    def start(c):
        @pl.when(c < n_total)
        def _():
            for cp in copies(c):
                cp.start()

    def land(lo, hi):
        def body(c, carry):
            slot = (c // MOE_WEIGHT_PIECES) % 2
            piece = c % MOE_WEIGHT_PIECES
            ring = c % MOE_WEIGHT_RING
            for k, cp in enumerate(copies(c)):
                cp.wait()
                rows = stage[k].shape[1]
                resident[k][slot, pl.ds(pl.multiple_of(piece * rows, rows), rows), :] = stage[k][ring].astype(BF16)
            start(c + MOE_WEIGHT_RING)
            return carry

        lax.fori_loop(lo, hi, body, 0)

    @pl.when(b < n_used_ref[0])
    def _():
        @pl.when(b == 0)
        def _():
            for c in range(MOE_WEIGHT_RING):
                start(c)
            land(0, MOE_WEIGHT_PIECES)

        slot = sched_ref[0, b]
        xb = x_ref[...].astype(BF16)
        gate = jnp.dot(xb, wg_b[slot], preferred_element_type=F32)
        up = jnp.dot(xb, wu_b[slot], preferred_element_type=F32)
        act = (_silu(gate) * up).astype(BF16)
        o_ref[...] = jnp.dot(act, wd_b[slot], preferred_element_type=F32)
        land(sched_ref[1, b], sched_ref[2, b])

    @pl.when(b >= n_used_ref[0])
    def _():
        o_ref[...] = jnp.zeros(o_ref.shape, o_ref.dtype)


def _moe_ffn(sched, order, n_used, xs, wg, wu, wd):
    R, d = xs.shape
    f = wg.shape[3]
    nb = R // MOE_ROWS
    np_ = MOE_WEIGHT_PIECES
    assert d % np_ == 0 and f % np_ == 0
    any_spec = pl.BlockSpec(memory_space=pl.ANY)
    return pl.pallas_call(
        _moe_ffn_body,
        grid_spec=pltpu.PrefetchScalarGridSpec(
            num_scalar_prefetch=3,
            grid=(nb,),
            in_specs=[pl.BlockSpec((MOE_ROWS, d), lambda b, sc, od, nu: (b, 0)), any_spec, any_spec, any_spec],
            out_specs=pl.BlockSpec((MOE_ROWS, d), lambda b, sc, od, nu: (b, 0)),
            scratch_shapes=[
                pltpu.VMEM((2, d, f), BF16), pltpu.VMEM((2, d, f), BF16), pltpu.VMEM((2, f, d), BF16),
                pltpu.VMEM((MOE_WEIGHT_RING, d // np_, f), F32), pltpu.VMEM((MOE_WEIGHT_RING, d // np_, f), F32),
                pltpu.VMEM((MOE_WEIGHT_RING, f // np_, d), F32),
                pltpu.SemaphoreType.DMA((MOE_WEIGHT_RING, 3)),
            ],
        ),
        out_shape=jax.ShapeDtypeStruct((R, d), F32),
        compiler_params=pltpu.CompilerParams(
            dimension_semantics=("arbitrary",), vmem_limit_bytes=VMEM_LIMIT),
        name="moe_ffn",
    )(sched, order, n_used, xs, wg, wu, wd)


def _moe_combine_body(dest_ref, x1_ref, wts_ref, nw_ref, yb_hbm, op_ref, os_ref, buf, sems, *, prompt_tiles):
    tt = x1_ref.shape[0]
    i = pl.program_id(0)

    def gather(tile, slot):
        def start(r, carry):
            for k in range(MOE_TOP_K):
                pltpu.make_async_copy(
                    yb_hbm.at[pl.ds(dest_ref[(tile * tt + r) * MOE_TOP_K + k], 1)],
                    buf.at[slot, k, pl.ds(r, 1)], sems.at[slot]).start(priority=k % 2)
            return carry

        lax.fori_loop(0, tt, start, 0, unroll=MOE_MOVE_UNROLL)

    @pl.when(i == 0)
    def _():
        gather(0, 0)

    @pl.when(i + 1 < pl.num_programs(0))
    def _():
        gather(i + 1, (i + 1) % 2)

    slot = i % 2
    for k in range(MOE_TOP_K):
        pltpu.make_async_copy(yb_hbm.at[pl.ds(0, tt)], buf.at[slot, k], sems.at[slot]).wait()

    def finish(o_ref):
        w = wts_ref[...]
        x2 = x1_ref[...] + (buf[slot, 0] * w[:, 0:1] + buf[slot, 1] * w[:, 1:2])
        o_ref[...] = x2 * lax.rsqrt(jnp.mean(x2 * x2, axis=-1, keepdims=True) + NORM_EPS) * nw_ref[...]

    _by_stream(i, prompt_tiles, finish, op_ref, os_ref)


def _moe_combine(dest, x1, wts, norm_w, yb, t_p):
    T, d = x1.shape
    tt = _tile(math.gcd(T, t_p), MOE_MOVE_TOKENS, SUBLANE)
    npt = t_p // tt
    op_spec, os_spec = _stream_specs((tt, d), npt)
    return pl.pallas_call(
        functools.partial(_moe_combine_body, prompt_tiles=npt),
        grid_spec=pltpu.PrefetchScalarGridSpec(
            num_scalar_prefetch=1,
            grid=(T // tt,),
            in_specs=[
                pl.BlockSpec((tt, d), lambda i, dr: (i, 0)),
                pl.BlockSpec((tt, LANE), lambda i, dr: (i, 0)),
                pl.BlockSpec((1, d), lambda i, dr: (0, 0)),
                pl.BlockSpec(memory_space=pl.ANY),
            ],
            out_specs=[op_spec, os_spec],
            scratch_shapes=[pltpu.VMEM((2, MOE_TOP_K, tt, d), F32), pltpu.SemaphoreType.DMA((2,))],
        ),
        out_shape=[jax.ShapeDtypeStruct((t_p, d), F32), jax.ShapeDtypeStruct((T - t_p, d), F32)],
        compiler_params=pltpu.CompilerParams(dimension_semantics=("arbitrary",), vmem_limit_bytes=VMEM_LIMIT),
        name="moe_combine",
    )(dest, x1, wts, norm_w, yb)


def _route_rows(eid, rank, counts, n_assign):
    n_experts = counts.shape[0]
    padded = (counts + MOE_ROWS - 1) // MOE_ROWS * MOE_ROWS
    pend = jnp.cumsum(padded)
    pstart = pend - padded
    experts = jnp.arange(n_experts, dtype=jnp.int32)
    onehot = eid[:, :, None] == experts[None, None, :]
    dest = (jnp.sum(jnp.where(onehot, pstart[None, None, :], 0), axis=-1) + rank).astype(jnp.int32)
    n_blocks = -(-n_assign // MOE_ROWS) + n_experts
    blk_start = jnp.arange(n_blocks, dtype=jnp.int32) * MOE_ROWS
    used = blk_start < pend[-1]
    blk_e = jnp.minimum(jnp.sum((blk_start[:, None] >= pend[None, :]).astype(jnp.int32), axis=1), n_experts - 1)
    n_used = (pend[-1] // MOE_ROWS).astype(jnp.int32).reshape(1)

    nonempty = counts > 0
    ordinal = jnp.cumsum(nonempty.astype(jnp.int32)) - 1
    n_nonempty = jnp.sum(nonempty.astype(jnp.int32))
    order = jnp.argsort(jnp.where(nonempty, experts, n_experts + experts)).astype(jnp.int32)
    n_slices = MOE_WEIGHT_PIECES * n_nonempty
    pos = blk_start // MOE_ROWS - (pstart // MOE_ROWS)[blk_e]
    nblk = jnp.maximum((padded // MOE_ROWS)[blk_e], 1)
    has_next = used & (ordinal[blk_e] + 1 < n_nonempty)
    base = MOE_WEIGHT_PIECES * (ordinal[blk_e] + 1)
    first = jnp.where(has_next, base + MOE_WEIGHT_PIECES * pos // nblk, n_slices)
    last = jnp.where(has_next, base + MOE_WEIGHT_PIECES * (pos + 1) // nblk, n_slices)
    slot = jnp.where(used, ordinal[blk_e] % 2, 0)
    sched = jnp.stack([slot, first, last]).astype(jnp.int32)
    order = jnp.concatenate([order, n_slices.reshape(1)]).astype(jnp.int32)
    fill_rows = jnp.concatenate([pstart + counts, padded - counts, n_used]).astype(jnp.int32)
    return dest.reshape(-1), sched, order, fill_rows, n_used, n_blocks


def _pad_lanes(v, fill=0.0):
    return jnp.pad(v.astype(F32), (0, LANE - v.shape[0]), constant_values=fill).reshape(1, LANE)


def kernel(x_prompt, x_sample, state_ssm, state_ssd_conv, state_short_conv, norm_mixer, w_in, ssd_conv_w,
           ssd_conv_b, ssd_dt_bias, ssd_a_log, ssd_d, ssd_norm, sc_conv_w, w_branch_out, w_out, norm_ffn,
           w_router_coarse, w_router_fine, w_expert_gate, w_expert_up, w_expert_down, norm_final):
    depth = w_in.shape[0]
    assert depth == 1
    n_p, seq_p, d = x_prompt.shape
    n_s, seq_s, _ = x_sample.shape
    d_inner = 2 * d
    n_heads = d_inner // HEAD_DIM
    n_groups = d_inner // GROUP_W
    gn = n_groups * STATE_DIM
    conv_dim = d_inner + 2 * gn
    assert conv_dim == 3 * d and n_heads <= LANE and ssd_conv_w.shape[2] == conv_dim
    t_p, t_s = n_p * seq_p, n_s * seq_s
    n_coarse = w_router_coarse.shape[2]
    n_experts = w_router_fine.shape[2]
    assert n_coarse + n_experts <= LANE

    off_dt = 2 * d + d_inner + conv_dim
    off_sc = off_dt + n_heads
    w_in_t = jnp.swapaxes(w_in, 1, 2)
    w_dt = jnp.pad(w_in_t[0, off_dt:off_sc, :].T, ((0, 0), (0, LANE - n_heads))).astype(BF16)
    head_of_col = jnp.arange(d_inner, dtype=jnp.int32) // HEAD_DIM
    expand = (jnp.arange(LANE, dtype=jnp.int32)[:, None] == head_of_col[None, :]).astype(BF16)
    group_of_n = jnp.arange(gn, dtype=jnp.int32) // STATE_DIM
    group_of_head = jnp.arange(LANE, dtype=jnp.int32) // HEADS_PER_GROUP
    gsel = ((group_of_n[:, None] == group_of_head[None, :])
            & (jnp.arange(LANE)[None, :] < n_heads)).astype(BF16)
    consts = (
        ssd_conv_w[0], ssd_conv_b[0].reshape(1, conv_dim), _pad_lanes(ssd_dt_bias[0]), _pad_lanes(ssd_a_log[0]),
        jnp.repeat(ssd_d[0].astype(F32), HEAD_DIM).reshape(1, d_inner), ssd_norm[0].reshape(1, d_inner),
        sc_conv_w[0], expand,
    )
    w_router = jnp.pad(jnp.concatenate([w_router_coarse[0], w_router_fine[0]], axis=1),
                       ((0, 0), (0, LANE - n_coarse - n_experts)))
    r_hi = w_router.astype(BF16)
    r_cat = jnp.concatenate([r_hi, (w_router - r_hi.astype(F32)).astype(BF16)], axis=1)

    x_p = x_prompt.reshape(t_p, d)
    x_s = x_sample.reshape(t_s, d)
    h, dt_raw = _prenorm(x_p, x_s, norm_mixer[0].reshape(1, d), w_dt)
    p, tails = _inproj(h, w_in_t[0], off_dt, off_sc, consts[0], consts[1], t_p, seq_p)
    ya_p, yb_p, p_ssm, p_sc_tail = _ssd_prompt(p, dt_raw, consts, n_p, seq_p, d)
    ya_s, yb_s, s_ssm, s_conv, s_sc = _ssd_sample(
        p, dt_raw, state_ssm[0].reshape(n_s, n_groups, GROUP_W, STATE_DIM), state_ssd_conv[0],
        state_short_conv[0], consts, gsel, t_p, seq_s, d)
    merged = _branch_out(ya_p, yb_p, ya_s, yb_s, p, w_branch_out)
    x1, h2, route, wts, counts = _mix_route(merged, x_p, x_s, w_out, norm_ffn[0].reshape(1, d), r_cat,
                                            n_coarse, n_experts // n_coarse)

    n_assign = (t_p + t_s) * MOE_TOP_K
    dest, sched, order, fill_rows, n_used, n_blocks = _route_rows(
        route[:, 0:MOE_TOP_K], route[:, MOE_TOP_K:2 * MOE_TOP_K],
        counts[0, n_coarse:n_coarse + n_experts].astype(jnp.int32), n_assign)
    xs = _moe_scatter(dest, fill_rows, h2, n_blocks * MOE_ROWS)
    yrows = _moe_ffn(sched, order, n_used, xs, w_expert_gate, w_expert_up, w_expert_down)
    out_p, out_s = _moe_combine(dest, x1, wts, norm_final.reshape(1, d), yrows, t_p)

    kw = ssd_conv_w.shape[1]
    kw2 = sc_conv_w.shape[1]
    tiles_per_seq = (tails.shape[0] // SUBLANE) * seq_p // (t_p + t_s)
    seq_tails = tails.reshape(-1, SUBLANE, tails.shape[1])[tiles_per_seq - 1:n_p * tiles_per_seq:tiles_per_seq]
    p_conv = seq_tails[:, SUBLANE - (kw - 1):, 4 * d:7 * d]
    return (
        out_p.reshape(n_p, seq_p, d),
        out_s.reshape(n_s, seq_s, d),
        p_ssm.reshape(1, n_p, n_heads, HEAD_DIM, STATE_DIM),
        p_conv[None],
        p_sc_tail[:, SUBLANE - (kw2 - 1):, :][None],
        s_ssm.reshape(1, n_s, n_heads, HEAD_DIM, STATE_DIM),
        s_conv[None],
        s_sc[None],
    )
```

```python
import functools
import math

import jax
import jax.numpy as jnp
from jax import lax
from jax.experimental import pallas as pl
from jax.experimental.pallas import tpu as pltpu

F32 = jnp.float32
BF16 = jnp.bfloat16

NORM_EPS = 1e-6
SSD_NORM_EPS = 1e-5
HEAD_DIM = 64
STATE_DIM = 128
HEADS_PER_GROUP = 8
GROUP_W = HEADS_PER_GROUP * HEAD_DIM
SSD_CHUNK = 128
LOG2_E = math.log2(math.e)
MOE_TOP_K = 2
MOE_ROWS = 256

LANE = 128
SUBLANE = 8
VMEM_LIMIT = 56 * 1024 * 1024

NT_DIMS = (((1,), (1,)), ((), ()))
TN_DIMS = (((0,), (0,)), ((), ()))


def _tile(n, target, align):
    best = None
    for t in range(align, min(n, target) + 1, align):
        if n % t == 0:
            best = t
    assert best is not None, (n, target, align)
    return best


def _split2(v):
    hi = v.astype(BF16)
    lo = (v - hi.astype(F32)).astype(BF16)
    return hi, lo


def _split3(v):
    hi = v.astype(BF16)
    r = v - hi.astype(F32)
    mid = r.astype(BF16)
    lo = (r - mid.astype(F32)).astype(BF16)
    return hi, mid, lo


def _softplus(x):
    return jnp.maximum(x, 0.0) + jnp.log1p(jnp.exp(-jnp.abs(x)))


def _silu(x):
    return x * jax.nn.sigmoid(x)


def _by_stream(tile, prompt_tiles, fn, prompt_ref, sample_ref):
    @pl.when(tile < prompt_tiles)
    def _():
        fn(prompt_ref)

    @pl.when(tile >= prompt_tiles)
    def _():
        fn(sample_ref)


def _stream_specs(block, prompt_tiles, **kwargs):
    pad = (0,) * (len(block) - 1)
    prompt = lambda i, *_: (jnp.minimum(i, prompt_tiles - 1),) + pad
    sample = lambda i, *_: (jnp.maximum(i - prompt_tiles, 0),) + pad
    return pl.BlockSpec(block, prompt, **kwargs), pl.BlockSpec(block, sample, **kwargs)


def _prenorm_body(xp_ref, xs_ref, nw_ref, wdt_ref, h_ref, dt_ref, *, prompt_tiles):
    def run(x_ref):
        x = x_ref[...]
        h = x * lax.rsqrt(jnp.mean(x * x, axis=-1, keepdims=True) + NORM_EPS) * nw_ref[...]
        hb = h.astype(BF16)
        h_ref[...] = hb
        dt_ref[...] = jnp.dot(hb, wdt_ref[...], preferred_element_type=F32)

    _by_stream(pl.program_id(0), prompt_tiles, run, xp_ref, xs_ref)


def _prenorm(x_p, x_s, norm_w, w_dt):
    t_p, D = x_p.shape
    T = t_p + x_s.shape[0]
    tm = _tile(math.gcd(t_p, x_s.shape[0]), 512, 16)
    npt = t_p // tm
    xp_spec, xs_spec = _stream_specs((tm, D), npt)
    return pl.pallas_call(
        functools.partial(_prenorm_body, prompt_tiles=npt),
        grid=(T // tm,),
        in_specs=[xp_spec, xs_spec, pl.BlockSpec((1, D), lambda i: (0, 0)),
                  pl.BlockSpec((D, LANE), lambda i: (0, 0))],
        out_specs=[pl.BlockSpec((tm, D), lambda i: (i, 0)), pl.BlockSpec((tm, LANE), lambda i: (i, 0))],
        out_shape=[jax.ShapeDtypeStruct((T, D), BF16), jax.ShapeDtypeStruct((T, LANE), F32)],
        compiler_params=pltpu.CompilerParams(
            dimension_semantics=("arbitrary",), vmem_limit_bytes=VMEM_LIMIT),
        name="prenorm",
    )(x_p, x_s, norm_w, w_dt)


INPROJ_CONV_SLABS = 4


def _inproj_body(h_ref, w_ref, cw_ref, cb_ref, p_ref, tail_ref, w_scr, cext,
                 *, blocks_per_d, prompt_tiles, tiles_per_seq):
    j = pl.program_id(0)
    i = pl.program_id(1)
    tm, tn = p_ref.shape
    kw = cw_ref.shape[0]

    @pl.when(i == 0)
    def _():
        w_scr[...] = w_ref[...].astype(BF16)

    is_conv = (j >= 4 * blocks_per_d) & (j < 7 * blocks_per_d) & (i < prompt_tiles)

    @pl.when(jnp.logical_not(is_conv))
    def _():
        raw = lax.dot_general(h_ref[...], w_scr[...], NT_DIMS, preferred_element_type=F32)
        p_ref[...] = raw
        tail_ref[...] = raw[tm - SUBLANE:tm, :]

    @pl.when(is_conv)
    def _():
        @pl.when(i % tiles_per_seq == 0)
        def _():
            cext[:, 0:SUBLANE, :] = jnp.zeros((cext.shape[0], SUBLANE, LANE), F32)

        ws = tn // INPROJ_CONV_SLABS
        first = SUBLANE - (kw - 1)
        for k in range(INPROJ_CONV_SLABS):
            raw = lax.dot_general(h_ref[...], w_scr[k * ws:(k + 1) * ws, :], NT_DIMS, preferred_element_type=F32)
            for s in range(k * ws // LANE, (k + 1) * ws // LANE):
                lo, hi = s * LANE, (s + 1) * LANE
                cext[s, SUBLANE:SUBLANE + tm, :] = raw[:, lo - k * ws:hi - k * ws]
                acc = cb_ref[:, lo:hi]
                for t in range(kw):
                    acc = acc + cw_ref[t:t + 1, lo:hi] * cext[s, first + t:first + t + tm, :]
                p_ref[:, lo:hi] = _silu(acc)
                tail = cext[s, tm:tm + SUBLANE, :]
                tail_ref[:, lo:hi] = tail
                cext[s, 0:SUBLANE, :] = tail


def _inproj(h, w_in_t, n_head_cols, tail_start, cw, cb, t_p, seq_p):
    T, D = h.shape
    n_tail = w_in_t.shape[0] - tail_start
    tm = _tile(math.gcd(seq_p, T - t_p), 1024, 16)
    tn = _tile(math.gcd(D, n_tail), 1024, LANE)
    main_tiles = n_head_cols // tn
    n_out = n_head_cols + n_tail
    bpd = D // tn
    assert tail_start % SUBLANE == 0 and n_head_cols % tn == 0
    conv_block = lambda j, i: (0, jnp.clip(j - 4 * bpd, 0, 3 * bpd - 1))
    w_row = lambda j, i: (pl.multiple_of(
        jnp.where(j < main_tiles, j * tn, tail_start + (j - main_tiles) * tn), SUBLANE), 0)
    return pl.pallas_call(
        functools.partial(_inproj_body, blocks_per_d=bpd, prompt_tiles=t_p // tm, tiles_per_seq=seq_p // tm),
        grid=(n_out // tn, T // tm),
        in_specs=[
            pl.BlockSpec((tm, D), lambda j, i: (i, 0)),
            pl.BlockSpec((pl.Element(tn), pl.Element(D)), w_row),
            pl.BlockSpec((cw.shape[0], tn), conv_block),
            pl.BlockSpec((1, tn), conv_block),
        ],
        out_specs=[pl.BlockSpec((tm, tn), lambda j, i: (i, j)),
                   pl.BlockSpec((SUBLANE, tn), lambda j, i: (i, j))],
        out_shape=[jax.ShapeDtypeStruct((T, n_out), F32),
                   jax.ShapeDtypeStruct((T // tm * SUBLANE, n_out), F32)],
        scratch_shapes=[pltpu.VMEM((tn, D), BF16), pltpu.VMEM((tn // LANE, tm + SUBLANE, LANE), F32)],
        compiler_params=pltpu.CompilerParams(
            dimension_semantics=("arbitrary", "arbitrary"), vmem_limit_bytes=VMEM_LIMIT),
        name="inproj",
    )(h, w_in_t, cw, cb)


def _slab_store(ext, row0, value, col0=0):
    for s in range(value.shape[1] // LANE):
        ext[col0 // LANE + s, row0:row0 + value.shape[0], :] = value[:, s * LANE:(s + 1) * LANE]


def _slab_load(ext, row0, rows, lo, hi):
    return jnp.concatenate([ext[s, row0:row0 + rows, :] for s in range(lo // LANE, hi // LANE)], axis=1)


def _conv(ext, lo, hi, q, width, w_ref, first):
    parts = []
    for s in range(lo // LANE, hi // LANE):
        acc = None
        for k in range(width):
            term = w_ref[k:k + 1, s * LANE:(s + 1) * LANE] * ext[s, first + k:first + k + q, :]
            acc = term if acc is None else acc + term
        parts.append(acc)
    return jnp.concatenate(parts, axis=1)


def _gated_norm(y, z, nw):
    g = y * _silu(z)
    return g * lax.rsqrt(jnp.mean(g * g, axis=-1, keepdims=True) + SSD_NORM_EPS) * nw


def _decay_col(cs_last_row, g):
    d = jnp.exp(cs_last_row)
    parts = [
        jnp.broadcast_to(d[0:1, g * HEADS_PER_GROUP + j:g * HEADS_PER_GROUP + j + 1], (HEAD_DIM, STATE_DIM))
        for j in range(HEADS_PER_GROUP)
    ]
    return jnp.concatenate(parts, axis=0)


def _ssd_prompt_body(x4, x5, x6, z2, z3, scb, scc, sch, dtr,
                     dtb, alog, dskip, nw, scw, expand,
                     ya_ref, yb_ref, st_ref, sct_ref, ext2):
    q, d = x4.shape
    n_groups = st_ref.shape[1]
    gn = n_groups * STATE_DIM
    kw2 = scw.shape[0]

    @pl.when(pl.program_id(1) == 0)
    def _():
        st_ref[...] = jnp.zeros(st_ref.shape, F32)
        ext2[:, 0:SUBLANE, :] = jnp.zeros((ext2.shape[0], SUBLANE, LANE), F32)

    _slab_store(ext2, SUBLANE, scc[...] * sch[...])
    v = _conv(ext2, 0, d, q, kw2, scw, SUBLANE - (kw2 - 1))
    yb_ref[...] = (scb[...] * v).astype(yb_ref.dtype)
    tail2 = _slab_load(ext2, q, SUBLANE, 0, d)
    _slab_store(ext2, 0, tail2)
    sct_ref[0] = tail2

    dt = _softplus(dtr[...] + dtb[...])
    a = dt * (-jnp.exp(alog[...]))
    row = lax.broadcasted_iota(jnp.int32, (q, q), 0)
    col = lax.broadcasted_iota(jnp.int32, (q, q), 1)
    causal = row >= col
    tri = causal.astype(BF16)
    cs = sum(jnp.dot(tri, part, preferred_element_type=F32) for part in _split3(a))
    cs2 = cs * LOG2_E
    cs2_t = cs2.T
    cs_last = cs[q - 1:q, :]
    dend = jnp.exp(cs_last - cs)
    ecs = jnp.exp(cs)
    stacked = jnp.concatenate([dt, dend, ecs], axis=0)
    st_b = stacked.astype(BF16)

    lane = lax.broadcasted_iota(jnp.int32, (q, LANE), 1)
    groups_per_block = d // GROUP_W

    for g in range(n_groups):
        c0 = g * GROUP_W
        xref = x4 if g < groups_per_block else x5
        bc = (g % groups_per_block) * GROUP_W
        xs = xref[:, bc:bc + GROUP_W]
        e_g = expand[:, c0:c0 + GROUP_W]
        ex = jnp.dot(st_b, e_g, preferred_element_type=F32)
        xdt = xs * ex[0:q]
        xdt_b = xdt.astype(BF16)
        xdd_b = (xdt * ex[q:2 * q]).astype(BF16)
        bg = x6[:, g * STATE_DIM:(g + 1) * STATE_DIM].astype(BF16)
        cg = x6[:, gn + g * STATE_DIM:gn + (g + 1) * STATE_DIM].astype(BF16)
        cbm = lax.dot_general(cg, bg, NT_DIMS, preferred_element_type=F32)
        state = st_ref[0, g]
        y_off = lax.dot_general(cg, state.astype(BF16), NT_DIMS, preferred_element_type=F32)
        y_parts = []
        for j in range(HEADS_PER_GROUP // 2):
            scores = []
            for h in (g * HEADS_PER_GROUP + 2 * j, g * HEADS_PER_GROUP + 2 * j + 1):
                seg = cs2[:, h:h + 1] - cs2_t[h:h + 1, :]
                dec = jnp.exp2(jnp.where(causal, seg, -jnp.inf))
                scores.append((cbm * dec).astype(BF16))
            xp = xdt_b[:, j * LANE:(j + 1) * LANE]
            zero = jnp.zeros_like(xp)
            rhs = jnp.concatenate(
                [jnp.where(lane < HEAD_DIM, xp, zero), jnp.where(lane >= HEAD_DIM, xp, zero)], axis=0)
            y_parts.append(jnp.dot(jnp.concatenate(scores, axis=1), rhs, preferred_element_type=F32))
        y = jnp.concatenate(y_parts, axis=1) + y_off * ex[2 * q:3 * q] + xs * dskip[:, c0:c0 + GROUP_W]
        zref = z2 if g < groups_per_block else z3
        ya_ref[:, c0:c0 + GROUP_W] = _gated_norm(
            y, zref[:, bc:bc + GROUP_W], nw[:, c0:c0 + GROUP_W]).astype(ya_ref.dtype)
        st_ref[0, g] = state * _decay_col(cs_last, g) + lax.dot_general(
            xdd_b, bg, TN_DIMS, preferred_element_type=F32)


def _ssd_prompt(p, dt_raw, consts, n_seq, seq_len, d):
    (_, _, dtb, alog, dskip, nw, scw, expand) = consts
    q = SSD_CHUNK if seq_len % SSD_CHUNK == 0 else seq_len
    nc = seq_len // q
    n_groups = 2 * d // GROUP_W
    t_p = n_seq * seq_len
    assert 2 * n_groups * STATE_DIM == d

    def blk(cidx):
        return pl.BlockSpec((q, d), lambda b, c, cidx=cidx: (b * nc + c, cidx))

    def const(arr):
        return pl.BlockSpec(arr.shape, lambda b, c: (0,) * arr.ndim)

    return pl.pallas_call(
        _ssd_prompt_body,
        grid=(n_seq, nc),
        in_specs=[blk(4), blk(5), blk(6), blk(2), blk(3), blk(7), blk(8), blk(9),
                  pl.BlockSpec((q, LANE), lambda b, c: (b * nc + c, 0)),
                  const(dtb), const(alog), const(dskip), const(nw), const(scw), const(expand)],
        out_specs=[
            pl.BlockSpec((q, 2 * d), lambda b, c: (b * nc + c, 0)),
            pl.BlockSpec((q, d), lambda b, c: (b * nc + c, 0)),
            pl.BlockSpec((1, n_groups, GROUP_W, STATE_DIM), lambda b, c: (b, 0, 0, 0)),
            pl.BlockSpec((1, SUBLANE, d), lambda b, c: (b, 0, 0)),
        ],
        out_shape=[
            jax.ShapeDtypeStruct((t_p, 2 * d), BF16),
            jax.ShapeDtypeStruct((t_p, d), BF16),
            jax.ShapeDtypeStruct((n_seq, n_groups, GROUP_W, STATE_DIM), F32),
            jax.ShapeDtypeStruct((n_seq, SUBLANE, d), F32),
        ],
        scratch_shapes=[pltpu.VMEM((d // LANE, q + SUBLANE, LANE), F32)],
        compiler_params=pltpu.CompilerParams(
            dimension_semantics=("arbitrary", "arbitrary"), vmem_limit_bytes=VMEM_LIMIT),
        name="ssd_prompt",
    )(p, p, p, p, p, p, p, p, dt_raw, dtb, alog, dskip, nw, scw, expand)


SAMPLE_SEQS_PER_STEP = 2


SAMPLE_STATE_SLOTS = 3


def _ssd_sample_body(x4, x5, x6, z2, z3, scb, scc, sch, dtr, ssm_hbm, conv_in, sc_in,
                     cw, cb, dtb, alog, dskip, nw, scw, expand, gsel,
                     ya_ref, yb_ref, ssm_out, conv_out, sc_out, exts, ext2s, ssm_buf, ssm_sems):
    d = x4.shape[1]
    n_groups = ssm_out.shape[1]
    step = pl.program_id(0)
    n_steps = pl.num_programs(0)

    def fetch(t):
        slot = t % SAMPLE_STATE_SLOTS
        return pltpu.make_async_copy(
            ssm_hbm.at[pl.ds(t * SAMPLE_SEQS_PER_STEP, SAMPLE_SEQS_PER_STEP)], ssm_buf.at[slot], ssm_sems.at[slot])

    @pl.when(step == 0)
    def _():
        for t in range(SAMPLE_STATE_SLOTS - 1):
            @pl.when(t < n_steps)
            def _():
                fetch(t).start(priority=1)

    @pl.when(step + SAMPLE_STATE_SLOTS - 1 < n_steps)
    def _():
        fetch(step + SAMPLE_STATE_SLOTS - 1).start(priority=1)

    fetch(step).wait()
    ssm_in = ssm_buf.at[step % SAMPLE_STATE_SLOTS]
    gn = n_groups * STATE_DIM
    d_inner = n_groups * GROUP_W
    kw = cw.shape[0]
    kw2 = scw.shape[0]
    q = x4.shape[0] // SAMPLE_SEQS_PER_STEP
    groups_per_block = d // GROUP_W
    first = SUBLANE - (kw - 1)
    first2 = SUBLANE - (kw2 - 1)
    nrep = q * q

    rep_t = lax.broadcasted_iota(jnp.int32, (nrep, LANE), 0) % q
    rep_s = lax.broadcasted_iota(jnp.int32, (nrep, LANE), 0) // q
    rep_causal = rep_t >= rep_s
    row_q = lax.broadcasted_iota(jnp.int32, (q, LANE), 0)

    def rep_rows(m):
        return jnp.concatenate([jnp.broadcast_to(m[s:s + 1], (q, m.shape[1])) for s in range(q)], axis=0)

    def tile_rows(m):
        return jnp.concatenate([m] * q, axis=0)

    yb_rows = []
    ya_rows = [[] for _ in range(n_groups)]
    for sidx in range(SAMPLE_SEQS_PER_STEP):
        r0 = sidx * q
        ext, ext2 = exts.at[sidx], ext2s.at[sidx]
        _slab_store(ext, first, conv_in[sidx])
        _slab_store(ext, SUBLANE, x4[r0:r0 + q, :])
        _slab_store(ext, SUBLANE, x5[r0:r0 + q, :], d)
        _slab_store(ext, SUBLANE, x6[r0:r0 + q, :], 2 * d)
        conv_out[sidx] = _slab_load(ext, SUBLANE + q - (kw - 1), kw - 1, 0, 3 * d)

        _slab_store(ext2, first2, sc_in[sidx])
        _slab_store(ext2, SUBLANE, scc[r0:r0 + q, :] * sch[r0:r0 + q, :])
        yb_rows.append(scb[r0:r0 + q, :] * _conv(ext2, 0, d, q, kw2, scw, first2))
        sc_out[sidx] = _slab_load(ext2, SUBLANE + q - (kw2 - 1), kw2 - 1, 0, d)

        dt = _softplus(dtr[r0:r0 + q, :] + dtb[...])
        a = dt * (-jnp.exp(alog[...]))
        cs = jnp.zeros((q, LANE), F32)
        for r in range(q):
            cs = cs + jnp.where(row_q >= r, jnp.broadcast_to(a[r:r + 1], (q, LANE)), 0.0)
        cs_last = cs[q - 1:q, :]
        dend = jnp.exp(cs_last - cs)
        ecs = jnp.exp(cs)

        bmat = _silu(_conv(ext, 2 * d, 2 * d + gn, q, kw, cw, first) + cb[:, 2 * d:2 * d + gn])
        cmat = _silu(_conv(ext, 2 * d + gn, 3 * d, q, kw, cw, first) + cb[:, 2 * d + gn:3 * d])

        cb_hi, cb_lo = _split2(tile_rows(cmat) * rep_rows(bmat))
        cbh = (jnp.dot(cb_hi, gsel[...], preferred_element_type=F32)
               + jnp.dot(cb_lo, gsel[...], preferred_element_type=F32))
        dec = jnp.exp(jnp.where(rep_causal, tile_rows(cs) - rep_rows(cs), -jnp.inf))
        stacked = jnp.concatenate([dt, dend, ecs], axis=0)
        st_hi = stacked.astype(BF16).astype(F32)
        pad = jnp.zeros((LANE - nrep - 6 * q, LANE), F32)
        lhs = jnp.concatenate([cbh * dec, st_hi, stacked - st_hi, pad], axis=0).astype(BF16)

        xdd_parts = []
        for g in range(n_groups):
            c0 = g * GROUP_W
            xs = _silu(_conv(ext, c0, c0 + GROUP_W, q, kw, cw, first) + cb[:, c0:c0 + GROUP_W])
            ex = jnp.dot(lhs, expand[:, c0:c0 + GROUP_W], preferred_element_type=F32)
            o = nrep
            dtx = ex[o:o + q] + ex[o + 3 * q:o + 4 * q]
            dendx = ex[o + q:o + 2 * q] + ex[o + 4 * q:o + 5 * q]
            ecsx = ex[o + 2 * q:o + 3 * q] + ex[o + 5 * q:o + 6 * q]
            xdt = xs * dtx
            xdd = xdt * dendx
            y = xs * dskip[:, c0:c0 + GROUP_W]
            for s in range(q):
                y = y + ex[s * q:(s + 1) * q] * jnp.broadcast_to(xdt[s:s + 1], (q, GROUP_W))
            state = ssm_in[sidx, g]
            cg = cmat[:, g * STATE_DIM:(g + 1) * STATE_DIM]
            y = y + lax.dot_general(cg, state, NT_DIMS, preferred_element_type=F32) * ecsx
            zref = z2 if g < groups_per_block else z3
            zc = (g % groups_per_block) * GROUP_W
            ya_rows[g].append(_gated_norm(y, zref[r0:r0 + q, zc:zc + GROUP_W], nw[:, c0:c0 + GROUP_W]))
            xdd_parts.append(xdd)

        zrows = jnp.zeros((LANE - q, d_inner), F32)
        xdd_t = jnp.concatenate([jnp.concatenate(xdd_parts, axis=1), zrows], axis=0).T
        for g in range(n_groups):
            b_pad = jnp.concatenate(
                [bmat[:, g * STATE_DIM:(g + 1) * STATE_DIM], jnp.zeros((LANE - q, STATE_DIM), F32)], axis=0)
            ssm_out[sidx, g] = ssm_in[sidx, g] * _decay_col(cs_last, g) + jnp.dot(
                xdd_t[g * GROUP_W:(g + 1) * GROUP_W, :], b_pad, preferred_element_type=F32)

    yb_ref[...] = jnp.concatenate(yb_rows, axis=0).astype(yb_ref.dtype)
    for g in range(n_groups):
        ya_ref[:, g * GROUP_W:(g + 1) * GROUP_W] = jnp.concatenate(ya_rows[g], axis=0).astype(ya_ref.dtype)


def _ssd_sample(p, dt_raw, ssm, conv_state, sc_state, consts, gsel, row0, seq_len, d):
    (cw, cb, dtb, alog, dskip, nw, scw, expand) = consts
    n_seq = ssm.shape[0]
    sp = SAMPLE_SEQS_PER_STEP
    rows = sp * seq_len
    assert seq_len == SUBLANE and n_seq % sp == 0 and row0 % rows == 0
    b0 = row0 // rows

    def blk(cidx):
        return pl.BlockSpec((rows, d), lambda i, cidx=cidx: (b0 + i, cidx))

    def const(arr):
        return pl.BlockSpec(arr.shape, lambda i: (0,) * arr.ndim)

    def per_seq(arr):
        return pl.BlockSpec((sp,) + arr.shape[1:], lambda i: (i,) + (0,) * (arr.ndim - 1))

    return pl.pallas_call(
        _ssd_sample_body,
        grid=(n_seq // sp,),
        in_specs=[blk(4), blk(5), blk(6), blk(2), blk(3), blk(7), blk(8), blk(9),
                  pl.BlockSpec((rows, LANE), lambda i: (b0 + i, 0)),
                  pl.BlockSpec(memory_space=pl.ANY), per_seq(conv_state), per_seq(sc_state),
                  const(cw), const(cb), const(dtb), const(alog), const(dskip), const(nw), const(scw),
                  const(expand), const(gsel)],
        out_specs=[
            pl.BlockSpec((rows, 2 * d), lambda i: (i, 0)),
            pl.BlockSpec((rows, d), lambda i: (i, 0)),
            per_seq(ssm), per_seq(conv_state), per_seq(sc_state),
        ],
        out_shape=[
            jax.ShapeDtypeStruct((n_seq * seq_len, 2 * d), BF16),
            jax.ShapeDtypeStruct((n_seq * seq_len, d), BF16),
            jax.ShapeDtypeStruct(ssm.shape, F32),
            jax.ShapeDtypeStruct(conv_state.shape, F32),
            jax.ShapeDtypeStruct(sc_state.shape, F32),
        ],
        scratch_shapes=[pltpu.VMEM((sp, 3 * d // LANE, 2 * SUBLANE, LANE), F32),
                        pltpu.VMEM((sp, d // LANE, 2 * SUBLANE, LANE), F32),
                        pltpu.VMEM((SAMPLE_STATE_SLOTS, sp) + ssm.shape[1:], F32),
                        pltpu.SemaphoreType.DMA((SAMPLE_STATE_SLOTS,))],
        compiler_params=pltpu.CompilerParams(
            dimension_semantics=("arbitrary",), vmem_limit_bytes=VMEM_LIMIT),
        name="ssd_sample",
    )(p, p, p, p, p, p, p, p, dt_raw, ssm, conv_state, sc_state,
      cw, cb, dtb, alog, dskip, nw, scw, expand, gsel)


def _branch_out_body(yap_ref, ybp_ref, yas_ref, ybs_ref, ga_ref, gb_ref, wa_ref, wb_ref, o_ref,
                     wa_scr, wb_scr, *, prompt_tiles):
    i = pl.program_id(1)

    @pl.when(i == 0)
    def _():
        wa_scr[...] = wa_ref[...].astype(BF16)
        wb_scr[...] = wb_ref[...].astype(BF16)

    def run(refs):
        ya_ref, yb_ref = refs
        pa = jnp.dot(ya_ref[...], wa_scr[...], preferred_element_type=F32)
        pb = jnp.dot(yb_ref[...], wb_scr[...], preferred_element_type=F32)
        merged = jax.nn.sigmoid(ga_ref[...]) * pa + jax.nn.sigmoid(gb_ref[...]) * pb
        o_ref[...] = merged.astype(o_ref.dtype)

    _by_stream(i, prompt_tiles, run, (yap_ref, ybp_ref), (yas_ref, ybs_ref))


def _branch_out(ya_p, yb_p, ya_s, yb_s, p, w_branch_out):
    t_p, d = yb_p.shape
    t_s = yb_s.shape[0]
    tm = _tile(math.gcd(t_p, t_s), 512, 16)
    tn = _tile(d, 512, LANE)
    nj = d // tn
    npt = t_p // tm
    prompt = lambda j, i: (jnp.minimum(i, npt - 1), 0)
    sample = lambda j, i: (jnp.maximum(i - npt, 0), 0)
    once = dict(pipeline_mode=pl.Buffered(1))
    return pl.pallas_call(
        functools.partial(_branch_out_body, prompt_tiles=npt),
        grid=(nj, (t_p + t_s) // tm),
        in_specs=[
            pl.BlockSpec((tm, 2 * d), prompt), pl.BlockSpec((tm, d), prompt),
            pl.BlockSpec((tm, 2 * d), sample), pl.BlockSpec((tm, d), sample),
            pl.BlockSpec((tm, tn), lambda j, i: (i, j)),
            pl.BlockSpec((tm, tn), lambda j, i: (i, nj + j)),
            pl.BlockSpec((None, 2 * d, tn), lambda j, i: (0, 0, j), **once),
            pl.BlockSpec((None, d, tn), lambda j, i: (0, 2, j), **once),
        ],
        out_specs=pl.BlockSpec((tm, tn), lambda j, i: (i, j)),
        out_shape=jax.ShapeDtypeStruct((t_p + t_s, d), BF16),
        scratch_shapes=[pltpu.VMEM((2 * d, tn), BF16), pltpu.VMEM((d, tn), BF16)],
        compiler_params=pltpu.CompilerParams(
            dimension_semantics=("arbitrary", "arbitrary"), vmem_limit_bytes=VMEM_LIMIT),
        name="branch_out",
    )(ya_p, yb_p, ya_s, yb_s, p, p, w_branch_out, w_branch_out)


def _mix_route_body(m_ref, xp_ref, xs_ref, wo_ref, nw_ref, rcat_ref,
                    x1_ref, h2_ref, route_ref, wts_ref, cnt_ref, wo_scr, *, n_coarse, per_group, prompt_tiles):
    @pl.when(pl.program_id(0) == 0)
    def _():
        cnt_ref[...] = jnp.zeros(cnt_ref.shape, F32)
        wo_scr[...] = wo_ref[...].astype(BF16)

    def run(x_ref):
        x1 = x_ref[...] + jnp.dot(m_ref[...], wo_scr[...], preferred_element_type=F32)
        x1_ref[...] = x1
        h2 = x1 * lax.rsqrt(jnp.mean(x1 * x1, axis=-1, keepdims=True) + NORM_EPS) * nw_ref[...]
        h2_ref[...] = h2
        h_hi, h_lo = _split2(h2)
        both = jnp.dot(h_hi, rcat_ref[...], preferred_element_type=F32)
        logits = (both[:, :LANE] + both[:, LANE:]
                  + jnp.dot(h_lo, rcat_ref[:, :LANE], preferred_element_type=F32))

        tm = logits.shape[0]
        n_fine = n_coarse * per_group
        lane = lax.broadcasted_iota(jnp.int32, logits.shape, 1)
        big = jnp.int32(LANE)
        neg = -jnp.inf
        is_c = lane < n_coarse
        lc = jnp.where(is_c, logits, neg)
        mc = jnp.max(lc, axis=-1, keepdims=True)
        grp = jnp.min(jnp.where(is_c & (lc == mc), lane, big), axis=-1, keepdims=True)
        p_grp = 1.0 / jnp.sum(jnp.where(is_c, jnp.exp(lc - mc), 0.0), axis=-1, keepdims=True)
        eidx = lane - n_coarse
        sel = (eidx >= 0) & (eidx < n_fine) & ((eidx // per_group) == grp)
        lf = jnp.where(sel, logits, neg)
        v1 = jnp.max(lf, axis=-1, keepdims=True)
        i1 = jnp.min(jnp.where(sel & (lf == v1), eidx, big), axis=-1, keepdims=True)
        sel2 = sel & (eidx != i1)
        lf2 = jnp.where(sel2, logits, neg)
        v2 = jnp.max(lf2, axis=-1, keepdims=True)
        i2 = jnp.min(jnp.where(sel2 & (lf2 == v2), eidx, big), axis=-1, keepdims=True)
        e2 = jnp.exp(v2 - v1)
        w1 = p_grp / (1.0 + e2)
        w2 = p_grp * e2 / (1.0 + e2)
        wts_ref[...] = jnp.where(lane == 0, w1, jnp.where(lane == 1, w2, 0.0))

        hit1 = eidx == i1
        hit2 = eidx == i2
        hits = hit1.astype(F32) + hit2.astype(F32)
        earlier = (lax.broadcasted_iota(jnp.int32, (tm, tm), 0)
                   > lax.broadcasted_iota(jnp.int32, (tm, tm), 1)).astype(BF16)
        before = jnp.dot(earlier, hits.astype(BF16), preferred_element_type=F32) + cnt_ref[...]
        r1 = jnp.sum(jnp.where(hit1, before, 0.0), axis=-1, keepdims=True).astype(jnp.int32)
        r2 = jnp.sum(jnp.where(hit2, before, 0.0), axis=-1, keepdims=True).astype(jnp.int32)
        cnt_ref[...] += jnp.sum(hits, axis=0, keepdims=True)
        route_ref[...] = jnp.where(lane == 0, i1, jnp.where(lane == 1, i2, jnp.where(
            lane == 2, r1, jnp.where(lane == 3, r2, 0))))

    _by_stream(pl.program_id(0), prompt_tiles, run, xp_ref, xs_ref)


def _mix_route(merged, x_p, x_s, wo, norm_w, r_cat, n_coarse, per_group):
    t_p, d = x_p.shape
    T = t_p + x_s.shape[0]
    tm = _tile(math.gcd(t_p, x_s.shape[0]), 256, 16)
    npt = t_p // tm
    row = lambda i: (i, 0)
    fixed = lambda i: (0, 0)
    xp_spec, xs_spec = _stream_specs((tm, d), npt)
    resident = dict(pipeline_mode=pl.Buffered(1))
    return pl.pallas_call(
        functools.partial(_mix_route_body, n_coarse=n_coarse, per_group=per_group, prompt_tiles=npt),
        grid=(T // tm,),
        in_specs=[
            pl.BlockSpec((tm, d), row), xp_spec, xs_spec, pl.BlockSpec((None, d, d), lambda i: (0, 0, 0), **resident),
            pl.BlockSpec((1, d), fixed), pl.BlockSpec((d, 2 * LANE), fixed, **resident),
        ],
        out_specs=[pl.BlockSpec((tm, d), row), pl.BlockSpec((tm, d), row),
                   pl.BlockSpec((tm, LANE), row), pl.BlockSpec((tm, LANE), row),
                   pl.BlockSpec((1, LANE), fixed)],
        out_shape=[
            jax.ShapeDtypeStruct((T, d), F32), jax.ShapeDtypeStruct((T, d), F32),
            jax.ShapeDtypeStruct((T, LANE), jnp.int32), jax.ShapeDtypeStruct((T, LANE), F32),
            jax.ShapeDtypeStruct((1, LANE), F32),
        ],
        scratch_shapes=[pltpu.VMEM((d, d), BF16)],
        compiler_params=pltpu.CompilerParams(
            dimension_semantics=("arbitrary",), vmem_limit_bytes=VMEM_LIMIT),
        name="mix_route",
    )(merged, x_p, x_s, wo, norm_w, r_cat)


MOE_MOVE_TOKENS = 512
MOE_MOVE_UNROLL = 8


def _moe_scatter_body(dest_ref, fill_ref, h_ref, xs_hbm, zeros, sem, zsem):
    tt = h_ref.shape[0]
    step = pl.program_id(0)
    t0 = step * tt
    n_experts = (fill_ref.shape[0] - 1) // 2
    n_blocks = xs_hbm.shape[0] // MOE_ROWS
    pad_bits = [1 << k for k in reversed(range(3, MOE_ROWS.bit_length() - 1))]

    def fills(action):
        def expert(e, carry):
            row, n_pad = fill_ref[e], fill_ref[n_experts + e]
            head = (-row) & (SUBLANE - 1)
            for r in range(SUBLANE - 1):
                @pl.when(r < head)
                def _():
                    action(pltpu.make_async_copy(zeros.at[pl.ds(0, 1)], xs_hbm.at[pl.ds(row + r, 1)], zsem))
            row, n_pad = row + head, n_pad - head
            for bit in pad_bits:
                @pl.when((n_pad & bit) != 0)
                def _():
                    action(pltpu.make_async_copy(
                        zeros.at[pl.ds(0, bit)], xs_hbm.at[pl.ds(pl.multiple_of(row, SUBLANE), bit)], zsem))
                row = row + (n_pad & bit)
            return carry

        def tail(blk, carry):
            action(pltpu.make_async_copy(
                zeros, xs_hbm.at[pl.ds(pl.multiple_of(blk * MOE_ROWS, MOE_ROWS), MOE_ROWS)], zsem))
            return carry

        lax.fori_loop(0, n_experts, expert, 0)
        lax.fori_loop(fill_ref[2 * n_experts], n_blocks, tail, 0)

    @pl.when(step == 0)
    def _():
        zeros[...] = jnp.zeros(zeros.shape, zeros.dtype)
        fills(lambda c: c.start())

    def start(i, carry):
        for k in range(MOE_TOP_K):
            pltpu.make_async_copy(
                h_ref.at[pl.ds(i, 1)], xs_hbm.at[pl.ds(dest_ref[(t0 + i) * MOE_TOP_K + k], 1)], sem
            ).start(priority=k % 2)
        return carry

    lax.fori_loop(0, tt, start, 0, unroll=MOE_MOVE_UNROLL)
    for _ in range(MOE_TOP_K):
        pltpu.make_async_copy(h_ref, xs_hbm.at[pl.ds(0, tt)], sem).wait()

    @pl.when(step == pl.num_programs(0) - 1)
    def _():
        fills(lambda c: c.wait())


def _moe_scatter(dest, fill_rows, h2, n_rows):
    T, d = h2.shape
    tt = _tile(T, MOE_MOVE_TOKENS, SUBLANE)
    return pl.pallas_call(
        _moe_scatter_body,
        grid_spec=pltpu.PrefetchScalarGridSpec(
            num_scalar_prefetch=2,
            grid=(T // tt,),
            in_specs=[pl.BlockSpec((tt, d), lambda i, dr, fr: (i, 0))],
            out_specs=pl.BlockSpec(memory_space=pl.ANY),
            scratch_shapes=[pltpu.VMEM((MOE_ROWS, d), h2.dtype), pltpu.SemaphoreType.DMA(()),
                            pltpu.SemaphoreType.DMA(())],
        ),
        out_shape=jax.ShapeDtypeStruct((n_rows, d), h2.dtype),
        compiler_params=pltpu.CompilerParams(dimension_semantics=("arbitrary",)),
        name="moe_scatter",
    )(dest, fill_rows, h2)


MOE_WEIGHT_PIECES = 8
MOE_WEIGHT_RING = 4


def _moe_ffn_body(sched_ref, order_ref, n_used_ref, x_ref, wg_hbm, wu_hbm, wd_hbm, o_ref,
                  wg_b, wu_b, wd_b, sg, su, sd, sems):
    b = pl.program_id(0)
    hbm = (wg_hbm, wu_hbm, wd_hbm)
    stage = (sg, su, sd)
    resident = (wg_b, wu_b, wd_b)
    n_total = order_ref[order_ref.shape[0] - 1]

    def copies(c):
        e = order_ref[c // MOE_WEIGHT_PIECES]
        piece = c % MOE_WEIGHT_PIECES
        ring = c % MOE_WEIGHT_RING
        out = []
        for k in range(3):
            rows = stage[k].shape[1]
            out.append(pltpu.make_async_copy(
                hbm[k].at[0, e, pl.ds(pl.multiple_of(piece * rows, rows), rows), :],
                stage[k].at[ring], sems.at[ring, k]))
        return out

    def start(c):
        @pl.when(c < n_total)
        def _():
            for k, cp in enumerate(copies(c)):
                cp.start(priority=k % 2)

    def land(lo, hi):
        def body(c, carry):
            slot = (c // MOE_WEIGHT_PIECES) % 2
            piece = c % MOE_WEIGHT_PIECES
            ring = c % MOE_WEIGHT_RING
            for k, cp in enumerate(copies(c)):
                cp.wait()
                rows = stage[k].shape[1]
                resident[k][slot, pl.ds(pl.multiple_of(piece * rows, rows), rows), :] = stage[k][ring].astype(BF16)
            start(c + MOE_WEIGHT_RING)
            return carry

        lax.fori_loop(lo, hi, body, 0)

    @pl.when(b < n_used_ref[0])
    def _():
        @pl.when(b == 0)
        def _():
            for c in range(MOE_WEIGHT_RING):
                start(c)
            land(0, MOE_WEIGHT_PIECES)

        slot = sched_ref[0, b]
        xb = x_ref[...].astype(BF16)
        gate = jnp.dot(xb, wg_b[slot], preferred_element_type=F32)
        up = jnp.dot(xb, wu_b[slot], preferred_element_type=F32)
        act = (_silu(gate) * up).astype(BF16)
        o_ref[...] = jnp.dot(act, wd_b[slot], preferred_element_type=F32)
        land(sched_ref[1, b], sched_ref[2, b])

    @pl.when(b >= n_used_ref[0])
    def _():
        o_ref[...] = jnp.zeros(o_ref.shape, o_ref.dtype)


def _moe_ffn(sched, order, n_used, xs, wg, wu, wd):
    R, d = xs.shape
    f = wg.shape[3]
    nb = R // MOE_ROWS
    np_ = MOE_WEIGHT_PIECES
    assert d % np_ == 0 and f % np_ == 0
    any_spec = pl.BlockSpec(memory_space=pl.ANY)
    return pl.pallas_call(
        _moe_ffn_body,
        grid_spec=pltpu.PrefetchScalarGridSpec(
            num_scalar_prefetch=3,
            grid=(nb,),
            in_specs=[pl.BlockSpec((MOE_ROWS, d), lambda b, sc, od, nu: (b, 0)), any_spec, any_spec, any_spec],
            out_specs=pl.BlockSpec((MOE_ROWS, d), lambda b, sc, od, nu: (b, 0)),
            scratch_shapes=[
                pltpu.VMEM((2, d, f), BF16), pltpu.VMEM((2, d, f), BF16), pltpu.VMEM((2, f, d), BF16),
                pltpu.VMEM((MOE_WEIGHT_RING, d // np_, f), F32), pltpu.VMEM((MOE_WEIGHT_RING, d // np_, f), F32),
                pltpu.VMEM((MOE_WEIGHT_RING, f // np_, d), F32),
                pltpu.SemaphoreType.DMA((MOE_WEIGHT_RING, 3)),
            ],
        ),
        out_shape=jax.ShapeDtypeStruct((R, d), F32),
        compiler_params=pltpu.CompilerParams(
            dimension_semantics=("arbitrary",), vmem_limit_bytes=VMEM_LIMIT),
        name="moe_ffn",
    )(sched, order, n_used, xs, wg, wu, wd)


def _moe_combine_body(dest_ref, x1_ref, wts_ref, nw_ref, yb_hbm, op_ref, os_ref, buf, sems, *, prompt_tiles):
    tt = x1_ref.shape[0]
    i = pl.program_id(0)

    def gather(tile, slot):
        def start(r, carry):
            for k in range(MOE_TOP_K):
                pltpu.make_async_copy(
                    yb_hbm.at[pl.ds(dest_ref[(tile * tt + r) * MOE_TOP_K + k], 1)],
                    buf.at[slot, k, pl.ds(r, 1)], sems.at[slot]).start(priority=k % 2)
            return carry

        lax.fori_loop(0, tt, start, 0, unroll=MOE_MOVE_UNROLL)

    @pl.when(i == 0)
    def _():
        gather(0, 0)

    @pl.when(i + 1 < pl.num_programs(0))
    def _():
        gather(i + 1, (i + 1) % 2)

    slot = i % 2
    for k in range(MOE_TOP_K):
        pltpu.make_async_copy(yb_hbm.at[pl.ds(0, tt)], buf.at[slot, k], sems.at[slot]).wait()

    def finish(o_ref):
        w = wts_ref[...]
        x2 = x1_ref[...] + (buf[slot, 0] * w[:, 0:1] + buf[slot, 1] * w[:, 1:2])
        o_ref[...] = x2 * lax.rsqrt(jnp.mean(x2 * x2, axis=-1, keepdims=True) + NORM_EPS) * nw_ref[...]

    _by_stream(i, prompt_tiles, finish, op_ref, os_ref)


def _moe_combine(dest, x1, wts, norm_w, yb, t_p):
    T, d = x1.shape
    tt = _tile(math.gcd(T, t_p), MOE_MOVE_TOKENS, SUBLANE)
    npt = t_p // tt
    op_spec, os_spec = _stream_specs((tt, d), npt)
    return pl.pallas_call(
        functools.partial(_moe_combine_body, prompt_tiles=npt),
        grid_spec=pltpu.PrefetchScalarGridSpec(
            num_scalar_prefetch=1,
            grid=(T // tt,),
            in_specs=[
                pl.BlockSpec((tt, d), lambda i, dr: (i, 0)),
                pl.BlockSpec((tt, LANE), lambda i, dr: (i, 0)),
                pl.BlockSpec((1, d), lambda i, dr: (0, 0)),
                pl.BlockSpec(memory_space=pl.ANY),
            ],
            out_specs=[op_spec, os_spec],
            scratch_shapes=[pltpu.VMEM((2, MOE_TOP_K, tt, d), F32), pltpu.SemaphoreType.DMA((2,))],
        ),
        out_shape=[jax.ShapeDtypeStruct((t_p, d), F32), jax.ShapeDtypeStruct((T - t_p, d), F32)],
        compiler_params=pltpu.CompilerParams(dimension_semantics=("arbitrary",), vmem_limit_bytes=VMEM_LIMIT),
        name="moe_combine",
    )(dest, x1, wts, norm_w, yb)


def _route_rows(eid, rank, counts, n_assign):
    n_experts = counts.shape[0]
    padded = (counts + MOE_ROWS - 1) // MOE_ROWS * MOE_ROWS
    pend = jnp.cumsum(padded)
    pstart = pend - padded
    experts = jnp.arange(n_experts, dtype=jnp.int32)
    onehot = eid[:, :, None] == experts[None, None, :]
    dest = (jnp.sum(jnp.where(onehot, pstart[None, None, :], 0), axis=-1) + rank).astype(jnp.int32)
    n_blocks = -(-n_assign // MOE_ROWS) + n_experts
    blk_start = jnp.arange(n_blocks, dtype=jnp.int32) * MOE_ROWS
    used = blk_start < pend[-1]
    blk_e = jnp.minimum(jnp.sum((blk_start[:, None] >= pend[None, :]).astype(jnp.int32), axis=1), n_experts - 1)
    n_used = (pend[-1] // MOE_ROWS).astype(jnp.int32).reshape(1)

    nonempty = counts > 0
    ordinal = jnp.cumsum(nonempty.astype(jnp.int32)) - 1
    n_nonempty = jnp.sum(nonempty.astype(jnp.int32))
    order = jnp.argsort(jnp.where(nonempty, experts, n_experts + experts)).astype(jnp.int32)
    n_slices = MOE_WEIGHT_PIECES * n_nonempty
    pos = blk_start // MOE_ROWS - (pstart // MOE_ROWS)[blk_e]
    nblk = jnp.maximum((padded // MOE_ROWS)[blk_e], 1)
    has_next = used & (ordinal[blk_e] + 1 < n_nonempty)
    base = MOE_WEIGHT_PIECES * (ordinal[blk_e] + 1)
    first = jnp.where(has_next, base + MOE_WEIGHT_PIECES * pos // nblk, n_slices)
    last = jnp.where(has_next, base + MOE_WEIGHT_PIECES * (pos + 1) // nblk, n_slices)
    slot = jnp.where(used, ordinal[blk_e] % 2, 0)
    sched = jnp.stack([slot, first, last]).astype(jnp.int32)
    order = jnp.concatenate([order, n_slices.reshape(1)]).astype(jnp.int32)
    fill_rows = jnp.concatenate([pstart + counts, padded - counts, n_used]).astype(jnp.int32)
    return dest.reshape(-1), sched, order, fill_rows, n_used, n_blocks


def _pad_lanes(v, fill=0.0):
    return jnp.pad(v.astype(F32), (0, LANE - v.shape[0]), constant_values=fill).reshape(1, LANE)


def kernel(x_prompt, x_sample, state_ssm, state_ssd_conv, state_short_conv, norm_mixer, w_in, ssd_conv_w,
           ssd_conv_b, ssd_dt_bias, ssd_a_log, ssd_d, ssd_norm, sc_conv_w, w_branch_out, w_out, norm_ffn,
           w_router_coarse, w_router_fine, w_expert_gate, w_expert_up, w_expert_down, norm_final):
    depth = w_in.shape[0]
    assert depth == 1
    n_p, seq_p, d = x_prompt.shape
    n_s, seq_s, _ = x_sample.shape
    d_inner = 2 * d
    n_heads = d_inner // HEAD_DIM
    n_groups = d_inner // GROUP_W
    gn = n_groups * STATE_DIM
    conv_dim = d_inner + 2 * gn
    assert conv_dim == 3 * d and n_heads <= LANE and ssd_conv_w.shape[2] == conv_dim
    t_p, t_s = n_p * seq_p, n_s * seq_s
    n_coarse = w_router_coarse.shape[2]
    n_experts = w_router_fine.shape[2]
    assert n_coarse + n_experts <= LANE

    off_dt = 2 * d + d_inner + conv_dim
    off_sc = off_dt + n_heads
    w_in_t = jnp.swapaxes(w_in, 1, 2)
    w_dt = jnp.pad(w_in_t[0, off_dt:off_sc, :].T, ((0, 0), (0, LANE - n_heads))).astype(BF16)
    head_of_col = jnp.arange(d_inner, dtype=jnp.int32) // HEAD_DIM
    expand = (jnp.arange(LANE, dtype=jnp.int32)[:, None] == head_of_col[None, :]).astype(BF16)
    group_of_n = jnp.arange(gn, dtype=jnp.int32) // STATE_DIM
    group_of_head = jnp.arange(LANE, dtype=jnp.int32) // HEADS_PER_GROUP
    gsel = ((group_of_n[:, None] == group_of_head[None, :])
            & (jnp.arange(LANE)[None, :] < n_heads)).astype(BF16)
    consts = (
        ssd_conv_w[0], ssd_conv_b[0].reshape(1, conv_dim), _pad_lanes(ssd_dt_bias[0]), _pad_lanes(ssd_a_log[0]),
        jnp.repeat(ssd_d[0].astype(F32), HEAD_DIM).reshape(1, d_inner), ssd_norm[0].reshape(1, d_inner),
        sc_conv_w[0], expand,
    )
    w_router = jnp.pad(jnp.concatenate([w_router_coarse[0], w_router_fine[0]], axis=1),
                       ((0, 0), (0, LANE - n_coarse - n_experts)))
    r_hi = w_router.astype(BF16)
    r_cat = jnp.concatenate([r_hi, (w_router - r_hi.astype(F32)).astype(BF16)], axis=1)

    x_p = x_prompt.reshape(t_p, d)
    x_s = x_sample.reshape(t_s, d)
    h, dt_raw = _prenorm(x_p, x_s, norm_mixer[0].reshape(1, d), w_dt)
    p, tails = _inproj(h, w_in_t[0], off_dt, off_sc, consts[0], consts[1], t_p, seq_p)
    ya_p, yb_p, p_ssm, p_sc_tail = _ssd_prompt(p, dt_raw, consts, n_p, seq_p, d)
    ya_s, yb_s, s_ssm, s_conv, s_sc = _ssd_sample(
        p, dt_raw, state_ssm[0].reshape(n_s, n_groups, GROUP_W, STATE_DIM), state_ssd_conv[0],
        state_short_conv[0], consts, gsel, t_p, seq_s, d)
    merged = _branch_out(ya_p, yb_p, ya_s, yb_s, p, w_branch_out)
    x1, h2, route, wts, counts = _mix_route(merged, x_p, x_s, w_out, norm_ffn[0].reshape(1, d), r_cat,
                                            n_coarse, n_experts // n_coarse)

    n_assign = (t_p + t_s) * MOE_TOP_K
    dest, sched, order, fill_rows, n_used, n_blocks = _route_rows(
        route[:, 0:MOE_TOP_K], route[:, MOE_TOP_K:2 * MOE_TOP_K],
        counts[0, n_coarse:n_coarse + n_experts].astype(jnp.int32), n_assign)
    xs = _moe_scatter(dest, fill_rows, h2, n_blocks * MOE_ROWS)
    yrows = _moe_ffn(sched, order, n_used, xs, w_expert_gate, w_expert_up, w_expert_down)
    out_p, out_s = _moe_combine(dest, x1, wts, norm_final.reshape(1, d), yrows, t_p)

    kw = ssd_conv_w.shape[1]
    kw2 = sc_conv_w.shape[1]
    tiles_per_seq = (tails.shape[0] // SUBLANE) * seq_p // (t_p + t_s)
    seq_tails = tails.reshape(-1, SUBLANE, tails.shape[1])[tiles_per_seq - 1:n_p * tiles_per_seq:tiles_per_seq]
    p_conv = seq_tails[:, SUBLANE - (kw - 1):, 4 * d:7 * d]
    return (
        out_p.reshape(n_p, seq_p, d),
        out_s.reshape(n_s, seq_s, d),
        p_ssm.reshape(1, n_p, n_heads, HEAD_DIM, STATE_DIM),
        p_conv[None],
        p_sc_tail[:, SUBLANE - (kw2 - 1):, :][None],
        s_ssm.reshape(1, n_s, n_heads, HEAD_DIM, STATE_DIM),
        s_conv[None],
        s_sc[None],
    )
```

```python
import functools
import math

import jax
import jax.numpy as jnp
from jax import lax
from jax.experimental import pallas as pl
from jax.experimental.pallas import tpu as pltpu

F32 = jnp.float32
BF16 = jnp.bfloat16

NORM_EPS = 1e-6
SSD_NORM_EPS = 1e-5
HEAD_DIM = 64
STATE_DIM = 128
HEADS_PER_GROUP = 8
GROUP_W = HEADS_PER_GROUP * HEAD_DIM
SSD_CHUNK = 128
LOG2_E = math.log2(math.e)
MOE_TOP_K = 2
MOE_ROWS = 256

LANE = 128
SUBLANE = 8
VMEM_LIMIT = 56 * 1024 * 1024

NT_DIMS = (((1,), (1,)), ((), ()))
TN_DIMS = (((0,), (0,)), ((), ()))


def _tile(n, target, align):
    best = None
    for t in range(align, min(n, target) + 1, align):
        if n % t == 0:
            best = t
    assert best is not None, (n, target, align)
    return best


def _split2(v):
    hi = v.astype(BF16)
    lo = (v - hi.astype(F32)).astype(BF16)
    return hi, lo


def _split3(v):
    hi = v.astype(BF16)
    r = v - hi.astype(F32)
    mid = r.astype(BF16)
    lo = (r - mid.astype(F32)).astype(BF16)
    return hi, mid, lo


def _softplus(x):
    return jnp.maximum(x, 0.0) + jnp.log1p(jnp.exp(-jnp.abs(x)))


def _silu(x):
    return x * jax.nn.sigmoid(x)


def _by_stream(tile, prompt_tiles, fn, prompt_ref, sample_ref):
    @pl.when(tile < prompt_tiles)
    def _():
        fn(prompt_ref)

    @pl.when(tile >= prompt_tiles)
    def _():
        fn(sample_ref)


def _stream_specs(block, prompt_tiles, **kwargs):
    pad = (0,) * (len(block) - 1)
    prompt = lambda i, *_: (jnp.minimum(i, prompt_tiles - 1),) + pad
    sample = lambda i, *_: (jnp.maximum(i - prompt_tiles, 0),) + pad
    return pl.BlockSpec(block, prompt, **kwargs), pl.BlockSpec(block, sample, **kwargs)


def _prenorm_body(xp_ref, xs_ref, nw_ref, wdt_ref, h_ref, dt_ref, *, prompt_tiles):
    def run(x_ref):
        x = x_ref[...]
        h = x * lax.rsqrt(jnp.mean(x * x, axis=-1, keepdims=True) + NORM_EPS) * nw_ref[...]
        hb = h.astype(BF16)
        h_ref[...] = hb
        dt_ref[...] = jnp.dot(hb, wdt_ref[...], preferred_element_type=F32)

    _by_stream(pl.program_id(0), prompt_tiles, run, xp_ref, xs_ref)


def _prenorm(x_p, x_s, norm_w, w_dt):
    t_p, D = x_p.shape
    T = t_p + x_s.shape[0]
    tm = _tile(math.gcd(t_p, x_s.shape[0]), 512, 16)
    npt = t_p // tm
    xp_spec, xs_spec = _stream_specs((tm, D), npt)
    return pl.pallas_call(
        functools.partial(_prenorm_body, prompt_tiles=npt),
        grid=(T // tm,),
        in_specs=[xp_spec, xs_spec, pl.BlockSpec((1, D), lambda i: (0, 0)),
                  pl.BlockSpec((D, LANE), lambda i: (0, 0))],
        out_specs=[pl.BlockSpec((tm, D), lambda i: (i, 0)), pl.BlockSpec((tm, LANE), lambda i: (i, 0))],
        out_shape=[jax.ShapeDtypeStruct((T, D), BF16), jax.ShapeDtypeStruct((T, LANE), F32)],
        compiler_params=pltpu.CompilerParams(
            dimension_semantics=("arbitrary",), vmem_limit_bytes=VMEM_LIMIT),
        name="prenorm",
    )(x_p, x_s, norm_w, w_dt)


INPROJ_CONV_SLABS = 4


def _inproj_body(h_ref, w_ref, cw_ref, cb_ref, p_ref, tail_ref, w_scr, cext,
                 *, blocks_per_d, prompt_tiles, tiles_per_seq):
    j = pl.program_id(0)
    i = pl.program_id(1)
    tm, tn = p_ref.shape
    kw = cw_ref.shape[0]

    @pl.when(i == 0)
    def _():
        w_scr[...] = w_ref[...].astype(BF16)

    is_conv = (j >= 4 * blocks_per_d) & (j < 7 * blocks_per_d) & (i < prompt_tiles)

    @pl.when(jnp.logical_not(is_conv))
    def _():
        raw = lax.dot_general(h_ref[...], w_scr[...], NT_DIMS, preferred_element_type=F32)
        p_ref[...] = raw
        tail_ref[...] = raw[tm - SUBLANE:tm, :]

    @pl.when(is_conv)
    def _():
        @pl.when(i % tiles_per_seq == 0)
        def _():
            cext[:, 0:SUBLANE, :] = jnp.zeros((cext.shape[0], SUBLANE, LANE), F32)

        ws = tn // INPROJ_CONV_SLABS
        first = SUBLANE - (kw - 1)
        for k in range(INPROJ_CONV_SLABS):
            raw = lax.dot_general(h_ref[...], w_scr[k * ws:(k + 1) * ws, :], NT_DIMS, preferred_element_type=F32)
            for s in range(k * ws // LANE, (k + 1) * ws // LANE):
                lo, hi = s * LANE, (s + 1) * LANE
                cext[s, SUBLANE:SUBLANE + tm, :] = raw[:, lo - k * ws:hi - k * ws]
                acc = cb_ref[:, lo:hi]
                for t in range(kw):
                    acc = acc + cw_ref[t:t + 1, lo:hi] * cext[s, first + t:first + t + tm, :]
                p_ref[:, lo:hi] = _silu(acc)
                tail = cext[s, tm:tm + SUBLANE, :]
                tail_ref[:, lo:hi] = tail
                cext[s, 0:SUBLANE, :] = tail


def _inproj(h, w_in_t, n_head_cols, tail_start, cw, cb, t_p, seq_p):
    T, D = h.shape
    n_tail = w_in_t.shape[0] - tail_start
    tm = _tile(math.gcd(seq_p, T - t_p), 1024, 16)
    tn = _tile(math.gcd(D, n_tail), 1024, LANE)
    main_tiles = n_head_cols // tn
    n_out = n_head_cols + n_tail
    bpd = D // tn
    assert tail_start % SUBLANE == 0 and n_head_cols % tn == 0
    conv_block = lambda j, i: (0, jnp.clip(j - 4 * bpd, 0, 3 * bpd - 1))
    w_row = lambda j, i: (pl.multiple_of(
        jnp.where(j < main_tiles, j * tn, tail_start + (j - main_tiles) * tn), SUBLANE), 0)
    return pl.pallas_call(
        functools.partial(_inproj_body, blocks_per_d=bpd, prompt_tiles=t_p // tm, tiles_per_seq=seq_p // tm),
        grid=(n_out // tn, T // tm),
        in_specs=[
            pl.BlockSpec((tm, D), lambda j, i: (i, 0)),
            pl.BlockSpec((pl.Element(tn), pl.Element(D)), w_row),
            pl.BlockSpec((cw.shape[0], tn), conv_block),
            pl.BlockSpec((1, tn), conv_block),
        ],
        out_specs=[pl.BlockSpec((tm, tn), lambda j, i: (i, j)),
                   pl.BlockSpec((SUBLANE, tn), lambda j, i: (i, j))],
        out_shape=[jax.ShapeDtypeStruct((T, n_out), F32),
                   jax.ShapeDtypeStruct((T // tm * SUBLANE, n_out), F32)],
        scratch_shapes=[pltpu.VMEM((tn, D), BF16), pltpu.VMEM((tn // LANE, tm + SUBLANE, LANE), F32)],
        compiler_params=pltpu.CompilerParams(
            dimension_semantics=("arbitrary", "arbitrary"), vmem_limit_bytes=VMEM_LIMIT),
        name="inproj",
    )(h, w_in_t, cw, cb)


def _slab_store(ext, row0, value, col0=0):
    for s in range(value.shape[1] // LANE):
        ext[col0 // LANE + s, row0:row0 + value.shape[0], :] = value[:, s * LANE:(s + 1) * LANE]


def _slab_load(ext, row0, rows, lo, hi):
    return jnp.concatenate([ext[s, row0:row0 + rows, :] for s in range(lo // LANE, hi // LANE)], axis=1)


def _conv(ext, lo, hi, q, width, w_ref, first):
    parts = []
    for s in range(lo // LANE, hi // LANE):
        acc = None
        for k in range(width):
            term = w_ref[k:k + 1, s * LANE:(s + 1) * LANE] * ext[s, first + k:first + k + q, :]
            acc = term if acc is None else acc + term
        parts.append(acc)
    return jnp.concatenate(parts, axis=1)


def _gated_norm(y, z, nw):
    g = y * _silu(z)
    return g * lax.rsqrt(jnp.mean(g * g, axis=-1, keepdims=True) + SSD_NORM_EPS) * nw


def _decay_col(cs_last_row, g):
    d = jnp.exp(cs_last_row)
    parts = [
        jnp.broadcast_to(d[0:1, g * HEADS_PER_GROUP + j:g * HEADS_PER_GROUP + j + 1], (HEAD_DIM, STATE_DIM))
        for j in range(HEADS_PER_GROUP)
    ]
    return jnp.concatenate(parts, axis=0)


def _ssd_prompt_body(x4, x5, x6, z2, z3, scb, scc, sch, dtr,
                     dtb, alog, dskip, nw, scw, expand,
                     ya_ref, yb_ref, st_ref, sct_ref, ext2):
    q, d = x4.shape
    n_groups = st_ref.shape[1]
    gn = n_groups * STATE_DIM
    kw2 = scw.shape[0]

    @pl.when(pl.program_id(1) == 0)
    def _():
        st_ref[...] = jnp.zeros(st_ref.shape, F32)
        ext2[:, 0:SUBLANE, :] = jnp.zeros((ext2.shape[0], SUBLANE, LANE), F32)

    _slab_store(ext2, SUBLANE, scc[...] * sch[...])
    v = _conv(ext2, 0, d, q, kw2, scw, SUBLANE - (kw2 - 1))
    yb_ref[...] = (scb[...] * v).astype(yb_ref.dtype)
    tail2 = _slab_load(ext2, q, SUBLANE, 0, d)
    _slab_store(ext2, 0, tail2)
    sct_ref[0] = tail2

    dt = _softplus(dtr[...] + dtb[...])
    a = dt * (-jnp.exp(alog[...]))
    row = lax.broadcasted_iota(jnp.int32, (q, q), 0)
    col = lax.broadcasted_iota(jnp.int32, (q, q), 1)
    causal = row >= col
    tri = causal.astype(BF16)
    cs = sum(jnp.dot(tri, part, preferred_element_type=F32) for part in _split3(a))
    cs2 = cs * LOG2_E
    cs2_t = cs2.T
    cs_last = cs[q - 1:q, :]
    dend = jnp.exp(cs_last - cs)
    ecs = jnp.exp(cs)
    stacked = jnp.concatenate([dt, dend, ecs], axis=0)
    st_b = stacked.astype(BF16)

    lane = lax.broadcasted_iota(jnp.int32, (q, LANE), 1)
    groups_per_block = d // GROUP_W

    for g in range(n_groups):
        c0 = g * GROUP_W
        xref = x4 if g < groups_per_block else x5
        bc = (g % groups_per_block) * GROUP_W
        xs = xref[:, bc:bc + GROUP_W]
        e_g = expand[:, c0:c0 + GROUP_W]
        ex = jnp.dot(st_b, e_g, preferred_element_type=F32)
        xdt = xs * ex[0:q]
        xdt_b = xdt.astype(BF16)
        xdd_b = (xdt * ex[q:2 * q]).astype(BF16)
        bg = x6[:, g * STATE_DIM:(g + 1) * STATE_DIM].astype(BF16)
        cg = x6[:, gn + g * STATE_DIM:gn + (g + 1) * STATE_DIM].astype(BF16)
        cbm = lax.dot_general(cg, bg, NT_DIMS, preferred_element_type=F32)
        state = st_ref[0, g]
        y_off = lax.dot_general(cg, state.astype(BF16), NT_DIMS, preferred_element_type=F32)
        y_parts = []
        for j in range(HEADS_PER_GROUP // 2):
            scores = []
            for h in (g * HEADS_PER_GROUP + 2 * j, g * HEADS_PER_GROUP + 2 * j + 1):
                seg = cs2[:, h:h + 1] - cs2_t[h:h + 1, :]
                dec = jnp.exp2(jnp.where(causal, seg, -jnp.inf))
                scores.append((cbm * dec).astype(BF16))
            xp = xdt_b[:, j * LANE:(j + 1) * LANE]
            zero = jnp.zeros_like(xp)
            rhs = jnp.concatenate(
                [jnp.where(lane < HEAD_DIM, xp, zero), jnp.where(lane >= HEAD_DIM, xp, zero)], axis=0)
            y_parts.append(jnp.dot(jnp.concatenate(scores, axis=1), rhs, preferred_element_type=F32))
        y = jnp.concatenate(y_parts, axis=1) + y_off * ex[2 * q:3 * q] + xs * dskip[:, c0:c0 + GROUP_W]
        zref = z2 if g < groups_per_block else z3
        ya_ref[:, c0:c0 + GROUP_W] = _gated_norm(
            y, zref[:, bc:bc + GROUP_W], nw[:, c0:c0 + GROUP_W]).astype(ya_ref.dtype)
        st_ref[0, g] = state * _decay_col(cs_last, g) + lax.dot_general(
            xdd_b, bg, TN_DIMS, preferred_element_type=F32)


def _ssd_prompt(p, dt_raw, consts, n_seq, seq_len, d):
    (_, _, dtb, alog, dskip, nw, scw, expand) = consts
    q = SSD_CHUNK if seq_len % SSD_CHUNK == 0 else seq_len
    nc = seq_len // q
    n_groups = 2 * d // GROUP_W
    t_p = n_seq * seq_len
    assert 2 * n_groups * STATE_DIM == d

    def blk(cidx):
        return pl.BlockSpec((q, d), lambda b, c, cidx=cidx: (b * nc + c, cidx))

    def const(arr):
        return pl.BlockSpec(arr.shape, lambda b, c: (0,) * arr.ndim)

    return pl.pallas_call(
        _ssd_prompt_body,
        grid=(n_seq, nc),
        in_specs=[blk(4), blk(5), blk(6), blk(2), blk(3), blk(7), blk(8), blk(9),
                  pl.BlockSpec((q, LANE), lambda b, c: (b * nc + c, 0)),
                  const(dtb), const(alog), const(dskip), const(nw), const(scw), const(expand)],
        out_specs=[
            pl.BlockSpec((q, 2 * d), lambda b, c: (b * nc + c, 0)),
            pl.BlockSpec((q, d), lambda b, c: (b * nc + c, 0)),
            pl.BlockSpec((1, n_groups, GROUP_W, STATE_DIM), lambda b, c: (b, 0, 0, 0)),
            pl.BlockSpec((1, SUBLANE, d), lambda b, c: (b, 0, 0)),
        ],
        out_shape=[
            jax.ShapeDtypeStruct((t_p, 2 * d), BF16),
            jax.ShapeDtypeStruct((t_p, d), BF16),
            jax.ShapeDtypeStruct((n_seq, n_groups, GROUP_W, STATE_DIM), F32),
            jax.ShapeDtypeStruct((n_seq, SUBLANE, d), F32),
        ],
        scratch_shapes=[pltpu.VMEM((d // LANE, q + SUBLANE, LANE), F32)],
        compiler_params=pltpu.CompilerParams(
            dimension_semantics=("arbitrary", "arbitrary"), vmem_limit_bytes=VMEM_LIMIT),
        name="ssd_prompt",
    )(p, p, p, p, p, p, p, p, dt_raw, dtb, alog, dskip, nw, scw, expand)


SAMPLE_SEQS_PER_STEP = 2


SAMPLE_STATE_SLOTS = 3


def _ssd_sample_body(x4, x5, x6, z2, z3, scb, scc, sch, dtr, ssm_hbm, conv_in, sc_in,
                     cw, cb, dtb, alog, dskip, nw, scw, expand, gsel,
                     ya_ref, yb_ref, ssm_out, conv_out, sc_out, exts, ext2s, ssm_buf, ssm_sems):
    d = x4.shape[1]
    n_groups = ssm_out.shape[1]
    step = pl.program_id(0)
    n_steps = pl.num_programs(0)

    def fetch(t):
        slot = t % SAMPLE_STATE_SLOTS
        return pltpu.make_async_copy(
            ssm_hbm.at[pl.ds(t * SAMPLE_SEQS_PER_STEP, SAMPLE_SEQS_PER_STEP)], ssm_buf.at[slot], ssm_sems.at[slot])

    @pl.when(step == 0)
    def _():
        for t in range(SAMPLE_STATE_SLOTS - 1):
            @pl.when(t < n_steps)
            def _():
                fetch(t).start(priority=1)

    @pl.when(step + SAMPLE_STATE_SLOTS - 1 < n_steps)
    def _():
        fetch(step + SAMPLE_STATE_SLOTS - 1).start(priority=1)

    fetch(step).wait()
    ssm_in = ssm_buf.at[step % SAMPLE_STATE_SLOTS]
    gn = n_groups * STATE_DIM
    d_inner = n_groups * GROUP_W
    kw = cw.shape[0]
    kw2 = scw.shape[0]
    q = x4.shape[0] // SAMPLE_SEQS_PER_STEP
    groups_per_block = d // GROUP_W
    first = SUBLANE - (kw - 1)
    first2 = SUBLANE - (kw2 - 1)
    nrep = q * q

    rep_t = lax.broadcasted_iota(jnp.int32, (nrep, LANE), 0) % q
    rep_s = lax.broadcasted_iota(jnp.int32, (nrep, LANE), 0) // q
    rep_causal = rep_t >= rep_s
    row_q = lax.broadcasted_iota(jnp.int32, (q, LANE), 0)

    def rep_rows(m):
        return jnp.concatenate([jnp.broadcast_to(m[s:s + 1], (q, m.shape[1])) for s in range(q)], axis=0)

    def tile_rows(m):
        return jnp.concatenate([m] * q, axis=0)

    yb_rows = []
    ya_rows = [[] for _ in range(n_groups)]
    for sidx in range(SAMPLE_SEQS_PER_STEP):
        r0 = sidx * q
        ext, ext2 = exts.at[sidx], ext2s.at[sidx]
        _slab_store(ext, first, conv_in[sidx])
        _slab_store(ext, SUBLANE, x4[r0:r0 + q, :])
        _slab_store(ext, SUBLANE, x5[r0:r0 + q, :], d)
        _slab_store(ext, SUBLANE, x6[r0:r0 + q, :], 2 * d)
        conv_out[sidx] = _slab_load(ext, SUBLANE + q - (kw - 1), kw - 1, 0, 3 * d)

        _slab_store(ext2, first2, sc_in[sidx])
        _slab_store(ext2, SUBLANE, scc[r0:r0 + q, :] * sch[r0:r0 + q, :])
        yb_rows.append(scb[r0:r0 + q, :] * _conv(ext2, 0, d, q, kw2, scw, first2))
        sc_out[sidx] = _slab_load(ext2, SUBLANE + q - (kw2 - 1), kw2 - 1, 0, d)

        dt = _softplus(dtr[r0:r0 + q, :] + dtb[...])
        a = dt * (-jnp.exp(alog[...]))
        cs = jnp.zeros((q, LANE), F32)
        for r in range(q):
            cs = cs + jnp.where(row_q >= r, jnp.broadcast_to(a[r:r + 1], (q, LANE)), 0.0)
        cs_last = cs[q - 1:q, :]
        dend = jnp.exp(cs_last - cs)
        ecs = jnp.exp(cs)

        bmat = _silu(_conv(ext, 2 * d, 2 * d + gn, q, kw, cw, first) + cb[:, 2 * d:2 * d + gn])
        cmat = _silu(_conv(ext, 2 * d + gn, 3 * d, q, kw, cw, first) + cb[:, 2 * d + gn:3 * d])

        cb_hi, cb_lo = _split2(tile_rows(cmat) * rep_rows(bmat))
        cbh = (jnp.dot(cb_hi, gsel[...], preferred_element_type=F32)
               + jnp.dot(cb_lo, gsel[...], preferred_element_type=F32))
        dec = jnp.exp(jnp.where(rep_causal, tile_rows(cs) - rep_rows(cs), -jnp.inf))
        stacked = jnp.concatenate([dt, dend, ecs], axis=0)
        st_hi = stacked.astype(BF16).astype(F32)
        pad = jnp.zeros((LANE - nrep - 6 * q, LANE), F32)
        lhs = jnp.concatenate([cbh * dec, st_hi, stacked - st_hi, pad], axis=0).astype(BF16)

        xdd_parts = []
        for g in range(n_groups):
            c0 = g * GROUP_W
            xs = _silu(_conv(ext, c0, c0 + GROUP_W, q, kw, cw, first) + cb[:, c0:c0 + GROUP_W])
            ex = jnp.dot(lhs, expand[:, c0:c0 + GROUP_W], preferred_element_type=F32)
            o = nrep
            dtx = ex[o:o + q] + ex[o + 3 * q:o + 4 * q]
            dendx = ex[o + q:o + 2 * q] + ex[o + 4 * q:o + 5 * q]
            ecsx = ex[o + 2 * q:o + 3 * q] + ex[o + 5 * q:o + 6 * q]
            xdt = xs * dtx
            xdd = xdt * dendx
            y = xs * dskip[:, c0:c0 + GROUP_W]
            for s in range(q):
                y = y + ex[s * q:(s + 1) * q] * jnp.broadcast_to(xdt[s:s + 1], (q, GROUP_W))
            state = ssm_in[sidx, g]
            cg = cmat[:, g * STATE_DIM:(g + 1) * STATE_DIM]
            y = y + lax.dot_general(cg, state, NT_DIMS, preferred_element_type=F32) * ecsx
            zref = z2 if g < groups_per_block else z3
            zc = (g % groups_per_block) * GROUP_W
            ya_rows[g].append(_gated_norm(y, zref[r0:r0 + q, zc:zc + GROUP_W], nw[:, c0:c0 + GROUP_W]))
            xdd_parts.append(xdd)

        zrows = jnp.zeros((LANE - q, d_inner), F32)
        xdd_t = jnp.concatenate([jnp.concatenate(xdd_parts, axis=1), zrows], axis=0).T
        for g in range(n_groups):
            b_pad = jnp.concatenate(
                [bmat[:, g * STATE_DIM:(g + 1) * STATE_DIM], jnp.zeros((LANE - q, STATE_DIM), F32)], axis=0)
            ssm_out[sidx, g] = ssm_in[sidx, g] * _decay_col(cs_last, g) + jnp.dot(
                xdd_t[g * GROUP_W:(g + 1) * GROUP_W, :].astype(BF16), b_pad.astype(BF16),
                preferred_element_type=F32)

    yb_ref[...] = jnp.concatenate(yb_rows, axis=0).astype(yb_ref.dtype)
    for g in range(n_groups):
        ya_ref[:, g * GROUP_W:(g + 1) * GROUP_W] = jnp.concatenate(ya_rows[g], axis=0).astype(ya_ref.dtype)


def _ssd_sample(p, dt_raw, ssm, conv_state, sc_state, consts, gsel, row0, seq_len, d):
    (cw, cb, dtb, alog, dskip, nw, scw, expand) = consts
    n_seq = ssm.shape[0]
    sp = SAMPLE_SEQS_PER_STEP
    rows = sp * seq_len
    assert seq_len == SUBLANE and n_seq % sp == 0 and row0 % rows == 0
    b0 = row0 // rows

    def blk(cidx):
        return pl.BlockSpec((rows, d), lambda i, cidx=cidx: (b0 + i, cidx))

    def const(arr):
        return pl.BlockSpec(arr.shape, lambda i: (0,) * arr.ndim)

    def per_seq(arr):
        return pl.BlockSpec((sp,) + arr.shape[1:], lambda i: (i,) + (0,) * (arr.ndim - 1))

    return pl.pallas_call(
        _ssd_sample_body,
        grid=(n_seq // sp,),
        in_specs=[blk(4), blk(5), blk(6), blk(2), blk(3), blk(7), blk(8), blk(9),
                  pl.BlockSpec((rows, LANE), lambda i: (b0 + i, 0)),
                  pl.BlockSpec(memory_space=pl.ANY), per_seq(conv_state), per_seq(sc_state),
                  const(cw), const(cb), const(dtb), const(alog), const(dskip), const(nw), const(scw),
                  const(expand), const(gsel)],
        out_specs=[
            pl.BlockSpec((rows, 2 * d), lambda i: (i, 0)),
            pl.BlockSpec((rows, d), lambda i: (i, 0)),
            per_seq(ssm), per_seq(conv_state), per_seq(sc_state),
        ],
        out_shape=[
            jax.ShapeDtypeStruct((n_seq * seq_len, 2 * d), BF16),
            jax.ShapeDtypeStruct((n_seq * seq_len, d), BF16),
            jax.ShapeDtypeStruct(ssm.shape, F32),
            jax.ShapeDtypeStruct(conv_state.shape, F32),
            jax.ShapeDtypeStruct(sc_state.shape, F32),
        ],
        scratch_shapes=[pltpu.VMEM((sp, 3 * d // LANE, 2 * SUBLANE, LANE), F32),
                        pltpu.VMEM((sp, d // LANE, 2 * SUBLANE, LANE), F32),
                        pltpu.VMEM((SAMPLE_STATE_SLOTS, sp) + ssm.shape[1:], F32),
                        pltpu.SemaphoreType.DMA((SAMPLE_STATE_SLOTS,))],
        compiler_params=pltpu.CompilerParams(
            dimension_semantics=("arbitrary",), vmem_limit_bytes=VMEM_LIMIT),
        name="ssd_sample",
    )(p, p, p, p, p, p, p, p, dt_raw, ssm, conv_state, sc_state,
      cw, cb, dtb, alog, dskip, nw, scw, expand, gsel)


def _branch_out_body(yap_ref, ybp_ref, yas_ref, ybs_ref, ga_ref, gb_ref, wa_ref, wb_ref, o_ref,
                     wa_scr, wb_scr, *, prompt_tiles):
    i = pl.program_id(1)

    @pl.when(i == 0)
    def _():
        wa_scr[...] = wa_ref[...].astype(BF16)
        wb_scr[...] = wb_ref[...].astype(BF16)

    def run(refs):
        ya_ref, yb_ref = refs
        pa = jnp.dot(ya_ref[...], wa_scr[...], preferred_element_type=F32)
        pb = jnp.dot(yb_ref[...], wb_scr[...], preferred_element_type=F32)
        merged = jax.nn.sigmoid(ga_ref[...]) * pa + jax.nn.sigmoid(gb_ref[...]) * pb
        o_ref[...] = merged.astype(o_ref.dtype)

    _by_stream(i, prompt_tiles, run, (yap_ref, ybp_ref), (yas_ref, ybs_ref))


def _branch_out(ya_p, yb_p, ya_s, yb_s, p, w_branch_out):
    t_p, d = yb_p.shape
    t_s = yb_s.shape[0]
    tm = _tile(math.gcd(t_p, t_s), 512, 16)
    tn = _tile(d, 512, LANE)
    nj = d // tn
    npt = t_p // tm
    prompt = lambda j, i: (jnp.minimum(i, npt - 1), 0)
    sample = lambda j, i: (jnp.maximum(i - npt, 0), 0)
    once = dict(pipeline_mode=pl.Buffered(1))
    return pl.pallas_call(
        functools.partial(_branch_out_body, prompt_tiles=npt),
        grid=(nj, (t_p + t_s) // tm),
        in_specs=[
            pl.BlockSpec((tm, 2 * d), prompt), pl.BlockSpec((tm, d), prompt),
            pl.BlockSpec((tm, 2 * d), sample), pl.BlockSpec((tm, d), sample),
            pl.BlockSpec((tm, tn), lambda j, i: (i, j)),
            pl.BlockSpec((tm, tn), lambda j, i: (i, nj + j)),
            pl.BlockSpec((None, 2 * d, tn), lambda j, i: (0, 0, j), **once),
            pl.BlockSpec((None, d, tn), lambda j, i: (0, 2, j), **once),
        ],
        out_specs=pl.BlockSpec((tm, tn), lambda j, i: (i, j)),
        out_shape=jax.ShapeDtypeStruct((t_p + t_s, d), BF16),
        scratch_shapes=[pltpu.VMEM((2 * d, tn), BF16), pltpu.VMEM((d, tn), BF16)],
        compiler_params=pltpu.CompilerParams(
            dimension_semantics=("arbitrary", "arbitrary"), vmem_limit_bytes=VMEM_LIMIT),
        name="branch_out",
    )(ya_p, yb_p, ya_s, yb_s, p, p, w_branch_out, w_branch_out)


def _mix_route_body(m_ref, xp_ref, xs_ref, wo_ref, nw_ref, rcat_ref,
                    x1_ref, h2_ref, route_ref, wts_ref, cnt_ref, wo_scr, *, n_coarse, per_group, prompt_tiles):
    @pl.when(pl.program_id(0) == 0)
    def _():
        cnt_ref[...] = jnp.zeros(cnt_ref.shape, F32)
        wo_scr[...] = wo_ref[...].astype(BF16)

    def run(x_ref):
        x1 = x_ref[...] + jnp.dot(m_ref[...], wo_scr[...], preferred_element_type=F32)
        x1_ref[...] = x1
        h2 = x1 * lax.rsqrt(jnp.mean(x1 * x1, axis=-1, keepdims=True) + NORM_EPS) * nw_ref[...]
        h2_ref[...] = h2
        h_hi, h_lo = _split2(h2)
        both = jnp.dot(h_hi, rcat_ref[...], preferred_element_type=F32)
        logits = (both[:, :LANE] + both[:, LANE:]
                  + jnp.dot(h_lo, rcat_ref[:, :LANE], preferred_element_type=F32))

        tm = logits.shape[0]
        n_fine = n_coarse * per_group
        lane = lax.broadcasted_iota(jnp.int32, logits.shape, 1)
        big = jnp.int32(LANE)
        neg = -jnp.inf
        is_c = lane < n_coarse
        lc = jnp.where(is_c, logits, neg)
        mc = jnp.max(lc, axis=-1, keepdims=True)
        grp = jnp.min(jnp.where(is_c & (lc == mc), lane, big), axis=-1, keepdims=True)
        p_grp = 1.0 / jnp.sum(jnp.where(is_c, jnp.exp(lc - mc), 0.0), axis=-1, keepdims=True)
        eidx = lane - n_coarse
        sel = (eidx >= 0) & (eidx < n_fine) & ((eidx // per_group) == grp)
        lf = jnp.where(sel, logits, neg)
        v1 = jnp.max(lf, axis=-1, keepdims=True)
        i1 = jnp.min(jnp.where(sel & (lf == v1), eidx, big), axis=-1, keepdims=True)
        sel2 = sel & (eidx != i1)
        lf2 = jnp.where(sel2, logits, neg)
        v2 = jnp.max(lf2, axis=-1, keepdims=True)
        i2 = jnp.min(jnp.where(sel2 & (lf2 == v2), eidx, big), axis=-1, keepdims=True)
        e2 = jnp.exp(v2 - v1)
        w1 = p_grp / (1.0 + e2)
        w2 = p_grp * e2 / (1.0 + e2)
        wts_ref[...] = jnp.where(lane == 0, w1, jnp.where(lane == 1, w2, 0.0))

        hit1 = eidx == i1
        hit2 = eidx == i2
        hits = hit1.astype(F32) + hit2.astype(F32)
        earlier = (lax.broadcasted_iota(jnp.int32, (tm, tm), 0)
                   > lax.broadcasted_iota(jnp.int32, (tm, tm), 1)).astype(BF16)
        before = jnp.dot(earlier, hits.astype(BF16), preferred_element_type=F32) + cnt_ref[...]
        r1 = jnp.sum(jnp.where(hit1, before, 0.0), axis=-1, keepdims=True).astype(jnp.int32)
        r2 = jnp.sum(jnp.where(hit2, before, 0.0), axis=-1, keepdims=True).astype(jnp.int32)
        cnt_ref[...] += jnp.sum(hits, axis=0, keepdims=True)
        route_ref[...] = jnp.where(lane == 0, i1, jnp.where(lane == 1, i2, jnp.where(
            lane == 2, r1, jnp.where(lane == 3, r2, 0))))

    _by_stream(pl.program_id(0), prompt_tiles, run, xp_ref, xs_ref)


def _mix_route(merged, x_p, x_s, wo, norm_w, r_cat, n_coarse, per_group):
    t_p, d = x_p.shape
    T = t_p + x_s.shape[0]
    tm = _tile(math.gcd(t_p, x_s.shape[0]), 256, 16)
    npt = t_p // tm
    row = lambda i: (i, 0)
    fixed = lambda i: (0, 0)
    xp_spec, xs_spec = _stream_specs((tm, d), npt)
    resident = dict(pipeline_mode=pl.Buffered(1))
    return pl.pallas_call(
        functools.partial(_mix_route_body, n_coarse=n_coarse, per_group=per_group, prompt_tiles=npt),
        grid=(T // tm,),
        in_specs=[
            pl.BlockSpec((tm, d), row), xp_spec, xs_spec, pl.BlockSpec((None, d, d), lambda i: (0, 0, 0), **resident),
            pl.BlockSpec((1, d), fixed), pl.BlockSpec((d, 2 * LANE), fixed, **resident),
        ],
        out_specs=[pl.BlockSpec((tm, d), row), pl.BlockSpec((tm, d), row),
                   pl.BlockSpec((tm, LANE), row), pl.BlockSpec((tm, LANE), row),
                   pl.BlockSpec((1, LANE), fixed)],
        out_shape=[
            jax.ShapeDtypeStruct((T, d), F32), jax.ShapeDtypeStruct((T, d), F32),
            jax.ShapeDtypeStruct((T, LANE), jnp.int32), jax.ShapeDtypeStruct((T, LANE), F32),
            jax.ShapeDtypeStruct((1, LANE), F32),
        ],
        scratch_shapes=[pltpu.VMEM((d, d), BF16)],
        compiler_params=pltpu.CompilerParams(
            dimension_semantics=("arbitrary",), vmem_limit_bytes=VMEM_LIMIT),
        name="mix_route",
    )(merged, x_p, x_s, wo, norm_w, r_cat)


MOE_MOVE_TOKENS = 512
MOE_MOVE_UNROLL = 8


def _moe_scatter_body(dest_ref, fill_ref, h_ref, xs_hbm, zeros, sem, zsem):
    tt = h_ref.shape[0]
    step = pl.program_id(0)
    t0 = step * tt
    n_experts = (fill_ref.shape[0] - 1) // 2
    n_blocks = xs_hbm.shape[0] // MOE_ROWS
    pad_bits = [1 << k for k in reversed(range(3, MOE_ROWS.bit_length() - 1))]

    def fills(action):
        def expert(e, carry):
            row, n_pad = fill_ref[e], fill_ref[n_experts + e]
            head = (-row) & (SUBLANE - 1)
            for r in range(SUBLANE - 1):
                @pl.when(r < head)
                def _():
                    action(pltpu.make_async_copy(zeros.at[pl.ds(0, 1)], xs_hbm.at[pl.ds(row + r, 1)], zsem))
            row, n_pad = row + head, n_pad - head
            for bit in pad_bits:
                @pl.when((n_pad & bit) != 0)
                def _():
                    action(pltpu.make_async_copy(
                        zeros.at[pl.ds(0, bit)], xs_hbm.at[pl.ds(pl.multiple_of(row, SUBLANE), bit)], zsem))
                row = row + (n_pad & bit)
            return carry

        def tail(blk, carry):
            action(pltpu.make_async_copy(
                zeros, xs_hbm.at[pl.ds(pl.multiple_of(blk * MOE_ROWS, MOE_ROWS), MOE_ROWS)], zsem))
            return carry

        lax.fori_loop(0, n_experts, expert, 0)
        lax.fori_loop(fill_ref[2 * n_experts], n_blocks, tail, 0)

    @pl.when(step == 0)
    def _():
        zeros[...] = jnp.zeros(zeros.shape, zeros.dtype)
        fills(lambda c: c.start())

    def start(i, carry):
        for k in range(MOE_TOP_K):
            pltpu.make_async_copy(
                h_ref.at[pl.ds(i, 1)], xs_hbm.at[pl.ds(dest_ref[(t0 + i) * MOE_TOP_K + k], 1)], sem
            ).start(priority=k % 2)
        return carry

    lax.fori_loop(0, tt, start, 0, unroll=MOE_MOVE_UNROLL)
    for _ in range(MOE_TOP_K):
        pltpu.make_async_copy(h_ref, xs_hbm.at[pl.ds(0, tt)], sem).wait()

    @pl.when(step == pl.num_programs(0) - 1)
    def _():
        fills(lambda c: c.wait())


def _moe_scatter(dest, fill_rows, h2, n_rows):
    T, d = h2.shape
    tt = _tile(T, MOE_MOVE_TOKENS, SUBLANE)
    return pl.pallas_call(
        _moe_scatter_body,
        grid_spec=pltpu.PrefetchScalarGridSpec(
            num_scalar_prefetch=2,
            grid=(T // tt,),
            in_specs=[pl.BlockSpec((tt, d), lambda i, dr, fr: (i, 0))],
            out_specs=pl.BlockSpec(memory_space=pl.ANY),
            scratch_shapes=[pltpu.VMEM((MOE_ROWS, d), h2.dtype), pltpu.SemaphoreType.DMA(()),
                            pltpu.SemaphoreType.DMA(())],
        ),
        out_shape=jax.ShapeDtypeStruct((n_rows, d), h2.dtype),
        compiler_params=pltpu.CompilerParams(dimension_semantics=("arbitrary",)),
        name="moe_scatter",
    )(dest, fill_rows, h2)


MOE_WEIGHT_PIECES = 8
MOE_WEIGHT_RING = 4


def _moe_ffn_body(sched_ref, order_ref, n_used_ref, x_ref, wg_hbm, wu_hbm, wd_hbm, o_ref,
                  wg_b, wu_b, wd_b, sg, su, sd, sems):
    b = pl.program_id(0)
    hbm = (wg_hbm, wu_hbm, wd_hbm)
    stage = (sg, su, sd)
    resident = (wg_b, wu_b, wd_b)
    n_total = order_ref[order_ref.shape[0] - 1]

    def copies(c):
        e = order_ref[c // MOE_WEIGHT_PIECES]
        piece = c % MOE_WEIGHT_PIECES
        ring = c % MOE_WEIGHT_RING
        out = []
        for k in range(3):
            rows = stage[k].shape[1]
            out.append(pltpu.make_async_copy(
                hbm[k].at[0, e, pl.ds(pl.multiple_of(piece * rows, rows), rows), :],
                stage[k].at[ring], sems.at[ring, k]))
        return out

    def start(c):
        @pl.when(c < n_total)
        def _():
            for k, cp in enumerate(copies(c)):
                cp.start(priority=k % 2)

    def land(lo, hi):
        def body(c, carry):
            slot = (c // MOE_WEIGHT_PIECES) % 2
            piece = c % MOE_WEIGHT_PIECES
            ring = c % MOE_WEIGHT_RING
            for k, cp in enumerate(copies(c)):
                cp.wait()
                rows = stage[k].shape[1]
                resident[k][slot, pl.ds(pl.multiple_of(piece * rows, rows), rows), :] = stage[k][ring].astype(BF16)
            start(c + MOE_WEIGHT_RING)
            return carry

        lax.fori_loop(lo, hi, body, 0)

    @pl.when(b < n_used_ref[0])
    def _():
        @pl.when(b == 0)
        def _():
            for c in range(MOE_WEIGHT_RING):
                start(c)
            land(0, MOE_WEIGHT_PIECES)

        slot = sched_ref[0, b]
        xb = x_ref[...].astype(BF16)
        gate = jnp.dot(xb, wg_b[slot], preferred_element_type=F32)
        up = jnp.dot(xb, wu_b[slot], preferred_element_type=F32)
        act = (_silu(gate) * up).astype(BF16)
        o_ref[...] = jnp.dot(act, wd_b[slot], preferred_element_type=F32)
        land(sched_ref[1, b], sched_ref[2, b])

    @pl.when(b >= n_used_ref[0])
    def _():
        o_ref[...] = jnp.zeros(o_ref.shape, o_ref.dtype)


def _moe_ffn(sched, order, n_used, xs, wg, wu, wd):
    R, d = xs.shape
    f = wg.shape[3]
    nb = R // MOE_ROWS
    np_ = MOE_WEIGHT_PIECES
    assert d % np_ == 0 and f % np_ == 0
    any_spec = pl.BlockSpec(memory_space=pl.ANY)
    return pl.pallas_call(
        _moe_ffn_body,
        grid_spec=pltpu.PrefetchScalarGridSpec(
            num_scalar_prefetch=3,
            grid=(nb,),
            in_specs=[pl.BlockSpec((MOE_ROWS, d), lambda b, sc, od, nu: (jnp.minimum(b, nu[0] - 1), 0)),
                      any_spec, any_spec, any_spec],
            out_specs=pl.BlockSpec((MOE_ROWS, d), lambda b, sc, od, nu: (b, 0)),
            scratch_shapes=[
                pltpu.VMEM((2, d, f), BF16), pltpu.VMEM((2, d, f), BF16), pltpu.VMEM((2, f, d), BF16),
                pltpu.VMEM((MOE_WEIGHT_RING, d // np_, f), F32), pltpu.VMEM((MOE_WEIGHT_RING, d // np_, f), F32),
                pltpu.VMEM((MOE_WEIGHT_RING, f // np_, d), F32),
                pltpu.SemaphoreType.DMA((MOE_WEIGHT_RING, 3)),
            ],
        ),
        out_shape=jax.ShapeDtypeStruct((R, d), F32),
        compiler_params=pltpu.CompilerParams(
            dimension_semantics=("arbitrary",), vmem_limit_bytes=VMEM_LIMIT),
        name="moe_ffn",
    )(sched, order, n_used, xs, wg, wu, wd)


def _moe_combine_body(dest_ref, x1_ref, wts_ref, nw_ref, yb_hbm, op_ref, os_ref, buf, sems, *, prompt_tiles):
    tt = x1_ref.shape[0]
    i = pl.program_id(0)

    def gather(tile, slot):
        def start(r, carry):
            for k in range(MOE_TOP_K):
                pltpu.make_async_copy(
                    yb_hbm.at[pl.ds(dest_ref[(tile * tt + r) * MOE_TOP_K + k], 1)],
                    buf.at[slot, k, pl.ds(r, 1)], sems.at[slot]).start(priority=k % 2)
            return carry

        lax.fori_loop(0, tt, start, 0, unroll=MOE_MOVE_UNROLL)

    @pl.when(i == 0)
    def _():
        gather(0, 0)

    @pl.when(i + 1 < pl.num_programs(0))
    def _():
        gather(i + 1, (i + 1) % 2)

    slot = i % 2
    for k in range(MOE_TOP_K):
        pltpu.make_async_copy(yb_hbm.at[pl.ds(0, tt)], buf.at[slot, k], sems.at[slot]).wait()

    def finish(o_ref):
        w = wts_ref[...]
        x2 = x1_ref[...] + (buf[slot, 0] * w[:, 0:1] + buf[slot, 1] * w[:, 1:2])
        o_ref[...] = x2 * lax.rsqrt(jnp.mean(x2 * x2, axis=-1, keepdims=True) + NORM_EPS) * nw_ref[...]

    _by_stream(i, prompt_tiles, finish, op_ref, os_ref)


def _moe_combine(dest, x1, wts, norm_w, yb, t_p):
    T, d = x1.shape
    tt = _tile(math.gcd(T, t_p), MOE_MOVE_TOKENS, SUBLANE)
    npt = t_p // tt
    op_spec, os_spec = _stream_specs((tt, d), npt)
    return pl.pallas_call(
        functools.partial(_moe_combine_body, prompt_tiles=npt),
        grid_spec=pltpu.PrefetchScalarGridSpec(
            num_scalar_prefetch=1,
            grid=(T // tt,),
            in_specs=[
                pl.BlockSpec((tt, d), lambda i, dr: (i, 0)),
                pl.BlockSpec((tt, LANE), lambda i, dr: (i, 0)),
                pl.BlockSpec((1, d), lambda i, dr: (0, 0)),
                pl.BlockSpec(memory_space=pl.ANY),
            ],
            out_specs=[op_spec, os_spec],
            scratch_shapes=[pltpu.VMEM((2, MOE_TOP_K, tt, d), F32), pltpu.SemaphoreType.DMA((2,))],
        ),
        out_shape=[jax.ShapeDtypeStruct((t_p, d), F32), jax.ShapeDtypeStruct((T - t_p, d), F32)],
        compiler_params=pltpu.CompilerParams(dimension_semantics=("arbitrary",), vmem_limit_bytes=VMEM_LIMIT),
        name="moe_combine",
    )(dest, x1, wts, norm_w, yb)


def _route_rows(eid, rank, counts, n_assign):
    n_experts = counts.shape[0]
    padded = (counts + MOE_ROWS - 1) // MOE_ROWS * MOE_ROWS
    pend = jnp.cumsum(padded)
    pstart = pend - padded
    experts = jnp.arange(n_experts, dtype=jnp.int32)
    onehot = eid[:, :, None] == experts[None, None, :]
    dest = (jnp.sum(jnp.where(onehot, pstart[None, None, :], 0), axis=-1) + rank).astype(jnp.int32)
    n_blocks = -(-n_assign // MOE_ROWS) + n_experts
    blk_start = jnp.arange(n_blocks, dtype=jnp.int32) * MOE_ROWS
    used = blk_start < pend[-1]
    blk_e = jnp.minimum(jnp.sum((blk_start[:, None] >= pend[None, :]).astype(jnp.int32), axis=1), n_experts - 1)
    n_used = (pend[-1] // MOE_ROWS).astype(jnp.int32).reshape(1)

    nonempty = counts > 0
    ordinal = jnp.cumsum(nonempty.astype(jnp.int32)) - 1
    n_nonempty = jnp.sum(nonempty.astype(jnp.int32))
    order = jnp.argsort(jnp.where(nonempty, experts, n_experts + experts)).astype(jnp.int32)
    n_slices = MOE_WEIGHT_PIECES * n_nonempty
    pos = blk_start // MOE_ROWS - (pstart // MOE_ROWS)[blk_e]
    nblk = jnp.maximum((padded // MOE_ROWS)[blk_e], 1)
    has_next = used & (ordinal[blk_e] + 1 < n_nonempty)
    base = MOE_WEIGHT_PIECES * (ordinal[blk_e] + 1)
    first = jnp.where(has_next, base + MOE_WEIGHT_PIECES * pos // nblk, n_slices)
    last = jnp.where(has_next, base + MOE_WEIGHT_PIECES * (pos + 1) // nblk, n_slices)
    slot = jnp.where(used, ordinal[blk_e] % 2, 0)
    sched = jnp.stack([slot, first, last]).astype(jnp.int32)
    order = jnp.concatenate([order, n_slices.reshape(1)]).astype(jnp.int32)
    fill_rows = jnp.concatenate([pstart + counts, padded - counts, n_used]).astype(jnp.int32)
    return dest.reshape(-1), sched, order, fill_rows, n_used, n_blocks


def _pad_lanes(v, fill=0.0):
    return jnp.pad(v.astype(F32), (0, LANE - v.shape[0]), constant_values=fill).reshape(1, LANE)


def kernel(x_prompt, x_sample, state_ssm, state_ssd_conv, state_short_conv, norm_mixer, w_in, ssd_conv_w,
           ssd_conv_b, ssd_dt_bias, ssd_a_log, ssd_d, ssd_norm, sc_conv_w, w_branch_out, w_out, norm_ffn,
           w_router_coarse, w_router_fine, w_expert_gate, w_expert_up, w_expert_down, norm_final):
    depth = w_in.shape[0]
    assert depth == 1
    n_p, seq_p, d = x_prompt.shape
    n_s, seq_s, _ = x_sample.shape
    d_inner = 2 * d
    n_heads = d_inner // HEAD_DIM
    n_groups = d_inner // GROUP_W
    gn = n_groups * STATE_DIM
    conv_dim = d_inner + 2 * gn
    assert conv_dim == 3 * d and n_heads <= LANE and ssd_conv_w.shape[2] == conv_dim
    t_p, t_s = n_p * seq_p, n_s * seq_s
    n_coarse = w_router_coarse.shape[2]
    n_experts = w_router_fine.shape[2]
    assert n_coarse + n_experts <= LANE

    off_dt = 2 * d + d_inner + conv_dim
    off_sc = off_dt + n_heads
    w_in_t = jnp.swapaxes(w_in, 1, 2)
    w_dt = jnp.pad(w_in_t[0, off_dt:off_sc, :].T, ((0, 0), (0, LANE - n_heads))).astype(BF16)
    head_of_col = jnp.arange(d_inner, dtype=jnp.int32) // HEAD_DIM
    expand = (jnp.arange(LANE, dtype=jnp.int32)[:, None] == head_of_col[None, :]).astype(BF16)
    group_of_n = jnp.arange(gn, dtype=jnp.int32) // STATE_DIM
    group_of_head = jnp.arange(LANE, dtype=jnp.int32) // HEADS_PER_GROUP
    gsel = ((group_of_n[:, None] == group_of_head[None, :])
            & (jnp.arange(LANE)[None, :] < n_heads)).astype(BF16)
    consts = (
        ssd_conv_w[0], ssd_conv_b[0].reshape(1, conv_dim), _pad_lanes(ssd_dt_bias[0]), _pad_lanes(ssd_a_log[0]),
        jnp.repeat(ssd_d[0].astype(F32), HEAD_DIM).reshape(1, d_inner), ssd_norm[0].reshape(1, d_inner),
        sc_conv_w[0], expand,
    )
    w_router = jnp.pad(jnp.concatenate([w_router_coarse[0], w_router_fine[0]], axis=1),
                       ((0, 0), (0, LANE - n_coarse - n_experts)))
    r_hi = w_router.astype(BF16)
    r_cat = jnp.concatenate([r_hi, (w_router - r_hi.astype(F32)).astype(BF16)], axis=1)

    x_p = x_prompt.reshape(t_p, d)
    x_s = x_sample.reshape(t_s, d)
    h, dt_raw = _prenorm(x_p, x_s, norm_mixer[0].reshape(1, d), w_dt)
    p, tails = _inproj(h, w_in_t[0], off_dt, off_sc, consts[0], consts[1], t_p, seq_p)
    ya_p, yb_p, p_ssm, p_sc_tail = _ssd_prompt(p, dt_raw, consts, n_p, seq_p, d)
    ya_s, yb_s, s_ssm, s_conv, s_sc = _ssd_sample(
        p, dt_raw, state_ssm[0].reshape(n_s, n_groups, GROUP_W, STATE_DIM), state_ssd_conv[0],
        state_short_conv[0], consts, gsel, t_p, seq_s, d)
    merged = _branch_out(ya_p, yb_p, ya_s, yb_s, p, w_branch_out)
    x1, h2, route, wts, counts = _mix_route(merged, x_p, x_s, w_out, norm_ffn[0].reshape(1, d), r_cat,
                                            n_coarse, n_experts // n_coarse)

    n_assign = (t_p + t_s) * MOE_TOP_K
    dest, sched, order, fill_rows, n_used, n_blocks = _route_rows(
        route[:, 0:MOE_TOP_K], route[:, MOE_TOP_K:2 * MOE_TOP_K],
        counts[0, n_coarse:n_coarse + n_experts].astype(jnp.int32), n_assign)
    xs = _moe_scatter(dest, fill_rows, h2, n_blocks * MOE_ROWS)
    yrows = _moe_ffn(sched, order, n_used, xs, w_expert_gate, w_expert_up, w_expert_down)
    out_p, out_s = _moe_combine(dest, x1, wts, norm_final.reshape(1, d), yrows, t_p)

    kw = ssd_conv_w.shape[1]
    kw2 = sc_conv_w.shape[1]
    tiles_per_seq = (tails.shape[0] // SUBLANE) * seq_p // (t_p + t_s)
    seq_tails = tails.reshape(-1, SUBLANE, tails.shape[1])[tiles_per_seq - 1:n_p * tiles_per_seq:tiles_per_seq]
    p_conv = seq_tails[:, SUBLANE - (kw - 1):, 4 * d:7 * d]
    return (
        out_p.reshape(n_p, seq_p, d),
        out_s.reshape(n_s, seq_s, d),
        p_ssm.reshape(1, n_p, n_heads, HEAD_DIM, STATE_DIM),
        p_conv[None],
        p_sc_tail[:, SUBLANE - (kw2 - 1):, :][None],
        s_ssm.reshape(1, n_s, n_heads, HEAD_DIM, STATE_DIM),
        s_conv[None],
        s_sc[None],
    )
```
